```python
import math
import jax, jax.numpy as jnp
from jax import lax
import numpy as np

D_MODEL = 1024
BATCH = 4
SEQ = 4096
DEPTH = 2

M_HEADS = 4
M_DQK = 64
M_DV = 128
M_CHUNK = 64
M_CONV = 4
M_QK_W = M_HEADS * M_DQK
M_V_W = M_HEADS * M_DV

A_HEADS = 8
A_KV_HEADS = 2
A_HEAD_DIM = 64
A_GROUP = A_HEADS // A_KV_HEADS
A_Q_W = A_HEADS * A_HEAD_DIM
A_KV_W = A_KV_HEADS * A_HEAD_DIM
WINDOW = 128

N_BUCKETS = 32
MAX_DISTANCE = 128

N_EXPERTS = 16
N_GROUPS = 4
EXPERTS_PER_GROUP = N_EXPERTS // N_GROUPS
GROUP_SCORE_TOPK = 2
TOP_K = 2
D_EXPERT = 512

IN_SIZES = (M_QK_W, M_QK_W, M_V_W, M_V_W, M_HEADS, M_HEADS,
            A_Q_W, A_KV_W, A_KV_W, D_MODEL, D_MODEL)
D_IN = M_QK_W * 2 + M_V_W * 2 + M_HEADS * 2 + A_Q_W + A_KV_W * 2 + D_MODEL * 2

EPS = 1e-6
NEG_INF = -1e30

kernel_name = 'hybrid_mlstm_swa_groupmoe_block'


def _rmsnorm(x, w):
    xf = x.astype(jnp.float32)
    y = xf * lax.rsqrt(jnp.mean(xf * xf, axis=-1, keepdims=True) + EPS)
    return (y * w.astype(jnp.float32)).astype(x.dtype)


def _modulate(h, shift, scale):
    return h * (1 + scale[:, None, :]) + shift[:, None, :]


def _split_points(sizes):
    pts, acc = [], 0
    for sz in sizes[:-1]:
        acc += sz
        pts.append(acc)
    return pts


def _causal_conv(x, w, b):
    s = x.shape[1]
    xp = jnp.pad(x, ((0, 0), (M_CONV - 1, 0), (0, 0)))
    y = b
    for j in range(M_CONV):
        y = y + w[j] * xp[:, j:j + s, :]
    return y


def _t5_bucket(n):
    max_exact = N_BUCKETS // 2
    large = max_exact + (np.log(np.maximum(n, 1) / max_exact)
                         / math.log(MAX_DISTANCE / max_exact)
                         * (N_BUCKETS - max_exact)).astype(np.int32)
    large = np.minimum(large, N_BUCKETS - 1)
    return np.where(n < max_exact, n, large).astype(np.int32)


def _chunk_step(carry, inp):
    C, n, m = carry
    C_loc, n_loc, m_loc, b_last = inp
    m_new = jnp.maximum(b_last + m, m_loc)
    decay = jnp.exp(b_last + m - m_new)
    fresh = jnp.exp(m_loc - m_new)
    C_new = decay[..., None, None] * C + fresh[..., None, None] * C_loc
    n_new = decay[..., None] * n + fresh[..., None] * n_loc
    return (C_new, n_new, m_new), (C, n, m)


def _mlstm(q, k, v, i_pre, f_pre):
    b, s, _ = q.shape
    nc = s // M_CHUNK

    def heads(t, d):
        t = t.astype(jnp.float32).reshape(b, s, M_HEADS, d).transpose(0, 2, 1, 3)
        return t.reshape(b, M_HEADS, nc, M_CHUNK, d)

    def gates(t):
        return t.astype(jnp.float32).transpose(0, 2, 1).reshape(b, M_HEADS, nc, M_CHUNK)

    q = heads(q, M_DQK) * (M_DQK ** -0.5)
    k = heads(k, M_DQK)
    v = heads(v, M_DV)
    ig = gates(i_pre)
    logf = jax.nn.log_sigmoid(gates(f_pre))
    bcum = jnp.cumsum(logf, axis=-1)
    b_last = bcum[..., -1]

    w_state = b_last[..., None] - bcum + ig
    m_loc = jnp.max(w_state, axis=-1)
    a = jnp.exp(w_state - m_loc[..., None])
    C_loc = jnp.einsum('bhcl,bhcld,bhcle->bhcde', a, v, k)
    n_loc = jnp.einsum('bhcl,bhcle->bhce', a, k)

    init = (jnp.zeros((b, M_HEADS, M_DV, M_DQK), jnp.float32),
            jnp.zeros((b, M_HEADS, M_DQK), jnp.float32),
            jnp.zeros((b, M_HEADS), jnp.float32))
    xs = (jnp.moveaxis(C_loc, 2, 0), jnp.moveaxis(n_loc, 2, 0),
          jnp.moveaxis(m_loc, 2, 0), jnp.moveaxis(b_last, 2, 0))
    _, (C_prev, n_prev, m_prev) = lax.scan(_chunk_step, init, xs)
    C_prev = jnp.moveaxis(C_prev, 0, 2)
    n_prev = jnp.moveaxis(n_prev, 0, 2)
    m_prev = jnp.moveaxis(m_prev, 0, 2)

    causal = jnp.tril(jnp.ones((M_CHUNK, M_CHUNK), dtype=bool))
    d_log = jnp.where(causal, bcum[..., :, None] - bcum[..., None, :] + ig[..., None, :], -jnp.inf)
    m_inter = bcum + m_prev[..., None]
    m_row = jnp.maximum(m_inter, jnp.max(d_log, axis=-1))
    scores = jnp.einsum('bhcld,bhcsd->bhcls', q, k) * jnp.exp(d_log - m_row[..., None])
    inter = jnp.exp(m_inter - m_row)
    num = (jnp.einsum('bhcls,bhcsd->bhcld', scores, v)
           + inter[..., None] * jnp.einsum('bhcle,bhcde->bhcld', q, C_prev))
    den = scores.sum(-1) + inter * jnp.einsum('bhcle,bhce->bhcl', q, n_prev)
    h = num / jnp.maximum(jnp.abs(den), jnp.exp(-m_row))[..., None]
    return h.reshape(b, M_HEADS, s, M_DV).transpose(0, 2, 1, 3).reshape(b, s, M_V_W)


def _sliding_window_attention(q, k, v, sinks, rel_bias):
    b, s, _ = q.shape
    nb = s // WINDOW
    qb = q.reshape(b, nb, WINDOW, A_KV_HEADS, A_GROUP, A_HEAD_DIM).astype(jnp.float32)

    def band(t):
        t = t.reshape(b, s, A_KV_HEADS, A_HEAD_DIM)
        tp = jnp.pad(t, ((0, 0), (WINDOW, 0), (0, 0), (0, 0)))
        tp = tp.reshape(b, nb + 1, WINDOW, A_KV_HEADS, A_HEAD_DIM)
        return jnp.concatenate([tp[:, :-1], tp[:, 1:]], axis=2)

    kb = band(k).astype(jnp.float32)
    vb = band(v)

    q_pos = np.arange(WINDOW)[:, None]
    k_pos = np.arange(2 * WINDOW)[None, :]
    dist = q_pos + WINDOW - k_pos
    in_window = (dist >= 0) & (dist < WINDOW)
    key_valid = (np.arange(nb)[:, None] * WINDOW + np.arange(2 * WINDOW)[None, :] - WINDOW) >= 0
    mask = jnp.asarray(in_window[None] & key_valid[:, None, :])
    bucket = jnp.asarray(_t5_bucket(np.maximum(dist, 0)))
    bias = jnp.take(rel_bias.astype(jnp.float32), bucket, axis=0)
    bias = bias.transpose(2, 0, 1).reshape(A_KV_HEADS, A_GROUP, WINDOW, 2 * WINDOW)

    scores = jnp.einsum('bnqkgd,bnskd->bnkgqs', qb, kb) * (A_HEAD_DIM ** -0.5)
    scores = jnp.where(mask[None, :, None, None], scores + bias, NEG_INF)
    sink = jnp.broadcast_to(sinks.astype(jnp.float32).reshape(A_KV_HEADS, A_GROUP, 1, 1),
                            scores.shape[:-1] + (1,))
    probs = jax.nn.softmax(jnp.concatenate([scores, sink], axis=-1), axis=-1)[..., :-1]
    out = jnp.einsum('bnkgqs,bnskd->bnqkgd', probs.astype(v.dtype), vb)
    return out.reshape(b, s, A_Q_W)


def _hybrid_mixer(h, w_in, conv_w, conv_b, b_igate, b_fgate, w_mnorm, sinks, rel_bias,
                  w_br_m, w_br_a, w_out):
    z = h @ w_in
    (q_m, k_m, v_m, o_m, i_m, f_m, q_a, k_a, v_a, g_m, g_a) = jnp.split(
        z, _split_points(IN_SIZES), axis=-1)
    qk = jax.nn.silu(_causal_conv(jnp.concatenate([q_m, k_m], axis=-1), conv_w, conv_b))
    q_m, k_m = jnp.split(qk, 2, axis=-1)
    h_m = _mlstm(q_m, k_m, v_m, i_m + b_igate, f_m + b_fgate).astype(h.dtype)
    bsz, s = h_m.shape[0], h_m.shape[1]
    h_m = _rmsnorm(h_m.reshape(bsz, s, M_HEADS, M_DV), w_mnorm.reshape(M_HEADS, M_DV))
    h_m = h_m.reshape(bsz, s, M_V_W) * jax.nn.sigmoid(o_m)
    h_a = _sliding_window_attention(q_a, k_a, v_a, sinks, rel_bias)
    merged = jax.nn.sigmoid(g_m) * (h_m @ w_br_m) + jax.nn.sigmoid(g_a) * (h_a @ w_br_a)
    return merged @ w_out


def _grouped_moe(h, w_router, router_bias, w_gate, w_up, w_down):
    b, s, d = h.shape
    t = h.reshape(b * s, d)
    scores = jax.nn.sigmoid((t @ w_router).astype(jnp.float32))
    sel = scores + router_bias.astype(jnp.float32)
    grp = lax.top_k(sel.reshape(-1, N_GROUPS, EXPERTS_PER_GROUP), GROUP_SCORE_TOPK)[0].sum(-1)
    gsel = jnp.argmax(grp, axis=-1)
    gmask = jnp.repeat(gsel[:, None] == jnp.arange(N_GROUPS)[None, :], EXPERTS_PER_GROUP, axis=-1)
    _, idx = lax.top_k(jnp.where(gmask, sel, -jnp.inf), TOP_K)
    wts = jnp.take_along_axis(scores, idx, axis=-1)
    wts = wts / jnp.sum(wts, axis=-1, keepdims=True)
    gates = jnp.sum(jax.nn.one_hot(idx, N_EXPERTS, dtype=jnp.float32) * wts[..., None], axis=1)
    gates = gates.astype(h.dtype)
    y = jnp.zeros_like(t)
    for e in range(N_EXPERTS):
        act = jax.nn.silu(t @ w_gate[e]) * (t @ w_up[e])
        y = y + gates[:, e:e + 1] * (act @ w_down[e])
    return y.reshape(b, s, d)


def setup_inputs(seed: int = 0) -> dict:
    key = jax.random.key(seed)
    ks = jax.random.split(key, 23)

    def nrm(k, shape, scale):
        return jax.random.normal(k, shape, jnp.float32) * scale

    return {
        'x': nrm(ks[0], (BATCH, SEQ, D_MODEL), 1.0),
        'c': nrm(ks[1], (BATCH, D_MODEL), 1.0),
        'w_ada': nrm(ks[2], (DEPTH, D_MODEL, 6 * D_MODEL), 0.5 * D_MODEL ** -0.5),
        'b_ada': nrm(ks[3], (DEPTH, 6 * D_MODEL), 0.02),
        'w_norm1': 1.0 + nrm(ks[4], (DEPTH, D_MODEL), 0.02),
        'w_in': nrm(ks[5], (DEPTH, D_MODEL, D_IN), D_MODEL ** -0.5),
        'conv_w': nrm(ks[6], (DEPTH, M_CONV, 2 * M_QK_W), M_CONV ** -0.5),
        'conv_b': nrm(ks[7], (DEPTH, 2 * M_QK_W), 0.02),
        'b_igate': nrm(ks[8], (DEPTH, M_HEADS), 0.1),
        'b_fgate': jnp.linspace(3.0, 6.0, M_HEADS, dtype=jnp.float32)[None, :]
                   + nrm(ks[9], (DEPTH, M_HEADS), 0.1),
        'w_mnorm': 1.0 + nrm(ks[10], (DEPTH, M_V_W), 0.02),
        'sinks': nrm(ks[11], (DEPTH, A_HEADS), 0.5),
        'rel_bias': nrm(ks[12], (N_BUCKETS, A_HEADS), 0.3),
        'w_br_m': nrm(ks[13], (DEPTH, M_V_W, D_MODEL), M_V_W ** -0.5),
        'w_br_a': nrm(ks[14], (DEPTH, A_Q_W, D_MODEL), A_Q_W ** -0.5),
        'w_out': nrm(ks[15], (DEPTH, D_MODEL, D_MODEL), D_MODEL ** -0.5),
        'w_norm2': 1.0 + nrm(ks[16], (DEPTH, D_MODEL), 0.02),
        'w_router': nrm(ks[17], (D_MODEL, N_EXPERTS), D_MODEL ** -0.5),
        'router_bias': nrm(ks[18], (N_EXPERTS,), 0.01),
        'w_gate_e': nrm(ks[19], (DEPTH, N_EXPERTS, D_MODEL, D_EXPERT), D_MODEL ** -0.5),
        'w_up_e': nrm(ks[20], (DEPTH, N_EXPERTS, D_MODEL, D_EXPERT), D_MODEL ** -0.5),
        'w_down_e': nrm(ks[21], (DEPTH, N_EXPERTS, D_EXPERT, D_MODEL), D_EXPERT ** -0.5),
        'w_final': 1.0 + nrm(ks[22], (D_MODEL,), 0.02),
    }


def reference(x, c, w_ada, b_ada, w_norm1, w_in, conv_w, conv_b, b_igate, b_fgate, w_mnorm,
              sinks, rel_bias, w_br_m, w_br_a, w_out, w_norm2, w_router, router_bias,
              w_gate_e, w_up_e, w_down_e, w_final):
    cond = jax.nn.silu(c)
    for l in range(DEPTH):
        mod = cond @ w_ada[l] + b_ada[l]
        sh1, sc1, g1, sh2, sc2, g2 = jnp.split(mod, 6, axis=-1)
        h = _modulate(_rmsnorm(x, w_norm1[l]), sh1, sc1)
        x = x + g1[:, None, :] * _hybrid_mixer(
            h, w_in[l], conv_w[l], conv_b[l], b_igate[l], b_fgate[l], w_mnorm[l], sinks[l],
            rel_bias, w_br_m[l], w_br_a[l], w_out[l])
        h = _modulate(_rmsnorm(x, w_norm2[l]), sh2, sc2)
        x = x + g2[:, None, :] * _grouped_moe(
            h, w_router, router_bias, w_gate_e[l], w_up_e[l], w_down_e[l])
    return _rmsnorm(x, w_final)
```

```python
import functools

import numpy as np
import jax
import jax.numpy as jnp
from jax import lax
from jax.experimental import pallas as pl
from jax.experimental.pallas import tpu as pltpu

F32 = jnp.float32
BF16 = jnp.bfloat16

D_MODEL = 1024
M_HEADS = 4
M_DQK = 64
M_DV = 128
M_CONV = 4
M_QK_W = M_HEADS * M_DQK
M_V_W = M_HEADS * M_DV
A_HEADS = 8
A_KV_HEADS = 2
A_HEAD_DIM = 64
A_Q_W = A_HEADS * A_HEAD_DIM
A_KV_W = A_KV_HEADS * A_HEAD_DIM
WINDOW = 128
N_BUCKETS = 32
MAX_DISTANCE = 128
N_EXPERTS = 16
N_GROUPS = 4
EXPERTS_PER_GROUP = N_EXPERTS // N_GROUPS
D_EXPERT = 512
EPS = 1e-6
NEG_INF = -1e30

LANES = 128
MLSTM_CHUNK = 128
TOKEN_TILE = 256
EXPERT_TILE = 256
ROW_TILE = 512
PAIRS_PER_GROUP = 6
N_CLASSES = N_GROUPS * PAIRS_PER_GROUP
CLASS_ROWS = 32
PAYLOAD_W = D_MODEL + LANES
VMEM_LIMIT = 48 * 1024 * 1024

C_QK = 0
C_V = C_QK + 2 * M_QK_W
C_O = C_V + M_V_W
C_QA = C_O + M_V_W
C_KVA = C_QA + A_Q_W
C_G = C_KVA + 2 * A_KV_W
C_IF = C_G + 2 * D_MODEL
C_END = C_IF + LANES

A_HEAD_ORDER = (0, 4, 1, 5, 2, 6, 3, 7)

_PAIRS = [(a, b) for a in range(EXPERTS_PER_GROUP) for b in range(a + 1, EXPERTS_PER_GROUP)]
_CLASS_EA = np.array([g * EXPERTS_PER_GROUP + a for g in range(N_GROUPS) for a, _ in _PAIRS], np.int32)
_CLASS_EB = np.array([g * EXPERTS_PER_GROUP + b for g in range(N_GROUPS) for _, b in _PAIRS], np.int32)

_NT = (((1,), (1,)), ((), ()))
_TN = (((0,), (0,)), ((), ()))


def _sigmoid(x):
    return 1.0 / (1.0 + jnp.exp(-x))


def _log_sigmoid(x):
    return -(jnp.maximum(-x, 0.0) + jnp.log1p(jnp.exp(-jnp.abs(x))))


def _dot(a, b, dims=None):
    if dims is None:
        return jnp.dot(a, b, preferred_element_type=F32)
    return lax.dot_general(a, b, dims, preferred_element_type=F32)


def _split2(a):
    hi = a.astype(BF16)
    lo = (a - hi.astype(F32)).astype(BF16)
    return hi, lo


def _split3(a):
    hi = a.astype(BF16)
    r = a - hi.astype(F32)
    mid = r.astype(BF16)
    lo = (r - mid.astype(F32)).astype(BF16)
    return hi, mid, lo


def _dot_hi(a, b, dims=None):
    ah, al = _split2(a)
    bh, bl = _split2(b)
    return _dot(ah, bh, dims) + _dot(ah, bl, dims) + _dot(al, bh, dims)


def _rms(x):
    return x * lax.rsqrt(jnp.mean(x * x, axis=-1, keepdims=True) + EPS)


def _params(*sem):
    return pltpu.CompilerParams(dimension_semantics=sem, vmem_limit_bytes=VMEM_LIMIT)


def _ada_kernel(c_ref, w_ref, b_ref, o_ref):
    c = c_ref[...]
    cond = c * _sigmoid(c)
    o_ref[0] = _dot_hi(cond, w_ref[0]) + b_ref[0]


def _ada_call(c8, w_ada, b_ada):
    depth = w_ada.shape[0]
    return pl.pallas_call(
        _ada_kernel,
        grid=(depth, 6),
        in_specs=[
            pl.BlockSpec((8, D_MODEL), lambda l, j: (0, 0)),
            pl.BlockSpec((1, D_MODEL, D_MODEL), lambda l, j: (l, 0, j)),
            pl.BlockSpec((1, 1, D_MODEL), lambda l, j: (l, 0, j)),
        ],
        out_specs=pl.BlockSpec((1, 8, D_MODEL), lambda l, j: (l, 0, j)),
        out_shape=jax.ShapeDtypeStruct((depth, 8, 6 * D_MODEL), F32),
        compiler_params=_params("arbitrary", "arbitrary"),
        name="adaln_mod",
    )(c8, w_ada, b_ada.reshape(depth, 1, 6 * D_MODEL))


def _inproj_kernel(x_ref, mod_ref, wn_ref, w_ref, wift_ref, bcol_ref, brow_ref,
                   qk_ref, v_ref, o_ref, qa_ref, kva_ref, g_ref, ifc_ref, ifr_ref):
    x = x_ref[0]
    h = _rms(x) * wn_ref[...] * (1.0 + mod_ref[0, 1:2, :]) + mod_ref[0, 0:1, :]
    hb = h.astype(BF16)

    def seg(a, b):
        return _dot(hb, w_ref[:, a:b])

    qk_ref[0] = seg(C_QK, C_V).astype(BF16)
    v_ref[0] = seg(C_V, C_O).astype(BF16)
    o_ref[0] = seg(C_O, C_QA).astype(BF16)
    qa_ref[0] = seg(C_QA, C_KVA).astype(BF16)
    kva_ref[0] = seg(C_KVA, C_G).astype(BF16)
    g_ref[0] = seg(C_G, C_IF).astype(BF16)
    zc = seg(C_IF, C_END) + bcol_ref[...]
    lane = lax.broadcasted_iota(jnp.int32, zc.shape, 1)
    ifc_ref[0] = jnp.where((lane >= M_HEADS) & (lane < 2 * M_HEADS), _log_sigmoid(zc), zc)
    zr = _dot(wift_ref[...], hb, _NT) + brow_ref[...]
    row = lax.broadcasted_iota(jnp.int32, zr.shape, 0)
    ifr_ref[0] = jnp.where(row >= M_HEADS, _log_sigmoid(zr), zr)


def _inproj_call(x, mod, wn, w, wift, bcol, brow):
    b, s, d = x.shape
    tm = TOKEN_TILE
    tok = lambda w_: pl.BlockSpec((1, tm, w_), lambda i, j: (i, j, 0))
    full = lambda a: pl.BlockSpec(a.shape, lambda i, j: (0,) * a.ndim)
    sds = lambda w_, dt: jax.ShapeDtypeStruct((b, s, w_), dt)
    return pl.pallas_call(
        _inproj_kernel,
        grid=(b, s // tm),
        in_specs=[tok(d), pl.BlockSpec((1, 6, d), lambda i, j: (i, 0, 0)), full(wn), full(w),
                  full(wift), full(bcol), full(brow)],
        out_specs=[tok(2 * M_QK_W), tok(M_V_W), tok(M_V_W), tok(A_Q_W), tok(2 * A_KV_W),
                   tok(2 * D_MODEL), tok(LANES),
                   pl.BlockSpec((1, 8, tm), lambda i, j: (i, 0, j))],
        out_shape=[sds(2 * M_QK_W, BF16), sds(M_V_W, BF16), sds(M_V_W, BF16), sds(A_Q_W, BF16),
                   sds(2 * A_KV_W, BF16), sds(2 * D_MODEL, BF16), sds(LANES, F32),
                   jax.ShapeDtypeStruct((b, 8, s), F32)],
        compiler_params=_params("arbitrary", "arbitrary"),
        name="norm1_inproj",
    )(x, mod, wn, w, wift, bcol, brow)


def _mlstm_kernel(qk_ref, v_ref, o_ref, ifc_ref, ifr_ref, cw_ref, cb_ref, wn_ref,
                  out_ref, xbuf, cstate, mstate):
    L = qk_ref.shape[1]
    pad = 8

    @pl.when(pl.program_id(1) == 0)
    def _():
        xbuf[0:pad, :] = jnp.zeros((pad, xbuf.shape[1]), F32)
        cstate[...] = jnp.zeros(cstate.shape, F32)
        mstate[...] = jnp.zeros(mstate.shape, F32)

    x = qk_ref[0].astype(F32)
    xbuf[pad:pad + L, :] = x
    acc = cb_ref[...] + cw_ref[0:1, :] * xbuf[pad - 3:pad - 3 + L, :]
    for j in range(1, M_CONV):
        acc = acc + cw_ref[j:j + 1, :] * xbuf[pad - 3 + j:pad - 3 + j + L, :]
    xbuf[0:pad, :] = x[L - pad:L, :]
    qk = acc * _sigmoid(acc)
    q = qk[:, :M_QK_W] * (M_DQK ** -0.5)
    kb = qk[:, M_QK_W:].astype(BF16)
    vb = v_ref[0]

    r_i = lax.broadcasted_iota(jnp.int32, (L, L), 0)
    c_i = lax.broadcasted_iota(jnp.int32, (L, L), 1)
    causal = c_i <= r_i
    tril = jnp.where(causal, 1.0, 0.0).astype(BF16)
    triu = jnp.where(r_i <= c_i, 1.0, 0.0).astype(BF16)

    icol = ifc_ref[0]
    irow = ifr_ref[0]
    ch, cm, cl = _split3(icol)
    bcum_col = _dot(tril, ch) + _dot(tril, cm) + _dot(tril, cl)
    rh, rm, rl = _split3(irow)
    bcum_row = _dot(rh, triu) + _dot(rm, triu) + _dot(rl, triu)

    lane = lax.broadcasted_iota(jnp.int32, (1, LANES), 1)
    lo_half = lane < M_DQK
    ones = jnp.ones((L, LANES), BF16)

    for h in range(M_HEADS):
        t, p = h // 2, h % 2
        qt = q[:, LANES * t:LANES * (t + 1)]
        qm = jnp.where(lo_half if p == 0 else jnp.logical_not(lo_half), qt, 0.0).astype(BF16)
        kt = kb[:, LANES * t:LANES * (t + 1)]
        s = _dot(qm, kt, _NT)

        bc_col = bcum_col[:, M_HEADS + h:M_HEADS + h + 1]
        i_col = icol[:, h:h + 1]
        bc_row = bcum_row[M_HEADS + h:M_HEADS + h + 1, :]
        i_row = irow[h:h + 1, :]
        b_last = bc_row[:, L - 1:L]
        m_prev = mstate[h][:, 0:1]

        dlog = jnp.where(causal, bc_col - bc_row + i_row, NEG_INF)
        m_inter = bc_col + m_prev
        m_row = jnp.maximum(m_inter, jnp.max(dlog, axis=-1, keepdims=True))
        sc = (s * jnp.exp(dlog - m_row)).astype(BF16)
        inter = jnp.exp(m_inter - m_row)

        vaug = jnp.concatenate([vb[:, M_DV * h:M_DV * (h + 1)], ones], axis=1)
        cst = cstate[h]
        num = _dot(sc, vaug) + inter * _dot(qm, cst.astype(BF16))
        den = num[:, M_DV:]
        hh = num[:, :M_DV] / jnp.maximum(jnp.abs(den), jnp.exp(-m_row))

        ws_row = b_last - bc_row + i_row
        m_loc = jnp.max(ws_row, axis=-1, keepdims=True)
        a_col = jnp.exp(b_last - bc_col + i_col - m_loc)
        av = (a_col * vaug.astype(F32)).astype(BF16)
        c_loc = _dot(kt, av, _TN)
        m_new = jnp.maximum(b_last + m_prev, m_loc)
        decay = jnp.exp(b_last + m_prev - m_new)
        fresh = jnp.exp(m_loc - m_new)
        cstate[h] = decay * cst + fresh * c_loc
        mstate[h] = jnp.broadcast_to(m_new, (1, LANES))

        hn = _rms(hh) * wn_ref[:, M_DV * h:M_DV * (h + 1)]
        og = _sigmoid(o_ref[0, :, M_DV * h:M_DV * (h + 1)].astype(F32))
        out_ref[0, :, M_DV * h:M_DV * (h + 1)] = (hn * og).astype(BF16)


def _mlstm_call(qk, v, o, ifc, ifr, conv_w, conv_b, w_mnorm):
    b, s, _ = qk.shape
    L = MLSTM_CHUNK
    tok = lambda w_: pl.BlockSpec((1, L, w_), lambda i, j: (i, j, 0))
    full = lambda a: pl.BlockSpec(a.shape, lambda i, j: (0,) * a.ndim)
    return pl.pallas_call(
        _mlstm_kernel,
        grid=(b, s // L),
        in_specs=[tok(2 * M_QK_W), tok(M_V_W), tok(M_V_W), tok(LANES),
                  pl.BlockSpec((1, 8, L), lambda i, j: (i, 0, j)),
                  full(conv_w), full(conv_b), full(w_mnorm)],
        out_specs=tok(M_V_W),
        out_shape=jax.ShapeDtypeStruct((b, s, M_V_W), BF16),
        scratch_shapes=[pltpu.VMEM((L + 8, 2 * M_QK_W), F32),
                        pltpu.VMEM((M_HEADS, LANES, 2 * M_DV), F32),
                        pltpu.VMEM((M_HEADS, 1, LANES), F32)],
        compiler_params=_params("arbitrary", "arbitrary"),
        name="mlstm",
    )(qk, v, o, ifc, ifr, conv_w, conv_b, w_mnorm)


def _swa_kernel(sink_ref, q_ref, kvp_ref, kvc_ref, bias_ref, out_ref):
    W = WINDOW
    first = pl.program_id(1) == 0
    kvp = kvp_ref[0]
    kvc = kvc_ref[0]
    keys = jnp.concatenate([kvp[:, :A_KV_W], kvc[:, :A_KV_W]], axis=0)
    vals = jnp.concatenate([kvp[:, A_KV_W:], kvc[:, A_KV_W:]], axis=0)
    lane = lax.broadcasted_iota(jnp.int32, (1, LANES), 1)
    lo_half = lane < A_HEAD_DIM
    hi_half = jnp.logical_not(lo_half)
    zero = jnp.zeros_like(vals)
    v_half = (jnp.where(lo_half, vals, zero), jnp.where(hi_half, vals, zero))

    q_pos = lax.broadcasted_iota(jnp.int32, (W, 2 * W), 0)
    k_pos = lax.broadcasted_iota(jnp.int32, (W, 2 * W), 1)
    dist = q_pos + W - k_pos
    in_window = jnp.where(dist >= 0, jnp.where(dist < W, 1, 0), 0)
    key_valid = jnp.where(k_pos >= W, 1, jnp.where(first, 0, 1))
    valid = (in_window * key_valid) > 0

    for j in range(A_HEADS // 2):
        qt = q_ref[0, :, LANES * j:LANES * (j + 1)]
        acc = jnp.zeros((W, LANES), F32)
        for p in range(2):
            h = A_HEAD_ORDER[2 * j + p]
            qm = jnp.where(lo_half if p == 0 else hi_half, qt, jnp.zeros_like(qt))
            s = _dot(qm, keys, _NT) * (A_HEAD_DIM ** -0.5) + bias_ref[h]
            s = jnp.where(valid, s, NEG_INF)
            sink = sink_ref[h]
            m = jnp.maximum(jnp.max(s, axis=-1, keepdims=True), sink)
            e = jnp.exp(s - m)
            denom = jnp.sum(e, axis=-1, keepdims=True) + jnp.exp(sink - m)
            acc = acc + _dot(e.astype(BF16), v_half[p]) / denom
        out_ref[0, :, LANES * j:LANES * (j + 1)] = acc.astype(BF16)


def _swa_call(sinks, qa, kva, bias):
    b, s, _ = qa.shape
    W = WINDOW
    return pl.pallas_call(
        _swa_kernel,
        grid=(b, s // W),
        in_specs=[pl.BlockSpec(memory_space=pltpu.SMEM),
                  pl.BlockSpec((1, W, A_Q_W), lambda i, j: (i, j, 0)),
                  pl.BlockSpec((1, W, 2 * A_KV_W), lambda i, j: (i, jnp.maximum(j - 1, 0), 0)),
                  pl.BlockSpec((1, W, 2 * A_KV_W), lambda i, j: (i, j, 0)),
                  pl.BlockSpec(bias.shape, lambda i, j: (0, 0, 0))],
        out_specs=pl.BlockSpec((1, W, A_Q_W), lambda i, j: (i, j, 0)),
        out_shape=jax.ShapeDtypeStruct((b, s, A_Q_W), BF16),
        compiler_params=_params("arbitrary", "arbitrary"),
        name="swa",
    )(sinks, qa, kva, kva, bias)


def _post_kernel(hm_ref, ha_ref, g_ref, x_ref, mod_ref, wbm_ref, wba_ref, wo_ref, wn2_ref,
                 wrh_ref, wrl_ref, rb_ref,
                 x1_ref, pay_ref, cls_ref, rank_ref, cnt_ref, carry):
    tm = x_ref.shape[1]

    @pl.when((pl.program_id(0) == 0) & (pl.program_id(1) == 0))
    def _():
        carry[...] = jnp.zeros(carry.shape, F32)

    g = g_ref[0]
    pm = _dot(hm_ref[0], wbm_ref[...])
    pa = _dot(ha_ref[0], wba_ref[...])
    merged = (_sigmoid(g[:, :D_MODEL].astype(F32)) * pm
              + _sigmoid(g[:, D_MODEL:].astype(F32)) * pa)
    mo = _dot(merged.astype(BF16), wo_ref[...])
    x1 = x_ref[0] + mod_ref[0, 2:3, :] * mo
    x1_ref[0] = x1
    h2 = _rms(x1) * wn2_ref[...] * (1.0 + mod_ref[0, 4:5, :]) + mod_ref[0, 3:4, :]
    pay_ref[0, :, :D_MODEL] = h2

    hh, hl = _split2(h2)
    wrh = wrh_ref[...]
    logits = _dot(wrh, hh, _NT) + _dot(wrh, hl, _NT) + _dot(wrl_ref[...], hh, _NT)
    scores = _sigmoid(logits)
    sel = scores + rb_ref[...]

    def row(a, r):
        return a[r:r + 1, :]

    grp = []
    for gi in range(N_GROUPS):
        v = [row(sel, gi * EXPERTS_PER_GROUP + k) for k in range(EXPERTS_PER_GROUP)]
        best = v[0] + v[1]
        for a, b in _PAIRS[1:]:
            best = jnp.maximum(best, v[a] + v[b])
        grp.append(best)
    gbest = grp[0]
    gsel = jnp.zeros_like(gbest)
    for gi in range(1, N_GROUPS):
        take = grp[gi] > gbest
        gbest = jnp.where(take, grp[gi], gbest)
        gsel = jnp.where(take, float(gi), gsel)

    sv, gv = [], []
    for k in range(EXPERTS_PER_GROUP):
        s_k = row(sel, k)
        g_k = row(scores, k)
        for gi in range(1, N_GROUPS):
            hit = gsel == float(gi)
            s_k = jnp.where(hit, row(sel, gi * EXPERTS_PER_GROUP + k), s_k)
            g_k = jnp.where(hit, row(scores, gi * EXPERTS_PER_GROUP + k), g_k)
        sv.append(s_k)
        gv.append(g_k)

    def argmax4(vals):
        bv, bi = vals[0], jnp.zeros_like(vals[0])
        for k in range(1, EXPERTS_PER_GROUP):
            take = vals[k] > bv
            bv = jnp.where(take, vals[k], bv)
            bi = jnp.where(take, float(k), bi)
        return bi

    i1 = argmax4(sv)
    i2 = argmax4([jnp.where(i1 == float(k), -jnp.inf, sv[k]) for k in range(EXPERTS_PER_GROUP)])
    w1 = jnp.zeros_like(i1)
    w2 = jnp.zeros_like(i1)
    for k in range(EXPERTS_PER_GROUP):
        w1 = jnp.where(i1 == float(k), gv[k], w1)
        w2 = jnp.where(i2 == float(k), gv[k], w2)
    wsum = w1 + w2
    w1 = w1 / wsum
    w2 = w2 / wsum
    first_low = i1 < i2
    e_lo = jnp.minimum(i1, i2)
    e_hi = jnp.maximum(i1, i2)
    gate_lo = jnp.where(first_low, w1, w2)
    gate_hi = jnp.where(first_low, w2, w1)
    pair_base = jnp.where(e_lo == 0.0, 0.0, jnp.where(e_lo == 1.0, 3.0, 5.0))
    cls_f = gsel * float(PAIRS_PER_GROUP) + pair_base + (e_hi - e_lo - 1.0)
    cls_ref[0] = cls_f.astype(jnp.int32)

    grow = lax.broadcasted_iota(jnp.int32, (LANES, tm), 0)
    gmat = jnp.where(grow == 0, gate_lo, jnp.where(grow == 1, gate_hi, 0.0))
    pay_ref[0, :, D_MODEL:] = gmat.T

    crow = lax.broadcasted_iota(jnp.int32, (CLASS_ROWS, tm), 0).astype(F32)
    onehot = crow == cls_f
    r_i = lax.broadcasted_iota(jnp.int32, (tm, tm), 0)
    c_i = lax.broadcasted_iota(jnp.int32, (tm, tm), 1)
    upper = jnp.where(r_i <= c_i, 1.0, 0.0).astype(BF16)
    cum = _dot(jnp.where(onehot, 1.0, 0.0).astype(BF16), upper)
    before = carry[:, 0:1]
    rank = jnp.sum(jnp.where(onehot, cum - 1.0 + before, 0.0), axis=0, keepdims=True)
    rank_ref[0] = rank.astype(jnp.int32)
    total = before + cum[:, tm - 1:tm]
    carry[...] = jnp.broadcast_to(total, carry.shape)
    cnt_ref[...] = jnp.broadcast_to(total, cnt_ref.shape)


def _post_call(hm, ha, g, x, mod, wbm, wba, wo, wn2, wrh, wrl, rb):
    b, s, d = x.shape
    tm = TOKEN_TILE
    tok = lambda w_: pl.BlockSpec((1, tm, w_), lambda i, j: (i, j, 0))
    full = lambda a: pl.BlockSpec(a.shape, lambda i, j: (0,) * a.ndim)
    lanes = pl.BlockSpec((1, 1, tm), lambda i, j: (i, 0, j))
    return pl.pallas_call(
        _post_kernel,
        grid=(b, s // tm),
        in_specs=[tok(M_V_W), tok(A_Q_W), tok(2 * d), tok(d),
                  pl.BlockSpec((1, 6, d), lambda i, j: (i, 0, 0)),
                  full(wbm), full(wba), full(wo), full(wn2), full(wrh), full(wrl), full(rb)],
        out_specs=[tok(d), tok(PAYLOAD_W), lanes, lanes,
                   pl.BlockSpec((CLASS_ROWS, LANES), lambda i, j: (0, 0))],
        out_shape=[jax.ShapeDtypeStruct((b, s, d), F32),
                   jax.ShapeDtypeStruct((b, s, PAYLOAD_W), F32),
                   jax.ShapeDtypeStruct((b, 1, s), jnp.int32),
                   jax.ShapeDtypeStruct((b, 1, s), jnp.int32),
                   jax.ShapeDtypeStruct((CLASS_ROWS, LANES), F32)],
        scratch_shapes=[pltpu.VMEM((CLASS_ROWS, LANES), F32)],
        compiler_params=_params("arbitrary", "arbitrary"),
        name="merge_outproj_router",
    )(hm, ha, g, x, mod, wbm, wba, wo, wn2, wrh, wrl, rb)


def _row_copy(src, src_row, dst, dst_row, sem):
    return pltpu.make_async_copy(src.at[pl.ds(src_row, 1), :], dst.at[pl.ds(dst_row, 1), :], sem)


def _dispatch_kernel(cls_s, rank_s, off_s, pay_ref, xs_in, xs_out, sem):
    del xs_in
    rows = pay_ref.shape[0]
    base = pl.program_id(0) * rows

    def issue(t, carry):
        pos = off_s[cls_s[base + t]] + rank_s[base + t]
        _row_copy(pay_ref, t, xs_out, pos, sem).start()
        return carry

    lax.fori_loop(0, rows, issue, 0, unroll=8)

    def drain(t, carry):
        _row_copy(pay_ref, 0, xs_out, 0, sem).wait()
        return carry

    lax.fori_loop(0, rows, drain, 0, unroll=8)


def _dispatch_call(cls, rank, off, payload, xs_zero):
    t = payload.shape[0]
    rows = ROW_TILE
    return pl.pallas_call(
        _dispatch_kernel,
        grid_spec=pltpu.PrefetchScalarGridSpec(
            num_scalar_prefetch=3,
            grid=(t // rows,),
            in_specs=[pl.BlockSpec((rows, PAYLOAD_W), lambda i, *_: (i, 0)),
                      pl.BlockSpec(memory_space=pl.ANY)],
            out_specs=pl.BlockSpec(memory_space=pl.ANY),
            scratch_shapes=[pltpu.SemaphoreType.DMA(())],
        ),
        out_shape=jax.ShapeDtypeStruct(xs_zero.shape, xs_zero.dtype),
        input_output_aliases={4: 0},
        compiler_params=_params("arbitrary"),
        name="moe_dispatch",
    )(cls, rank, off, payload, xs_zero)


def _moe_kernel(ea_s, eb_s, nu_s, x_ref, wga, wua, wda, wgb, wub, wdb, y_ref):
    del ea_s, eb_s

    @pl.when(pl.program_id(0) < nu_s[0])
    def _():
        x = x_ref[:, :D_MODEL].astype(BF16)
        gate_a = x_ref[:, D_MODEL:D_MODEL + 1]
        gate_b = x_ref[:, D_MODEL + 1:D_MODEL + 2]

        def expert(wg, wu, wd):
            hg = _dot(x, wg[0])
            hu = _dot(x, wu[0])
            act = (hg * _sigmoid(hg) * hu).astype(BF16)
            return _dot(act, wd[0])

        y_ref[...] = gate_a * expert(wga, wua, wda) + gate_b * expert(wgb, wub, wdb)

    @pl.when(pl.program_id(0) >= nu_s[0])
    def _():
        y_ref[...] = jnp.zeros(y_ref.shape, F32)


def _moe_call(tile_ea, tile_eb, n_used, xs, wg, wu, wd):
    npad = xs.shape[0]
    te = EXPERT_TILE
    last = lambda i, nu: jnp.minimum(i, nu[0] - 1)
    up = lambda tab: pl.BlockSpec((1, D_MODEL, D_EXPERT), lambda i, ea, eb, nu, tab=tab: ((ea, eb)[tab][i], 0, 0))
    down = lambda tab: pl.BlockSpec((1, D_EXPERT, D_MODEL), lambda i, ea, eb, nu, tab=tab: ((ea, eb)[tab][i], 0, 0))
    return pl.pallas_call(
        _moe_kernel,
        grid_spec=pltpu.PrefetchScalarGridSpec(
            num_scalar_prefetch=3,
            grid=(npad // te,),
            in_specs=[pl.BlockSpec((te, PAYLOAD_W), lambda i, ea, eb, nu: (last(i, nu), 0)),
                      up(0), up(0), down(0), up(1), up(1), down(1)],
            out_specs=pl.BlockSpec((te, D_MODEL), lambda i, ea, eb, nu: (i, 0)),
        ),
        out_shape=jax.ShapeDtypeStruct((npad, D_MODEL), F32),
        compiler_params=_params("arbitrary"),
        name="moe_experts",
    )(tile_ea, tile_eb, n_used, xs, wg, wu, wd, wg, wu, wd)


def _combine_kernel(cls_s, rank_s, off_s, ys_hbm, x1_ref, mod_ref, wf_ref, out_ref, ybuf, sem, *, final):
    rows = x1_ref.shape[1]
    base = (pl.program_id(0) * pl.num_programs(1) + pl.program_id(1)) * rows

    def issue(t, carry):
        pos = off_s[cls_s[base + t]] + rank_s[base + t]
        _row_copy(ys_hbm, pos, ybuf, t, sem).start()
        return carry

    lax.fori_loop(0, rows, issue, 0, unroll=8)

    def drain(t, carry):
        _row_copy(ys_hbm, 0, ybuf, 0, sem).wait()
        return carry

    lax.fori_loop(0, rows, drain, 0, unroll=8)

    x2 = x1_ref[0] + mod_ref[0, 5:6, :] * ybuf[...]
    if final:
        x2 = _rms(x2) * wf_ref[...]
    out_ref[0] = x2


def _combine_call(cls, rank, off, ys, x1, mod, wf, final):
    b, s, d = x1.shape
    rows = ROW_TILE
    return pl.pallas_call(
        functools.partial(_combine_kernel, final=final),
        grid_spec=pltpu.PrefetchScalarGridSpec(
            num_scalar_prefetch=3,
            grid=(b, s // rows),
            in_specs=[pl.BlockSpec(memory_space=pl.ANY),
                      pl.BlockSpec((1, rows, d), lambda i, j, *_: (i, j, 0)),
                      pl.BlockSpec((1, 6, d), lambda i, j, *_: (i, 0, 0)),
                      pl.BlockSpec((1, d), lambda i, j, *_: (0, 0))],
            out_specs=pl.BlockSpec((1, rows, d), lambda i, j, *_: (i, j, 0)),
            scratch_shapes=[pltpu.VMEM((rows, d), F32), pltpu.SemaphoreType.DMA(())],
        ),
        out_shape=jax.ShapeDtypeStruct((b, s, d), F32),
        compiler_params=_params("arbitrary", "arbitrary"),
        name="moe_combine",
    )(cls, rank, off, ys, x1, mod, wf)


def _t5_bucket(n):
    max_exact = N_BUCKETS // 2
    large = max_exact + (np.log(np.maximum(n, 1) / max_exact)
                         / np.log(MAX_DISTANCE / max_exact)
                         * (N_BUCKETS - max_exact)).astype(np.int32)
    large = np.minimum(large, N_BUCKETS - 1)
    return np.where(n < max_exact, n, large).astype(np.int32)


def _bias_table(rel_bias):
    dist = np.arange(WINDOW)[:, None] + WINDOW - np.arange(2 * WINDOW)[None, :]
    bucket = jnp.asarray(_t5_bucket(np.maximum(dist, 0)))
    return jnp.take(rel_bias.astype(F32), bucket, axis=0).transpose(2, 0, 1)


def _layout_w_in(w_in_l):
    pts = np.cumsum([0, M_QK_W, M_QK_W, M_V_W, M_V_W, M_HEADS, M_HEADS,
                     A_Q_W, A_KV_W, A_KV_W, D_MODEL, D_MODEL])
    col = lambda k: w_in_l[:, pts[k]:pts[k + 1]]
    qa = col(6).reshape(D_MODEL, A_HEADS, A_HEAD_DIM)[:, np.array(A_HEAD_ORDER), :].reshape(D_MODEL, A_Q_W)
    w_if = jnp.concatenate([col(4), col(5)], axis=1)
    w_if_pad = jnp.pad(w_if, ((0, 0), (0, LANES - 2 * M_HEADS)))
    w = jnp.concatenate([col(0), col(1), col(2), col(3), qa, col(7), col(8), col(9), col(10), w_if_pad],
                        axis=1)
    return w.astype(BF16), w_if.T.astype(BF16)


def _tile_tables(counts, n_tiles):
    te = EXPERT_TILE
    tiles = (counts + te - 1) // te
    ends = jnp.cumsum(tiles)
    off = (ends - tiles) * te
    n_used = ends[-1]
    tile_idx = jnp.minimum(jnp.arange(n_tiles, dtype=jnp.int32), n_used - 1)
    tile_cls = jnp.minimum(jnp.searchsorted(ends, tile_idx, side="right"), N_CLASSES - 1)
    off_pad = jnp.pad(off, (0, CLASS_ROWS - N_CLASSES)).astype(jnp.int32)
    return (off_pad, jnp.asarray(_CLASS_EA)[tile_cls], jnp.asarray(_CLASS_EB)[tile_cls],
            n_used.reshape(1).astype(jnp.int32))


def kernel(x, c, w_ada, b_ada, w_norm1, w_in, conv_w, conv_b, b_igate, b_fgate, w_mnorm, sinks, rel_bias,
           w_br_m, w_br_a, w_out, w_norm2, w_router, router_bias, w_gate_e, w_up_e, w_down_e, w_final):
    b, s, d = x.shape
    depth = w_ada.shape[0]
    t = b * s
    n_tiles = t // EXPERT_TILE + N_CLASSES
    npad = n_tiles * EXPERT_TILE

    mod_all = _ada_call(jnp.pad(c, ((0, 8 - b), (0, 0))), w_ada, b_ada)[:, :b]
    bias_tab = _bias_table(rel_bias)
    wrt = w_router.T
    wrh = wrt.astype(BF16)
    wrl = (wrt - wrh.astype(F32)).astype(BF16)
    rb = router_bias.reshape(N_EXPERTS, 1).astype(F32)
    row = lambda v: v.reshape(1, -1).astype(F32)

    for l in range(depth):
        mod = mod_all[l].reshape(b, 6, d)
        w_l, wift = _layout_w_in(w_in[l])
        gate_bias = jnp.concatenate([b_igate[l], b_fgate[l]]).astype(F32)
        bcol = jnp.pad(gate_bias, (0, LANES - 2 * M_HEADS)).reshape(1, LANES)
        brow = gate_bias.reshape(2 * M_HEADS, 1)

        qk, v, o, qa, kva, g, ifc, ifr = _inproj_call(x, mod, row(w_norm1[l]), w_l, wift, bcol, brow)
        hm = _mlstm_call(qk, v, o, ifc, ifr, conv_w[l].astype(F32), row(conv_b[l]), row(w_mnorm[l]))
        ha = _swa_call(sinks[l].astype(F32), qa, kva, bias_tab)

        wba = w_br_a[l].reshape(A_HEADS, A_HEAD_DIM, d)[np.array(A_HEAD_ORDER)].reshape(A_Q_W, d)
        x1, payload, cls, rank, cnt = _post_call(
            hm, ha, g, x, mod, w_br_m[l].astype(BF16), wba.astype(BF16), w_out[l].astype(BF16),
            row(w_norm2[l]), wrh, wrl, rb)

        counts = cnt[:N_CLASSES, 0].astype(jnp.int32)
        off, tile_ea, tile_eb, n_used = _tile_tables(counts, n_tiles)
        cls = cls.reshape(t)
        rank = rank.reshape(t)
        xs = _dispatch_call(cls, rank, off, payload.reshape(t, PAYLOAD_W),
                            jnp.zeros((npad, PAYLOAD_W), F32))
        ys = _moe_call(tile_ea, tile_eb, n_used, xs, w_gate_e[l].astype(BF16), w_up_e[l].astype(BF16),
                       w_down_e[l].astype(BF16))
        x = _combine_call(cls, rank, off, ys, x1, mod, row(w_final), final=(l == depth - 1))
    return x
```

```python
import numpy as np
import jax
import jax.numpy as jnp
from jax import lax
from jax.experimental import pallas as pl
from jax.experimental.pallas import tpu as pltpu

F32 = jnp.float32
BF16 = jnp.bfloat16

D_MODEL = 1024
M_HEADS = 4
M_DQK = 64
M_DV = 128
M_CONV = 4
M_QK_W = M_HEADS * M_DQK
M_V_W = M_HEADS * M_DV
A_HEADS = 8
A_KV_HEADS = 2
A_HEAD_DIM = 64
A_Q_W = A_HEADS * A_HEAD_DIM
A_KV_W = A_KV_HEADS * A_HEAD_DIM
WINDOW = 128
N_BUCKETS = 32
MAX_DISTANCE = 128
N_EXPERTS = 16
N_GROUPS = 4
EXPERTS_PER_GROUP = N_EXPERTS // N_GROUPS
D_EXPERT = 512
EPS = 1e-6
NEG_INF = -1e30

LANES = 128
MLSTM_CHUNK = 128
TOKEN_TILE = 256
EXPERT_TILE = 256
ROW_TILE = 512
PAIRS_PER_GROUP = 6
N_CLASSES = N_GROUPS * PAIRS_PER_GROUP
CLASS_ROWS = 32
PAYLOAD_W = D_MODEL + LANES
VMEM_LIMIT = 48 * 1024 * 1024

C_QK = 0
C_V = C_QK + 2 * M_QK_W
C_O = C_V + M_V_W
C_QA = C_O + M_V_W
C_KVA = C_QA + A_Q_W
C_G = C_KVA + 2 * A_KV_W
C_IF = C_G + 2 * D_MODEL
C_END = C_IF + LANES

A_HEAD_ORDER = (0, 4, 1, 5, 2, 6, 3, 7)

_PAIRS = [(a, b) for a in range(EXPERTS_PER_GROUP) for b in range(a + 1, EXPERTS_PER_GROUP)]
_CLASS_EA = np.array([g * EXPERTS_PER_GROUP + a for g in range(N_GROUPS) for a, _ in _PAIRS], np.int32)
_CLASS_EB = np.array([g * EXPERTS_PER_GROUP + b for g in range(N_GROUPS) for _, b in _PAIRS], np.int32)

_NT = (((1,), (1,)), ((), ()))
_TN = (((0,), (0,)), ((), ()))


def _sigmoid(x):
    return 1.0 / (1.0 + jnp.exp(-x))


def _log_sigmoid(x):
    return -(jnp.maximum(-x, 0.0) + jnp.log1p(jnp.exp(-jnp.abs(x))))


def _dot(a, b, dims=None):
    if dims is None:
        return jnp.dot(a, b, preferred_element_type=F32)
    return lax.dot_general(a, b, dims, preferred_element_type=F32)


def _split2(a):
    hi = a.astype(BF16)
    lo = (a - hi.astype(F32)).astype(BF16)
    return hi, lo


def _split3(a):
    hi = a.astype(BF16)
    r = a - hi.astype(F32)
    mid = r.astype(BF16)
    lo = (r - mid.astype(F32)).astype(BF16)
    return hi, mid, lo


def _dot_hi(a, b, dims=None):
    ah, al = _split2(a)
    bh, bl = _split2(b)
    return _dot(ah, bh, dims) + _dot(ah, bl, dims) + _dot(al, bh, dims)


def _rms(x):
    return x * lax.rsqrt(jnp.mean(x * x, axis=-1, keepdims=True) + EPS)


def _params(*sem):
    return pltpu.CompilerParams(dimension_semantics=sem, vmem_limit_bytes=VMEM_LIMIT)


def _ada_kernel(c_ref, w_ref, b_ref, o_ref):
    c = c_ref[...]
    cond = c * _sigmoid(c)
    o_ref[0] = _dot_hi(cond, w_ref[0]) + b_ref[0]


def _ada_call(c8, w_ada, b_ada):
    depth = w_ada.shape[0]
    return pl.pallas_call(
        _ada_kernel,
        grid=(depth, 6),
        in_specs=[
            pl.BlockSpec((8, D_MODEL), lambda l, j: (0, 0)),
            pl.BlockSpec((1, D_MODEL, D_MODEL), lambda l, j: (l, 0, j)),
            pl.BlockSpec((1, 1, D_MODEL), lambda l, j: (l, 0, j)),
        ],
        out_specs=pl.BlockSpec((1, 8, D_MODEL), lambda l, j: (l, 0, j)),
        out_shape=jax.ShapeDtypeStruct((depth, 8, 6 * D_MODEL), F32),
        compiler_params=_params("arbitrary", "arbitrary"),
        name="adaln_mod",
    )(c8, w_ada, b_ada.reshape(depth, 1, 6 * D_MODEL))


def _project(x, mod_ref, wn_ref, w_ref, wift_ref, bcol_ref, brow_ref, outs):
    qk_ref, v_ref, o_ref, qa_ref, kva_ref, g_ref, ifc_ref, ifr_ref = outs
    h = _rms(x) * wn_ref[...] * (1.0 + mod_ref[0, 1:2, :]) + mod_ref[0, 0:1, :]
    hb = h.astype(BF16)

    def seg(a, b):
        return _dot(hb, w_ref[:, a:b])

    qk_ref[0] = seg(C_QK, C_V).astype(BF16)
    v_ref[0] = seg(C_V, C_O).astype(BF16)
    o_ref[0] = seg(C_O, C_QA).astype(BF16)
    qa_ref[0] = seg(C_QA, C_KVA).astype(BF16)
    kva_ref[0] = seg(C_KVA, C_G).astype(BF16)
    g_ref[0] = seg(C_G, C_IF).astype(BF16)
    zc = seg(C_IF, C_END) + bcol_ref[...]
    lane = lax.broadcasted_iota(jnp.int32, zc.shape, 1)
    ifc_ref[0] = jnp.where((lane >= M_HEADS) & (lane < 2 * M_HEADS), _log_sigmoid(zc), zc)
    zr = _dot(wift_ref[...], hb, _NT) + brow_ref[...]
    row = lax.broadcasted_iota(jnp.int32, zr.shape, 0)
    ifr_ref[0] = jnp.where(row >= M_HEADS, _log_sigmoid(zr), zr)


def _inproj_kernel(x_ref, mod_ref, wn_ref, w_ref, wift_ref, bcol_ref, brow_ref, *outs):
    _project(x_ref[0], mod_ref, wn_ref, w_ref, wift_ref, bcol_ref, brow_ref, outs)


def _row_copy(src, src_row, dst, dst_row, sem):
    return pltpu.make_async_copy(src.at[pl.ds(src_row, 1), :], dst.at[pl.ds(dst_row, 1), :], sem)


def _gather_rows(cls_s, rank_s, off_s, ys_hbm, first_token, dst, sem):
    def issue(t, carry):
        tok = first_token + t
        pos = off_s[cls_s[tok]] + rank_s[tok]
        _row_copy(ys_hbm, pos, dst, t, sem).start()
        return carry

    lax.fori_loop(0, dst.shape[0], issue, 0, unroll=8)


def _wait_rows(ys_hbm, dst, sem):
    def drain(t, carry):
        _row_copy(ys_hbm, 0, dst, 0, sem).wait()
        return carry

    lax.fori_loop(0, dst.shape[0], drain, 0, unroll=8)


def _moe_residual(cls_s, rank_s, off_s, ys_hbm, x1_ref, mod_ref, ybuf, sem):
    rows = x1_ref.shape[1]
    step = pl.program_id(0) * pl.num_programs(1) + pl.program_id(1)
    n_steps = pl.num_programs(0) * pl.num_programs(1)
    slot = lax.rem(step, 2)

    @pl.when(step == 0)
    def _():
        _gather_rows(cls_s, rank_s, off_s, ys_hbm, 0, ybuf.at[0], sem.at[0])

    @pl.when(step + 1 < n_steps)
    def _():
        _gather_rows(cls_s, rank_s, off_s, ys_hbm, (step + 1) * rows, ybuf.at[1 - slot], sem.at[1 - slot])

    _wait_rows(ys_hbm, ybuf.at[slot], sem.at[slot])
    return x1_ref[0] + mod_ref[0, 5:6, :] * ybuf[slot]


def _combine_inproj_kernel(cls_s, rank_s, off_s, ys_hbm, x1_ref, modp_ref, mod_ref, wn_ref, w_ref,
                           wift_ref, bcol_ref, brow_ref, xo_ref, *rest):
    outs, (ybuf, sem) = rest[:8], rest[8:]
    x = _moe_residual(cls_s, rank_s, off_s, ys_hbm, x1_ref, modp_ref, ybuf, sem)
    xo_ref[0] = x
    _project(x, mod_ref, wn_ref, w_ref, wift_ref, bcol_ref, brow_ref, outs)


def _inproj_specs(b, s, tm):
    tok = lambda w_: pl.BlockSpec((1, tm, w_), lambda i, j, *_: (i, j, 0))
    sds = lambda w_, dt: jax.ShapeDtypeStruct((b, s, w_), dt)
    out_specs = [tok(2 * M_QK_W), tok(M_V_W), tok(M_V_W), tok(A_Q_W), tok(2 * A_KV_W),
                 tok(2 * D_MODEL), tok(LANES), pl.BlockSpec((1, 8, tm), lambda i, j, *_: (i, 0, j))]
    out_shape = [sds(2 * M_QK_W, BF16), sds(M_V_W, BF16), sds(M_V_W, BF16), sds(A_Q_W, BF16),
                 sds(2 * A_KV_W, BF16), sds(2 * D_MODEL, BF16), sds(LANES, F32),
                 jax.ShapeDtypeStruct((b, 8, s), F32)]
    return tok, out_specs, out_shape


def _inproj_call(x, mod, wn, w, wift, bcol, brow):
    b, s, d = x.shape
    tm = TOKEN_TILE
    tok, out_specs, out_shape = _inproj_specs(b, s, tm)
    full = lambda a: pl.BlockSpec(a.shape, lambda i, j: (0,) * a.ndim)
    return pl.pallas_call(
        _inproj_kernel,
        grid=(b, s // tm),
        in_specs=[tok(d), pl.BlockSpec((1, 6, d), lambda i, j: (i, 0, 0)), full(wn), full(w),
                  full(wift), full(bcol), full(brow)],
        out_specs=out_specs,
        out_shape=out_shape,
        compiler_params=_params("arbitrary", "arbitrary"),
        name="norm1_inproj",
    )(x, mod, wn, w, wift, bcol, brow)


def _combine_inproj_call(cls, rank, off, ys, x1, mod_prev, mod, wn, w, wift, bcol, brow):
    b, s, d = x1.shape
    tm = TOKEN_TILE
    tok, out_specs, out_shape = _inproj_specs(b, s, tm)
    full = lambda a: pl.BlockSpec(a.shape, lambda i, j, *_: (0,) * a.ndim)
    modspec = pl.BlockSpec((1, 6, d), lambda i, j, *_: (i, 0, 0))
    return pl.pallas_call(
        _combine_inproj_kernel,
        grid_spec=pltpu.PrefetchScalarGridSpec(
            num_scalar_prefetch=3,
            grid=(b, s // tm),
            in_specs=[pl.BlockSpec(memory_space=pl.ANY), tok(d), modspec, modspec, full(wn), full(w),
                      full(wift), full(bcol), full(brow)],
            out_specs=[tok(d)] + out_specs,
            scratch_shapes=[pltpu.VMEM((2, tm, d), F32), pltpu.SemaphoreType.DMA((2,))],
        ),
        out_shape=[jax.ShapeDtypeStruct((b, s, d), F32)] + out_shape,
        compiler_params=_params("arbitrary", "arbitrary"),
        name="combine_norm1_inproj",
    )(cls, rank, off, ys, x1, mod_prev, mod, wn, w, wift, bcol, brow)


def _mlstm_kernel(qk_ref, v_ref, o_ref, ifc_ref, ifr_ref, cw_ref, cb_ref, wn_ref,
                  out_ref, xbuf, cstate, mstate):
    L = qk_ref.shape[1]
    pad = 8

    @pl.when(pl.program_id(1) == 0)
    def _():
        xbuf[0:pad, :] = jnp.zeros((pad, xbuf.shape[1]), F32)
        cstate[...] = jnp.zeros(cstate.shape, F32)
        mstate[...] = jnp.zeros(mstate.shape, F32)

    x = qk_ref[0].astype(F32)
    xbuf[pad:pad + L, :] = x
    acc = cb_ref[...] + cw_ref[0:1, :] * xbuf[pad - 3:pad - 3 + L, :]
    for j in range(1, M_CONV):
        acc = acc + cw_ref[j:j + 1, :] * xbuf[pad - 3 + j:pad - 3 + j + L, :]
    xbuf[0:pad, :] = x[L - pad:L, :]
    qk = acc * _sigmoid(acc)
    q = qk[:, :M_QK_W] * (M_DQK ** -0.5)
    kb = qk[:, M_QK_W:].astype(BF16)
    vb = v_ref[0]

    r_i = lax.broadcasted_iota(jnp.int32, (L, L), 0)
    c_i = lax.broadcasted_iota(jnp.int32, (L, L), 1)
    causal = c_i <= r_i
    tril = jnp.where(causal, 1.0, 0.0).astype(BF16)
    triu = jnp.where(r_i <= c_i, 1.0, 0.0).astype(BF16)

    icol = ifc_ref[0]
    irow = ifr_ref[0]
    ch, cm, cl = _split3(icol)
    bcum_col = _dot(tril, ch) + _dot(tril, cm) + _dot(tril, cl)
    rh, rm, rl = _split3(irow)
    bcum_row = _dot(rh, triu) + _dot(rm, triu) + _dot(rl, triu)

    lane = lax.broadcasted_iota(jnp.int32, (1, LANES), 1)
    lo_half = lane < M_DQK
    ones = jnp.ones((L, LANES), BF16)

    for h in range(M_HEADS):
        t, p = h // 2, h % 2
        qt = q[:, LANES * t:LANES * (t + 1)]
        qm = jnp.where(lo_half if p == 0 else jnp.logical_not(lo_half), qt, 0.0).astype(BF16)
        kt = kb[:, LANES * t:LANES * (t + 1)]
        s = _dot(qm, kt, _NT)

        bc_col = bcum_col[:, M_HEADS + h:M_HEADS + h + 1]
        i_col = icol[:, h:h + 1]
        bc_row = bcum_row[M_HEADS + h:M_HEADS + h + 1, :]
        i_row = irow[h:h + 1, :]
        b_last = bc_row[:, L - 1:L]
        m_prev = mstate[h][:, 0:1]

        dlog = jnp.where(causal, bc_col - bc_row + i_row, NEG_INF)
        m_inter = bc_col + m_prev
        m_row = jnp.maximum(m_inter, jnp.max(dlog, axis=-1, keepdims=True))
        sc = (s * jnp.exp(dlog - m_row)).astype(BF16)
        inter = jnp.exp(m_inter - m_row)

        vaug = jnp.concatenate([vb[:, M_DV * h:M_DV * (h + 1)], ones], axis=1)
        cst = cstate[h]
        num = _dot(sc, vaug) + inter * _dot(qm, cst.astype(BF16))
        den = num[:, M_DV:]
        hh = num[:, :M_DV] / jnp.maximum(jnp.abs(den), jnp.exp(-m_row))

        ws_row = b_last - bc_row + i_row
        m_loc = jnp.max(ws_row, axis=-1, keepdims=True)
        a_col = jnp.exp(b_last - bc_col + i_col - m_loc)
        av = (a_col * vaug.astype(F32)).astype(BF16)
        c_loc = _dot(kt, av, _TN)
        m_new = jnp.maximum(b_last + m_prev, m_loc)
        decay = jnp.exp(b_last + m_prev - m_new)
        fresh = jnp.exp(m_loc - m_new)
        cstate[h] = decay * cst + fresh * c_loc
        mstate[h] = jnp.broadcast_to(m_new, (1, LANES))

        hn = _rms(hh) * wn_ref[:, M_DV * h:M_DV * (h + 1)]
        og = _sigmoid(o_ref[0, :, M_DV * h:M_DV * (h + 1)].astype(F32))
        out_ref[0, :, M_DV * h:M_DV * (h + 1)] = (hn * og).astype(BF16)


def _mlstm_call(qk, v, o, ifc, ifr, conv_w, conv_b, w_mnorm):
    b, s, _ = qk.shape
    L = MLSTM_CHUNK
    tok = lambda w_: pl.BlockSpec((1, L, w_), lambda i, j: (i, j, 0))
    full = lambda a: pl.BlockSpec(a.shape, lambda i, j: (0,) * a.ndim)
    return pl.pallas_call(
        _mlstm_kernel,
        grid=(b, s // L),
        in_specs=[tok(2 * M_QK_W), tok(M_V_W), tok(M_V_W), tok(LANES),
                  pl.BlockSpec((1, 8, L), lambda i, j: (i, 0, j)),
                  full(conv_w), full(conv_b), full(w_mnorm)],
        out_specs=tok(M_V_W),
        out_shape=jax.ShapeDtypeStruct((b, s, M_V_W), BF16),
        scratch_shapes=[pltpu.VMEM((L + 8, 2 * M_QK_W), F32),
                        pltpu.VMEM((M_HEADS, LANES, 2 * M_DV), F32),
                        pltpu.VMEM((M_HEADS, 1, LANES), F32)],
        compiler_params=_params("arbitrary", "arbitrary"),
        name="mlstm",
    )(qk, v, o, ifc, ifr, conv_w, conv_b, w_mnorm)


def _swa_kernel(sink_ref, q_ref, kvp_ref, kvc_ref, bias_ref, out_ref):
    W = WINDOW
    first = pl.program_id(1) == 0
    kvp = kvp_ref[0]
    kvc = kvc_ref[0]
    keys = jnp.concatenate([kvp[:, :A_KV_W], kvc[:, :A_KV_W]], axis=0)
    vals = jnp.concatenate([kvp[:, A_KV_W:], kvc[:, A_KV_W:]], axis=0)
    lane = lax.broadcasted_iota(jnp.int32, (1, LANES), 1)
    lo_half = lane < A_HEAD_DIM
    hi_half = jnp.logical_not(lo_half)
    zero = jnp.zeros_like(vals)
    v_half = (jnp.where(lo_half, vals, zero), jnp.where(hi_half, vals, zero))

    q_pos = lax.broadcasted_iota(jnp.int32, (W, 2 * W), 0)
    k_pos = lax.broadcasted_iota(jnp.int32, (W, 2 * W), 1)
    dist = q_pos + W - k_pos
    in_window = jnp.where(dist >= 0, jnp.where(dist < W, 1, 0), 0)
    key_valid = jnp.where(k_pos >= W, 1, jnp.where(first, 0, 1))
    valid = (in_window * key_valid) > 0

    for j in range(A_HEADS // 2):
        qt = q_ref[0, :, LANES * j:LANES * (j + 1)]
        acc = jnp.zeros((W, LANES), F32)
        for p in range(2):
            h = A_HEAD_ORDER[2 * j + p]
            qm = jnp.where(lo_half if p == 0 else hi_half, qt, jnp.zeros_like(qt))
            s = _dot(qm, keys, _NT) * (A_HEAD_DIM ** -0.5) + bias_ref[h]
            s = jnp.where(valid, s, NEG_INF)
            sink = sink_ref[h]
            m = jnp.maximum(jnp.max(s, axis=-1, keepdims=True), sink)
            e = jnp.exp(s - m)
            denom = jnp.sum(e, axis=-1, keepdims=True) + jnp.exp(sink - m)
            acc = acc + _dot(e.astype(BF16), v_half[p]) / denom
        out_ref[0, :, LANES * j:LANES * (j + 1)] = acc.astype(BF16)


def _swa_call(sinks, qa, kva, bias):
    b, s, _ = qa.shape
    W = WINDOW
    return pl.pallas_call(
        _swa_kernel,
        grid=(b, s // W),
        in_specs=[pl.BlockSpec(memory_space=pltpu.SMEM),
                  pl.BlockSpec((1, W, A_Q_W), lambda i, j: (i, j, 0)),
                  pl.BlockSpec((1, W, 2 * A_KV_W), lambda i, j: (i, jnp.maximum(j - 1, 0), 0)),
                  pl.BlockSpec((1, W, 2 * A_KV_W), lambda i, j: (i, j, 0)),
                  pl.BlockSpec(bias.shape, lambda i, j: (0, 0, 0))],
        out_specs=pl.BlockSpec((1, W, A_Q_W), lambda i, j: (i, j, 0)),
        out_shape=jax.ShapeDtypeStruct((b, s, A_Q_W), BF16),
        compiler_params=_params("arbitrary", "arbitrary"),
        name="swa",
    )(sinks, qa, kva, kva, bias)


def _post_kernel(hm_ref, ha_ref, g_ref, x_ref, mod_ref, wbm_ref, wba_ref, wo_ref, wn2_ref,
                 wrh_ref, wrl_ref, rb_ref,
                 x1_ref, pay_ref, cls_ref, rank_ref, cnt_ref, carry):
    tm = x_ref.shape[1]

    @pl.when((pl.program_id(0) == 0) & (pl.program_id(1) == 0))
    def _():
        carry[...] = jnp.zeros(carry.shape, F32)

    g = g_ref[0]
    pm = _dot(hm_ref[0], wbm_ref[...])
    pa = _dot(ha_ref[0], wba_ref[...])
    merged = (_sigmoid(g[:, :D_MODEL].astype(F32)) * pm
              + _sigmoid(g[:, D_MODEL:].astype(F32)) * pa)
    mo = _dot(merged.astype(BF16), wo_ref[...])
    x1 = x_ref[0] + mod_ref[0, 2:3, :] * mo
    x1_ref[0] = x1
    h2 = _rms(x1) * wn2_ref[...] * (1.0 + mod_ref[0, 4:5, :]) + mod_ref[0, 3:4, :]
    pay_ref[0, :, :D_MODEL] = h2

    hh, hl = _split2(h2)
    wrh = wrh_ref[...]
    logits = _dot(wrh, hh, _NT) + _dot(wrh, hl, _NT) + _dot(wrl_ref[...], hh, _NT)
    scores = _sigmoid(logits)
    sel = scores + rb_ref[...]

    def row(a, r):
        return a[r:r + 1, :]

    grp = []
    for gi in range(N_GROUPS):
        v = [row(sel, gi * EXPERTS_PER_GROUP + k) for k in range(EXPERTS_PER_GROUP)]
        best = v[0] + v[1]
        for a, b in _PAIRS[1:]:
            best = jnp.maximum(best, v[a] + v[b])
        grp.append(best)
    gbest = grp[0]
    gsel = jnp.zeros_like(gbest)
    for gi in range(1, N_GROUPS):
        take = grp[gi] > gbest
        gbest = jnp.where(take, grp[gi], gbest)
        gsel = jnp.where(take, float(gi), gsel)

    sv, gv = [], []
    for k in range(EXPERTS_PER_GROUP):
        s_k = row(sel, k)
        g_k = row(scores, k)
        for gi in range(1, N_GROUPS):
            hit = gsel == float(gi)
            s_k = jnp.where(hit, row(sel, gi * EXPERTS_PER_GROUP + k), s_k)
            g_k = jnp.where(hit, row(scores, gi * EXPERTS_PER_GROUP + k), g_k)
        sv.append(s_k)
        gv.append(g_k)

    def argmax4(vals):
        bv, bi = vals[0], jnp.zeros_like(vals[0])
        for k in range(1, EXPERTS_PER_GROUP):
            take = vals[k] > bv
            bv = jnp.where(take, vals[k], bv)
            bi = jnp.where(take, float(k), bi)
        return bi

    i1 = argmax4(sv)
    i2 = argmax4([jnp.where(i1 == float(k), -jnp.inf, sv[k]) for k in range(EXPERTS_PER_GROUP)])
    w1 = jnp.zeros_like(i1)
    w2 = jnp.zeros_like(i1)
    for k in range(EXPERTS_PER_GROUP):
        w1 = jnp.where(i1 == float(k), gv[k], w1)
        w2 = jnp.where(i2 == float(k), gv[k], w2)
    wsum = w1 + w2
    w1 = w1 / wsum
    w2 = w2 / wsum
    first_low = i1 < i2
    e_lo = jnp.minimum(i1, i2)
    e_hi = jnp.maximum(i1, i2)
    gate_lo = jnp.where(first_low, w1, w2)
    gate_hi = jnp.where(first_low, w2, w1)
    pair_base = jnp.where(e_lo == 0.0, 0.0, jnp.where(e_lo == 1.0, 3.0, 5.0))
    cls_f = gsel * float(PAIRS_PER_GROUP) + pair_base + (e_hi - e_lo - 1.0)
    cls_ref[0] = cls_f.astype(jnp.int32)

    grow = lax.broadcasted_iota(jnp.int32, (LANES, tm), 0)
    gmat = jnp.where(grow == 0, gate_lo, jnp.where(grow == 1, gate_hi, 0.0))
    pay_ref[0, :, D_MODEL:] = gmat.T

    crow = lax.broadcasted_iota(jnp.int32, (CLASS_ROWS, tm), 0).astype(F32)
    onehot = crow == cls_f
    r_i = lax.broadcasted_iota(jnp.int32, (tm, tm), 0)
    c_i = lax.broadcasted_iota(jnp.int32, (tm, tm), 1)
    upper = jnp.where(r_i <= c_i, 1.0, 0.0).astype(BF16)
    cum = _dot(jnp.where(onehot, 1.0, 0.0).astype(BF16), upper)
    before = carry[:, 0:1]
    rank = jnp.sum(jnp.where(onehot, cum - 1.0 + before, 0.0), axis=0, keepdims=True)
    rank_ref[0] = rank.astype(jnp.int32)
    total = before + cum[:, tm - 1:tm]
    carry[...] = jnp.broadcast_to(total, carry.shape)
    cnt_ref[...] = jnp.broadcast_to(total, cnt_ref.shape)


def _post_call(hm, ha, g, x, mod, wbm, wba, wo, wn2, wrh, wrl, rb):
    b, s, d = x.shape
    tm = TOKEN_TILE
    tok = lambda w_: pl.BlockSpec((1, tm, w_), lambda i, j: (i, j, 0))
    full = lambda a: pl.BlockSpec(a.shape, lambda i, j: (0,) * a.ndim)
    lanes = pl.BlockSpec((1, 1, tm), lambda i, j: (i, 0, j))
    return pl.pallas_call(
        _post_kernel,
        grid=(b, s // tm),
        in_specs=[tok(M_V_W), tok(A_Q_W), tok(2 * d), tok(d),
                  pl.BlockSpec((1, 6, d), lambda i, j: (i, 0, 0)),
                  full(wbm), full(wba), full(wo), full(wn2), full(wrh), full(wrl), full(rb)],
        out_specs=[tok(d), tok(PAYLOAD_W), lanes, lanes,
                   pl.BlockSpec((CLASS_ROWS, LANES), lambda i, j: (0, 0))],
        out_shape=[jax.ShapeDtypeStruct((b, s, d), F32),
                   jax.ShapeDtypeStruct((b, s, PAYLOAD_W), F32),
                   jax.ShapeDtypeStruct((b, 1, s), jnp.int32),
                   jax.ShapeDtypeStruct((b, 1, s), jnp.int32),
                   jax.ShapeDtypeStruct((CLASS_ROWS, LANES), F32)],
        scratch_shapes=[pltpu.VMEM((CLASS_ROWS, LANES), F32)],
        compiler_params=_params("arbitrary", "arbitrary"),
        name="merge_outproj_router",
    )(hm, ha, g, x, mod, wbm, wba, wo, wn2, wrh, wrl, rb)


def _dispatch_kernel(cls_s, rank_s, off_s, pay_ref, xs_in, xs_out, sem):
    del xs_in
    rows = pay_ref.shape[0]
    base = pl.program_id(0) * rows

    def issue(t, carry):
        pos = off_s[cls_s[base + t]] + rank_s[base + t]
        _row_copy(pay_ref, t, xs_out, pos, sem).start()
        return carry

    lax.fori_loop(0, rows, issue, 0, unroll=8)

    def drain(t, carry):
        _row_copy(pay_ref, 0, xs_out, 0, sem).wait()
        return carry

    lax.fori_loop(0, rows, drain, 0, unroll=8)


def _dispatch_call(cls, rank, off, payload, xs_zero):
    t = payload.shape[0]
    rows = ROW_TILE
    return pl.pallas_call(
        _dispatch_kernel,
        grid_spec=pltpu.PrefetchScalarGridSpec(
            num_scalar_prefetch=3,
            grid=(t // rows,),
            in_specs=[pl.BlockSpec((rows, PAYLOAD_W), lambda i, *_: (i, 0)),
                      pl.BlockSpec(memory_space=pl.ANY)],
            out_specs=pl.BlockSpec(memory_space=pl.ANY),
            scratch_shapes=[pltpu.SemaphoreType.DMA(())],
        ),
        out_shape=jax.ShapeDtypeStruct(xs_zero.shape, xs_zero.dtype),
        input_output_aliases={4: 0},
        compiler_params=_params("arbitrary"),
        name="moe_dispatch",
    )(cls, rank, off, payload, xs_zero)


def _moe_kernel(ea_s, eb_s, nu_s, x_ref, wga, wua, wda, wgb, wub, wdb, y_ref):
    del ea_s, eb_s

    @pl.when(pl.program_id(0) < nu_s[0])
    def _():
        x = x_ref[:, :D_MODEL].astype(BF16)
        gate_a = x_ref[:, D_MODEL:D_MODEL + 1]
        gate_b = x_ref[:, D_MODEL + 1:D_MODEL + 2]

        def expert(wg, wu, wd):
            hg = _dot(x, wg[0])
            hu = _dot(x, wu[0])
            act = (hg * _sigmoid(hg) * hu).astype(BF16)
            return _dot(act, wd[0])

        y_ref[...] = gate_a * expert(wga, wua, wda) + gate_b * expert(wgb, wub, wdb)

    @pl.when(pl.program_id(0) >= nu_s[0])
    def _():
        y_ref[...] = jnp.zeros(y_ref.shape, F32)


def _moe_call(tile_ea, tile_eb, n_used, xs, wg, wu, wd):
    npad = xs.shape[0]
    te = EXPERT_TILE
    last = lambda i, nu: jnp.minimum(i, nu[0] - 1)
    up = lambda tab: pl.BlockSpec((1, D_MODEL, D_EXPERT), lambda i, ea, eb, nu, tab=tab: ((ea, eb)[tab][i], 0, 0))
    down = lambda tab: pl.BlockSpec((1, D_EXPERT, D_MODEL), lambda i, ea, eb, nu, tab=tab: ((ea, eb)[tab][i], 0, 0))
    return pl.pallas_call(
        _moe_kernel,
        grid_spec=pltpu.PrefetchScalarGridSpec(
            num_scalar_prefetch=3,
            grid=(npad // te,),
            in_specs=[pl.BlockSpec((te, PAYLOAD_W), lambda i, ea, eb, nu: (last(i, nu), 0)),
                      up(0), up(0), down(0), up(1), up(1), down(1)],
            out_specs=pl.BlockSpec((te, D_MODEL), lambda i, ea, eb, nu: (i, 0)),
        ),
        out_shape=jax.ShapeDtypeStruct((npad, D_MODEL), F32),
        compiler_params=_params("arbitrary"),
        name="moe_experts",
    )(tile_ea, tile_eb, n_used, xs, wg, wu, wd, wg, wu, wd)


def _combine_kernel(cls_s, rank_s, off_s, ys_hbm, x1_ref, mod_ref, wf_ref, out_ref, ybuf, sem):
    x2 = _moe_residual(cls_s, rank_s, off_s, ys_hbm, x1_ref, mod_ref, ybuf, sem)
    out_ref[0] = _rms(x2) * wf_ref[...]


def _combine_call(cls, rank, off, ys, x1, mod, wf):
    b, s, d = x1.shape
    rows = ROW_TILE
    return pl.pallas_call(
        _combine_kernel,
        grid_spec=pltpu.PrefetchScalarGridSpec(
            num_scalar_prefetch=3,
            grid=(b, s // rows),
            in_specs=[pl.BlockSpec(memory_space=pl.ANY),
                      pl.BlockSpec((1, rows, d), lambda i, j, *_: (i, j, 0)),
                      pl.BlockSpec((1, 6, d), lambda i, j, *_: (i, 0, 0)),
                      pl.BlockSpec((1, d), lambda i, j, *_: (0, 0))],
            out_specs=pl.BlockSpec((1, rows, d), lambda i, j, *_: (i, j, 0)),
            scratch_shapes=[pltpu.VMEM((2, rows, d), F32), pltpu.SemaphoreType.DMA((2,))],
        ),
        out_shape=jax.ShapeDtypeStruct((b, s, d), F32),
        compiler_params=_params("arbitrary", "arbitrary"),
        name="moe_combine",
    )(cls, rank, off, ys, x1, mod, wf)


def _t5_bucket(n):
    max_exact = N_BUCKETS // 2
    large = max_exact + (np.log(np.maximum(n, 1) / max_exact)
                         / np.log(MAX_DISTANCE / max_exact)
                         * (N_BUCKETS - max_exact)).astype(np.int32)
    large = np.minimum(large, N_BUCKETS - 1)
    return np.where(n < max_exact, n, large).astype(np.int32)


def _bias_table(rel_bias):
    dist = np.arange(WINDOW)[:, None] + WINDOW - np.arange(2 * WINDOW)[None, :]
    bucket = _t5_bucket(np.maximum(dist, 0)).reshape(1, -1)
    onehot = (jnp.arange(N_BUCKETS, dtype=jnp.int32)[:, None] == jnp.asarray(bucket)).astype(F32)
    tab = jnp.dot(rel_bias.astype(F32).T, onehot, precision=lax.Precision.HIGHEST)
    return tab.reshape(A_HEADS, WINDOW, 2 * WINDOW)


def _layout_w_in(w_in_l):
    pts = np.cumsum([0, M_QK_W, M_QK_W, M_V_W, M_V_W, M_HEADS, M_HEADS,
                     A_Q_W, A_KV_W, A_KV_W, D_MODEL, D_MODEL])
    wb = w_in_l.astype(BF16)
    col = lambda k: wb[:, pts[k]:pts[k + 1]]
    head = lambda h: wb[:, pts[6] + A_HEAD_DIM * h:pts[6] + A_HEAD_DIM * (h + 1)]
    w_if = jnp.concatenate([col(4), col(5)], axis=1)
    w_if_pad = jnp.pad(w_if, ((0, 0), (0, LANES - 2 * M_HEADS)))
    w = jnp.concatenate([col(0), col(1), col(2), col(3)] + [head(h) for h in A_HEAD_ORDER]
                        + [col(7), col(8), col(9), col(10), w_if_pad], axis=1)
    return w, w_if.T


def _tile_tables(counts, n_tiles):
    te = EXPERT_TILE
    tiles = (counts + te - 1) // te
    ends = jnp.cumsum(tiles)
    off = (ends - tiles) * te
    n_used = ends[-1]
    tile_idx = jnp.minimum(jnp.arange(n_tiles, dtype=jnp.int32), n_used - 1)
    tile_cls = jnp.minimum(jnp.searchsorted(ends, tile_idx, side="right"), N_CLASSES - 1)
    off_pad = jnp.pad(off, (0, CLASS_ROWS - N_CLASSES)).astype(jnp.int32)
    return (off_pad, jnp.asarray(_CLASS_EA)[tile_cls], jnp.asarray(_CLASS_EB)[tile_cls],
            n_used.reshape(1).astype(jnp.int32))


def kernel(x, c, w_ada, b_ada, w_norm1, w_in, conv_w, conv_b, b_igate, b_fgate, w_mnorm, sinks, rel_bias,
           w_br_m, w_br_a, w_out, w_norm2, w_router, router_bias, w_gate_e, w_up_e, w_down_e, w_final):
    b, s, d = x.shape
    depth = w_ada.shape[0]
    t = b * s
    n_tiles = t // EXPERT_TILE + N_CLASSES
    npad = n_tiles * EXPERT_TILE

    mod_all = _ada_call(jnp.pad(c, ((0, 8 - b), (0, 0))), w_ada, b_ada)[:, :b]
    bias_tab = _bias_table(rel_bias)
    wrt = w_router.T
    wrh = wrt.astype(BF16)
    wrl = (wrt - wrh.astype(F32)).astype(BF16)
    rb = router_bias.reshape(N_EXPERTS, 1).astype(F32)
    row = lambda v: v.reshape(1, -1).astype(F32)

    moe = None
    for l in range(depth):
        mod = mod_all[l].reshape(b, 6, d)
        w_l, wift = _layout_w_in(w_in[l])
        gate_bias = jnp.concatenate([b_igate[l], b_fgate[l]]).astype(F32)
        bcol = jnp.pad(gate_bias, (0, LANES - 2 * M_HEADS)).reshape(1, LANES)
        brow = gate_bias.reshape(2 * M_HEADS, 1)

        proj_args = (mod, row(w_norm1[l]), w_l, wift, bcol, brow)
        if moe is None:
            qk, v, o, qa, kva, g, ifc, ifr = _inproj_call(x, *proj_args)
        else:
            x, qk, v, o, qa, kva, g, ifc, ifr = _combine_inproj_call(*moe, *proj_args)
        hm = _mlstm_call(qk, v, o, ifc, ifr, conv_w[l].astype(F32), row(conv_b[l]), row(w_mnorm[l]))
        ha = _swa_call(sinks[l].astype(F32), qa, kva, bias_tab)

        wba = jnp.concatenate([w_br_a[l][A_HEAD_DIM * h:A_HEAD_DIM * (h + 1)] for h in A_HEAD_ORDER])
        x1, payload, cls, rank, cnt = _post_call(
            hm, ha, g, x, mod, w_br_m[l].astype(BF16), wba.astype(BF16), w_out[l].astype(BF16),
            row(w_norm2[l]), wrh, wrl, rb)

        counts = cnt[:N_CLASSES, 0].astype(jnp.int32)
        off, tile_ea, tile_eb, n_used = _tile_tables(counts, n_tiles)
        cls = cls.reshape(t)
        rank = rank.reshape(t)
        xs = _dispatch_call(cls, rank, off, payload.reshape(t, PAYLOAD_W),
                            jnp.zeros((npad, PAYLOAD_W), F32))
        ys = _moe_call(tile_ea, tile_eb, n_used, xs, w_gate_e[l].astype(BF16), w_up_e[l].astype(BF16),
                       w_down_e[l].astype(BF16))
        moe = (cls, rank, off, ys, x1, mod)
    return _combine_call(*moe, row(w_final))
```

```python
import numpy as np
import jax
import jax.numpy as jnp
from jax import lax
from jax.experimental import pallas as pl
from jax.experimental.pallas import tpu as pltpu

F32 = jnp.float32
BF16 = jnp.bfloat16

D_MODEL = 1024
M_HEADS = 4
M_DQK = 64
M_DV = 128
M_CONV = 4
M_QK_W = M_HEADS * M_DQK
M_V_W = M_HEADS * M_DV
A_HEADS = 8
A_KV_HEADS = 2
A_HEAD_DIM = 64
A_Q_W = A_HEADS * A_HEAD_DIM
A_KV_W = A_KV_HEADS * A_HEAD_DIM
WINDOW = 128
N_BUCKETS = 32
MAX_DISTANCE = 128
N_EXPERTS = 16
N_GROUPS = 4
EXPERTS_PER_GROUP = N_EXPERTS // N_GROUPS
D_EXPERT = 512
EPS = 1e-6
NEG_INF = -1e30

LANES = 128
SUBLANES = 8
MLSTM_CHUNK = 128
MLSTM_BATCH = 1
TOKEN_TILE = 256
EXPERT_TILE = 256
ROW_TILE = 512
PAIRS_PER_GROUP = 6
N_CLASSES = N_GROUPS * PAIRS_PER_GROUP
CLASS_ROWS = 32
PAYLOAD_W = D_MODEL + LANES
VMEM_LIMIT = 48 * 1024 * 1024

C_QK = 0
C_V = C_QK + 2 * M_QK_W
C_O = C_V + M_V_W
C_QA = C_O + M_V_W
C_KVA = C_QA + A_Q_W
C_G = C_KVA + 2 * A_KV_W
C_IF = C_G + 2 * D_MODEL
C_END = C_IF + LANES

A_HEAD_ORDER = (0, 4, 1, 5, 2, 6, 3, 7)

_PAIRS = [(a, b) for a in range(EXPERTS_PER_GROUP) for b in range(a + 1, EXPERTS_PER_GROUP)]
_CLASS_EA = np.array([g * EXPERTS_PER_GROUP + a for g in range(N_GROUPS) for a, _ in _PAIRS], np.int32)
_CLASS_EB = np.array([g * EXPERTS_PER_GROUP + b for g in range(N_GROUPS) for _, b in _PAIRS], np.int32)

_NT = (((1,), (1,)), ((), ()))
_TN = (((0,), (0,)), ((), ()))


def _sigmoid(x):
    return 1.0 / (1.0 + jnp.exp(-x))


def _log_sigmoid(x):
    return -(jnp.maximum(-x, 0.0) + jnp.log1p(jnp.exp(-jnp.abs(x))))


def _dot(a, b, dims=None):
    if dims is None:
        return jnp.dot(a, b, preferred_element_type=F32)
    return lax.dot_general(a, b, dims, preferred_element_type=F32)


def _split2(a):
    hi = a.astype(BF16)
    lo = (a - hi.astype(F32)).astype(BF16)
    return hi, lo


def _split3(a):
    hi = a.astype(BF16)
    r = a - hi.astype(F32)
    mid = r.astype(BF16)
    lo = (r - mid.astype(F32)).astype(BF16)
    return hi, mid, lo


def _dot_hi(a, b, dims=None):
    ah, al = _split2(a)
    bh, bl = _split2(b)
    return _dot(ah, bh, dims) + _dot(ah, bl, dims) + _dot(al, bh, dims)


def _rms(x):
    return x * lax.rsqrt(jnp.mean(x * x, axis=-1, keepdims=True) + EPS)


def _params(*sem):
    return pltpu.CompilerParams(dimension_semantics=sem, vmem_limit_bytes=VMEM_LIMIT)


def _cast_kernel(w_ref, o_ref):
    o_ref[...] = w_ref[...].astype(BF16)


def _cast_call(w):
    e, a, b = w.shape
    blk = 2
    spec = pl.BlockSpec((blk, a, b), lambda i: (i, 0, 0))
    return pl.pallas_call(
        _cast_kernel,
        grid=(e // blk,),
        in_specs=[spec],
        out_specs=spec,
        out_shape=jax.ShapeDtypeStruct(w.shape, BF16),
        compiler_params=_params("arbitrary"),
        name="expert_weights_bf16",
    )(w)


def _ada_kernel(c_ref, w_ref, b_ref, o_ref):
    c = c_ref[...]
    cond = c * _sigmoid(c)
    o_ref[0] = _dot_hi(cond, w_ref[0]) + b_ref[0]


def _ada_call(c8, w_ada, b_ada):
    depth = w_ada.shape[0]
    return pl.pallas_call(
        _ada_kernel,
        grid=(depth, 6),
        in_specs=[
            pl.BlockSpec((8, D_MODEL), lambda l, j: (0, 0)),
            pl.BlockSpec((1, D_MODEL, D_MODEL), lambda l, j: (l, 0, j)),
            pl.BlockSpec((1, 1, D_MODEL), lambda l, j: (l, 0, j)),
        ],
        out_specs=pl.BlockSpec((1, 8, D_MODEL), lambda l, j: (l, 0, j)),
        out_shape=jax.ShapeDtypeStruct((depth, 8, 6 * D_MODEL), F32),
        compiler_params=_params("arbitrary", "arbitrary"),
        name="adaln_mod",
    )(c8, w_ada, b_ada.reshape(depth, 1, 6 * D_MODEL))


def _project(x, mod_ref, wn_ref, w_ref, wift_ref, bcol_ref, brow_ref, outs):
    qk_ref, v_ref, o_ref, qa_ref, kva_ref, g_ref, ifc_ref, ifr_ref = outs
    h = _rms(x) * wn_ref[...] * (1.0 + mod_ref[0, 1:2, :]) + mod_ref[0, 0:1, :]
    hb = h.astype(BF16)

    def seg(a, b):
        return _dot(hb, w_ref[:, a:b])

    qk_ref[0] = seg(C_QK, C_V).astype(BF16)
    v_ref[0] = seg(C_V, C_O).astype(BF16)
    o_ref[0] = seg(C_O, C_QA).astype(BF16)
    qa_ref[0] = seg(C_QA, C_KVA).astype(BF16)
    kva_ref[0] = seg(C_KVA, C_G).astype(BF16)
    g_ref[0] = seg(C_G, C_IF).astype(BF16)
    zc = seg(C_IF, C_END) + bcol_ref[...]
    lane = lax.broadcasted_iota(jnp.int32, zc.shape, 1)
    ifc_ref[0] = jnp.where((lane >= M_HEADS) & (lane < 2 * M_HEADS), _log_sigmoid(zc), zc)
    zr = _dot(wift_ref[...], hb, _NT) + brow_ref[...]
    row = lax.broadcasted_iota(jnp.int32, zr.shape, 0)
    ifr_ref[0] = jnp.where(row >= M_HEADS, _log_sigmoid(zr), zr)


def _inproj_kernel(x_ref, mod_ref, wn_ref, w_ref, wift_ref, bcol_ref, brow_ref, *outs):
    _project(x_ref[0], mod_ref, wn_ref, w_ref, wift_ref, bcol_ref, brow_ref, outs)


def _row_copies(pos_s, first_token, sorted_hbm, tiles, sem, *, to_sorted, wait):
    def copy(i, k, pos):
        pair = (tiles.at[i, pl.ds(k, 1), :], sorted_hbm.at[pl.ds(pos, 1), :])
        return pltpu.make_async_copy(*(pair if to_sorted else pair[::-1]), sem)

    def body(i, carry):
        for k in range(SUBLANES):
            if wait:
                copy(0, k, 0).wait()
            else:
                copy(i, k, pos_s[first_token + i * SUBLANES + k]).start(priority=k % 2)
        return carry

    lax.fori_loop(0, tiles.shape[0], body, 0, unroll=2)


def _moe_residual(pos_s, ys_hbm, x1_ref, mod_ref, ybuf, sem):
    rows = x1_ref.shape[1]
    step = pl.program_id(0) * pl.num_programs(1) + pl.program_id(1)
    n_steps = pl.num_programs(0) * pl.num_programs(1)
    slot = lax.rem(step, 2)

    def fetch(first_token, slot_, wait):
        _row_copies(pos_s, first_token, ys_hbm, ybuf.at[slot_], sem.at[slot_], to_sorted=False, wait=wait)

    @pl.when(step == 0)
    def _():
        fetch(0, 0, False)

    @pl.when(step + 1 < n_steps)
    def _():
        fetch((step + 1) * rows, 1 - slot, False)

    fetch(0, slot, True)
    y = ybuf[slot].reshape(rows, ybuf.shape[-1])
    return x1_ref[0] + mod_ref[0, 5:6, :] * y


def _combine_inproj_kernel(pos_s, ys_hbm, x1_ref, modp_ref, mod_ref, wn_ref, w_ref,
                           wift_ref, bcol_ref, brow_ref, xo_ref, *rest):
    outs, (ybuf, sem) = rest[:8], rest[8:]
    x = _moe_residual(pos_s, ys_hbm, x1_ref, modp_ref, ybuf, sem)
    xo_ref[0] = x
    _project(x, mod_ref, wn_ref, w_ref, wift_ref, bcol_ref, brow_ref, outs)


def _inproj_specs(b, s, tm):
    tok = lambda w_: pl.BlockSpec((1, tm, w_), lambda i, j, *_: (i, j, 0))
    sds = lambda w_, dt: jax.ShapeDtypeStruct((b, s, w_), dt)
    out_specs = [tok(2 * M_QK_W), tok(M_V_W), tok(M_V_W), tok(A_Q_W), tok(2 * A_KV_W),
                 tok(2 * D_MODEL), tok(LANES), pl.BlockSpec((1, 8, tm), lambda i, j, *_: (i, 0, j))]
    out_shape = [sds(2 * M_QK_W, BF16), sds(M_V_W, BF16), sds(M_V_W, BF16), sds(A_Q_W, BF16),
                 sds(2 * A_KV_W, BF16), sds(2 * D_MODEL, BF16), sds(LANES, F32),
                 jax.ShapeDtypeStruct((b, 8, s), F32)]
    return tok, out_specs, out_shape


def _inproj_call(x, mod, wn, w, wift, bcol, brow):
    b, s, d = x.shape
    tm = TOKEN_TILE
    tok, out_specs, out_shape = _inproj_specs(b, s, tm)
    full = lambda a: pl.BlockSpec(a.shape, lambda i, j: (0,) * a.ndim)
    return pl.pallas_call(
        _inproj_kernel,
        grid=(b, s // tm),
        in_specs=[tok(d), pl.BlockSpec((1, 6, d), lambda i, j: (i, 0, 0)), full(wn), full(w),
                  full(wift), full(bcol), full(brow)],
        out_specs=out_specs,
        out_shape=out_shape,
        compiler_params=_params("arbitrary", "arbitrary"),
        name="norm1_inproj",
    )(x, mod, wn, w, wift, bcol, brow)


def _combine_inproj_call(pos, ys, x1, mod_prev, mod, wn, w, wift, bcol, brow):
    b, s, d = x1.shape
    tm = TOKEN_TILE
    tok, out_specs, out_shape = _inproj_specs(b, s, tm)
    full = lambda a: pl.BlockSpec(a.shape, lambda i, j, *_: (0,) * a.ndim)
    modspec = pl.BlockSpec((1, 6, d), lambda i, j, *_: (i, 0, 0))
    return pl.pallas_call(
        _combine_inproj_kernel,
        grid_spec=pltpu.PrefetchScalarGridSpec(
            num_scalar_prefetch=1,
            grid=(b, s // tm),
            in_specs=[pl.BlockSpec(memory_space=pl.ANY), tok(d), modspec, modspec, full(wn), full(w),
                      full(wift), full(bcol), full(brow)],
            out_specs=[tok(d)] + out_specs,
            scratch_shapes=[pltpu.VMEM((2, tm // SUBLANES, SUBLANES, d), F32),
                            pltpu.SemaphoreType.DMA((2,))],
        ),
        out_shape=[jax.ShapeDtypeStruct((b, s, d), F32)] + out_shape,
        compiler_params=_params("arbitrary", "arbitrary"),
        name="combine_norm1_inproj",
    )(pos, ys, x1, mod_prev, mod, wn, w, wift, bcol, brow)


def _mlstm_kernel(qk_ref, v_ref, o_ref, ifc_ref, ifr_ref, cw_ref, cb_ref, wn_ref,
                  out_ref, xbuf, cstate, mstate):
    @pl.when(pl.program_id(1) == 0)
    def _():
        xbuf[...] = jnp.zeros(xbuf.shape, F32)
        cstate[...] = jnp.zeros(cstate.shape, F32)
        mstate[...] = jnp.zeros(mstate.shape, F32)

    for bb in range(qk_ref.shape[0]):
        one = lambda ref: ref.at[pl.ds(bb, 1)]
        _mlstm_chunk(one(qk_ref), one(v_ref), one(o_ref), one(ifc_ref), one(ifr_ref), cw_ref, cb_ref, wn_ref,
                     one(out_ref), xbuf.at[bb], cstate.at[bb], mstate.at[bb])


def _mlstm_chunk(qk_ref, v_ref, o_ref, ifc_ref, ifr_ref, cw_ref, cb_ref, wn_ref,
                 out_ref, xbuf, cstate, mstate):
    L = qk_ref.shape[1]
    pad = 8

    x = qk_ref[0].astype(F32)
    xbuf[pad:pad + L, :] = x
    acc = cb_ref[...] + cw_ref[0:1, :] * xbuf[pad - 3:pad - 3 + L, :]
    for j in range(1, M_CONV):
        acc = acc + cw_ref[j:j + 1, :] * xbuf[pad - 3 + j:pad - 3 + j + L, :]
    xbuf[0:pad, :] = x[L - pad:L, :]
    qk = acc * _sigmoid(acc)
    q = qk[:, :M_QK_W] * (M_DQK ** -0.5)
    kb = qk[:, M_QK_W:].astype(BF16)
    vb = v_ref[0]

    r_i = lax.broadcasted_iota(jnp.int32, (L, L), 0)
    c_i = lax.broadcasted_iota(jnp.int32, (L, L), 1)
    causal = c_i <= r_i
    tril = jnp.where(causal, 1.0, 0.0).astype(BF16)
    triu = jnp.where(r_i <= c_i, 1.0, 0.0).astype(BF16)

    icol = ifc_ref[0]
    irow = ifr_ref[0]
    ch, cm, cl = _split3(icol)
    bcum_col = _dot(tril, ch) + _dot(tril, cm) + _dot(tril, cl)
    rh, rm, rl = _split3(irow)
    bcum_row = _dot(rh, triu) + _dot(rm, triu) + _dot(rl, triu)

    lane = lax.broadcasted_iota(jnp.int32, (1, LANES), 1)
    lo_half = lane < M_DQK
    ones = jnp.ones((L, LANES), BF16)

    for h in range(M_HEADS):
        t, p = h // 2, h % 2
        qt = q[:, LANES * t:LANES * (t + 1)]
        qm = jnp.where(lo_half if p == 0 else jnp.logical_not(lo_half), qt, 0.0).astype(BF16)
        kt = kb[:, LANES * t:LANES * (t + 1)]
        s = _dot(qm, kt, _NT)

        bc_col = bcum_col[:, M_HEADS + h:M_HEADS + h + 1]
        i_col = icol[:, h:h + 1]
        bc_row = bcum_row[M_HEADS + h:M_HEADS + h + 1, :]
        i_row = irow[h:h + 1, :]
        b_last = bc_row[:, L - 1:L]
        m_prev = mstate[h][:, 0:1]

        dlog = jnp.where(causal, bc_col - bc_row + i_row, NEG_INF)
        m_inter = bc_col + m_prev
        m_row = jnp.maximum(m_inter, jnp.max(dlog, axis=-1, keepdims=True))
        sc = (s * jnp.exp(dlog - m_row)).astype(BF16)
        inter = jnp.exp(m_inter - m_row)

        vaug = jnp.concatenate([vb[:, M_DV * h:M_DV * (h + 1)], ones], axis=1)
        cst = cstate[h]
        num = _dot(sc, vaug) + inter * _dot(qm, cst.astype(BF16))
        den = num[:, M_DV:]
        hh = num[:, :M_DV] / jnp.maximum(jnp.abs(den), jnp.exp(-m_row))

        ws_row = b_last - bc_row + i_row
        m_loc = jnp.max(ws_row, axis=-1, keepdims=True)
        a_col = jnp.exp(b_last - bc_col + i_col - m_loc)
        av = (a_col * vaug.astype(F32)).astype(BF16)
        c_loc = _dot(kt, av, _TN)
        m_new = jnp.maximum(b_last + m_prev, m_loc)
        decay = jnp.exp(b_last + m_prev - m_new)
        fresh = jnp.exp(m_loc - m_new)
        cstate[h] = decay * cst + fresh * c_loc
        mstate[h] = jnp.broadcast_to(m_new, (1, LANES))

        hn = _rms(hh) * wn_ref[:, M_DV * h:M_DV * (h + 1)]
        og = _sigmoid(o_ref[0, :, M_DV * h:M_DV * (h + 1)].astype(F32))
        out_ref[0, :, M_DV * h:M_DV * (h + 1)] = (hn * og).astype(BF16)


def _mlstm_call(qk, v, o, ifc, ifr, conv_w, conv_b, w_mnorm):
    b, s, _ = qk.shape
    L = MLSTM_CHUNK
    nb = MLSTM_BATCH
    assert b % nb == 0
    tok = lambda w_: pl.BlockSpec((nb, L, w_), lambda i, j: (i, j, 0))
    full = lambda a: pl.BlockSpec(a.shape, lambda i, j: (0,) * a.ndim)
    return pl.pallas_call(
        _mlstm_kernel,
        grid=(b // nb, s // L),
        in_specs=[tok(2 * M_QK_W), tok(M_V_W), tok(M_V_W), tok(LANES),
                  pl.BlockSpec((nb, 8, L), lambda i, j: (i, 0, j)),
                  full(conv_w), full(conv_b), full(w_mnorm)],
        out_specs=tok(M_V_W),
        out_shape=jax.ShapeDtypeStruct((b, s, M_V_W), BF16),
        scratch_shapes=[pltpu.VMEM((nb, L + 8, 2 * M_QK_W), F32),
                        pltpu.VMEM((nb, M_HEADS, LANES, 2 * M_DV), F32),
                        pltpu.VMEM((nb, M_HEADS, 1, LANES), F32)],
        compiler_params=_params("arbitrary", "arbitrary"),
        name="mlstm",
    )(qk, v, o, ifc, ifr, conv_w, conv_b, w_mnorm)


def _swa_kernel(sink_ref, q_ref, kvp_ref, kvc_ref, bias_ref, out_ref):
    W = WINDOW
    first = pl.program_id(1) == 0
    kvp = kvp_ref[0]
    kvc = kvc_ref[0]
    keys = jnp.concatenate([kvp[:, :A_KV_W], kvc[:, :A_KV_W]], axis=0)
    vals = jnp.concatenate([kvp[:, A_KV_W:], kvc[:, A_KV_W:]], axis=0)
    lane = lax.broadcasted_iota(jnp.int32, (1, LANES), 1)
    lo_half = lane < A_HEAD_DIM
    hi_half = jnp.logical_not(lo_half)
    zero = jnp.zeros_like(vals)
    v_half = (jnp.where(lo_half, vals, zero), jnp.where(hi_half, vals, zero))

    q_pos = lax.broadcasted_iota(jnp.int32, (W, 2 * W), 0)
    k_pos = lax.broadcasted_iota(jnp.int32, (W, 2 * W), 1)
    dist = q_pos + W - k_pos
    in_window = jnp.where(dist >= 0, jnp.where(dist < W, 1, 0), 0)
    key_valid = jnp.where(k_pos >= W, 1, jnp.where(first, 0, 1))
    valid = (in_window * key_valid) > 0

    for j in range(A_HEADS // 2):
        qt = q_ref[0, :, LANES * j:LANES * (j + 1)]
        acc = jnp.zeros((W, LANES), F32)
        for p in range(2):
            h = A_HEAD_ORDER[2 * j + p]
            qm = jnp.where(lo_half if p == 0 else hi_half, qt, jnp.zeros_like(qt))
            s = _dot(qm, keys, _NT) * (A_HEAD_DIM ** -0.5) + bias_ref[h]
            s = jnp.where(valid, s, NEG_INF)
            sink = sink_ref[h]
            m = jnp.maximum(jnp.max(s, axis=-1, keepdims=True), sink)
            e = jnp.exp(s - m)
            denom = jnp.sum(e, axis=-1, keepdims=True) + jnp.exp(sink - m)
            acc = acc + _dot(e.astype(BF16), v_half[p]) / denom
        out_ref[0, :, LANES * j:LANES * (j + 1)] = acc.astype(BF16)


def _swa_call(sinks, qa, kva, bias):
    b, s, _ = qa.shape
    W = WINDOW
    return pl.pallas_call(
        _swa_kernel,
        grid=(b, s // W),
        in_specs=[pl.BlockSpec(memory_space=pltpu.SMEM),
                  pl.BlockSpec((1, W, A_Q_W), lambda i, j: (i, j, 0)),
                  pl.BlockSpec((1, W, 2 * A_KV_W), lambda i, j: (i, jnp.maximum(j - 1, 0), 0)),
                  pl.BlockSpec((1, W, 2 * A_KV_W), lambda i, j: (i, j, 0)),
                  pl.BlockSpec(bias.shape, lambda i, j: (0, 0, 0))],
        out_specs=pl.BlockSpec((1, W, A_Q_W), lambda i, j: (i, j, 0)),
        out_shape=jax.ShapeDtypeStruct((b, s, A_Q_W), BF16),
        compiler_params=_params("arbitrary", "arbitrary"),
        name="swa",
    )(sinks, qa, kva, kva, bias)


def _post_kernel(hm_ref, ha_ref, g_ref, x_ref, mod_ref, wbm_ref, wba_ref, wo_ref, wn2_ref,
                 wrh_ref, wrl_ref, rb_ref,
                 x1_ref, pay_ref, cls_ref, rank_ref, cnt_ref, carry):
    tm = x_ref.shape[1]

    @pl.when((pl.program_id(0) == 0) & (pl.program_id(1) == 0))
    def _():
        carry[...] = jnp.zeros(carry.shape, F32)

    g = g_ref[0]
    pm = _dot(hm_ref[0], wbm_ref[...])
    pa = _dot(ha_ref[0], wba_ref[...])
    merged = (_sigmoid(g[:, :D_MODEL].astype(F32)) * pm
              + _sigmoid(g[:, D_MODEL:].astype(F32)) * pa)
    mo = _dot(merged.astype(BF16), wo_ref[...])
    x1 = x_ref[0] + mod_ref[0, 2:3, :] * mo
    x1_ref[0] = x1
    h2 = _rms(x1) * wn2_ref[...] * (1.0 + mod_ref[0, 4:5, :]) + mod_ref[0, 3:4, :]
    pay_ref[0, :, :D_MODEL] = h2

    hh, hl = _split2(h2)
    wrh = wrh_ref[...]
    logits = _dot(wrh, hh, _NT) + _dot(wrh, hl, _NT) + _dot(wrl_ref[...], hh, _NT)
    scores = _sigmoid(logits)
    sel = scores + rb_ref[...]

    def row(a, r):
        return a[r:r + 1, :]

    grp = []
    for gi in range(N_GROUPS):
        v = [row(sel, gi * EXPERTS_PER_GROUP + k) for k in range(EXPERTS_PER_GROUP)]
        best = v[0] + v[1]
        for a, b in _PAIRS[1:]:
            best = jnp.maximum(best, v[a] + v[b])
        grp.append(best)
    gbest = grp[0]
    gsel = jnp.zeros_like(gbest)
    for gi in range(1, N_GROUPS):
        take = grp[gi] > gbest
        gbest = jnp.where(take, grp[gi], gbest)
        gsel = jnp.where(take, float(gi), gsel)

    sv, gv = [], []
    for k in range(EXPERTS_PER_GROUP):
        s_k = row(sel, k)
        g_k = row(scores, k)
        for gi in range(1, N_GROUPS):
            hit = gsel == float(gi)
            s_k = jnp.where(hit, row(sel, gi * EXPERTS_PER_GROUP + k), s_k)
            g_k = jnp.where(hit, row(scores, gi * EXPERTS_PER_GROUP + k), g_k)
        sv.append(s_k)
        gv.append(g_k)

    def argmax4(vals):
        bv, bi = vals[0], jnp.zeros_like(vals[0])
        for k in range(1, EXPERTS_PER_GROUP):
            take = vals[k] > bv
            bv = jnp.where(take, vals[k], bv)
            bi = jnp.where(take, float(k), bi)
        return bi

    i1 = argmax4(sv)
    i2 = argmax4([jnp.where(i1 == float(k), -jnp.inf, sv[k]) for k in range(EXPERTS_PER_GROUP)])
    w1 = jnp.zeros_like(i1)
    w2 = jnp.zeros_like(i1)
    for k in range(EXPERTS_PER_GROUP):
        w1 = jnp.where(i1 == float(k), gv[k], w1)
        w2 = jnp.where(i2 == float(k), gv[k], w2)
    wsum = w1 + w2
    w1 = w1 / wsum
    w2 = w2 / wsum
    first_low = i1 < i2
    e_lo = jnp.minimum(i1, i2)
    e_hi = jnp.maximum(i1, i2)
    gate_lo = jnp.where(first_low, w1, w2)
    gate_hi = jnp.where(first_low, w2, w1)
    pair_base = jnp.where(e_lo == 0.0, 0.0, jnp.where(e_lo == 1.0, 3.0, 5.0))
    cls_f = gsel * float(PAIRS_PER_GROUP) + pair_base + (e_hi - e_lo - 1.0)
    cls_ref[0] = cls_f.astype(jnp.int32)

    grow = lax.broadcasted_iota(jnp.int32, (LANES, tm), 0)
    gmat = jnp.where(grow == 0, gate_lo, jnp.where(grow == 1, gate_hi, 0.0))
    pay_ref[0, :, D_MODEL:] = gmat.T

    crow = lax.broadcasted_iota(jnp.int32, (CLASS_ROWS, tm), 0).astype(F32)
    onehot = crow == cls_f
    r_i = lax.broadcasted_iota(jnp.int32, (tm, tm), 0)
    c_i = lax.broadcasted_iota(jnp.int32, (tm, tm), 1)
    upper = jnp.where(r_i <= c_i, 1.0, 0.0).astype(BF16)
    cum = _dot(jnp.where(onehot, 1.0, 0.0).astype(BF16), upper)
    before = carry[:, 0:1]
    rank = jnp.sum(jnp.where(onehot, cum - 1.0 + before, 0.0), axis=0, keepdims=True)
    rank_ref[0] = rank.astype(jnp.int32)
    total = before + cum[:, tm - 1:tm]
    carry[...] = jnp.broadcast_to(total, carry.shape)
    cnt_ref[...] = jnp.broadcast_to(total, cnt_ref.shape)


def _post_call(hm, ha, g, x, mod, wbm, wba, wo, wn2, wrh, wrl, rb):
    b, s, d = x.shape
    tm = TOKEN_TILE
    tok = lambda w_: pl.BlockSpec((1, tm, w_), lambda i, j: (i, j, 0))
    full = lambda a: pl.BlockSpec(a.shape, lambda i, j: (0,) * a.ndim)
    lanes = pl.BlockSpec((1, 1, tm), lambda i, j: (i, 0, j))
    return pl.pallas_call(
        _post_kernel,
        grid=(b, s // tm),
        in_specs=[tok(M_V_W), tok(A_Q_W), tok(2 * d), tok(d),
                  pl.BlockSpec((1, 6, d), lambda i, j: (i, 0, 0)),
                  full(wbm), full(wba), full(wo), full(wn2), full(wrh), full(wrl), full(rb)],
        out_specs=[tok(d), tok(PAYLOAD_W), lanes, lanes,
                   pl.BlockSpec((CLASS_ROWS, LANES), lambda i, j: (0, 0))],
        out_shape=[jax.ShapeDtypeStruct((b, s, d), F32),
                   jax.ShapeDtypeStruct((b, s, PAYLOAD_W), F32),
                   jax.ShapeDtypeStruct((b, 1, s), jnp.int32),
                   jax.ShapeDtypeStruct((b, 1, s), jnp.int32),
                   jax.ShapeDtypeStruct((CLASS_ROWS, LANES), F32)],
        scratch_shapes=[pltpu.VMEM((CLASS_ROWS, LANES), F32)],
        compiler_params=_params("arbitrary", "arbitrary"),
        name="merge_outproj_router",
    )(hm, ha, g, x, mod, wbm, wba, wo, wn2, wrh, wrl, rb)


def _zero_fill(off_s, cnt_s, nu_s, xs_out, zbuf, zsem, *, wait):
    chunk = zbuf.shape[0]

    def zero_rows(first, n):
        cp = pltpu.make_async_copy(zbuf.at[pl.ds(0, n), :], xs_out.at[pl.ds(first, n), :], zsem)
        cp.wait() if wait else cp.start()

    def per_class(c, carry):
        n = cnt_s[c]
        fill = (-n) & (EXPERT_TILE - 1)
        head = fill & (SUBLANES - 1)
        for k in range(SUBLANES - 1):
            pl.when(k < head)(lambda k=k: zero_rows(off_s[c] + n + k, 1))
        cur = off_s[c] + n + head
        p = chunk
        while p >= SUBLANES:
            pl.when((fill & p) != 0)(lambda cur=cur, p=p: zero_rows(pl.multiple_of(cur, SUBLANES), p))
            cur = cur + (fill & p)
            p //= 2
        return carry

    lax.fori_loop(0, N_CLASSES, per_class, 0)

    def per_chunk(r, carry):
        zero_rows(pl.multiple_of(r * chunk, SUBLANES), chunk)
        return carry

    per_tile = EXPERT_TILE // chunk
    lax.fori_loop(nu_s[0] * per_tile, (xs_out.shape[0] // EXPERT_TILE) * per_tile, per_chunk, 0)


def _dispatch_kernel(pos_s, off_s, cnt_s, nu_s, pay_ref, xs_out, zbuf, sem, zsem):
    @pl.when(pl.program_id(0) == 0)
    def _():
        zbuf[...] = jnp.zeros(zbuf.shape, F32)
        _zero_fill(off_s, cnt_s, nu_s, xs_out, zbuf, zsem, wait=False)
        _zero_fill(off_s, cnt_s, nu_s, xs_out, zbuf, zsem, wait=True)

    first_token = pl.program_id(0) * pay_ref.shape[0] * SUBLANES
    for wait in (False, True):
        _row_copies(pos_s, first_token, xs_out, pay_ref, sem, to_sorted=True, wait=wait)


def _dispatch_call(pos, off, cnt, n_used, payload, npad):
    t8, _, width = payload.shape
    tiles = ROW_TILE // SUBLANES
    return pl.pallas_call(
        _dispatch_kernel,
        grid_spec=pltpu.PrefetchScalarGridSpec(
            num_scalar_prefetch=4,
            grid=(t8 // tiles,),
            in_specs=[pl.BlockSpec((tiles, SUBLANES, width), lambda i, *_: (i, 0, 0))],
            out_specs=pl.BlockSpec(memory_space=pl.ANY),
            scratch_shapes=[pltpu.VMEM((EXPERT_TILE // 2, width), F32),
                            pltpu.SemaphoreType.DMA(()), pltpu.SemaphoreType.DMA(())],
        ),
        out_shape=jax.ShapeDtypeStruct((npad, width), F32),
        compiler_params=_params("arbitrary"),
        name="moe_dispatch",
    )(pos, off, cnt, n_used, payload)


def _moe_kernel(ea_s, eb_s, nu_s, x_ref, wga, wua, wda, wgb, wub, wdb, y_ref):
    del ea_s, eb_s

    @pl.when(pl.program_id(0) < nu_s[0])
    def _():
        x = x_ref[:, :D_MODEL].astype(BF16)
        gate_a = x_ref[:, D_MODEL:D_MODEL + 1]
        gate_b = x_ref[:, D_MODEL + 1:D_MODEL + 2]

        def expert(wg, wu, wd):
            hg = _dot(x, wg[0])
            hu = _dot(x, wu[0])
            act = (hg * _sigmoid(hg) * hu).astype(BF16)
            return _dot(act, wd[0])

        y_ref[...] = gate_a * expert(wga, wua, wda) + gate_b * expert(wgb, wub, wdb)

    @pl.when(pl.program_id(0) >= nu_s[0])
    def _():
        y_ref[...] = jnp.zeros(y_ref.shape, F32)


def _moe_call(tile_ea, tile_eb, n_used, xs, wg, wu, wd):
    npad = xs.shape[0]
    te = EXPERT_TILE
    last = lambda i, nu: jnp.minimum(i, nu[0] - 1)
    up = lambda tab: pl.BlockSpec((1, D_MODEL, D_EXPERT), lambda i, ea, eb, nu, tab=tab: ((ea, eb)[tab][i], 0, 0))
    down = lambda tab: pl.BlockSpec((1, D_EXPERT, D_MODEL), lambda i, ea, eb, nu, tab=tab: ((ea, eb)[tab][i], 0, 0))
    return pl.pallas_call(
        _moe_kernel,
        grid_spec=pltpu.PrefetchScalarGridSpec(
            num_scalar_prefetch=3,
            grid=(npad // te,),
            in_specs=[pl.BlockSpec((te, PAYLOAD_W), lambda i, ea, eb, nu: (last(i, nu), 0)),
                      up(0), up(0), down(0), up(1), up(1), down(1)],
            out_specs=pl.BlockSpec((te, D_MODEL), lambda i, ea, eb, nu: (i, 0)),
        ),
        out_shape=jax.ShapeDtypeStruct((npad, D_MODEL), F32),
        compiler_params=_params("arbitrary"),
        name="moe_experts",
    )(tile_ea, tile_eb, n_used, xs, wg, wu, wd, wg, wu, wd)


def _combine_kernel(pos_s, ys_hbm, x1_ref, mod_ref, wf_ref, out_ref, ybuf, sem):
    x2 = _moe_residual(pos_s, ys_hbm, x1_ref, mod_ref, ybuf, sem)
    out_ref[0] = _rms(x2) * wf_ref[...]


def _combine_call(pos, ys, x1, mod, wf):
    b, s, d = x1.shape
    rows = ROW_TILE
    return pl.pallas_call(
        _combine_kernel,
        grid_spec=pltpu.PrefetchScalarGridSpec(
            num_scalar_prefetch=1,
            grid=(b, s // rows),
            in_specs=[pl.BlockSpec(memory_space=pl.ANY),
                      pl.BlockSpec((1, rows, d), lambda i, j, *_: (i, j, 0)),
                      pl.BlockSpec((1, 6, d), lambda i, j, *_: (i, 0, 0)),
                      pl.BlockSpec((1, d), lambda i, j, *_: (0, 0))],
            out_specs=pl.BlockSpec((1, rows, d), lambda i, j, *_: (i, j, 0)),
            scratch_shapes=[pltpu.VMEM((2, rows // SUBLANES, SUBLANES, d), F32),
                            pltpu.SemaphoreType.DMA((2,))],
        ),
        out_shape=jax.ShapeDtypeStruct((b, s, d), F32),
        compiler_params=_params("arbitrary", "arbitrary"),
        name="moe_combine",
    )(pos, ys, x1, mod, wf)


def _t5_bucket(n):
    max_exact = N_BUCKETS // 2
    large = max_exact + (np.log(np.maximum(n, 1) / max_exact)
                         / np.log(MAX_DISTANCE / max_exact)
                         * (N_BUCKETS - max_exact)).astype(np.int32)
    large = np.minimum(large, N_BUCKETS - 1)
    return np.where(n < max_exact, n, large).astype(np.int32)


def _bias_table(rel_bias):
    dist = np.arange(WINDOW)[:, None] + WINDOW - np.arange(2 * WINDOW)[None, :]
    bucket = _t5_bucket(np.maximum(dist, 0)).reshape(1, -1)
    onehot = (jnp.arange(N_BUCKETS, dtype=jnp.int32)[:, None] == jnp.asarray(bucket)).astype(F32)
    tab = jnp.dot(rel_bias.astype(F32).T, onehot, precision=lax.Precision.HIGHEST)
    return tab.reshape(A_HEADS, WINDOW, 2 * WINDOW)


def _layout_w_in(w_in_l):
    pts = np.cumsum([0, M_QK_W, M_QK_W, M_V_W, M_V_W, M_HEADS, M_HEADS,
                     A_Q_W, A_KV_W, A_KV_W, D_MODEL, D_MODEL])
    wb = w_in_l.astype(BF16)
    col = lambda k: wb[:, pts[k]:pts[k + 1]]
    head = lambda h: wb[:, pts[6] + A_HEAD_DIM * h:pts[6] + A_HEAD_DIM * (h + 1)]
    w_if = jnp.concatenate([col(4), col(5)], axis=1)
    w_if_pad = jnp.pad(w_if, ((0, 0), (0, LANES - 2 * M_HEADS)))
    w = jnp.concatenate([col(0), col(1), col(2), col(3)] + [head(h) for h in A_HEAD_ORDER]
                        + [col(7), col(8), col(9), col(10), w_if_pad], axis=1)
    return w, w_if.T


def _tile_tables(counts, n_tiles):
    te = EXPERT_TILE
    tiles = (counts + te - 1) // te
    ends = jnp.cumsum(tiles)
    off = (ends - tiles) * te
    n_used = ends[-1]
    tile_idx = jnp.minimum(jnp.arange(n_tiles, dtype=jnp.int32), n_used - 1)
    tile_cls = jnp.sum((ends[None, :] <= tile_idx[:, None]).astype(jnp.int32), axis=1)
    tile_cls = jnp.minimum(tile_cls, N_CLASSES - 1)
    off_pad = jnp.pad(off, (0, CLASS_ROWS - N_CLASSES)).astype(jnp.int32)
    return (off_pad, jnp.asarray(_CLASS_EA)[tile_cls], jnp.asarray(_CLASS_EB)[tile_cls],
            n_used.reshape(1).astype(jnp.int32))


def kernel(x, c, w_ada, b_ada, w_norm1, w_in, conv_w, conv_b, b_igate, b_fgate, w_mnorm, sinks, rel_bias,
           w_br_m, w_br_a, w_out, w_norm2, w_router, router_bias, w_gate_e, w_up_e, w_down_e, w_final):
    b, s, d = x.shape
    depth = w_ada.shape[0]
    t = b * s
    n_tiles = t // EXPERT_TILE + N_CLASSES
    npad = n_tiles * EXPERT_TILE

    mod_all = _ada_call(jnp.pad(c, ((0, 8 - b), (0, 0))), w_ada, b_ada)[:, :b]
    bias_tab = _bias_table(rel_bias)
    wrt = w_router.T
    wrh = wrt.astype(BF16)
    wrl = (wrt - wrh.astype(F32)).astype(BF16)
    rb = router_bias.reshape(N_EXPERTS, 1).astype(F32)
    row = lambda v: v.reshape(1, -1).astype(F32)

    moe = None
    for l in range(depth):
        mod = mod_all[l].reshape(b, 6, d)
        w_l, wift = _layout_w_in(w_in[l])
        gate_bias = jnp.concatenate([b_igate[l], b_fgate[l]]).astype(F32)
        bcol = jnp.pad(gate_bias, (0, LANES - 2 * M_HEADS)).reshape(1, LANES)
        brow = gate_bias.reshape(2 * M_HEADS, 1)

        proj_args = (mod, row(w_norm1[l]), w_l, wift, bcol, brow)
        if moe is None:
            qk, v, o, qa, kva, g, ifc, ifr = _inproj_call(x, *proj_args)
        else:
            x, qk, v, o, qa, kva, g, ifc, ifr = _combine_inproj_call(*moe, *proj_args)
        hm = _mlstm_call(qk, v, o, ifc, ifr, conv_w[l].astype(F32), row(conv_b[l]), row(w_mnorm[l]))
        ha = _swa_call(sinks[l].astype(F32), qa, kva, bias_tab)

        wba = jnp.concatenate([w_br_a[l][A_HEAD_DIM * h:A_HEAD_DIM * (h + 1)] for h in A_HEAD_ORDER])
        x1, payload, cls, rank, cnt = _post_call(
            hm, ha, g, x, mod, w_br_m[l].astype(BF16), wba.astype(BF16), w_out[l].astype(BF16),
            row(w_norm2[l]), wrh, wrl, rb)

        counts = cnt[:, 0].astype(jnp.int32)
        off, tile_ea, tile_eb, n_used = _tile_tables(counts[:N_CLASSES], n_tiles)
        hit = cls.reshape(t, 1) == jnp.arange(CLASS_ROWS, dtype=jnp.int32)[None, :]
        pos = jnp.sum(jnp.where(hit, off[None, :], 0), axis=1) + rank.reshape(t)
        xs = _dispatch_call(pos, off, counts, n_used,
                            payload.reshape(t // SUBLANES, SUBLANES, PAYLOAD_W), npad)
        ys = _moe_call(tile_ea, tile_eb, n_used, xs, _cast_call(w_gate_e[l]), _cast_call(w_up_e[l]),
                       _cast_call(w_down_e[l]))
        moe = (pos, ys, x1, mod)
    return _combine_call(*moe, row(w_final))
```

```python
import numpy as np
import jax
import jax.numpy as jnp
from jax import lax
from jax.experimental import pallas as pl
from jax.experimental.pallas import tpu as pltpu

F32 = jnp.float32
BF16 = jnp.bfloat16

D_MODEL = 1024
M_HEADS = 4
M_DQK = 64
M_DV = 128
M_CONV = 4
M_QK_W = M_HEADS * M_DQK
M_V_W = M_HEADS * M_DV
A_HEADS = 8
A_KV_HEADS = 2
A_HEAD_DIM = 64
A_Q_W = A_HEADS * A_HEAD_DIM
A_KV_W = A_KV_HEADS * A_HEAD_DIM
WINDOW = 128
N_BUCKETS = 32
MAX_DISTANCE = 128
N_EXPERTS = 16
N_GROUPS = 4
EXPERTS_PER_GROUP = N_EXPERTS // N_GROUPS
D_EXPERT = 512
EPS = 1e-6
NEG_INF = -1e30

LANES = 128
SUBLANES = 8
MLSTM_CHUNK = 128
MLSTM_BATCH = 1
TOKEN_TILE = 256
ROUTER_TILE = 512
EXPERT_TILE = 256
ROW_TILE = 512
PAIRS_PER_GROUP = 6
N_CLASSES = N_GROUPS * PAIRS_PER_GROUP
CLASS_ROWS = 32
PAYLOAD_W = D_MODEL + LANES
VMEM_LIMIT = 48 * 1024 * 1024

C_QK = 0
C_V = C_QK + 2 * M_QK_W
C_O = C_V + M_V_W
C_QA = C_O + M_V_W
C_KVA = C_QA + A_Q_W
C_G = C_KVA + 2 * A_KV_W
C_IF = C_G + 2 * D_MODEL
C_END = C_IF + LANES

A_HEAD_ORDER = (0, 4, 1, 5, 2, 6, 3, 7)

_PAIRS = [(a, b) for a in range(EXPERTS_PER_GROUP) for b in range(a + 1, EXPERTS_PER_GROUP)]
_CLASS_PAIRS = ((0, 1), (2, 1), (2, 0), (3, 0), (3, 1), (3, 2))
assert sorted(tuple(sorted(p)) for p in _CLASS_PAIRS) == _PAIRS
_CLASS_EA = np.array([g * EXPERTS_PER_GROUP + a for g in range(N_GROUPS) for a, _ in _CLASS_PAIRS], np.int32)
_CLASS_EB = np.array([g * EXPERTS_PER_GROUP + b for g in range(N_GROUPS) for _, b in _CLASS_PAIRS], np.int32)

_NT = (((1,), (1,)), ((), ()))
_TN = (((0,), (0,)), ((), ()))


def _sigmoid(x):
    return 1.0 / (1.0 + jnp.exp(-x))


def _log_sigmoid(x):
    return -(jnp.maximum(-x, 0.0) + jnp.log1p(jnp.exp(-jnp.abs(x))))


def _dot(a, b, dims=None):
    if dims is None:
        return jnp.dot(a, b, preferred_element_type=F32)
    return lax.dot_general(a, b, dims, preferred_element_type=F32)


def _split2(a):
    hi = a.astype(BF16)
    lo = (a - hi.astype(F32)).astype(BF16)
    return hi, lo


def _split3(a):
    hi = a.astype(BF16)
    r = a - hi.astype(F32)
    mid = r.astype(BF16)
    lo = (r - mid.astype(F32)).astype(BF16)
    return hi, mid, lo


def _dot_hi(a, b, dims=None):
    ah, al = _split2(a)
    bh, bl = _split2(b)
    return _dot(ah, bh, dims) + _dot(ah, bl, dims) + _dot(al, bh, dims)


def _rms(x):
    return x * lax.rsqrt(jnp.mean(x * x, axis=-1, keepdims=True) + EPS)


def _params(*sem):
    return pltpu.CompilerParams(dimension_semantics=sem, vmem_limit_bytes=VMEM_LIMIT)


def _cast_kernel(w_ref, o_ref):
    o_ref[...] = w_ref[...].astype(BF16)


def _cast_call(w):
    w = w.reshape((-1,) + w.shape[2:])
    e, a, b = w.shape
    blk = 2
    spec = pl.BlockSpec((blk, a, b), lambda i: (i, 0, 0))
    return pl.pallas_call(
        _cast_kernel,
        grid=(e // blk,),
        in_specs=[spec],
        out_specs=spec,
        out_shape=jax.ShapeDtypeStruct(w.shape, BF16),
        compiler_params=_params("arbitrary"),
        name="expert_weights_bf16",
    )(w)


def _ada_kernel(c_ref, w_ref, b_ref, o_ref):
    c = c_ref[...]
    cond = c * _sigmoid(c)
    o_ref[0] = _dot_hi(cond, w_ref[0]) + b_ref[0]


def _ada_call(c8, w_ada, b_ada):
    depth = w_ada.shape[0]
    return pl.pallas_call(
        _ada_kernel,
        grid=(depth, 6),
        in_specs=[
            pl.BlockSpec((8, D_MODEL), lambda l, j: (0, 0)),
            pl.BlockSpec((1, D_MODEL, D_MODEL), lambda l, j: (l, 0, j)),
            pl.BlockSpec((1, 1, D_MODEL), lambda l, j: (l, 0, j)),
        ],
        out_specs=pl.BlockSpec((1, 8, D_MODEL), lambda l, j: (l, 0, j)),
        out_shape=jax.ShapeDtypeStruct((depth, 8, 6 * D_MODEL), F32),
        compiler_params=_params("arbitrary", "arbitrary"),
        name="adaln_mod",
    )(c8, w_ada, b_ada.reshape(depth, 1, 6 * D_MODEL))


def _project(x, mod_ref, wn_ref, w_ref, wift_ref, bcol_ref, brow_ref, outs):
    qk_ref, v_ref, o_ref, qa_ref, kva_ref, g_ref, ifc_ref, ifr_ref = outs
    h = _rms(x) * wn_ref[...] * (1.0 + mod_ref[0, 1:2, :]) + mod_ref[0, 0:1, :]
    hb = h.astype(BF16)

    def seg(a, b):
        return _dot(hb, w_ref[:, a:b])

    qk_ref[0] = seg(C_QK, C_V).astype(BF16)
    v_ref[0] = seg(C_V, C_O).astype(BF16)
    o_ref[0] = seg(C_O, C_QA).astype(BF16)
    qa_ref[0] = seg(C_QA, C_KVA).astype(BF16)
    kva_ref[0] = seg(C_KVA, C_G).astype(BF16)
    g_ref[0] = seg(C_G, C_IF).astype(BF16)
    zc = seg(C_IF, C_END) + bcol_ref[...]
    lane = lax.broadcasted_iota(jnp.int32, zc.shape, 1)
    ifc_ref[0] = jnp.where((lane >= M_HEADS) & (lane < 2 * M_HEADS), _log_sigmoid(zc), zc)
    zr = _dot(wift_ref[...], hb, _NT) + brow_ref[...]
    row = lax.broadcasted_iota(jnp.int32, zr.shape, 0)
    ifr_ref[0] = jnp.where(row >= M_HEADS, _log_sigmoid(zr), zr)


def _inproj_kernel(x_ref, mod_ref, wn_ref, w_ref, wift_ref, bcol_ref, brow_ref, *outs):
    _project(x_ref[0], mod_ref, wn_ref, w_ref, wift_ref, bcol_ref, brow_ref, outs)


def _row_copies(pos_s, first_token, sorted_hbm, tiles, sem, *, to_sorted, wait):
    def copy(i, k, pos):
        pair = (tiles.at[i, pl.ds(k, 1), :], sorted_hbm.at[pl.ds(pos, 1), :])
        return pltpu.make_async_copy(*(pair if to_sorted else pair[::-1]), sem)

    def body(i, carry):
        for k in range(SUBLANES):
            if wait:
                copy(0, k, 0).wait()
            else:
                copy(i, k, pos_s[first_token + i * SUBLANES + k]).start(priority=k % 2)
        return carry

    lax.fori_loop(0, tiles.shape[0], body, 0, unroll=2)


def _moe_residual(pos_s, ys_hbm, x1_ref, mod_ref, ybuf, sem):
    rows = x1_ref.shape[1]
    step = pl.program_id(0) * pl.num_programs(1) + pl.program_id(1)
    n_steps = pl.num_programs(0) * pl.num_programs(1)
    slot = lax.rem(step, 2)

    def fetch(first_token, slot_, wait):
        _row_copies(pos_s, first_token, ys_hbm, ybuf.at[slot_], sem.at[slot_], to_sorted=False, wait=wait)

    @pl.when(step == 0)
    def _():
        fetch(0, 0, False)

    @pl.when(step + 1 < n_steps)
    def _():
        fetch((step + 1) * rows, 1 - slot, False)

    fetch(0, slot, True)
    y = ybuf[slot].reshape(rows, ybuf.shape[-1])
    return x1_ref[0] + mod_ref[0, 5:6, :] * y


def _combine_inproj_kernel(pos_s, ys_hbm, x1_ref, modp_ref, mod_ref, wn_ref, w_ref,
                           wift_ref, bcol_ref, brow_ref, xo_ref, *rest):
    outs, (ybuf, sem) = rest[:8], rest[8:]
    x = _moe_residual(pos_s, ys_hbm, x1_ref, modp_ref, ybuf, sem)
    xo_ref[0] = x
    _project(x, mod_ref, wn_ref, w_ref, wift_ref, bcol_ref, brow_ref, outs)


def _inproj_specs(b, s, tm):
    tok = lambda w_: pl.BlockSpec((1, tm, w_), lambda i, j, *_: (i, j, 0))
    sds = lambda w_, dt: jax.ShapeDtypeStruct((b, s, w_), dt)
    out_specs = [tok(2 * M_QK_W), tok(M_V_W), tok(M_V_W), tok(A_Q_W), tok(2 * A_KV_W),
                 tok(2 * D_MODEL), tok(LANES), pl.BlockSpec((1, 8, tm), lambda i, j, *_: (i, 0, j))]
    out_shape = [sds(2 * M_QK_W, BF16), sds(M_V_W, BF16), sds(M_V_W, BF16), sds(A_Q_W, BF16),
                 sds(2 * A_KV_W, BF16), sds(2 * D_MODEL, BF16), sds(LANES, F32),
                 jax.ShapeDtypeStruct((b, 8, s), F32)]
    return tok, out_specs, out_shape


def _inproj_call(x, mod, wn, w, wift, bcol, brow):
    b, s, d = x.shape
    tm = TOKEN_TILE
    tok, out_specs, out_shape = _inproj_specs(b, s, tm)
    full = lambda a: pl.BlockSpec(a.shape, lambda i, j: (0,) * a.ndim)
    return pl.pallas_call(
        _inproj_kernel,
        grid=(b, s // tm),
        in_specs=[tok(d), pl.BlockSpec((1, 6, d), lambda i, j: (i, 0, 0)), full(wn), full(w),
                  full(wift), full(bcol), full(brow)],
        out_specs=out_specs,
        out_shape=out_shape,
        compiler_params=_params("arbitrary", "arbitrary"),
        name="norm1_inproj",
    )(x, mod, wn, w, wift, bcol, brow)


def _combine_inproj_call(pos, ys, x1, mod_prev, mod, wn, w, wift, bcol, brow):
    b, s, d = x1.shape
    tm = TOKEN_TILE
    tok, out_specs, out_shape = _inproj_specs(b, s, tm)
    full = lambda a: pl.BlockSpec(a.shape, lambda i, j, *_: (0,) * a.ndim)
    modspec = pl.BlockSpec((1, 6, d), lambda i, j, *_: (i, 0, 0))
    return pl.pallas_call(
        _combine_inproj_kernel,
        grid_spec=pltpu.PrefetchScalarGridSpec(
            num_scalar_prefetch=1,
            grid=(b, s // tm),
            in_specs=[pl.BlockSpec(memory_space=pl.ANY), tok(d), modspec, modspec, full(wn), full(w),
                      full(wift), full(bcol), full(brow)],
            out_specs=[tok(d)] + out_specs,
            scratch_shapes=[pltpu.VMEM((2, tm // SUBLANES, SUBLANES, d), F32),
                            pltpu.SemaphoreType.DMA((2,))],
        ),
        out_shape=[jax.ShapeDtypeStruct((b, s, d), F32)] + out_shape,
        compiler_params=_params("arbitrary", "arbitrary"),
        name="combine_norm1_inproj",
    )(pos, ys, x1, mod_prev, mod, wn, w, wift, bcol, brow)


def _mlstm_kernel(qk_ref, v_ref, o_ref, ifc_ref, ifr_ref, cw_ref, cb_ref, wn_ref,
                  out_ref, xbuf, cstate, mstate):
    @pl.when(pl.program_id(1) == 0)
    def _():
        xbuf[...] = jnp.zeros(xbuf.shape, xbuf.dtype)
        cstate[...] = jnp.zeros(cstate.shape, F32)
        mstate[...] = jnp.zeros(mstate.shape, F32)

    def chunk(bb):
        one = lambda ref: ref.at[pl.ds(bb, 1)]
        return _mlstm_chunk(one(qk_ref), one(v_ref), one(o_ref), one(ifc_ref), one(ifr_ref), cw_ref, cb_ref,
                            wn_ref, one(out_ref), xbuf.at[bb], cstate.at[bb], mstate.at[bb])

    for _ in zip(*[chunk(bb) for bb in range(qk_ref.shape[0])]):
        pass


def _mlstm_chunk(qk_ref, v_ref, o_ref, ifc_ref, ifr_ref, cw_ref, cb_ref, wn_ref,
                 out_ref, xbuf, cstate, mstate):
    L = qk_ref.shape[1]
    pad = xbuf.shape[0] - L
    heads = range(M_HEADS)

    xb = qk_ref[0]
    xbuf[pad:pad + L, :] = xb
    xall = xbuf[...]
    xbuf[0:pad, :] = xb[L - pad:L, :]
    r_s = lax.broadcasted_iota(jnp.int32, (L, L + pad), 0)
    c_s = lax.broadcasted_iota(jnp.int32, (L, L + pad), 1)
    acc = cb_ref[...] + cw_ref[M_CONV - 1:M_CONV, :] * xb.astype(F32)
    for j in range(M_CONV - 1):
        shift = jnp.where(c_s == r_s + (pad - (M_CONV - 1) + j), 1.0, 0.0).astype(BF16)
        acc = acc + cw_ref[j:j + 1, :] * _dot(shift, xall)
    qk = acc * _sigmoid(acc)
    q = qk[:, :M_QK_W] * (M_DQK ** -0.5)
    kf = qk[:, M_QK_W:]
    kb = kf.astype(BF16)
    vb = v_ref[0]
    yield

    r_i = lax.broadcasted_iota(jnp.int32, (L, L), 0)
    c_i = lax.broadcasted_iota(jnp.int32, (L, L), 1)
    causal = c_i <= r_i
    tril = jnp.where(causal, 1.0, 0.0).astype(BF16)
    triu = jnp.where(r_i <= c_i, 1.0, 0.0).astype(BF16)

    icol = ifc_ref[0]
    irow = ifr_ref[0]
    ch, cm, cl = _split3(icol)
    bcum_col = _dot(tril, ch) + _dot(tril, cm) + _dot(tril, cl)
    rh, rm, rl = _split3(irow)
    bcum_row = _dot(rh, triu) + _dot(rm, triu) + _dot(rl, triu)
    yield

    lane = lax.broadcasted_iota(jnp.int32, (1, LANES), 1)
    lo_half = lane < M_DQK
    hi_half = jnp.logical_not(lo_half)
    ones = jnp.ones((L, LANES), BF16)
    tile = lambda a, h: a[:, LANES * (h // 2):LANES * (h // 2 + 1)]

    cst = [cstate[h] for h in heads]
    m_prev = [mstate[h][:, 0:1] for h in heads]
    qmask = [jnp.where(lo_half if h % 2 == 0 else hi_half, tile(q, h), 0.0) for h in heads]
    kt = [tile(kb, h) for h in heads]
    vaug = [jnp.concatenate([vb[:, M_DV * h:M_DV * (h + 1)], ones], axis=1) for h in heads]
    bc_col = [bcum_col[:, M_HEADS + h:M_HEADS + h + 1] for h in heads]
    i_col = [icol[:, h:h + 1] for h in heads]
    bc_row = [bcum_row[M_HEADS + h:M_HEADS + h + 1, :] for h in heads]
    i_row = [irow[h:h + 1, :] for h in heads]
    b_last = [bc_row[h][:, L - 1:L] for h in heads]

    s = [_dot(qmask[h].astype(BF16), kt[h], _NT) for h in heads]
    yield

    dlog = [jnp.where(causal, bc_col[h] - bc_row[h] + i_row[h], NEG_INF) for h in heads]
    m_inter = [bc_col[h] + m_prev[h] for h in heads]
    m_row = [jnp.maximum(m_inter[h], jnp.max(dlog[h], axis=-1, keepdims=True)) for h in heads]
    sc = [(s[h] * jnp.exp(dlog[h] - m_row[h])).astype(BF16) for h in heads]
    qi = [(jnp.exp(m_inter[h] - m_row[h]) * qmask[h]).astype(BF16) for h in heads]
    yield

    num = [_dot(sc[h], vaug[h]) + _dot(qi[h], cst[h].astype(BF16)) for h in heads]
    yield
    for h in heads:
        den = num[h][:, M_DV:]
        hh = num[h][:, :M_DV] / jnp.maximum(jnp.abs(den), jnp.exp(-m_row[h]))
        hn = _rms(hh) * wn_ref[:, M_DV * h:M_DV * (h + 1)]
        og = _sigmoid(o_ref[0, :, M_DV * h:M_DV * (h + 1)].astype(F32))
        out_ref[0, :, M_DV * h:M_DV * (h + 1)] = (hn * og).astype(BF16)

    yield
    m_loc = [jnp.max(b_last[h] - bc_row[h] + i_row[h], axis=-1, keepdims=True) for h in heads]
    ak = [(jnp.exp(b_last[h] - bc_col[h] + i_col[h] - m_loc[h]) * tile(kf, h)).astype(BF16) for h in heads]
    c_loc = [_dot(ak[h], vaug[h], _TN) for h in heads]
    for h in heads:
        m_new = jnp.maximum(b_last[h] + m_prev[h], m_loc[h])
        decay = jnp.exp(b_last[h] + m_prev[h] - m_new)
        fresh = jnp.exp(m_loc[h] - m_new)
        cstate[h] = decay * cst[h] + fresh * c_loc[h]
        mstate[h] = jnp.broadcast_to(m_new, (1, LANES))
    yield


def _mlstm_call(qk, v, o, ifc, ifr, conv_w, conv_b, w_mnorm):
    b, s, _ = qk.shape
    L = MLSTM_CHUNK
    nb = MLSTM_BATCH
    assert b % nb == 0
    tok = lambda w_: pl.BlockSpec((nb, L, w_), lambda i, j: (i, j, 0))
    full = lambda a: pl.BlockSpec(a.shape, lambda i, j: (0,) * a.ndim)
    return pl.pallas_call(
        _mlstm_kernel,
        grid=(b // nb, s // L),
        in_specs=[tok(2 * M_QK_W), tok(M_V_W), tok(M_V_W), tok(LANES),
                  pl.BlockSpec((nb, 8, L), lambda i, j: (i, 0, j)),
                  full(conv_w), full(conv_b), full(w_mnorm)],
        out_specs=tok(M_V_W),
        out_shape=jax.ShapeDtypeStruct((b, s, M_V_W), BF16),
        scratch_shapes=[pltpu.VMEM((nb, L + 2 * SUBLANES, 2 * M_QK_W), BF16),
                        pltpu.VMEM((nb, M_HEADS, LANES, 2 * M_DV), F32),
                        pltpu.VMEM((nb, M_HEADS, 1, LANES), F32)],
        compiler_params=_params("arbitrary", "arbitrary"),
        name="mlstm",
    )(qk, v, o, ifc, ifr, conv_w, conv_b, w_mnorm)


def _swa_kernel(sink_ref, q_ref, kvp_ref, kvc_ref, bias_ref, out_ref):
    W = WINDOW
    first = pl.program_id(1) == 0
    kvp = kvp_ref[0]
    kvc = kvc_ref[0]
    keys = jnp.concatenate([kvp[:, :A_KV_W], kvc[:, :A_KV_W]], axis=0)
    vals = jnp.concatenate([kvp[:, A_KV_W:], kvc[:, A_KV_W:]], axis=0)
    lane = lax.broadcasted_iota(jnp.int32, (1, LANES), 1)
    lo_half = lane < A_HEAD_DIM
    hi_half = jnp.logical_not(lo_half)
    zero = jnp.zeros_like(vals)
    v_half = (jnp.where(lo_half, vals, zero), jnp.where(hi_half, vals, zero))

    q_pos = lax.broadcasted_iota(jnp.int32, (W, 2 * W), 0)
    k_pos = lax.broadcasted_iota(jnp.int32, (W, 2 * W), 1)
    dist = q_pos + W - k_pos
    in_window = jnp.where(dist >= 0, jnp.where(dist < W, 1, 0), 0)
    key_valid = jnp.where(k_pos >= W, 1, jnp.where(first, 0, 1))
    valid = (in_window * key_valid) > 0

    for j in range(A_HEADS // 2):
        qt = q_ref[0, :, LANES * j:LANES * (j + 1)]
        acc = jnp.zeros((W, LANES), F32)
        for p in range(2):
            h = A_HEAD_ORDER[2 * j + p]
            qm = jnp.where(lo_half if p == 0 else hi_half, qt, jnp.zeros_like(qt))
            s = _dot(qm, keys, _NT) * (A_HEAD_DIM ** -0.5) + bias_ref[h]
            s = jnp.where(valid, s, NEG_INF)
            sink = sink_ref[h]
            m = jnp.maximum(jnp.max(s, axis=-1, keepdims=True), sink)
            e = jnp.exp(s - m)
            denom = jnp.sum(e, axis=-1, keepdims=True) + jnp.exp(sink - m)
            acc = acc + _dot(e.astype(BF16), v_half[p]) / denom
        out_ref[0, :, LANES * j:LANES * (j + 1)] = acc.astype(BF16)


def _swa_call(sinks, qa, kva, bias):
    b, s, _ = qa.shape
    W = WINDOW
    return pl.pallas_call(
        _swa_kernel,
        grid=(b, s // W),
        in_specs=[pl.BlockSpec(memory_space=pltpu.SMEM),
                  pl.BlockSpec((1, W, A_Q_W), lambda i, j: (i, j, 0)),
                  pl.BlockSpec((1, W, 2 * A_KV_W), lambda i, j: (i, jnp.maximum(j - 1, 0), 0)),
                  pl.BlockSpec((1, W, 2 * A_KV_W), lambda i, j: (i, j, 0)),
                  pl.BlockSpec(bias.shape, lambda i, j: (0, 0, 0))],
        out_specs=pl.BlockSpec((1, W, A_Q_W), lambda i, j: (i, j, 0)),
        out_shape=jax.ShapeDtypeStruct((b, s, A_Q_W), BF16),
        compiler_params=_params("arbitrary", "arbitrary"),
        name="swa",
    )(sinks, qa, kva, kva, bias)


def _post_kernel(hm_ref, ha_ref, g_ref, x_ref, mod_ref, wbm_ref, wba_ref, wo_ref, wn2_ref,
                 wrh_ref, wrl_ref, rb_ref,
                 x1_ref, pay_ref, cls_ref, rank_ref, cnt_ref, carry):
    tm = x_ref.shape[1]

    @pl.when((pl.program_id(0) == 0) & (pl.program_id(1) == 0))
    def _():
        carry[...] = jnp.zeros(carry.shape, F32)

    g = g_ref[0]
    pm = _dot(hm_ref[0], wbm_ref[...])
    pa = _dot(ha_ref[0], wba_ref[...])
    merged = (_sigmoid(g[:, :D_MODEL].astype(F32)) * pm
              + _sigmoid(g[:, D_MODEL:].astype(F32)) * pa)
    mo = _dot(merged.astype(BF16), wo_ref[...])
    x1 = x_ref[0] + mod_ref[0, 2:3, :] * mo
    x1_ref[0] = x1
    h2 = _rms(x1) * wn2_ref[...] * (1.0 + mod_ref[0, 4:5, :]) + mod_ref[0, 3:4, :]
    pay_ref[0, :, :D_MODEL] = h2

    hh, hl = _split2(h2)
    wrh = wrh_ref[...]
    logits = _dot(wrh, hh, _NT) + _dot(wrh, hl, _NT) + _dot(wrl_ref[...], hh, _NT)
    scores = _sigmoid(logits)
    sel = scores + rb_ref[...]

    def row(a, r):
        return a[r:r + 1, :]

    grp = []
    for gi in range(N_GROUPS):
        v = [row(sel, gi * EXPERTS_PER_GROUP + k) for k in range(EXPERTS_PER_GROUP)]
        best = v[0] + v[1]
        for a, b in _PAIRS[1:]:
            best = jnp.maximum(best, v[a] + v[b])
        grp.append(best)
    gbest = grp[0]
    gsel = jnp.zeros_like(gbest)
    for gi in range(1, N_GROUPS):
        take = grp[gi] > gbest
        gbest = jnp.where(take, grp[gi], gbest)
        gsel = jnp.where(take, float(gi), gsel)

    sv, gv = [], []
    for k in range(EXPERTS_PER_GROUP):
        s_k = row(sel, k)
        g_k = row(scores, k)
        for gi in range(1, N_GROUPS):
            hit = gsel == float(gi)
            s_k = jnp.where(hit, row(sel, gi * EXPERTS_PER_GROUP + k), s_k)
            g_k = jnp.where(hit, row(scores, gi * EXPERTS_PER_GROUP + k), g_k)
        sv.append(s_k)
        gv.append(g_k)

    def argmax4(vals):
        bv, bi = vals[0], jnp.zeros_like(vals[0])
        for k in range(1, EXPERTS_PER_GROUP):
            take = vals[k] > bv
            bv = jnp.where(take, vals[k], bv)
            bi = jnp.where(take, float(k), bi)
        return bi

    i1 = argmax4(sv)
    i2 = argmax4([jnp.where(i1 == float(k), -jnp.inf, sv[k]) for k in range(EXPERTS_PER_GROUP)])
    w1 = jnp.zeros_like(i1)
    w2 = jnp.zeros_like(i1)
    for k in range(EXPERTS_PER_GROUP):
        w1 = jnp.where(i1 == float(k), gv[k], w1)
        w2 = jnp.where(i2 == float(k), gv[k], w2)
    wsum = w1 + w2
    w1 = w1 / wsum
    w2 = w2 / wsum
    code = i1 * float(EXPERTS_PER_GROUP) + i2
    pair_idx = jnp.zeros_like(code)
    gate_a, gate_b = w1, w2
    for k, (a, b) in enumerate(_CLASS_PAIRS):
        fwd = code == float(a * EXPERTS_PER_GROUP + b)
        rev = code == float(b * EXPERTS_PER_GROUP + a)
        pair_idx = jnp.where(fwd, float(k), jnp.where(rev, float(k), pair_idx))
        gate_a = jnp.where(rev, w2, gate_a)
        gate_b = jnp.where(rev, w1, gate_b)
    cls_f = gsel * float(PAIRS_PER_GROUP) + pair_idx
    cls_ref[0] = cls_f.astype(jnp.int32)

    grow = lax.broadcasted_iota(jnp.int32, (LANES, tm), 0)
    gmat = jnp.where(grow == 0, gate_a, jnp.where(grow == 1, gate_b, 0.0))
    pay_ref[0, :, D_MODEL:] = gmat.T

    crow = lax.broadcasted_iota(jnp.int32, (CLASS_ROWS, tm), 0).astype(F32)
    onehot = crow == cls_f
    r_i = lax.broadcasted_iota(jnp.int32, (tm, tm), 0)
    c_i = lax.broadcasted_iota(jnp.int32, (tm, tm), 1)
    upper = jnp.where(r_i <= c_i, 1.0, 0.0).astype(BF16)
    cum = _dot(jnp.where(onehot, 1.0, 0.0).astype(BF16), upper)
    before = carry[:, 0:1]
    rank = jnp.sum(jnp.where(onehot, cum - 1.0 + before, 0.0), axis=0, keepdims=True)
    rank_ref[0] = rank.astype(jnp.int32)
    total = before + cum[:, tm - 1:tm]
    carry[...] = jnp.broadcast_to(total, carry.shape)
    cnt_ref[...] = jnp.broadcast_to(total, cnt_ref.shape)


def _post_call(hm, ha, g, x, mod, wbm, wba, wo, wn2, wrh, wrl, rb):
    b, s, d = x.shape
    tm = ROUTER_TILE
    tok = lambda w_: pl.BlockSpec((1, tm, w_), lambda i, j: (i, j, 0))
    full = lambda a: pl.BlockSpec(a.shape, lambda i, j: (0,) * a.ndim)
    lanes = pl.BlockSpec((1, 1, tm), lambda i, j: (i, 0, j))
    return pl.pallas_call(
        _post_kernel,
        grid=(b, s // tm),
        in_specs=[tok(M_V_W), tok(A_Q_W), tok(2 * d), tok(d),
                  pl.BlockSpec((1, 6, d), lambda i, j: (i, 0, 0)),
                  full(wbm), full(wba), full(wo), full(wn2), full(wrh), full(wrl), full(rb)],
        out_specs=[tok(d), tok(PAYLOAD_W), lanes, lanes,
                   pl.BlockSpec((CLASS_ROWS, LANES), lambda i, j: (0, 0))],
        out_shape=[jax.ShapeDtypeStruct((b, s, d), F32),
                   jax.ShapeDtypeStruct((b, s, PAYLOAD_W), F32),
                   jax.ShapeDtypeStruct((b, 1, s), jnp.int32),
                   jax.ShapeDtypeStruct((b, 1, s), jnp.int32),
                   jax.ShapeDtypeStruct((CLASS_ROWS, LANES), F32)],
        scratch_shapes=[pltpu.VMEM((CLASS_ROWS, LANES), F32)],
        compiler_params=_params("arbitrary", "arbitrary"),
        name="merge_outproj_router",
    )(hm, ha, g, x, mod, wbm, wba, wo, wn2, wrh, wrl, rb)


def _zero_fill(off_s, cnt_s, nu_s, xs_out, zbuf, zsem, *, wait):
    chunk = zbuf.shape[0]

    def zero_rows(first, n):
        cp = pltpu.make_async_copy(zbuf.at[pl.ds(0, n), :], xs_out.at[pl.ds(first, n), :], zsem)
        cp.wait() if wait else cp.start()

    def per_class(c, carry):
        n = cnt_s[c]
        fill = (-n) & (EXPERT_TILE - 1)
        head = fill & (SUBLANES - 1)
        for k in range(SUBLANES - 1):
            pl.when(k < head)(lambda k=k: zero_rows(off_s[c] + n + k, 1))
        cur = off_s[c] + n + head
        p = chunk
        while p >= SUBLANES:
            pl.when((fill & p) != 0)(lambda cur=cur, p=p: zero_rows(pl.multiple_of(cur, SUBLANES), p))
            cur = cur + (fill & p)
            p //= 2
        return carry

    lax.fori_loop(0, N_CLASSES, per_class, 0)

    def per_chunk(r, carry):
        zero_rows(pl.multiple_of(r * chunk, SUBLANES), chunk)
        return carry

    per_tile = EXPERT_TILE // chunk
    lax.fori_loop(nu_s[0] * per_tile, (xs_out.shape[0] // EXPERT_TILE) * per_tile, per_chunk, 0)


def _dispatch_kernel(pos_s, off_s, cnt_s, nu_s, pay_ref, xs_out, zbuf, sem, zsem):
    @pl.when(pl.program_id(0) == 0)
    def _():
        zbuf[...] = jnp.zeros(zbuf.shape, F32)
        _zero_fill(off_s, cnt_s, nu_s, xs_out, zbuf, zsem, wait=False)
        _zero_fill(off_s, cnt_s, nu_s, xs_out, zbuf, zsem, wait=True)

    first_token = pl.program_id(0) * pay_ref.shape[0] * SUBLANES
    for wait in (False, True):
        _row_copies(pos_s, first_token, xs_out, pay_ref, sem, to_sorted=True, wait=wait)


def _dispatch_call(pos, off, cnt, n_used, payload, npad):
    t8, _, width = payload.shape
    tiles = ROW_TILE // SUBLANES
    return pl.pallas_call(
        _dispatch_kernel,
        grid_spec=pltpu.PrefetchScalarGridSpec(
            num_scalar_prefetch=4,
            grid=(t8 // tiles,),
            in_specs=[pl.BlockSpec((tiles, SUBLANES, width), lambda i, *_: (i, 0, 0))],
            out_specs=pl.BlockSpec(memory_space=pl.ANY),
            scratch_shapes=[pltpu.VMEM((EXPERT_TILE // 2, width), F32),
                            pltpu.SemaphoreType.DMA(()), pltpu.SemaphoreType.DMA(())],
        ),
        out_shape=jax.ShapeDtypeStruct((npad, width), F32),
        compiler_params=_params("arbitrary"),
        name="moe_dispatch",
    )(pos, off, cnt, n_used, payload)


def _moe_kernel(ea_s, eb_s, nu_s, x_ref, wga, wua, wda, wgb, wub, wdb, y_ref):
    del ea_s, eb_s

    @pl.when(pl.program_id(0) < nu_s[0])
    def _():
        x = x_ref[:, :D_MODEL].astype(BF16)
        gate_a = x_ref[:, D_MODEL:D_MODEL + 1]
        gate_b = x_ref[:, D_MODEL + 1:D_MODEL + 2]

        def expert(wg, wu, wd):
            hg = _dot(x, wg[0])
            hu = _dot(x, wu[0])
            act = (hg * _sigmoid(hg) * hu).astype(BF16)
            return _dot(act, wd[0])

        y_ref[...] = gate_a * expert(wga, wua, wda) + gate_b * expert(wgb, wub, wdb)

    @pl.when(pl.program_id(0) >= nu_s[0])
    def _():
        y_ref[...] = jnp.zeros(y_ref.shape, F32)


def _moe_call(tile_ea, tile_eb, n_used, xs, wg, wu, wd, layer):
    npad = xs.shape[0]
    te = EXPERT_TILE
    last = lambda i, nu: jnp.minimum(i, nu[0] - 1)
    first = layer * N_EXPERTS
    pick = lambda tab: (lambda i, ea, eb, nu: (first + (ea, eb)[tab][i], 0, 0))
    up = lambda tab: pl.BlockSpec((1, D_MODEL, D_EXPERT), pick(tab))
    down = lambda tab: pl.BlockSpec((1, D_EXPERT, D_MODEL), pick(tab))
    return pl.pallas_call(
        _moe_kernel,
        grid_spec=pltpu.PrefetchScalarGridSpec(
            num_scalar_prefetch=3,
            grid=(npad // te,),
            in_specs=[pl.BlockSpec((te, PAYLOAD_W), lambda i, ea, eb, nu: (last(i, nu), 0)),
                      up(0), up(0), down(0), up(1), up(1), down(1)],
            out_specs=pl.BlockSpec((te, D_MODEL), lambda i, ea, eb, nu: (i, 0)),
        ),
        out_shape=jax.ShapeDtypeStruct((npad, D_MODEL), F32),
        compiler_params=_params("arbitrary"),
        name="moe_experts",
    )(tile_ea, tile_eb, n_used, xs, wg, wu, wd, wg, wu, wd)


def _combine_kernel(pos_s, ys_hbm, x1_ref, mod_ref, wf_ref, out_ref, ybuf, sem):
    x2 = _moe_residual(pos_s, ys_hbm, x1_ref, mod_ref, ybuf, sem)
    out_ref[0] = _rms(x2) * wf_ref[...]


def _combine_call(pos, ys, x1, mod, wf):
    b, s, d = x1.shape
    rows = ROW_TILE
    return pl.pallas_call(
        _combine_kernel,
        grid_spec=pltpu.PrefetchScalarGridSpec(
            num_scalar_prefetch=1,
            grid=(b, s // rows),
            in_specs=[pl.BlockSpec(memory_space=pl.ANY),
                      pl.BlockSpec((1, rows, d), lambda i, j, *_: (i, j, 0)),
                      pl.BlockSpec((1, 6, d), lambda i, j, *_: (i, 0, 0)),
                      pl.BlockSpec((1, d), lambda i, j, *_: (0, 0))],
            out_specs=pl.BlockSpec((1, rows, d), lambda i, j, *_: (i, j, 0)),
            scratch_shapes=[pltpu.VMEM((2, rows // SUBLANES, SUBLANES, d), F32),
                            pltpu.SemaphoreType.DMA((2,))],
        ),
        out_shape=jax.ShapeDtypeStruct((b, s, d), F32),
        compiler_params=_params("arbitrary", "arbitrary"),
        name="moe_combine",
    )(pos, ys, x1, mod, wf)


def _t5_bucket(n):
    max_exact = N_BUCKETS // 2
    large = max_exact + (np.log(np.maximum(n, 1) / max_exact)
                         / np.log(MAX_DISTANCE / max_exact)
                         * (N_BUCKETS - max_exact)).astype(np.int32)
    large = np.minimum(large, N_BUCKETS - 1)
    return np.where(n < max_exact, n, large).astype(np.int32)


def _bias_table(rel_bias):
    dist = np.arange(WINDOW)[:, None] + WINDOW - np.arange(2 * WINDOW)[None, :]
    bucket = _t5_bucket(np.maximum(dist, 0)).reshape(1, -1)
    onehot = (jnp.arange(N_BUCKETS, dtype=jnp.int32)[:, None] == jnp.asarray(bucket)).astype(F32)
    tab = jnp.dot(rel_bias.astype(F32).T, onehot, precision=lax.Precision.HIGHEST)
    return tab.reshape(A_HEADS, WINDOW, 2 * WINDOW)


def _layout_w_in(w_in_l):
    pts = np.cumsum([0, M_QK_W, M_QK_W, M_V_W, M_V_W, M_HEADS, M_HEADS,
                     A_Q_W, A_KV_W, A_KV_W, D_MODEL, D_MODEL])
    wb = w_in_l.astype(BF16)
    col = lambda k: wb[:, pts[k]:pts[k + 1]]
    head = lambda h: wb[:, pts[6] + A_HEAD_DIM * h:pts[6] + A_HEAD_DIM * (h + 1)]
    w_if = jnp.concatenate([col(4), col(5)], axis=1)
    w_if_pad = jnp.pad(w_if, ((0, 0), (0, LANES - 2 * M_HEADS)))
    w = jnp.concatenate([col(0), col(1), col(2), col(3)] + [head(h) for h in A_HEAD_ORDER]
                        + [col(7), col(8), col(9), col(10), w_if_pad], axis=1)
    return w, w_if.T


def _tile_tables(counts, n_tiles):
    te = EXPERT_TILE
    tiles = (counts + te - 1) // te
    ends = jnp.cumsum(tiles)
    off = (ends - tiles) * te
    n_used = ends[-1]
    tile_idx = jnp.minimum(jnp.arange(n_tiles, dtype=jnp.int32), n_used - 1)
    tile_cls = jnp.sum((ends[None, :] <= tile_idx[:, None]).astype(jnp.int32), axis=1)
    tile_cls = jnp.minimum(tile_cls, N_CLASSES - 1)
    off_pad = jnp.pad(off, (0, CLASS_ROWS - N_CLASSES)).astype(jnp.int32)
    return (off_pad, jnp.asarray(_CLASS_EA)[tile_cls], jnp.asarray(_CLASS_EB)[tile_cls],
            n_used.reshape(1).astype(jnp.int32))


def kernel(x, c, w_ada, b_ada, w_norm1, w_in, conv_w, conv_b, b_igate, b_fgate, w_mnorm, sinks, rel_bias,
           w_br_m, w_br_a, w_out, w_norm2, w_router, router_bias, w_gate_e, w_up_e, w_down_e, w_final):
    b, s, d = x.shape
    depth = w_ada.shape[0]
    t = b * s
    n_tiles = t // EXPERT_TILE + N_CLASSES
    npad = n_tiles * EXPERT_TILE

    mod_all = _ada_call(jnp.pad(c, ((0, 8 - b), (0, 0))), w_ada, b_ada)[:, :b]
    bias_tab = _bias_table(rel_bias)
    wrt = w_router.T
    wrh = wrt.astype(BF16)
    wrl = (wrt - wrh.astype(F32)).astype(BF16)
    rb = router_bias.reshape(N_EXPERTS, 1).astype(F32)
    row = lambda v: v.reshape(1, -1).astype(F32)

    wg_all, wu_all, wd_all = _cast_call(w_gate_e), _cast_call(w_up_e), _cast_call(w_down_e)

    moe = None
    for l in range(depth):
        mod = mod_all[l].reshape(b, 6, d)
        w_l, wift = _layout_w_in(w_in[l])
        gate_bias = jnp.concatenate([b_igate[l], b_fgate[l]]).astype(F32)
        bcol = jnp.pad(gate_bias, (0, LANES - 2 * M_HEADS)).reshape(1, LANES)
        brow = gate_bias.reshape(2 * M_HEADS, 1)

        proj_args = (mod, row(w_norm1[l]), w_l, wift, bcol, brow)
        if moe is None:
            qk, v, o, qa, kva, g, ifc, ifr = _inproj_call(x, *proj_args)
        else:
            x, qk, v, o, qa, kva, g, ifc, ifr = _combine_inproj_call(*moe, *proj_args)
        hm = _mlstm_call(qk, v, o, ifc, ifr, conv_w[l].astype(F32), row(conv_b[l]), row(w_mnorm[l]))
        ha = _swa_call(sinks[l].astype(F32), qa, kva, bias_tab)

        wba = jnp.concatenate([w_br_a[l][A_HEAD_DIM * h:A_HEAD_DIM * (h + 1)] for h in A_HEAD_ORDER])
        x1, payload, cls, rank, cnt = _post_call(
            hm, ha, g, x, mod, w_br_m[l].astype(BF16), wba.astype(BF16), w_out[l].astype(BF16),
            row(w_norm2[l]), wrh, wrl, rb)

        counts = cnt[:, 0].astype(jnp.int32)
        off, tile_ea, tile_eb, n_used = _tile_tables(counts[:N_CLASSES], n_tiles)
        hit = cls.reshape(t, 1) == jnp.arange(CLASS_ROWS, dtype=jnp.int32)[None, :]
        pos = jnp.sum(jnp.where(hit, off[None, :], 0), axis=1) + rank.reshape(t)
        xs = _dispatch_call(pos, off, counts, n_used,
                            payload.reshape(t // SUBLANES, SUBLANES, PAYLOAD_W), npad)
        ys = _moe_call(tile_ea, tile_eb, n_used, xs, wg_all, wu_all, wd_all, layer=l)
        moe = (pos, ys, x1, mod)
    return _combine_call(*moe, row(w_final))
```

```python
import numpy as np
import jax
import jax.numpy as jnp
from jax import lax
from jax.experimental import pallas as pl
from jax.experimental.pallas import tpu as pltpu

F32 = jnp.float32
BF16 = jnp.bfloat16

D_MODEL = 1024
M_HEADS = 4
M_DQK = 64
M_DV = 128
M_CONV = 4
M_QK_W = M_HEADS * M_DQK
M_V_W = M_HEADS * M_DV
A_HEADS = 8
A_KV_HEADS = 2
A_HEAD_DIM = 64
A_Q_W = A_HEADS * A_HEAD_DIM
A_KV_W = A_KV_HEADS * A_HEAD_DIM
WINDOW = 128
N_BUCKETS = 32
MAX_DISTANCE = 128
N_EXPERTS = 16
N_GROUPS = 4
EXPERTS_PER_GROUP = N_EXPERTS // N_GROUPS
D_EXPERT = 512
EPS = 1e-6
NEG_INF = -1e30

LANES = 128
SUBLANES = 8
MLSTM_CHUNK = 128
TOKEN_TILE = 512
ROUTER_TILE = 512
EXPERT_TILE = 256
ROW_TILE = 512
PAIRS_PER_GROUP = 6
N_CLASSES = N_GROUPS * PAIRS_PER_GROUP
CLASS_ROWS = 32
PAYLOAD_W = D_MODEL + LANES
VMEM_LIMIT = 48 * 1024 * 1024

C_Q = 0
C_V = C_Q + M_QK_W
C_O = C_V + M_V_W
C_QA = C_O + M_V_W
C_KVA = C_QA + A_Q_W
C_G = C_KVA + 2 * A_KV_W
C_IF = C_G + 2 * D_MODEL
C_END = C_IF + LANES

A_HEAD_ORDER = (0, 4, 1, 5, 2, 6, 3, 7)

_PAIRS = [(a, b) for a in range(EXPERTS_PER_GROUP) for b in range(a + 1, EXPERTS_PER_GROUP)]
_CLASS_PAIRS = ((0, 1), (2, 1), (2, 0), (3, 0), (3, 1), (3, 2))
assert sorted(tuple(sorted(p)) for p in _CLASS_PAIRS) == _PAIRS
_CLASS_EA = np.array([g * EXPERTS_PER_GROUP + a for g in range(N_GROUPS) for a, _ in _CLASS_PAIRS], np.int32)
_CLASS_EB = np.array([g * EXPERTS_PER_GROUP + b for g in range(N_GROUPS) for _, b in _CLASS_PAIRS], np.int32)

_NT = (((1,), (1,)), ((), ()))


def _sigmoid(x):
    return 1.0 / (1.0 + jnp.exp(-x))


def _log_sigmoid(x):
    return -(jnp.maximum(-x, 0.0) + jnp.log1p(jnp.exp(-jnp.abs(x))))


def _dot(a, b, dims=None):
    if dims is None:
        return jnp.dot(a, b, preferred_element_type=F32)
    return lax.dot_general(a, b, dims, preferred_element_type=F32)


def _split2(a):
    hi = a.astype(BF16)
    lo = (a - hi.astype(F32)).astype(BF16)
    return hi, lo


def _split3(a):
    hi = a.astype(BF16)
    r = a - hi.astype(F32)
    mid = r.astype(BF16)
    lo = (r - mid.astype(F32)).astype(BF16)
    return hi, mid, lo


def _dot_hi(a, b, dims=None):
    ah, al = _split2(a)
    bh, bl = _split2(b)
    return _dot(ah, bh, dims) + _dot(ah, bl, dims) + _dot(al, bh, dims)


def _rms(x):
    return x * lax.rsqrt(jnp.mean(x * x, axis=-1, keepdims=True) + EPS)


def _params(*sem):
    return pltpu.CompilerParams(dimension_semantics=sem, vmem_limit_bytes=VMEM_LIMIT)


def _cast_kernel(w_ref, o_ref):
    o_ref[...] = w_ref[...].astype(BF16)


def _cast_call(w):
    w = w.reshape((-1,) + w.shape[2:])
    e, a, b = w.shape
    blk = 2
    spec = pl.BlockSpec((blk, a, b), lambda i: (i, 0, 0))
    return pl.pallas_call(
        _cast_kernel,
        grid=(e // blk,),
        in_specs=[spec],
        out_specs=spec,
        out_shape=jax.ShapeDtypeStruct(w.shape, BF16),
        compiler_params=_params("arbitrary"),
        name="expert_weights_bf16",
    )(w)


def _ada_kernel(c_ref, w_ref, b_ref, o_ref):
    c = c_ref[...]
    cond = c * _sigmoid(c)
    o_ref[0] = _dot_hi(cond, w_ref[0]) + b_ref[0]


def _ada_call(c8, w_ada, b_ada):
    depth = w_ada.shape[0]
    return pl.pallas_call(
        _ada_kernel,
        grid=(depth, 6),
        in_specs=[
            pl.BlockSpec((8, D_MODEL), lambda l, j: (0, 0)),
            pl.BlockSpec((1, D_MODEL, D_MODEL), lambda l, j: (l, 0, j)),
            pl.BlockSpec((1, 1, D_MODEL), lambda l, j: (l, 0, j)),
        ],
        out_specs=pl.BlockSpec((1, 8, D_MODEL), lambda l, j: (l, 0, j)),
        out_shape=jax.ShapeDtypeStruct((depth, 8, 6 * D_MODEL), F32),
        compiler_params=_params("arbitrary", "arbitrary"),
        name="adaln_mod",
    )(c8, w_ada, b_ada.reshape(depth, 1, 6 * D_MODEL))


def _project(x, mod_ref, wn_ref, w_ref, wift_ref, bcol_ref, brow_ref, outs):
    q_ref, kt_ref, v_ref, o_ref, qa_ref, kva_ref, g_ref, ifc_ref, ifr_ref = outs
    h = _rms(x) * wn_ref[...] * (1.0 + mod_ref[0, 1:2, :]) + mod_ref[0, 0:1, :]
    hb = h.astype(BF16)

    def seg(a, b):
        return _dot(hb, w_ref[:, a:b])

    q_ref[0] = seg(C_Q, C_V).astype(BF16)
    v_ref[0] = seg(C_V, C_O).astype(BF16)
    o_ref[0] = seg(C_O, C_QA).astype(BF16)
    qa_ref[0] = seg(C_QA, C_KVA).astype(BF16)
    kva_ref[0] = seg(C_KVA, C_G).astype(BF16)
    g_ref[0] = seg(C_G, C_IF).astype(BF16)
    zc = seg(C_IF, C_END) + bcol_ref[...]
    lane = lax.broadcasted_iota(jnp.int32, zc.shape, 1)
    ifc_ref[0] = jnp.where((lane >= M_HEADS) & (lane < 2 * M_HEADS), _log_sigmoid(zc), zc)
    zt = _dot(wift_ref[...], hb, _NT)
    kt_ref[0] = zt[:M_QK_W].astype(BF16)
    zr = zt[M_QK_W:M_QK_W + 2 * M_HEADS] + brow_ref[...]
    row = lax.broadcasted_iota(jnp.int32, zr.shape, 0)
    ifr_ref[0] = jnp.where(row >= M_HEADS, _log_sigmoid(zr), zr)


def _inproj_kernel(x_ref, mod_ref, wn_ref, w_ref, wift_ref, bcol_ref, brow_ref, *outs):
    _project(x_ref[0], mod_ref, wn_ref, w_ref, wift_ref, bcol_ref, brow_ref, outs)


def _row_copies(pos_s, first_token, sorted_hbm, tiles, sem, *, to_sorted, wait):
    def copy(i, k, pos):
        pair = (tiles.at[i, pl.ds(k, 1), :], sorted_hbm.at[pl.ds(pos, 1), :])
        return pltpu.make_async_copy(*(pair if to_sorted else pair[::-1]), sem)

    def body(i, carry):
        for k in range(SUBLANES):
            if wait:
                copy(0, k, 0).wait()
            else:
                copy(i, k, pos_s[first_token + i * SUBLANES + k]).start(priority=k % 2)
        return carry

    lax.fori_loop(0, tiles.shape[0], body, 0, unroll=2)


def _moe_residual(pos_s, ys_hbm, x1_ref, mod_ref, ybuf, sem):
    rows = x1_ref.shape[1]
    step = pl.program_id(0) * pl.num_programs(1) + pl.program_id(1)
    n_steps = pl.num_programs(0) * pl.num_programs(1)
    slot = lax.rem(step, 2)

    def fetch(first_token, slot_, wait):
        _row_copies(pos_s, first_token, ys_hbm, ybuf.at[slot_], sem.at[slot_], to_sorted=False, wait=wait)

    @pl.when(step == 0)
    def _():
        fetch(0, 0, False)

    @pl.when(step + 1 < n_steps)
    def _():
        fetch((step + 1) * rows, 1 - slot, False)

    fetch(0, slot, True)
    y = ybuf[slot].reshape(rows, ybuf.shape[-1])
    return x1_ref[0] + mod_ref[0, 5:6, :] * y


def _combine_inproj_kernel(pos_s, ys_hbm, x1_ref, modp_ref, mod_ref, wn_ref, w_ref,
                           wift_ref, bcol_ref, brow_ref, xo_ref, *rest):
    outs, (ybuf, sem) = rest[:-2], rest[-2:]
    x = _moe_residual(pos_s, ys_hbm, x1_ref, modp_ref, ybuf, sem)
    xo_ref[0] = x
    _project(x, mod_ref, wn_ref, w_ref, wift_ref, bcol_ref, brow_ref, outs)


def _inproj_specs(b, s, tm):
    tok = lambda w_: pl.BlockSpec((1, tm, w_), lambda i, j, *_: (i, j, 0))
    sds = lambda w_, dt: jax.ShapeDtypeStruct((b, s, w_), dt)
    rows = lambda n: pl.BlockSpec((1, n, tm), lambda i, j, *_: (i, 0, j))
    out_specs = [tok(M_QK_W), rows(M_QK_W), tok(M_V_W), tok(M_V_W), tok(A_Q_W), tok(2 * A_KV_W),
                 tok(2 * D_MODEL), tok(LANES), rows(2 * M_HEADS)]
    out_shape = [sds(M_QK_W, BF16), jax.ShapeDtypeStruct((b, M_QK_W, s), BF16), sds(M_V_W, BF16),
                 sds(M_V_W, BF16), sds(A_Q_W, BF16), sds(2 * A_KV_W, BF16), sds(2 * D_MODEL, BF16),
                 sds(LANES, F32), jax.ShapeDtypeStruct((b, 2 * M_HEADS, s), F32)]
    return tok, out_specs, out_shape


def _inproj_call(x, mod, wn, w, wift, bcol, brow):
    b, s, d = x.shape
    tm = TOKEN_TILE
    tok, out_specs, out_shape = _inproj_specs(b, s, tm)
    full = lambda a: pl.BlockSpec(a.shape, lambda i, j: (0,) * a.ndim)
    return pl.pallas_call(
        _inproj_kernel,
        grid=(b, s // tm),
        in_specs=[tok(d), pl.BlockSpec((1, 6, d), lambda i, j: (i, 0, 0)), full(wn), full(w),
                  full(wift), full(bcol), full(brow)],
        out_specs=out_specs,
        out_shape=out_shape,
        compiler_params=_params("arbitrary", "arbitrary"),
        name="norm1_inproj",
    )(x, mod, wn, w, wift, bcol, brow)


def _combine_inproj_call(pos, ys, x1, mod_prev, mod, wn, w, wift, bcol, brow):
    b, s, d = x1.shape
    tm = TOKEN_TILE
    tok, out_specs, out_shape = _inproj_specs(b, s, tm)
    full = lambda a: pl.BlockSpec(a.shape, lambda i, j, *_: (0,) * a.ndim)
    modspec = pl.BlockSpec((1, 6, d), lambda i, j, *_: (i, 0, 0))
    return pl.pallas_call(
        _combine_inproj_kernel,
        grid_spec=pltpu.PrefetchScalarGridSpec(
            num_scalar_prefetch=1,
            grid=(b, s // tm),
            in_specs=[pl.BlockSpec(memory_space=pl.ANY), tok(d), modspec, modspec, full(wn), full(w),
                      full(wift), full(bcol), full(brow)],
            out_specs=[tok(d)] + out_specs,
            scratch_shapes=[pltpu.VMEM((2, tm // SUBLANES, SUBLANES, d), F32),
                            pltpu.SemaphoreType.DMA((2,))],
        ),
        out_shape=[jax.ShapeDtypeStruct((b, s, d), F32)] + out_shape,
        compiler_params=_params("arbitrary", "arbitrary"),
        name="combine_norm1_inproj",
    )(pos, ys, x1, mod_prev, mod, wn, w, wift, bcol, brow)


def _mlstm_kernel(q_ref, kt_ref, v_ref, o_ref, ifc_ref, ifr_ref, cwq_ref, cbq_ref, cwk_ref, cbk_ref, wn_ref,
                  out_ref, qbuf, kbuf, cstate, mstate):
    L = q_ref.shape[1]
    pad = qbuf.shape[0] - L
    heads = range(M_HEADS)

    @pl.when(pl.program_id(1) == 0)
    def _():
        qbuf[...] = jnp.zeros(qbuf.shape, qbuf.dtype)
        kbuf[...] = jnp.zeros(kbuf.shape, kbuf.dtype)
        cstate[...] = jnp.zeros(cstate.shape, F32)
        mstate[...] = jnp.zeros(mstate.shape, F32)

    xq = q_ref[0]
    qbuf[pad:pad + L, :] = xq
    q_all = qbuf[...]
    qbuf[0:pad, :] = xq[L - pad:L, :]
    xk = kt_ref[0]
    kbuf[:, L:] = xk
    k_all = kbuf[...]
    kbuf[:, :L] = xk
    r_s = lax.broadcasted_iota(jnp.int32, (L, L + pad), 0)
    c_s = lax.broadcasted_iota(jnp.int32, (L, L + pad), 1)
    r_t = lax.broadcasted_iota(jnp.int32, (2 * L, L), 0)
    c_t = lax.broadcasted_iota(jnp.int32, (2 * L, L), 1)
    acc_q = cbq_ref[...] + cwq_ref[M_CONV - 1:M_CONV, :] * xq.astype(F32)
    acc_k = cbk_ref[...] + cwk_ref[M_CONV - 1] * xk.astype(F32)
    for j in range(M_CONV - 1):
        back = M_CONV - 1 - j
        shift_q = jnp.where(c_s == r_s + (pad - back), 1.0, 0.0).astype(BF16)
        acc_q = acc_q + cwq_ref[j:j + 1, :] * _dot(shift_q, q_all)
        shift_k = jnp.where(r_t == c_t + (L - back), 1.0, 0.0).astype(BF16)
        acc_k = acc_k + cwk_ref[j] * _dot(k_all, shift_k)
    q = acc_q * _sigmoid(acc_q) * (M_DQK ** -0.5)
    kf = acc_k * _sigmoid(acc_k)
    kb = kf.astype(BF16)
    vb = v_ref[0]

    r_i = lax.broadcasted_iota(jnp.int32, (L, L), 0)
    c_i = lax.broadcasted_iota(jnp.int32, (L, L), 1)
    causal = c_i <= r_i
    tril = jnp.where(causal, 1.0, 0.0).astype(BF16)
    triu = jnp.where(r_i <= c_i, 1.0, 0.0).astype(BF16)

    icol = ifc_ref[0]
    irow = ifr_ref[0]
    ch, cm, cl = _split3(icol)
    bcum_col = _dot(tril, ch) + _dot(tril, cm) + _dot(tril, cl)
    rh, rm, rl = _split3(irow)
    bcum_row = _dot(rh, triu) + _dot(rm, triu) + _dot(rl, triu)

    lane = lax.broadcasted_iota(jnp.int32, (1, LANES), 1)
    lo_half = lane < M_DQK
    hi_half = jnp.logical_not(lo_half)
    ones = jnp.ones((L, LANES), BF16)
    pair = lambda h: slice(LANES * (h // 2), LANES * (h // 2 + 1))

    cst = [cstate[h] for h in heads]
    m_prev = [mstate[h][:, 0:1] for h in heads]
    qmask = [jnp.where(lo_half if h % 2 == 0 else hi_half, q[:, pair(h)], 0.0) for h in heads]
    vaug = [jnp.concatenate([vb[:, M_DV * h:M_DV * (h + 1)], ones], axis=1) for h in heads]
    bc_col = [bcum_col[:, M_HEADS + h:M_HEADS + h + 1] for h in heads]
    bc_row = [bcum_row[M_HEADS + h:M_HEADS + h + 1, :] for h in heads]
    i_row = [irow[h:h + 1, :] for h in heads]
    b_last = [bc_row[h][:, L - 1:L] for h in heads]

    s = [_dot(qmask[h].astype(BF16), kb[pair(h), :]) for h in heads]

    dlog = [jnp.where(causal, bc_col[h] - bc_row[h] + i_row[h], NEG_INF) for h in heads]
    m_inter = [bc_col[h] + m_prev[h] for h in heads]
    m_row = [jnp.maximum(m_inter[h], jnp.max(dlog[h], axis=-1, keepdims=True)) for h in heads]
    sc = [(s[h] * jnp.exp(dlog[h] - m_row[h])).astype(BF16) for h in heads]
    qi = [(jnp.exp(m_inter[h] - m_row[h]) * qmask[h]).astype(BF16) for h in heads]

    num = [_dot(sc[h], vaug[h]) + _dot(qi[h], cst[h].astype(BF16)) for h in heads]
    for h in heads:
        den = num[h][:, M_DV:]
        hh = num[h][:, :M_DV] / jnp.maximum(jnp.abs(den), jnp.exp(-m_row[h]))
        hn = _rms(hh) * wn_ref[:, M_DV * h:M_DV * (h + 1)]
        og = _sigmoid(o_ref[0, :, M_DV * h:M_DV * (h + 1)].astype(F32))
        out_ref[0, :, M_DV * h:M_DV * (h + 1)] = (hn * og).astype(BF16)

    w_state = [b_last[h] - bc_row[h] + i_row[h] for h in heads]
    m_loc = [jnp.max(w_state[h], axis=-1, keepdims=True) for h in heads]
    ak = [(jnp.exp(w_state[h] - m_loc[h]) * kf[pair(h), :]).astype(BF16) for h in heads]
    c_loc = [_dot(ak[h], vaug[h]) for h in heads]
    for h in heads:
        m_new = jnp.maximum(b_last[h] + m_prev[h], m_loc[h])
        decay = jnp.exp(b_last[h] + m_prev[h] - m_new)
        fresh = jnp.exp(m_loc[h] - m_new)
        cstate[h] = decay * cst[h] + fresh * c_loc[h]
        mstate[h] = jnp.broadcast_to(m_new, (1, LANES))


def _mlstm_call(q, kt, v, o, ifc, ifr, cwq, cbq, cwk, cbk, w_mnorm):
    b, s, _ = q.shape
    L = MLSTM_CHUNK
    tok = lambda w_: pl.BlockSpec((1, L, w_), lambda i, j: (i, j, 0))
    rows = lambda n: pl.BlockSpec((1, n, L), lambda i, j: (i, 0, j))
    full = lambda a: pl.BlockSpec(a.shape, lambda i, j: (0,) * a.ndim)
    return pl.pallas_call(
        _mlstm_kernel,
        grid=(b, s // L),
        in_specs=[tok(M_QK_W), rows(M_QK_W), tok(M_V_W), tok(M_V_W), tok(LANES), rows(2 * M_HEADS),
                  full(cwq), full(cbq), full(cwk), full(cbk), full(w_mnorm)],
        out_specs=tok(M_V_W),
        out_shape=jax.ShapeDtypeStruct((b, s, M_V_W), BF16),
        scratch_shapes=[pltpu.VMEM((L + 2 * SUBLANES, M_QK_W), BF16),
                        pltpu.VMEM((M_QK_W, 2 * L), BF16),
                        pltpu.VMEM((M_HEADS, LANES, 2 * M_DV), F32),
                        pltpu.VMEM((M_HEADS, 1, LANES), F32)],
        compiler_params=_params("arbitrary", "arbitrary"),
        name="mlstm",
    )(q, kt, v, o, ifc, ifr, cwq, cbq, cwk, cbk, w_mnorm)


def _swa_kernel(sink_ref, q_ref, kvp_ref, kvc_ref, bias_ref, out_ref):
    W = WINDOW
    first = pl.program_id(1) == 0
    kvp = kvp_ref[0]
    kvc = kvc_ref[0]
    keys = jnp.concatenate([kvp[:, :A_KV_W], kvc[:, :A_KV_W]], axis=0)
    vals = jnp.concatenate([kvp[:, A_KV_W:], kvc[:, A_KV_W:]], axis=0)
    lane = lax.broadcasted_iota(jnp.int32, (1, LANES), 1)
    lo_half = lane < A_HEAD_DIM
    hi_half = jnp.logical_not(lo_half)
    zero = jnp.zeros_like(vals)
    v_half = (jnp.where(lo_half, vals, zero), jnp.where(hi_half, vals, zero))
    table = jnp.where(first, 1, 0)

    slots = [(j, p) for j in range(A_HEADS // 2) for p in range(2)]
    head = {jp: A_HEAD_ORDER[2 * jp[0] + jp[1]] for jp in slots}
    scale = jnp.asarray(A_HEAD_DIM ** -0.5, BF16)
    qt = [q_ref[0, :, LANES * j:LANES * (j + 1)] * scale for j in range(A_HEADS // 2)]
    qm = {(j, p): jnp.where(lo_half if p == 0 else hi_half, qt[j], jnp.zeros_like(qt[j])) for j, p in slots}
    s = {jp: _dot(qm[jp], keys, _NT) + bias_ref[table, head[jp]] for jp in slots}
    m = {jp: jnp.maximum(jnp.max(s[jp], axis=-1, keepdims=True), sink_ref[head[jp]]) for jp in slots}
    e = {jp: jnp.exp(s[jp] - m[jp]) for jp in slots}
    denom = {jp: jnp.sum(e[jp], axis=-1, keepdims=True) + jnp.exp(sink_ref[head[jp]] - m[jp]) for jp in slots}
    pv = {(j, p): _dot(e[j, p].astype(BF16), v_half[p]) for j, p in slots}
    for j in range(A_HEADS // 2):
        out_ref[0, :, LANES * j:LANES * (j + 1)] = (pv[j, 0] / denom[j, 0] + pv[j, 1] / denom[j, 1]).astype(BF16)


def _swa_call(sinks, qa, kva, bias):
    b, s, _ = qa.shape
    W = WINDOW
    return pl.pallas_call(
        _swa_kernel,
        grid=(b, s // W),
        in_specs=[pl.BlockSpec(memory_space=pltpu.SMEM),
                  pl.BlockSpec((1, W, A_Q_W), lambda i, j: (i, j, 0)),
                  pl.BlockSpec((1, W, 2 * A_KV_W), lambda i, j: (i, jnp.maximum(j - 1, 0), 0)),
                  pl.BlockSpec((1, W, 2 * A_KV_W), lambda i, j: (i, j, 0)),
                  pl.BlockSpec(bias.shape, lambda i, j: (0,) * bias.ndim)],
        out_specs=pl.BlockSpec((1, W, A_Q_W), lambda i, j: (i, j, 0)),
        out_shape=jax.ShapeDtypeStruct((b, s, A_Q_W), BF16),
        compiler_params=_params("arbitrary", "arbitrary"),
        name="swa",
    )(sinks, qa, kva, kva, bias)


def _post_kernel(hm_ref, ha_ref, g_ref, x_ref, mod_ref, wbm_ref, wba_ref, wo_ref, wn2_ref,
                 wrh_ref, wrl_ref, rb_ref,
                 x1_ref, pay_ref, cls_ref, rank_ref, cnt_ref, carry):
    tm = x_ref.shape[1]

    @pl.when((pl.program_id(0) == 0) & (pl.program_id(1) == 0))
    def _():
        carry[...] = jnp.zeros(carry.shape, F32)

    g = g_ref[0]
    pm = _dot(hm_ref[0], wbm_ref[...])
    pa = _dot(ha_ref[0], wba_ref[...])
    merged = (_sigmoid(g[:, :D_MODEL].astype(F32)) * pm
              + _sigmoid(g[:, D_MODEL:].astype(F32)) * pa)
    mo = _dot(merged.astype(BF16), wo_ref[...])
    x1 = x_ref[0] + mod_ref[0, 2:3, :] * mo
    x1_ref[0] = x1
    h2 = _rms(x1) * wn2_ref[...] * (1.0 + mod_ref[0, 4:5, :]) + mod_ref[0, 3:4, :]
    pay_ref[0, :, :D_MODEL] = h2

    hh, hl = _split2(h2)
    wrh = wrh_ref[...]
    logits = _dot(wrh, hh, _NT) + _dot(wrh, hl, _NT) + _dot(wrl_ref[...], hh, _NT)
    scores = _sigmoid(logits)
    sel = scores + rb_ref[...]

    def row(a, r):
        return a[r:r + 1, :]

    grp = []
    for gi in range(N_GROUPS):
        v = [row(sel, gi * EXPERTS_PER_GROUP + k) for k in range(EXPERTS_PER_GROUP)]
        best = v[0] + v[1]
        for a, b in _PAIRS[1:]:
            best = jnp.maximum(best, v[a] + v[b])
        grp.append(best)
    gbest = grp[0]
    gsel = jnp.zeros_like(gbest)
    for gi in range(1, N_GROUPS):
        take = grp[gi] > gbest
        gbest = jnp.where(take, grp[gi], gbest)
        gsel = jnp.where(take, float(gi), gsel)

    sv, gv = [], []
    for k in range(EXPERTS_PER_GROUP):
        s_k = row(sel, k)
        g_k = row(scores, k)
        for gi in range(1, N_GROUPS):
            hit = gsel == float(gi)
            s_k = jnp.where(hit, row(sel, gi * EXPERTS_PER_GROUP + k), s_k)
            g_k = jnp.where(hit, row(scores, gi * EXPERTS_PER_GROUP + k), g_k)
        sv.append(s_k)
        gv.append(g_k)

    def argmax4(vals):
        bv, bi = vals[0], jnp.zeros_like(vals[0])
        for k in range(1, EXPERTS_PER_GROUP):
            take = vals[k] > bv
            bv = jnp.where(take, vals[k], bv)
            bi = jnp.where(take, float(k), bi)
        return bi

    i1 = argmax4(sv)
    i2 = argmax4([jnp.where(i1 == float(k), -jnp.inf, sv[k]) for k in range(EXPERTS_PER_GROUP)])
    w1 = jnp.zeros_like(i1)
    w2 = jnp.zeros_like(i1)
    for k in range(EXPERTS_PER_GROUP):
        w1 = jnp.where(i1 == float(k), gv[k], w1)
        w2 = jnp.where(i2 == float(k), gv[k], w2)
    wsum = w1 + w2
    w1 = w1 / wsum
    w2 = w2 / wsum
    code = i1 * float(EXPERTS_PER_GROUP) + i2
    pair_idx = jnp.zeros_like(code)
    gate_a, gate_b = w1, w2
    for k, (a, b) in enumerate(_CLASS_PAIRS):
        fwd = code == float(a * EXPERTS_PER_GROUP + b)
        rev = code == float(b * EXPERTS_PER_GROUP + a)
        pair_idx = jnp.where(fwd, float(k), jnp.where(rev, float(k), pair_idx))
        gate_a = jnp.where(rev, w2, gate_a)
        gate_b = jnp.where(rev, w1, gate_b)
    cls_f = gsel * float(PAIRS_PER_GROUP) + pair_idx
    cls_ref[0] = cls_f.astype(jnp.int32)

    grow = lax.broadcasted_iota(jnp.int32, (LANES, tm), 0)
    gmat = jnp.where(grow == 0, gate_a, jnp.where(grow == 1, gate_b, 0.0))
    pay_ref[0, :, D_MODEL:] = gmat.T

    crow = lax.broadcasted_iota(jnp.int32, (CLASS_ROWS, tm), 0).astype(F32)
    onehot = crow == cls_f
    r_i = lax.broadcasted_iota(jnp.int32, (tm, tm), 0)
    c_i = lax.broadcasted_iota(jnp.int32, (tm, tm), 1)
    upper = jnp.where(r_i <= c_i, 1.0, 0.0).astype(BF16)
    cum = _dot(jnp.where(onehot, 1.0, 0.0).astype(BF16), upper)
    before = carry[:, 0:1]
    rank = jnp.sum(jnp.where(onehot, cum - 1.0 + before, 0.0), axis=0, keepdims=True)
    rank_ref[0] = rank.astype(jnp.int32)
    total = before + cum[:, tm - 1:tm]
    carry[...] = jnp.broadcast_to(total, carry.shape)
    cnt_ref[...] = jnp.broadcast_to(total, cnt_ref.shape)


def _post_call(hm, ha, g, x, mod, wbm, wba, wo, wn2, wrh, wrl, rb):
    b, s, d = x.shape
    tm = ROUTER_TILE
    tok = lambda w_: pl.BlockSpec((1, tm, w_), lambda i, j: (i, j, 0))
    full = lambda a: pl.BlockSpec(a.shape, lambda i, j: (0,) * a.ndim)
    lanes = pl.BlockSpec((1, 1, tm), lambda i, j: (i, 0, j))
    return pl.pallas_call(
        _post_kernel,
        grid=(b, s // tm),
        in_specs=[tok(M_V_W), tok(A_Q_W), tok(2 * d), tok(d),
                  pl.BlockSpec((1, 6, d), lambda i, j: (i, 0, 0)),
                  full(wbm), full(wba), full(wo), full(wn2), full(wrh), full(wrl), full(rb)],
        out_specs=[tok(d), tok(PAYLOAD_W), lanes, lanes,
                   pl.BlockSpec((CLASS_ROWS, LANES), lambda i, j: (0, 0))],
        out_shape=[jax.ShapeDtypeStruct((b, s, d), F32),
                   jax.ShapeDtypeStruct((b, s, PAYLOAD_W), F32),
                   jax.ShapeDtypeStruct((b, 1, s), jnp.int32),
                   jax.ShapeDtypeStruct((b, 1, s), jnp.int32),
                   jax.ShapeDtypeStruct((CLASS_ROWS, LANES), F32)],
        scratch_shapes=[pltpu.VMEM((CLASS_ROWS, LANES), F32)],
        compiler_params=_params("arbitrary", "arbitrary"),
        name="merge_outproj_router",
    )(hm, ha, g, x, mod, wbm, wba, wo, wn2, wrh, wrl, rb)


def _zero_fill(off_s, cnt_s, nu_s, xs_out, zbuf, zsem, *, wait):
    chunk = zbuf.shape[0]

    def zero_rows(first, n):
        cp = pltpu.make_async_copy(zbuf.at[pl.ds(0, n), :], xs_out.at[pl.ds(first, n), :], zsem)
        cp.wait() if wait else cp.start()

    def per_class(c, carry):
        n = cnt_s[c]
        fill = (-n) & (EXPERT_TILE - 1)
        head = fill & (SUBLANES - 1)
        for k in range(SUBLANES - 1):
            pl.when(k < head)(lambda k=k: zero_rows(off_s[c] + n + k, 1))
        cur = off_s[c] + n + head
        p = chunk
        while p >= SUBLANES:
            pl.when((fill & p) != 0)(lambda cur=cur, p=p: zero_rows(pl.multiple_of(cur, SUBLANES), p))
            cur = cur + (fill & p)
            p //= 2
        return carry

    lax.fori_loop(0, N_CLASSES, per_class, 0)

    def per_chunk(r, carry):
        zero_rows(pl.multiple_of(r * chunk, SUBLANES), chunk)
        return carry

    per_tile = EXPERT_TILE // chunk
    lax.fori_loop(nu_s[0] * per_tile, (xs_out.shape[0] // EXPERT_TILE) * per_tile, per_chunk, 0)


def _dispatch_kernel(pos_s, off_s, cnt_s, nu_s, pay_ref, xs_out, zbuf, sem, zsem):
    @pl.when(pl.program_id(0) == 0)
    def _():
        zbuf[...] = jnp.zeros(zbuf.shape, F32)
        _zero_fill(off_s, cnt_s, nu_s, xs_out, zbuf, zsem, wait=False)
        _zero_fill(off_s, cnt_s, nu_s, xs_out, zbuf, zsem, wait=True)

    first_token = pl.program_id(0) * pay_ref.shape[0] * SUBLANES
    for wait in (False, True):
        _row_copies(pos_s, first_token, xs_out, pay_ref, sem, to_sorted=True, wait=wait)


def _dispatch_call(pos, off, cnt, n_used, payload, npad):
    t8, _, width = payload.shape
    tiles = ROW_TILE // SUBLANES
    return pl.pallas_call(
        _dispatch_kernel,
        grid_spec=pltpu.PrefetchScalarGridSpec(
            num_scalar_prefetch=4,
            grid=(t8 // tiles,),
            in_specs=[pl.BlockSpec((tiles, SUBLANES, width), lambda i, *_: (i, 0, 0))],
            out_specs=pl.BlockSpec(memory_space=pl.ANY),
            scratch_shapes=[pltpu.VMEM((EXPERT_TILE // 2, width), F32),
                            pltpu.SemaphoreType.DMA(()), pltpu.SemaphoreType.DMA(())],
        ),
        out_shape=jax.ShapeDtypeStruct((npad, width), F32),
        compiler_params=_params("arbitrary"),
        name="moe_dispatch",
    )(pos, off, cnt, n_used, payload)


def _moe_kernel(ea_s, eb_s, nu_s, x_ref, wga, wua, wda, wgb, wub, wdb, y_ref):
    del ea_s, eb_s

    @pl.when(pl.program_id(0) < nu_s[0])
    def _():
        x = x_ref[:, :D_MODEL].astype(BF16)
        gate_a = x_ref[:, D_MODEL:D_MODEL + 1]
        gate_b = x_ref[:, D_MODEL + 1:D_MODEL + 2]

        up = [(_dot(x, wg[0]), _dot(x, wu[0])) for wg, wu in ((wga, wua), (wgb, wub))]
        act = [(hg * _sigmoid(hg) * hu).astype(BF16) for hg, hu in up]
        y_ref[...] = gate_a * _dot(act[0], wda[0]) + gate_b * _dot(act[1], wdb[0])

    @pl.when(pl.program_id(0) >= nu_s[0])
    def _():
        y_ref[...] = jnp.zeros(y_ref.shape, F32)


def _moe_call(tile_ea, tile_eb, n_used, xs, wg, wu, wd, layer):
    npad = xs.shape[0]
    te = EXPERT_TILE
    last = lambda i, nu: jnp.minimum(i, nu[0] - 1)
    first = layer * N_EXPERTS
    pick = lambda tab: (lambda i, ea, eb, nu: (first + (ea, eb)[tab][i], 0, 0))
    up = lambda tab: pl.BlockSpec((1, D_MODEL, D_EXPERT), pick(tab))
    down = lambda tab: pl.BlockSpec((1, D_EXPERT, D_MODEL), pick(tab))
    return pl.pallas_call(
        _moe_kernel,
        grid_spec=pltpu.PrefetchScalarGridSpec(
            num_scalar_prefetch=3,
            grid=(npad // te,),
            in_specs=[pl.BlockSpec((te, PAYLOAD_W), lambda i, ea, eb, nu: (last(i, nu), 0)),
                      up(0), up(0), down(0), up(1), up(1), down(1)],
            out_specs=pl.BlockSpec((te, D_MODEL), lambda i, ea, eb, nu: (i, 0)),
        ),
        out_shape=jax.ShapeDtypeStruct((npad, D_MODEL), F32),
        compiler_params=_params("arbitrary"),
        name="moe_experts",
    )(tile_ea, tile_eb, n_used, xs, wg, wu, wd, wg, wu, wd)


def _combine_kernel(pos_s, ys_hbm, x1_ref, mod_ref, wf_ref, out_ref, ybuf, sem):
    x2 = _moe_residual(pos_s, ys_hbm, x1_ref, mod_ref, ybuf, sem)
    out_ref[0] = _rms(x2) * wf_ref[...]


def _combine_call(pos, ys, x1, mod, wf):
    b, s, d = x1.shape
    rows = ROW_TILE
    return pl.pallas_call(
        _combine_kernel,
        grid_spec=pltpu.PrefetchScalarGridSpec(
            num_scalar_prefetch=1,
            grid=(b, s // rows),
            in_specs=[pl.BlockSpec(memory_space=pl.ANY),
                      pl.BlockSpec((1, rows, d), lambda i, j, *_: (i, j, 0)),
                      pl.BlockSpec((1, 6, d), lambda i, j, *_: (i, 0, 0)),
                      pl.BlockSpec((1, d), lambda i, j, *_: (0, 0))],
            out_specs=pl.BlockSpec((1, rows, d), lambda i, j, *_: (i, j, 0)),
            scratch_shapes=[pltpu.VMEM((2, rows // SUBLANES, SUBLANES, d), F32),
                            pltpu.SemaphoreType.DMA((2,))],
        ),
        out_shape=jax.ShapeDtypeStruct((b, s, d), F32),
        compiler_params=_params("arbitrary", "arbitrary"),
        name="moe_combine",
    )(pos, ys, x1, mod, wf)


def _t5_bucket(n):
    max_exact = N_BUCKETS // 2
    large = max_exact + (np.log(np.maximum(n, 1) / max_exact)
                         / np.log(MAX_DISTANCE / max_exact)
                         * (N_BUCKETS - max_exact)).astype(np.int32)
    large = np.minimum(large, N_BUCKETS - 1)
    return np.where(n < max_exact, n, large).astype(np.int32)


def _bias_table(rel_bias):
    dist = np.arange(WINDOW)[:, None] + WINDOW - np.arange(2 * WINDOW)[None, :]
    bucket = _t5_bucket(np.maximum(dist, 0)).reshape(1, -1)
    onehot = (jnp.arange(N_BUCKETS, dtype=jnp.int32)[:, None] == jnp.asarray(bucket)).astype(F32)
    tab = jnp.dot(rel_bias.astype(F32).T, onehot, precision=lax.Precision.HIGHEST)
    tab = tab.reshape(A_HEADS, WINDOW, 2 * WINDOW)
    in_window = (dist >= 0) & (dist < WINDOW)
    has_key = np.arange(2 * WINDOW)[None, :] >= WINDOW
    return jnp.stack([jnp.where(jnp.asarray(in_window), tab, NEG_INF),
                      jnp.where(jnp.asarray(in_window & has_key), tab, NEG_INF)])


def _layout_w_in(w_in_l):
    pts = np.cumsum([0, M_QK_W, M_QK_W, M_V_W, M_V_W, M_HEADS, M_HEADS,
                     A_Q_W, A_KV_W, A_KV_W, D_MODEL, D_MODEL])
    wb = w_in_l.astype(BF16)
    col = lambda k: wb[:, pts[k]:pts[k + 1]]
    head = lambda h: wb[:, pts[6] + A_HEAD_DIM * h:pts[6] + A_HEAD_DIM * (h + 1)]
    w_if = jnp.concatenate([col(4), col(5)], axis=1)
    w_if_pad = jnp.pad(w_if, ((0, 0), (0, LANES - 2 * M_HEADS)))
    w = jnp.concatenate([col(0), col(2), col(3)] + [head(h) for h in A_HEAD_ORDER]
                        + [col(7), col(8), col(9), col(10), w_if_pad], axis=1)
    w_t = jnp.concatenate([col(1), w_if], axis=1).T
    return w, jnp.pad(w_t, ((0, -w_t.shape[0] % (2 * SUBLANES)), (0, 0)))


def _tile_tables(counts, n_tiles):
    te = EXPERT_TILE
    tiles = (counts + te - 1) // te
    ends = jnp.cumsum(tiles)
    off = (ends - tiles) * te
    n_used = ends[-1]
    tile_idx = jnp.minimum(jnp.arange(n_tiles, dtype=jnp.int32), n_used - 1)
    tile_cls = jnp.sum((ends[None, :] <= tile_idx[:, None]).astype(jnp.int32), axis=1)
    tile_cls = jnp.minimum(tile_cls, N_CLASSES - 1)
    off_pad = jnp.pad(off, (0, CLASS_ROWS - N_CLASSES)).astype(jnp.int32)
    return (off_pad, jnp.asarray(_CLASS_EA)[tile_cls], jnp.asarray(_CLASS_EB)[tile_cls],
            n_used.reshape(1).astype(jnp.int32))


def kernel(x, c, w_ada, b_ada, w_norm1, w_in, conv_w, conv_b, b_igate, b_fgate, w_mnorm, sinks, rel_bias,
           w_br_m, w_br_a, w_out, w_norm2, w_router, router_bias, w_gate_e, w_up_e, w_down_e, w_final):
    b, s, d = x.shape
    depth = w_ada.shape[0]
    t = b * s
    n_tiles = t // EXPERT_TILE + N_CLASSES
    npad = n_tiles * EXPERT_TILE

    mod_all = _ada_call(jnp.pad(c, ((0, 8 - b), (0, 0))), w_ada, b_ada)[:, :b]
    bias_tab = _bias_table(rel_bias)
    wrt = w_router.T
    wrh = wrt.astype(BF16)
    wrl = (wrt - wrh.astype(F32)).astype(BF16)
    rb = router_bias.reshape(N_EXPERTS, 1).astype(F32)
    row = lambda v: v.reshape(1, -1).astype(F32)

    wg_all, wu_all, wd_all = _cast_call(w_gate_e), _cast_call(w_up_e), _cast_call(w_down_e)

    moe = None
    for l in range(depth):
        mod = mod_all[l].reshape(b, 6, d)
        w_l, wift = _layout_w_in(w_in[l])
        gate_bias = jnp.concatenate([b_igate[l], b_fgate[l]]).astype(F32)
        bcol = jnp.pad(gate_bias, (0, LANES - 2 * M_HEADS)).reshape(1, LANES)
        brow = gate_bias.reshape(2 * M_HEADS, 1)

        proj_args = (mod, row(w_norm1[l]), w_l, wift, bcol, brow)
        if moe is None:
            q, kt, v, o, qa, kva, g, ifc, ifr = _inproj_call(x, *proj_args)
        else:
            x, q, kt, v, o, qa, kva, g, ifc, ifr = _combine_inproj_call(*moe, *proj_args)
        cw, cb = conv_w[l].astype(F32), conv_b[l].astype(F32)
        cwk = jnp.broadcast_to(cw[:, M_QK_W:, None], (M_CONV, M_QK_W, MLSTM_CHUNK))
        cbk = jnp.broadcast_to(cb[M_QK_W:, None], (M_QK_W, MLSTM_CHUNK))
        hm = _mlstm_call(q, kt, v, o, ifc, ifr, cw[:, :M_QK_W], row(cb[:M_QK_W]), cwk, cbk, row(w_mnorm[l]))
        ha = _swa_call(sinks[l].astype(F32), qa, kva, bias_tab)

        wba = jnp.concatenate([w_br_a[l][A_HEAD_DIM * h:A_HEAD_DIM * (h + 1)] for h in A_HEAD_ORDER])
        x1, payload, cls, rank, cnt = _post_call(
            hm, ha, g, x, mod, w_br_m[l].astype(BF16), wba.astype(BF16), w_out[l].astype(BF16),
            row(w_norm2[l]), wrh, wrl, rb)

        counts = cnt[:, 0].astype(jnp.int32)
        off, tile_ea, tile_eb, n_used = _tile_tables(counts[:N_CLASSES], n_tiles)
        hit = cls.reshape(t, 1) == jnp.arange(CLASS_ROWS, dtype=jnp.int32)[None, :]
        pos = jnp.sum(jnp.where(hit, off[None, :], 0), axis=1) + rank.reshape(t)
        xs = _dispatch_call(pos, off, counts, n_used,
                            payload.reshape(t // SUBLANES, SUBLANES, PAYLOAD_W), npad)
        ys = _moe_call(tile_ea, tile_eb, n_used, xs, wg_all, wu_all, wd_all, layer=l)
        moe = (pos, ys, x1, mod)
    return _combine_call(*moe, row(w_final))
```

```python
import numpy as np
import jax
import jax.numpy as jnp
from jax import lax
from jax.experimental import pallas as pl
from jax.experimental.pallas import tpu as pltpu

F32 = jnp.float32
BF16 = jnp.bfloat16

D_MODEL = 1024
M_HEADS = 4
M_DQK = 64
M_DV = 128
M_CONV = 4
M_QK_W = M_HEADS * M_DQK
M_V_W = M_HEADS * M_DV
A_HEADS = 8
A_KV_HEADS = 2
A_HEAD_DIM = 64
A_Q_W = A_HEADS * A_HEAD_DIM
A_KV_W = A_KV_HEADS * A_HEAD_DIM
WINDOW = 128
N_BUCKETS = 32
MAX_DISTANCE = 128
N_EXPERTS = 16
N_GROUPS = 4
EXPERTS_PER_GROUP = N_EXPERTS // N_GROUPS
D_EXPERT = 512
EPS = 1e-6
NEG_INF = -1e30

LANES = 128
SUBLANES = 8
MLSTM_CHUNK = 128
SWA_BLOCKS = 2
TOKEN_TILE = 512
ROUTER_TILE = 512
EXPERT_TILE = 256
ROW_TILE = 512
PAIRS_PER_GROUP = 6
N_CLASSES = N_GROUPS * PAIRS_PER_GROUP
CLASS_ROWS = 32
PAYLOAD_W = D_MODEL + LANES
VMEM_LIMIT = 48 * 1024 * 1024

C_Q = 0
C_V = C_Q + M_QK_W
C_O = C_V + M_V_W
C_QA = C_O + M_V_W
C_KVA = C_QA + A_Q_W
C_G = C_KVA + 2 * A_KV_W
C_IF = C_G + 2 * D_MODEL
C_END = C_IF + LANES

A_HEAD_ORDER = (0, 4, 1, 5, 2, 6, 3, 7)

_PAIRS = [(a, b) for a in range(EXPERTS_PER_GROUP) for b in range(a + 1, EXPERTS_PER_GROUP)]
_CLASS_PAIRS = ((0, 1), (2, 1), (2, 0), (3, 0), (3, 1), (3, 2))
assert sorted(tuple(sorted(p)) for p in _CLASS_PAIRS) == _PAIRS
_CLASS_EA = np.array([g * EXPERTS_PER_GROUP + a for g in range(N_GROUPS) for a, _ in _CLASS_PAIRS], np.int32)
_CLASS_EB = np.array([g * EXPERTS_PER_GROUP + b for g in range(N_GROUPS) for _, b in _CLASS_PAIRS], np.int32)

_NT = (((1,), (1,)), ((), ()))


def _sigmoid(x):
    return 0.5 * jnp.tanh(0.5 * x) + 0.5


def _log_sigmoid(x):
    return -(jnp.maximum(-x, 0.0) + jnp.log1p(jnp.exp(-jnp.abs(x))))


def _dot(a, b, dims=None):
    if dims is None:
        return jnp.dot(a, b, preferred_element_type=F32)
    return lax.dot_general(a, b, dims, preferred_element_type=F32)


def _split2(a):
    hi = a.astype(BF16)
    lo = (a - hi.astype(F32)).astype(BF16)
    return hi, lo


def _split3(a):
    hi = a.astype(BF16)
    r = a - hi.astype(F32)
    mid = r.astype(BF16)
    lo = (r - mid.astype(F32)).astype(BF16)
    return hi, mid, lo


def _dot_hi(a, b, dims=None):
    ah, al = _split2(a)
    bh, bl = _split2(b)
    return _dot(ah, bh, dims) + _dot(ah, bl, dims) + _dot(al, bh, dims)


def _rms(x):
    return x * lax.rsqrt(jnp.mean(x * x, axis=-1, keepdims=True) + EPS)


def _params(*sem):
    return pltpu.CompilerParams(dimension_semantics=sem, vmem_limit_bytes=VMEM_LIMIT)


def _cast_kernel(w_ref, o_ref):
    o_ref[...] = w_ref[...].astype(BF16)


def _cast_call(w):
    w = w.reshape((-1,) + w.shape[2:])
    e, a, b = w.shape
    blk = 2
    spec = pl.BlockSpec((blk, a, b), lambda i: (i, 0, 0))
    return pl.pallas_call(
        _cast_kernel,
        grid=(e // blk,),
        in_specs=[spec],
        out_specs=spec,
        out_shape=jax.ShapeDtypeStruct(w.shape, BF16),
        compiler_params=_params("arbitrary"),
        name="expert_weights_bf16",
    )(w)


def _ada_kernel(c_ref, w_ref, b_ref, o_ref):
    c = c_ref[...]
    cond = c * _sigmoid(c)
    o_ref[0] = _dot_hi(cond, w_ref[0]) + b_ref[0]


def _ada_call(c8, w_ada, b_ada):
    depth = w_ada.shape[0]
    return pl.pallas_call(
        _ada_kernel,
        grid=(depth, 6),
        in_specs=[
            pl.BlockSpec((8, D_MODEL), lambda l, j: (0, 0)),
            pl.BlockSpec((1, D_MODEL, D_MODEL), lambda l, j: (l, 0, j)),
            pl.BlockSpec((1, 1, D_MODEL), lambda l, j: (l, 0, j)),
        ],
        out_specs=pl.BlockSpec((1, 8, D_MODEL), lambda l, j: (l, 0, j)),
        out_shape=jax.ShapeDtypeStruct((depth, 8, 6 * D_MODEL), F32),
        compiler_params=_params("arbitrary", "arbitrary"),
        name="adaln_mod",
    )(c8, w_ada, b_ada.reshape(depth, 1, 6 * D_MODEL))


def _project(x, mod_ref, wn_ref, w_ref, wift_ref, bcol_ref, brow_ref, outs):
    q_ref, kt_ref, v_ref, o_ref, qa_ref, kva_ref, g_ref, ifc_ref, ifr_ref = outs
    h = _rms(x) * (wn_ref[...] * (1.0 + mod_ref[0, 1:2, :])) + mod_ref[0, 0:1, :]
    hb = h.astype(BF16)

    def seg(a, b):
        return _dot(hb, w_ref[:, a:b])

    q_ref[0] = seg(C_Q, C_V).astype(BF16)
    v_ref[0] = seg(C_V, C_O).astype(BF16)
    o_ref[0] = seg(C_O, C_QA).astype(BF16)
    qa_ref[0] = seg(C_QA, C_KVA).astype(BF16)
    kva_ref[0] = seg(C_KVA, C_G).astype(BF16)
    g_ref[0] = seg(C_G, C_IF).astype(BF16)
    zc = seg(C_IF, C_END) + bcol_ref[...]
    lane = lax.broadcasted_iota(jnp.int32, zc.shape, 1)
    ifc_ref[0] = jnp.where((lane >= M_HEADS) & (lane < 2 * M_HEADS), _log_sigmoid(zc), zc)
    zt = _dot(wift_ref[...], hb, _NT)
    kt_ref[0] = zt[:M_QK_W].astype(BF16)
    zr = zt[M_QK_W:M_QK_W + 2 * M_HEADS] + brow_ref[...]
    row = lax.broadcasted_iota(jnp.int32, zr.shape, 0)
    ifr_ref[0] = jnp.where(row >= M_HEADS, _log_sigmoid(zr), zr)


def _inproj_kernel(x_ref, mod_ref, wn_ref, w_ref, wift_ref, bcol_ref, brow_ref, *outs):
    _project(x_ref[0], mod_ref, wn_ref, w_ref, wift_ref, bcol_ref, brow_ref, outs)


def _row_copies(pos_s, first_token, sorted_hbm, tiles, sem, *, to_sorted, wait):
    def copy(i, k, pos):
        pair = (tiles.at[i, pl.ds(k, 1), :], sorted_hbm.at[pl.ds(pos, 1), :])
        return pltpu.make_async_copy(*(pair if to_sorted else pair[::-1]), sem)

    def body(i, carry):
        for k in range(SUBLANES):
            if wait:
                copy(0, k, 0).wait()
            else:
                copy(i, k, pos_s[first_token + i * SUBLANES + k]).start(priority=k % 2)
        return carry

    lax.fori_loop(0, tiles.shape[0], body, 0, unroll=2)


def _moe_residual(pos_s, ys_hbm, x1_ref, mod_ref, ybuf, sem):
    rows = x1_ref.shape[1]
    step = pl.program_id(0) * pl.num_programs(1) + pl.program_id(1)
    n_steps = pl.num_programs(0) * pl.num_programs(1)
    slot = lax.rem(step, 2)

    def fetch(first_token, slot_, wait):
        _row_copies(pos_s, first_token, ys_hbm, ybuf.at[slot_], sem.at[slot_], to_sorted=False, wait=wait)

    @pl.when(step == 0)
    def _():
        fetch(0, 0, False)

    @pl.when(step + 1 < n_steps)
    def _():
        fetch((step + 1) * rows, 1 - slot, False)

    fetch(0, slot, True)
    y = ybuf[slot].reshape(rows, ybuf.shape[-1])
    return x1_ref[0] + mod_ref[0, 5:6, :] * y


def _combine_inproj_kernel(pos_s, ys_hbm, x1_ref, modp_ref, mod_ref, wn_ref, w_ref,
                           wift_ref, bcol_ref, brow_ref, xo_ref, *rest):
    outs, (ybuf, sem) = rest[:-2], rest[-2:]
    x = _moe_residual(pos_s, ys_hbm, x1_ref, modp_ref, ybuf, sem)
    xo_ref[0] = x
    _project(x, mod_ref, wn_ref, w_ref, wift_ref, bcol_ref, brow_ref, outs)


def _inproj_specs(b, s, tm):
    tok = lambda w_: pl.BlockSpec((1, tm, w_), lambda i, j, *_: (i, j, 0))
    sds = lambda w_, dt: jax.ShapeDtypeStruct((b, s, w_), dt)
    rows = lambda n: pl.BlockSpec((1, n, tm), lambda i, j, *_: (i, 0, j))
    out_specs = [tok(M_QK_W), rows(M_QK_W), tok(M_V_W), tok(M_V_W), tok(A_Q_W), tok(2 * A_KV_W),
                 tok(2 * D_MODEL), tok(LANES), rows(2 * M_HEADS)]
    out_shape = [sds(M_QK_W, BF16), jax.ShapeDtypeStruct((b, M_QK_W, s), BF16), sds(M_V_W, BF16),
                 sds(M_V_W, BF16), sds(A_Q_W, BF16), sds(2 * A_KV_W, BF16), sds(2 * D_MODEL, BF16),
                 sds(LANES, F32), jax.ShapeDtypeStruct((b, 2 * M_HEADS, s), F32)]
    return tok, out_specs, out_shape


def _inproj_call(x, mod, wn, w, wift, bcol, brow):
    b, s, d = x.shape
    tm = TOKEN_TILE
    tok, out_specs, out_shape = _inproj_specs(b, s, tm)
    full = lambda a: pl.BlockSpec(a.shape, lambda i, j: (0,) * a.ndim)
    return pl.pallas_call(
        _inproj_kernel,
        grid=(b, s // tm),
        in_specs=[tok(d), pl.BlockSpec((1, 6, d), lambda i, j: (i, 0, 0)), full(wn), full(w),
                  full(wift), full(bcol), full(brow)],
        out_specs=out_specs,
        out_shape=out_shape,
        compiler_params=_params("arbitrary", "arbitrary"),
        name="norm1_inproj",
    )(x, mod, wn, w, wift, bcol, brow)


def _combine_inproj_call(pos, ys, x1, mod_prev, mod, wn, w, wift, bcol, brow):
    b, s, d = x1.shape
    tm = TOKEN_TILE
    tok, out_specs, out_shape = _inproj_specs(b, s, tm)
    full = lambda a: pl.BlockSpec(a.shape, lambda i, j, *_: (0,) * a.ndim)
    modspec = pl.BlockSpec((1, 6, d), lambda i, j, *_: (i, 0, 0))
    return pl.pallas_call(
        _combine_inproj_kernel,
        grid_spec=pltpu.PrefetchScalarGridSpec(
            num_scalar_prefetch=1,
            grid=(b, s // tm),
            in_specs=[pl.BlockSpec(memory_space=pl.ANY), tok(d), modspec, modspec, full(wn), full(w),
                      full(wift), full(bcol), full(brow)],
            out_specs=[tok(d)] + out_specs,
            scratch_shapes=[pltpu.VMEM((2, tm // SUBLANES, SUBLANES, d), F32),
                            pltpu.SemaphoreType.DMA((2,))],
        ),
        out_shape=[jax.ShapeDtypeStruct((b, s, d), F32)] + out_shape,
        compiler_params=_params("arbitrary", "arbitrary"),
        name="combine_norm1_inproj",
    )(pos, ys, x1, mod_prev, mod, wn, w, wift, bcol, brow)


def _mlstm_kernel(q_ref, kt_ref, v_ref, o_ref, ifc_ref, ifr_ref, cwq_ref, cbq_ref, cwk_ref, cbk_ref, wn_ref,
                  out_ref, qbuf, kbuf, cstate, mstate):
    L = q_ref.shape[1]
    pad = qbuf.shape[0] - L
    heads = range(M_HEADS)

    @pl.when(pl.program_id(1) == 0)
    def _():
        qbuf[...] = jnp.zeros(qbuf.shape, qbuf.dtype)
        kbuf[...] = jnp.zeros(kbuf.shape, kbuf.dtype)
        cstate[...] = jnp.zeros(cstate.shape, F32)
        mstate[...] = jnp.zeros(mstate.shape, F32)

    xq = q_ref[0]
    qbuf[pad:pad + L, :] = xq
    q_all = qbuf[...]
    qbuf[0:pad, :] = xq[L - pad:L, :]
    xk = kt_ref[0]
    kbuf[:, L:] = xk
    k_all = kbuf[...]
    kbuf[:, :L] = xk
    r_s = lax.broadcasted_iota(jnp.int32, (L, L + pad), 0)
    c_s = lax.broadcasted_iota(jnp.int32, (L, L + pad), 1)
    r_t = lax.broadcasted_iota(jnp.int32, (2 * L, L), 0)
    c_t = lax.broadcasted_iota(jnp.int32, (2 * L, L), 1)
    acc_q = cbq_ref[...] + cwq_ref[M_CONV - 1:M_CONV, :] * xq.astype(F32)
    acc_k = cbk_ref[...] + cwk_ref[M_CONV - 1] * xk.astype(F32)
    for j in range(M_CONV - 1):
        back = M_CONV - 1 - j
        shift_q = jnp.where(c_s == r_s + (pad - back), 1.0, 0.0).astype(BF16)
        acc_q = acc_q + cwq_ref[j:j + 1, :] * _dot(shift_q, q_all)
        shift_k = jnp.where(r_t == c_t + (L - back), 1.0, 0.0).astype(BF16)
        acc_k = acc_k + cwk_ref[j] * _dot(k_all, shift_k)
    q = acc_q * _sigmoid(acc_q) * (M_DQK ** -0.5)
    kf = acc_k * _sigmoid(acc_k)
    kb = kf.astype(BF16)
    vb = v_ref[0]

    r_i = lax.broadcasted_iota(jnp.int32, (L, L), 0)
    c_i = lax.broadcasted_iota(jnp.int32, (L, L), 1)
    causal = c_i <= r_i
    tril = jnp.where(causal, 1.0, 0.0).astype(BF16)
    triu = jnp.where(r_i <= c_i, 1.0, 0.0).astype(BF16)

    icol = ifc_ref[0]
    irow = ifr_ref[0]
    ch, cm, cl = _split3(icol)
    bcum_col = _dot(tril, ch) + _dot(tril, cm) + _dot(tril, cl)
    rh, rm, rl = _split3(irow)
    bcum_row = _dot(rh, triu) + _dot(rm, triu) + _dot(rl, triu)

    lane = lax.broadcasted_iota(jnp.int32, (1, LANES), 1)
    lo_half = lane < M_DQK
    hi_half = jnp.logical_not(lo_half)
    ones = jnp.ones((L, LANES), BF16)
    pair = lambda h: slice(LANES * (h // 2), LANES * (h // 2 + 1))

    cst = [cstate[h] for h in heads]
    m_prev = [mstate[h][:, 0:1] for h in heads]
    qmask = [jnp.where(lo_half if h % 2 == 0 else hi_half, q[:, pair(h)], 0.0) for h in heads]
    vaug = [jnp.concatenate([vb[:, M_DV * h:M_DV * (h + 1)], ones], axis=1) for h in heads]
    bc_col = [bcum_col[:, M_HEADS + h:M_HEADS + h + 1] for h in heads]
    bc_row = [bcum_row[M_HEADS + h:M_HEADS + h + 1, :] for h in heads]
    i_row = [irow[h:h + 1, :] for h in heads]
    b_last = [bc_row[h][:, L - 1:L] for h in heads]

    s = [_dot(qmask[h].astype(BF16), kb[pair(h), :]) for h in heads]

    dlog = [jnp.where(causal, bc_col[h] - bc_row[h] + i_row[h], NEG_INF) for h in heads]
    m_inter = [bc_col[h] + m_prev[h] for h in heads]
    m_row = [jnp.maximum(m_inter[h], jnp.max(dlog[h], axis=-1, keepdims=True)) for h in heads]
    sc = [(s[h] * jnp.exp(dlog[h] - m_row[h])).astype(BF16) for h in heads]
    qi = [(jnp.exp(m_inter[h] - m_row[h]) * qmask[h]).astype(BF16) for h in heads]

    num = [_dot(sc[h], vaug[h]) + _dot(qi[h], cst[h].astype(BF16)) for h in heads]
    for h in heads:
        den = num[h][:, M_DV:]
        hh = num[h][:, :M_DV] / jnp.maximum(jnp.abs(den), jnp.exp(-m_row[h]))
        hn = _rms(hh) * wn_ref[:, M_DV * h:M_DV * (h + 1)]
        og = _sigmoid(o_ref[0, :, M_DV * h:M_DV * (h + 1)].astype(F32))
        out_ref[0, :, M_DV * h:M_DV * (h + 1)] = (hn * og).astype(BF16)

    w_state = [b_last[h] - bc_row[h] + i_row[h] for h in heads]
    m_loc = [jnp.max(w_state[h], axis=-1, keepdims=True) for h in heads]
    ak = [(jnp.exp(w_state[h] - m_loc[h]) * kf[pair(h), :]).astype(BF16) for h in heads]
    c_loc = [_dot(ak[h], vaug[h]) for h in heads]
    for h in heads:
        m_new = jnp.maximum(b_last[h] + m_prev[h], m_loc[h])
        decay = jnp.exp(b_last[h] + m_prev[h] - m_new)
        fresh = jnp.exp(m_loc[h] - m_new)
        cstate[h] = decay * cst[h] + fresh * c_loc[h]
        mstate[h] = jnp.broadcast_to(m_new, (1, LANES))


def _mlstm_call(q, kt, v, o, ifc, ifr, cwq, cbq, cwk, cbk, w_mnorm):
    b, s, _ = q.shape
    L = MLSTM_CHUNK
    tok = lambda w_: pl.BlockSpec((1, L, w_), lambda i, j: (i, j, 0))
    rows = lambda n: pl.BlockSpec((1, n, L), lambda i, j: (i, 0, j))
    full = lambda a: pl.BlockSpec(a.shape, lambda i, j: (0,) * a.ndim)
    return pl.pallas_call(
        _mlstm_kernel,
        grid=(b, s // L),
        in_specs=[tok(M_QK_W), rows(M_QK_W), tok(M_V_W), tok(M_V_W), tok(LANES), rows(2 * M_HEADS),
                  full(cwq), full(cbq), full(cwk), full(cbk), full(w_mnorm)],
        out_specs=tok(M_V_W),
        out_shape=jax.ShapeDtypeStruct((b, s, M_V_W), BF16),
        scratch_shapes=[pltpu.VMEM((L + 2 * SUBLANES, M_QK_W), BF16),
                        pltpu.VMEM((M_QK_W, 2 * L), BF16),
                        pltpu.VMEM((M_HEADS, LANES, 2 * M_DV), F32),
                        pltpu.VMEM((M_HEADS, 1, LANES), F32)],
        compiler_params=_params("arbitrary", "arbitrary"),
        name="mlstm",
    )(q, kt, v, o, ifc, ifr, cwq, cbq, cwk, cbk, w_mnorm)


def _swa_kernel(sink_ref, q_ref, kvp_ref, kvc_ref, bias_ref, out_ref):
    W = WINDOW
    first = pl.program_id(1) == 0
    kv = [kvp_ref[0]] + [kvc_ref[0, W * u:W * (u + 1), :] for u in range(SWA_BLOCKS)]
    lane = lax.broadcasted_iota(jnp.int32, (1, LANES), 1)
    lo_half = lane < A_HEAD_DIM
    hi_half = jnp.logical_not(lo_half)
    keys, v_half = [], []
    for u in range(SWA_BLOCKS):
        keys.append(jnp.concatenate([kv[u][:, :A_KV_W], kv[u + 1][:, :A_KV_W]], axis=0))
        vals = jnp.concatenate([kv[u][:, A_KV_W:], kv[u + 1][:, A_KV_W:]], axis=0)
        zero = jnp.zeros_like(vals)
        v_half.append((jnp.where(lo_half, vals, zero), jnp.where(hi_half, vals, zero)))
    table = [jnp.where(first, 1, 0)] + [0] * (SWA_BLOCKS - 1)

    tiles = range(A_HEADS // 2)
    slots = [(u, j, p) for u in range(SWA_BLOCKS) for j in tiles for p in range(2)]
    head = {ujp: A_HEAD_ORDER[2 * ujp[1] + ujp[2]] for ujp in slots}
    scale = jnp.asarray(A_HEAD_DIM ** -0.5, BF16)
    qt = {(u, j): q_ref[0, W * u:W * (u + 1), LANES * j:LANES * (j + 1)] * scale
          for u in range(SWA_BLOCKS) for j in tiles}
    qm = {(u, j, p): jnp.where(lo_half if p == 0 else hi_half, qt[u, j], jnp.zeros_like(qt[u, j]))
          for u, j, p in slots}
    s = {ujp: _dot(qm[ujp], keys[ujp[0]], _NT) + bias_ref[table[ujp[0]], head[ujp]] for ujp in slots}
    m = {ujp: jnp.maximum(jnp.max(s[ujp], axis=-1, keepdims=True), sink_ref[head[ujp]]) for ujp in slots}
    e = {ujp: jnp.exp(s[ujp] - m[ujp]) for ujp in slots}
    denom = {ujp: jnp.sum(e[ujp], axis=-1, keepdims=True) + jnp.exp(sink_ref[head[ujp]] - m[ujp])
             for ujp in slots}
    pv = {(u, j, p): _dot(e[u, j, p].astype(BF16), v_half[u][p]) for u, j, p in slots}
    for u in range(SWA_BLOCKS):
        for j in tiles:
            out = pv[u, j, 0] / denom[u, j, 0] + pv[u, j, 1] / denom[u, j, 1]
            out_ref[0, W * u:W * (u + 1), LANES * j:LANES * (j + 1)] = out.astype(BF16)


def _swa_call(sinks, qa, kva, bias):
    b, s, _ = qa.shape
    W = WINDOW
    rows = SWA_BLOCKS * W
    return pl.pallas_call(
        _swa_kernel,
        grid=(b, s // rows),
        in_specs=[pl.BlockSpec(memory_space=pltpu.SMEM),
                  pl.BlockSpec((1, rows, A_Q_W), lambda i, j: (i, j, 0)),
                  pl.BlockSpec((1, W, 2 * A_KV_W), lambda i, j: (i, jnp.maximum(SWA_BLOCKS * j - 1, 0), 0)),
                  pl.BlockSpec((1, rows, 2 * A_KV_W), lambda i, j: (i, j, 0)),
                  pl.BlockSpec(bias.shape, lambda i, j: (0,) * bias.ndim)],
        out_specs=pl.BlockSpec((1, rows, A_Q_W), lambda i, j: (i, j, 0)),
        out_shape=jax.ShapeDtypeStruct((b, s, A_Q_W), BF16),
        compiler_params=_params("arbitrary", "arbitrary"),
        name="swa",
    )(sinks, qa, kva, kva, bias)


def _post_kernel(hm_ref, ha_ref, g_ref, x_ref, mod_ref, wbm_ref, wba_ref, wo_ref, wn2_ref,
                 wrh_ref, wrl_ref, rb_ref,
                 x1_ref, pay_ref, cls_ref, rank_ref, cnt_ref, carry):
    tm = x_ref.shape[1]

    @pl.when((pl.program_id(0) == 0) & (pl.program_id(1) == 0))
    def _():
        carry[...] = jnp.zeros(carry.shape, F32)

    g = g_ref[0]
    pm = _dot(hm_ref[0], wbm_ref[...])
    pa = _dot(ha_ref[0], wba_ref[...])
    merged = (_sigmoid(g[:, :D_MODEL]) * pm.astype(BF16)
              + _sigmoid(g[:, D_MODEL:]) * pa.astype(BF16))
    mo = _dot(merged, wo_ref[...])
    x1 = x_ref[0] + mod_ref[0, 2:3, :] * mo
    x1_ref[0] = x1
    h2 = _rms(x1) * (wn2_ref[...] * (1.0 + mod_ref[0, 4:5, :])) + mod_ref[0, 3:4, :]
    pay_ref[0, :, :D_MODEL] = h2

    hh, hl = _split2(h2)
    wrh = wrh_ref[...]
    logits = _dot(wrh, hh, _NT) + _dot(wrh, hl, _NT) + _dot(wrl_ref[...], hh, _NT)
    scores = _sigmoid(logits)
    sel = scores + rb_ref[...]

    def row(a, r):
        return a[r:r + 1, :]

    grp = []
    for gi in range(N_GROUPS):
        v = [row(sel, gi * EXPERTS_PER_GROUP + k) for k in range(EXPERTS_PER_GROUP)]
        best = v[0] + v[1]
        for a, b in _PAIRS[1:]:
            best = jnp.maximum(best, v[a] + v[b])
        grp.append(best)
    gbest = grp[0]
    gsel = jnp.zeros_like(gbest)
    for gi in range(1, N_GROUPS):
        take = grp[gi] > gbest
        gbest = jnp.where(take, grp[gi], gbest)
        gsel = jnp.where(take, float(gi), gsel)

    sv, gv = [], []
    for k in range(EXPERTS_PER_GROUP):
        s_k = row(sel, k)
        g_k = row(scores, k)
        for gi in range(1, N_GROUPS):
            hit = gsel == float(gi)
            s_k = jnp.where(hit, row(sel, gi * EXPERTS_PER_GROUP + k), s_k)
            g_k = jnp.where(hit, row(scores, gi * EXPERTS_PER_GROUP + k), g_k)
        sv.append(s_k)
        gv.append(g_k)

    def argmax4(vals):
        bv, bi = vals[0], jnp.zeros_like(vals[0])
        for k in range(1, EXPERTS_PER_GROUP):
            take = vals[k] > bv
            bv = jnp.where(take, vals[k], bv)
            bi = jnp.where(take, float(k), bi)
        return bi

    i1 = argmax4(sv)
    i2 = argmax4([jnp.where(i1 == float(k), -jnp.inf, sv[k]) for k in range(EXPERTS_PER_GROUP)])
    w1 = jnp.zeros_like(i1)
    w2 = jnp.zeros_like(i1)
    for k in range(EXPERTS_PER_GROUP):
        w1 = jnp.where(i1 == float(k), gv[k], w1)
        w2 = jnp.where(i2 == float(k), gv[k], w2)
    wsum = w1 + w2
    w1 = w1 / wsum
    w2 = w2 / wsum
    code = i1 * float(EXPERTS_PER_GROUP) + i2
    pair_idx = jnp.zeros_like(code)
    gate_a, gate_b = w1, w2
    for k, (a, b) in enumerate(_CLASS_PAIRS):
        fwd = code == float(a * EXPERTS_PER_GROUP + b)
        rev = code == float(b * EXPERTS_PER_GROUP + a)
        pair_idx = jnp.where(fwd, float(k), jnp.where(rev, float(k), pair_idx))
        gate_a = jnp.where(rev, w2, gate_a)
        gate_b = jnp.where(rev, w1, gate_b)
    cls_f = gsel * float(PAIRS_PER_GROUP) + pair_idx
    cls_ref[0] = cls_f.astype(jnp.int32)

    grow = lax.broadcasted_iota(jnp.int32, (LANES, tm), 0)
    gmat = jnp.where(grow == 0, gate_a, jnp.where(grow == 1, gate_b, 0.0))
    pay_ref[0, :, D_MODEL:] = gmat.T

    crow = lax.broadcasted_iota(jnp.int32, (CLASS_ROWS, tm), 0).astype(F32)
    onehot = crow == cls_f
    r_i = lax.broadcasted_iota(jnp.int32, (tm, tm), 0)
    c_i = lax.broadcasted_iota(jnp.int32, (tm, tm), 1)
    upper = jnp.where(r_i <= c_i, 1.0, 0.0).astype(BF16)
    cum = _dot(jnp.where(onehot, 1.0, 0.0).astype(BF16), upper)
    before = carry[:, 0:1]
    rank = jnp.sum(jnp.where(onehot, cum - 1.0 + before, 0.0), axis=0, keepdims=True)
    rank_ref[0] = rank.astype(jnp.int32)
    total = before + cum[:, tm - 1:tm]
    carry[...] = jnp.broadcast_to(total, carry.shape)
    cnt_ref[...] = jnp.broadcast_to(total, cnt_ref.shape)


def _post_call(hm, ha, g, x, mod, wbm, wba, wo, wn2, wrh, wrl, rb):
    b, s, d = x.shape
    tm = ROUTER_TILE
    tok = lambda w_: pl.BlockSpec((1, tm, w_), lambda i, j: (i, j, 0))
    full = lambda a: pl.BlockSpec(a.shape, lambda i, j: (0,) * a.ndim)
    lanes = pl.BlockSpec((1, 1, tm), lambda i, j: (i, 0, j))
    return pl.pallas_call(
        _post_kernel,
        grid=(b, s // tm),
        in_specs=[tok(M_V_W), tok(A_Q_W), tok(2 * d), tok(d),
                  pl.BlockSpec((1, 6, d), lambda i, j: (i, 0, 0)),
                  full(wbm), full(wba), full(wo), full(wn2), full(wrh), full(wrl), full(rb)],
        out_specs=[tok(d), tok(PAYLOAD_W), lanes, lanes,
                   pl.BlockSpec((CLASS_ROWS, LANES), lambda i, j: (0, 0))],
        out_shape=[jax.ShapeDtypeStruct((b, s, d), F32),
                   jax.ShapeDtypeStruct((b, s, PAYLOAD_W), F32),
                   jax.ShapeDtypeStruct((b, 1, s), jnp.int32),
                   jax.ShapeDtypeStruct((b, 1, s), jnp.int32),
                   jax.ShapeDtypeStruct((CLASS_ROWS, LANES), F32)],
        scratch_shapes=[pltpu.VMEM((CLASS_ROWS, LANES), F32)],
        compiler_params=_params("arbitrary", "arbitrary"),
        name="merge_outproj_router",
    )(hm, ha, g, x, mod, wbm, wba, wo, wn2, wrh, wrl, rb)


def _zero_fill(off_s, cnt_s, nu_s, xs_out, zbuf, zsem, *, wait):
    chunk = zbuf.shape[0]

    def zero_rows(first, n):
        cp = pltpu.make_async_copy(zbuf.at[pl.ds(0, n), :], xs_out.at[pl.ds(first, n), :], zsem)
        cp.wait() if wait else cp.start()

    def per_class(c, carry):
        n = cnt_s[c]
        fill = (-n) & (EXPERT_TILE - 1)
        head = fill & (SUBLANES - 1)
        for k in range(SUBLANES - 1):
            pl.when(k < head)(lambda k=k: zero_rows(off_s[c] + n + k, 1))
        cur = off_s[c] + n + head
        p = chunk
        while p >= SUBLANES:
            pl.when((fill & p) != 0)(lambda cur=cur, p=p: zero_rows(pl.multiple_of(cur, SUBLANES), p))
            cur = cur + (fill & p)
            p //= 2
        return carry

    lax.fori_loop(0, N_CLASSES, per_class, 0)

    def per_chunk(r, carry):
        zero_rows(pl.multiple_of(r * chunk, SUBLANES), chunk)
        return carry

    per_tile = EXPERT_TILE // chunk
    lax.fori_loop(nu_s[0] * per_tile, (xs_out.shape[0] // EXPERT_TILE) * per_tile, per_chunk, 0)


def _dispatch_kernel(pos_s, off_s, cnt_s, nu_s, pay_ref, xs_out, zbuf, sem, zsem):
    @pl.when(pl.program_id(0) == 0)
    def _():
        zbuf[...] = jnp.zeros(zbuf.shape, F32)
        _zero_fill(off_s, cnt_s, nu_s, xs_out, zbuf, zsem, wait=False)
        _zero_fill(off_s, cnt_s, nu_s, xs_out, zbuf, zsem, wait=True)

    first_token = pl.program_id(0) * pay_ref.shape[0] * SUBLANES
    for wait in (False, True):
        _row_copies(pos_s, first_token, xs_out, pay_ref, sem, to_sorted=True, wait=wait)


def _dispatch_call(pos, off, cnt, n_used, payload, npad):
    t8, _, width = payload.shape
    tiles = ROW_TILE // SUBLANES
    return pl.pallas_call(
        _dispatch_kernel,
        grid_spec=pltpu.PrefetchScalarGridSpec(
            num_scalar_prefetch=4,
            grid=(t8 // tiles,),
            in_specs=[pl.BlockSpec((tiles, SUBLANES, width), lambda i, *_: (i, 0, 0))],
            out_specs=pl.BlockSpec(memory_space=pl.ANY),
            scratch_shapes=[pltpu.VMEM((EXPERT_TILE // 2, width), F32),
                            pltpu.SemaphoreType.DMA(()), pltpu.SemaphoreType.DMA(())],
        ),
        out_shape=jax.ShapeDtypeStruct((npad, width), F32),
        compiler_params=_params("arbitrary"),
        name="moe_dispatch",
    )(pos, off, cnt, n_used, payload)


def _moe_kernel(ea_s, eb_s, nu_s, x_ref, wga, wua, wda, wgb, wub, wdb, y_ref):
    del ea_s, eb_s

    @pl.when(pl.program_id(0) < nu_s[0])
    def _():
        x = x_ref[:, :D_MODEL].astype(BF16)
        gate_a = x_ref[:, D_MODEL:D_MODEL + 1]
        gate_b = x_ref[:, D_MODEL + 1:D_MODEL + 2]

        up = [(_dot(x, wg[0]), _dot(x, wu[0])) for wg, wu in ((wga, wua), (wgb, wub))]
        act = [(hg * _sigmoid(hg) * hu).astype(BF16) for hg, hu in up]
        y_ref[...] = gate_a * _dot(act[0], wda[0]) + gate_b * _dot(act[1], wdb[0])

    @pl.when(pl.program_id(0) >= nu_s[0])
    def _():
        y_ref[...] = jnp.zeros(y_ref.shape, F32)


def _moe_call(tile_ea, tile_eb, n_used, xs, wg, wu, wd, layer):
    npad = xs.shape[0]
    te = EXPERT_TILE
    last = lambda i, nu: jnp.minimum(i, nu[0] - 1)
    first = layer * N_EXPERTS
    pick = lambda tab: (lambda i, ea, eb, nu: (first + (ea, eb)[tab][i], 0, 0))
    up = lambda tab: pl.BlockSpec((1, D_MODEL, D_EXPERT), pick(tab))
    down = lambda tab: pl.BlockSpec((1, D_EXPERT, D_MODEL), pick(tab))
    return pl.pallas_call(
        _moe_kernel,
        grid_spec=pltpu.PrefetchScalarGridSpec(
            num_scalar_prefetch=3,
            grid=(npad // te,),
            in_specs=[pl.BlockSpec((te, PAYLOAD_W), lambda i, ea, eb, nu: (last(i, nu), 0)),
                      up(0), up(0), down(0), up(1), up(1), down(1)],
            out_specs=pl.BlockSpec((te, D_MODEL), lambda i, ea, eb, nu: (i, 0)),
        ),
        out_shape=jax.ShapeDtypeStruct((npad, D_MODEL), F32),
        compiler_params=_params("arbitrary"),
        name="moe_experts",
    )(tile_ea, tile_eb, n_used, xs, wg, wu, wd, wg, wu, wd)


def _combine_kernel(pos_s, ys_hbm, x1_ref, mod_ref, wf_ref, out_ref, ybuf, sem):
    x2 = _moe_residual(pos_s, ys_hbm, x1_ref, mod_ref, ybuf, sem)
    out_ref[0] = _rms(x2) * wf_ref[...]


def _combine_call(pos, ys, x1, mod, wf):
    b, s, d = x1.shape
    rows = ROW_TILE
    return pl.pallas_call(
        _combine_kernel,
        grid_spec=pltpu.PrefetchScalarGridSpec(
            num_scalar_prefetch=1,
            grid=(b, s // rows),
            in_specs=[pl.BlockSpec(memory_space=pl.ANY),
                      pl.BlockSpec((1, rows, d), lambda i, j, *_: (i, j, 0)),
                      pl.BlockSpec((1, 6, d), lambda i, j, *_: (i, 0, 0)),
                      pl.BlockSpec((1, d), lambda i, j, *_: (0, 0))],
            out_specs=pl.BlockSpec((1, rows, d), lambda i, j, *_: (i, j, 0)),
            scratch_shapes=[pltpu.VMEM((2, rows // SUBLANES, SUBLANES, d), F32),
                            pltpu.SemaphoreType.DMA((2,))],
        ),
        out_shape=jax.ShapeDtypeStruct((b, s, d), F32),
        compiler_params=_params("arbitrary", "arbitrary"),
        name="moe_combine",
    )(pos, ys, x1, mod, wf)


def _t5_bucket(n):
    max_exact = N_BUCKETS // 2
    large = max_exact + (np.log(np.maximum(n, 1) / max_exact)
                         / np.log(MAX_DISTANCE / max_exact)
                         * (N_BUCKETS - max_exact)).astype(np.int32)
    large = np.minimum(large, N_BUCKETS - 1)
    return np.where(n < max_exact, n, large).astype(np.int32)


def _bias_table(rel_bias):
    dist = np.arange(WINDOW)[:, None] + WINDOW - np.arange(2 * WINDOW)[None, :]
    bucket = _t5_bucket(np.maximum(dist, 0)).reshape(1, -1)
    onehot = (jnp.arange(N_BUCKETS, dtype=jnp.int32)[:, None] == jnp.asarray(bucket)).astype(F32)
    tab = jnp.dot(rel_bias.astype(F32).T, onehot, precision=lax.Precision.HIGHEST)
    tab = tab.reshape(A_HEADS, WINDOW, 2 * WINDOW)
    in_window = (dist >= 0) & (dist < WINDOW)
    has_key = np.arange(2 * WINDOW)[None, :] >= WINDOW
    return jnp.stack([jnp.where(jnp.asarray(in_window), tab, NEG_INF),
                      jnp.where(jnp.asarray(in_window & has_key), tab, NEG_INF)])


def _layout_w_in(w_in_l):
    pts = np.cumsum([0, M_QK_W, M_QK_W, M_V_W, M_V_W, M_HEADS, M_HEADS,
                     A_Q_W, A_KV_W, A_KV_W, D_MODEL, D_MODEL])
    wb = w_in_l.astype(BF16)
    col = lambda k: wb[:, pts[k]:pts[k + 1]]
    head = lambda h: wb[:, pts[6] + A_HEAD_DIM * h:pts[6] + A_HEAD_DIM * (h + 1)]
    w_if = jnp.concatenate([col(4), col(5)], axis=1)
    w_if_pad = jnp.pad(w_if, ((0, 0), (0, LANES - 2 * M_HEADS)))
    w = jnp.concatenate([col(0), col(2), col(3)] + [head(h) for h in A_HEAD_ORDER]
                        + [col(7), col(8), col(9), col(10), w_if_pad], axis=1)
    w_t = jnp.concatenate([col(1), w_if], axis=1).T
    return w, jnp.pad(w_t, ((0, -w_t.shape[0] % (2 * SUBLANES)), (0, 0)))


def _tile_tables(counts, n_tiles):
    te = EXPERT_TILE
    tiles = (counts + te - 1) // te
    ends = jnp.cumsum(tiles)
    off = (ends - tiles) * te
    n_used = ends[-1]
    tile_idx = jnp.minimum(jnp.arange(n_tiles, dtype=jnp.int32), n_used - 1)
    tile_cls = jnp.sum((ends[None, :] <= tile_idx[:, None]).astype(jnp.int32), axis=1)
    tile_cls = jnp.minimum(tile_cls, N_CLASSES - 1)
    off_pad = jnp.pad(off, (0, CLASS_ROWS - N_CLASSES)).astype(jnp.int32)
    return (off_pad, jnp.asarray(_CLASS_EA)[tile_cls], jnp.asarray(_CLASS_EB)[tile_cls],
            n_used.reshape(1).astype(jnp.int32))


def kernel(x, c, w_ada, b_ada, w_norm1, w_in, conv_w, conv_b, b_igate, b_fgate, w_mnorm, sinks, rel_bias,
           w_br_m, w_br_a, w_out, w_norm2, w_router, router_bias, w_gate_e, w_up_e, w_down_e, w_final):
    b, s, d = x.shape
    depth = w_ada.shape[0]
    t = b * s
    n_tiles = t // EXPERT_TILE + N_CLASSES
    npad = n_tiles * EXPERT_TILE

    mod_all = _ada_call(jnp.pad(c, ((0, 8 - b), (0, 0))), w_ada, b_ada)[:, :b]
    bias_tab = _bias_table(rel_bias)
    wrt = w_router.T
    wrh = wrt.astype(BF16)
    wrl = (wrt - wrh.astype(F32)).astype(BF16)
    rb = router_bias.reshape(N_EXPERTS, 1).astype(F32)
    row = lambda v: v.reshape(1, -1).astype(F32)

    wg_all, wu_all, wd_all = _cast_call(w_gate_e), _cast_call(w_up_e), _cast_call(w_down_e)

    moe = None
    for l in range(depth):
        mod = mod_all[l].reshape(b, 6, d)
        w_l, wift = _layout_w_in(w_in[l])
        gate_bias = jnp.concatenate([b_igate[l], b_fgate[l]]).astype(F32)
        bcol = jnp.pad(gate_bias, (0, LANES - 2 * M_HEADS)).reshape(1, LANES)
        brow = gate_bias.reshape(2 * M_HEADS, 1)

        proj_args = (mod, row(w_norm1[l]), w_l, wift, bcol, brow)
        if moe is None:
            q, kt, v, o, qa, kva, g, ifc, ifr = _inproj_call(x, *proj_args)
        else:
            x, q, kt, v, o, qa, kva, g, ifc, ifr = _combine_inproj_call(*moe, *proj_args)
        cw, cb = conv_w[l].astype(F32), conv_b[l].astype(F32)
        cwk = jnp.broadcast_to(cw[:, M_QK_W:, None], (M_CONV, M_QK_W, MLSTM_CHUNK))
        cbk = jnp.broadcast_to(cb[M_QK_W:, None], (M_QK_W, MLSTM_CHUNK))
        hm = _mlstm_call(q, kt, v, o, ifc, ifr, cw[:, :M_QK_W], row(cb[:M_QK_W]), cwk, cbk, row(w_mnorm[l]))
        ha = _swa_call(sinks[l].astype(F32), qa, kva, bias_tab)

        wba = jnp.concatenate([w_br_a[l][A_HEAD_DIM * h:A_HEAD_DIM * (h + 1)] for h in A_HEAD_ORDER])
        x1, payload, cls, rank, cnt = _post_call(
            hm, ha, g, x, mod, w_br_m[l].astype(BF16), wba.astype(BF16), w_out[l].astype(BF16),
            row(w_norm2[l]), wrh, wrl, rb)

        counts = cnt[:, 0].astype(jnp.int32)
        off, tile_ea, tile_eb, n_used = _tile_tables(counts[:N_CLASSES], n_tiles)
        hit = cls.reshape(t, 1) == jnp.arange(CLASS_ROWS, dtype=jnp.int32)[None, :]
        pos = jnp.sum(jnp.where(hit, off[None, :], 0), axis=1) + rank.reshape(t)
        xs = _dispatch_call(pos, off, counts, n_used,
                            payload.reshape(t // SUBLANES, SUBLANES, PAYLOAD_W), npad)
        ys = _moe_call(tile_ea, tile_eb, n_used, xs, wg_all, wu_all, wd_all, layer=l)
        moe = (pos, ys, x1, mod)
    return _combine_call(*moe, row(w_final))
```

```python
import numpy as np
import jax
import jax.numpy as jnp
from jax import lax
from jax.experimental import pallas as pl
from jax.experimental.pallas import tpu as pltpu

F32 = jnp.float32
BF16 = jnp.bfloat16

D_MODEL = 1024
M_HEADS = 4
M_DQK = 64
M_DV = 128
M_CONV = 4
M_QK_W = M_HEADS * M_DQK
M_V_W = M_HEADS * M_DV
A_HEADS = 8
A_KV_HEADS = 2
A_HEAD_DIM = 64
A_Q_W = A_HEADS * A_HEAD_DIM
A_KV_W = A_KV_HEADS * A_HEAD_DIM
WINDOW = 128
N_BUCKETS = 32
MAX_DISTANCE = 128
N_EXPERTS = 16
N_GROUPS = 4
EXPERTS_PER_GROUP = N_EXPERTS // N_GROUPS
D_EXPERT = 512
EPS = 1e-6
NEG_INF = -1e30

LANES = 128
SUBLANES = 8
MLSTM_CHUNK = 128
MLSTM_CHUNKS_PER_STEP = 4
SWA_BLOCKS = 4
TOKEN_TILE = 512
ROUTER_TILE = 512
EXPERT_TILE = 256
ROW_TILE = 512
PAIRS_PER_GROUP = 6
N_CLASSES = N_GROUPS * PAIRS_PER_GROUP
CLASS_ROWS = 32
PAYLOAD_W = D_MODEL + LANES
VMEM_LIMIT = 48 * 1024 * 1024

C_Q = 0
C_V = C_Q + M_QK_W
C_O = C_V + M_V_W
C_QA = C_O + M_V_W
C_KVA = C_QA + A_Q_W
C_G = C_KVA + 2 * A_KV_W
C_IF = C_G + 2 * D_MODEL
C_END = C_IF + LANES

A_HEAD_ORDER = (0, 4, 1, 5, 2, 6, 3, 7)

_PAIRS = [(a, b) for a in range(EXPERTS_PER_GROUP) for b in range(a + 1, EXPERTS_PER_GROUP)]
_CLASS_PAIRS = ((0, 1), (2, 1), (2, 0), (3, 0), (3, 1), (3, 2))
assert sorted(tuple(sorted(p)) for p in _CLASS_PAIRS) == _PAIRS
_CLASS_EA = np.array([g * EXPERTS_PER_GROUP + a for g in range(N_GROUPS) for a, _ in _CLASS_PAIRS], np.int32)
_CLASS_EB = np.array([g * EXPERTS_PER_GROUP + b for g in range(N_GROUPS) for _, b in _CLASS_PAIRS], np.int32)

_NT = (((1,), (1,)), ((), ()))


def _sigmoid(x):
    return 0.5 * jnp.tanh(0.5 * x) + 0.5


def _log_sigmoid(x):
    return -(jnp.maximum(-x, 0.0) + jnp.log1p(jnp.exp(-jnp.abs(x))))


def _dot(a, b, dims=None):
    if dims is None:
        return jnp.dot(a, b, preferred_element_type=F32)
    return lax.dot_general(a, b, dims, preferred_element_type=F32)


def _split2(a):
    hi = a.astype(BF16)
    lo = (a - hi.astype(F32)).astype(BF16)
    return hi, lo


def _split3(a):
    hi = a.astype(BF16)
    r = a - hi.astype(F32)
    mid = r.astype(BF16)
    lo = (r - mid.astype(F32)).astype(BF16)
    return hi, mid, lo


def _dot_hi(a, b, dims=None):
    ah, al = _split2(a)
    bh, bl = _split2(b)
    return _dot(ah, bh, dims) + _dot(ah, bl, dims) + _dot(al, bh, dims)


def _rms(x):
    return x * lax.rsqrt(jnp.mean(x * x, axis=-1, keepdims=True) + EPS)


def _params(*sem):
    return pltpu.CompilerParams(dimension_semantics=sem, vmem_limit_bytes=VMEM_LIMIT)


def _cast_kernel(w_ref, o_ref):
    o_ref[...] = w_ref[...].astype(BF16)


def _cast_call(w):
    w = w.reshape((-1,) + w.shape[2:])
    e, a, b = w.shape
    blk = 2
    spec = pl.BlockSpec((blk, a, b), lambda i: (i, 0, 0))
    return pl.pallas_call(
        _cast_kernel,
        grid=(e // blk,),
        in_specs=[spec],
        out_specs=spec,
        out_shape=jax.ShapeDtypeStruct(w.shape, BF16),
        compiler_params=_params("arbitrary"),
        name="expert_weights_bf16",
    )(w)


def _ada_kernel(c_ref, w_ref, b_ref, o_ref):
    c = c_ref[...]
    cond = c * _sigmoid(c)
    o_ref[0] = _dot_hi(cond, w_ref[0]) + b_ref[0]


def _ada_call(c8, w_ada, b_ada):
    depth = w_ada.shape[0]
    return pl.pallas_call(
        _ada_kernel,
        grid=(depth, 6),
        in_specs=[
            pl.BlockSpec((8, D_MODEL), lambda l, j: (0, 0)),
            pl.BlockSpec((1, D_MODEL, D_MODEL), lambda l, j: (l, 0, j)),
            pl.BlockSpec((1, 1, D_MODEL), lambda l, j: (l, 0, j)),
        ],
        out_specs=pl.BlockSpec((1, 8, D_MODEL), lambda l, j: (l, 0, j)),
        out_shape=jax.ShapeDtypeStruct((depth, 8, 6 * D_MODEL), F32),
        compiler_params=_params("arbitrary", "arbitrary"),
        name="adaln_mod",
    )(c8, w_ada, b_ada.reshape(depth, 1, 6 * D_MODEL))


def _project(x, mod_ref, wn_ref, w_ref, wift_ref, bcol_ref, brow_ref, outs):
    q_ref, kt_ref, v_ref, o_ref, qa_ref, kva_ref, g_ref, ifc_ref, ifr_ref = outs
    h = _rms(x) * (wn_ref[...] * (1.0 + mod_ref[0, 1:2, :])) + mod_ref[0, 0:1, :]
    hb = h.astype(BF16)

    def seg(a, b):
        return _dot(hb, w_ref[:, a:b])

    q_ref[0] = seg(C_Q, C_V).astype(BF16)
    v_ref[0] = seg(C_V, C_O).astype(BF16)
    o_ref[0] = seg(C_O, C_QA).astype(BF16)
    qa_ref[0] = seg(C_QA, C_KVA).astype(BF16)
    kva_ref[0] = seg(C_KVA, C_G).astype(BF16)
    g_ref[0] = seg(C_G, C_IF).astype(BF16)
    zc = seg(C_IF, C_END) + bcol_ref[...]
    lane = lax.broadcasted_iota(jnp.int32, zc.shape, 1)
    ifc_ref[0] = jnp.where((lane >= M_HEADS) & (lane < 2 * M_HEADS), _log_sigmoid(zc), zc)
    zt = _dot(wift_ref[...], hb, _NT)
    kt_ref[0] = zt[:M_QK_W].astype(BF16)
    zr = zt[M_QK_W:M_QK_W + 2 * M_HEADS] + brow_ref[...]
    row = lax.broadcasted_iota(jnp.int32, zr.shape, 0)
    ifr_ref[0] = jnp.where(row >= M_HEADS, _log_sigmoid(zr), zr)


def _inproj_kernel(x_ref, mod_ref, wn_ref, w_ref, wift_ref, bcol_ref, brow_ref, *outs):
    _project(x_ref[0], mod_ref, wn_ref, w_ref, wift_ref, bcol_ref, brow_ref, outs)


def _row_copies(pos_s, first_token, sorted_hbm, tiles, sem, *, to_sorted, wait):
    def copy(i, k, pos):
        pair = (tiles.at[i, pl.ds(k, 1), :], sorted_hbm.at[pl.ds(pos, 1), :])
        return pltpu.make_async_copy(*(pair if to_sorted else pair[::-1]), sem)

    def body(i, carry):
        for k in range(SUBLANES):
            if wait:
                copy(0, k, 0).wait()
            else:
                copy(i, k, pos_s[first_token + i * SUBLANES + k]).start(priority=k % 2)
        return carry

    lax.fori_loop(0, tiles.shape[0], body, 0, unroll=2)


def _moe_residual(pos_s, ys_hbm, x1_ref, mod_ref, ybuf, sem):
    rows = x1_ref.shape[1]
    step = pl.program_id(0) * pl.num_programs(1) + pl.program_id(1)
    n_steps = pl.num_programs(0) * pl.num_programs(1)
    slot = lax.rem(step, 2)

    def fetch(first_token, slot_, wait):
        _row_copies(pos_s, first_token, ys_hbm, ybuf.at[slot_], sem.at[slot_], to_sorted=False, wait=wait)

    @pl.when(step == 0)
    def _():
        fetch(0, 0, False)

    @pl.when(step + 1 < n_steps)
    def _():
        fetch((step + 1) * rows, 1 - slot, False)

    fetch(0, slot, True)
    y = ybuf[slot].reshape(rows, ybuf.shape[-1])
    return x1_ref[0] + mod_ref[0, 5:6, :] * y


def _combine_inproj_kernel(pos_s, ys_hbm, x1_ref, modp_ref, mod_ref, wn_ref, w_ref,
                           wift_ref, bcol_ref, brow_ref, xo_ref, *rest):
    outs, (ybuf, sem) = rest[:-2], rest[-2:]
    x = _moe_residual(pos_s, ys_hbm, x1_ref, modp_ref, ybuf, sem)
    xo_ref[0] = x
    _project(x, mod_ref, wn_ref, w_ref, wift_ref, bcol_ref, brow_ref, outs)


def _inproj_specs(b, s, tm):
    tok = lambda w_: pl.BlockSpec((1, tm, w_), lambda i, j, *_: (i, j, 0))
    sds = lambda w_, dt: jax.ShapeDtypeStruct((b, s, w_), dt)
    rows = lambda n: pl.BlockSpec((1, n, tm), lambda i, j, *_: (i, 0, j))
    out_specs = [tok(M_QK_W), rows(M_QK_W), tok(M_V_W), tok(M_V_W), tok(A_Q_W), tok(2 * A_KV_W),
                 tok(2 * D_MODEL), tok(LANES), rows(2 * M_HEADS)]
    out_shape = [sds(M_QK_W, BF16), jax.ShapeDtypeStruct((b, M_QK_W, s), BF16), sds(M_V_W, BF16),
                 sds(M_V_W, BF16), sds(A_Q_W, BF16), sds(2 * A_KV_W, BF16), sds(2 * D_MODEL, BF16),
                 sds(LANES, F32), jax.ShapeDtypeStruct((b, 2 * M_HEADS, s), F32)]
    return tok, out_specs, out_shape


def _inproj_call(x, mod, wn, w, wift, bcol, brow):
    b, s, d = x.shape
    tm = TOKEN_TILE
    tok, out_specs, out_shape = _inproj_specs(b, s, tm)
    full = lambda a: pl.BlockSpec(a.shape, lambda i, j: (0,) * a.ndim)
    return pl.pallas_call(
        _inproj_kernel,
        grid=(b, s // tm),
        in_specs=[tok(d), pl.BlockSpec((1, 6, d), lambda i, j: (i, 0, 0)), full(wn), full(w),
                  full(wift), full(bcol), full(brow)],
        out_specs=out_specs,
        out_shape=out_shape,
        compiler_params=_params("arbitrary", "arbitrary"),
        name="norm1_inproj",
    )(x, mod, wn, w, wift, bcol, brow)


def _combine_inproj_call(pos, ys, x1, mod_prev, mod, wn, w, wift, bcol, brow):
    b, s, d = x1.shape
    tm = TOKEN_TILE
    tok, out_specs, out_shape = _inproj_specs(b, s, tm)
    full = lambda a: pl.BlockSpec(a.shape, lambda i, j, *_: (0,) * a.ndim)
    modspec = pl.BlockSpec((1, 6, d), lambda i, j, *_: (i, 0, 0))
    return pl.pallas_call(
        _combine_inproj_kernel,
        grid_spec=pltpu.PrefetchScalarGridSpec(
            num_scalar_prefetch=1,
            grid=(b, s // tm),
            in_specs=[pl.BlockSpec(memory_space=pl.ANY), tok(d), modspec, modspec, full(wn), full(w),
                      full(wift), full(bcol), full(brow)],
            out_specs=[tok(d)] + out_specs,
            scratch_shapes=[pltpu.VMEM((2, tm // SUBLANES, SUBLANES, d), F32),
                            pltpu.SemaphoreType.DMA((2,))],
        ),
        out_shape=[jax.ShapeDtypeStruct((b, s, d), F32)] + out_shape,
        compiler_params=_params("arbitrary", "arbitrary"),
        name="combine_norm1_inproj",
    )(pos, ys, x1, mod_prev, mod, wn, w, wift, bcol, brow)


def _mlstm_kernel(q_ref, kt_ref, v_ref, o_ref, ifc_ref, ifr_ref, cwq_ref, cbq_ref, cwk_ref, cbk_ref, wn_ref,
                  out_ref, qbuf, kbuf, cstate, mstate):
    @pl.when(pl.program_id(1) == 0)
    def _():
        qbuf[...] = jnp.zeros(qbuf.shape, qbuf.dtype)
        kbuf[...] = jnp.zeros(kbuf.shape, kbuf.dtype)
        cstate[...] = jnp.zeros(cstate.shape, F32)
        mstate[...] = jnp.zeros(mstate.shape, F32)

    heads = range(M_HEADS)
    carry = (qbuf[...], kbuf[...], [cstate[h] for h in heads], [mstate[h][:, 0:1] for h in heads])
    for c in range(q_ref.shape[1] // MLSTM_CHUNK):
        carry = _mlstm_chunk(c, carry, q_ref, kt_ref, v_ref, o_ref, ifc_ref, ifr_ref, cwq_ref, cbq_ref,
                             cwk_ref, cbk_ref, wn_ref, out_ref)
    qbuf[...], kbuf[...] = carry[0], carry[1]
    for h in heads:
        cstate[h] = carry[2][h]
        mstate[h] = jnp.broadcast_to(carry[3][h], (1, LANES))


def _mlstm_chunk(c, carry, q_ref, kt_ref, v_ref, o_ref, ifc_ref, ifr_ref, cwq_ref, cbq_ref, cwk_ref, cbk_ref,
                 wn_ref, out_ref):
    L = MLSTM_CHUNK
    span = slice(L * c, L * (c + 1))
    q_tail, k_prev, cst, m_prev = carry
    pad = q_tail.shape[0]
    heads = range(M_HEADS)

    xq = q_ref[0, span, :]
    q_all = jnp.concatenate([q_tail, xq], axis=0)
    xk = kt_ref[0, :, span]
    k_all = jnp.concatenate([k_prev, xk], axis=1)
    r_s = lax.broadcasted_iota(jnp.int32, (L, L + pad), 0)
    c_s = lax.broadcasted_iota(jnp.int32, (L, L + pad), 1)
    r_t = lax.broadcasted_iota(jnp.int32, (2 * L, L), 0)
    c_t = lax.broadcasted_iota(jnp.int32, (2 * L, L), 1)
    acc_q = cbq_ref[...] + cwq_ref[M_CONV - 1:M_CONV, :] * xq.astype(F32)
    acc_k = cbk_ref[...] + cwk_ref[M_CONV - 1] * xk.astype(F32)
    for j in range(M_CONV - 1):
        back = M_CONV - 1 - j
        shift_q = jnp.where(c_s == r_s + (pad - back), 1.0, 0.0).astype(BF16)
        acc_q = acc_q + cwq_ref[j:j + 1, :] * _dot(shift_q, q_all)
        shift_k = jnp.where(r_t == c_t + (L - back), 1.0, 0.0).astype(BF16)
        acc_k = acc_k + cwk_ref[j] * _dot(k_all, shift_k)
    q = acc_q * _sigmoid(acc_q) * (M_DQK ** -0.5)
    kf = acc_k * _sigmoid(acc_k)
    kb = kf.astype(BF16)
    vb = v_ref[0, span, :]

    r_i = lax.broadcasted_iota(jnp.int32, (L, L), 0)
    c_i = lax.broadcasted_iota(jnp.int32, (L, L), 1)
    causal = c_i <= r_i
    tril = jnp.where(causal, 1.0, 0.0).astype(BF16)
    triu = jnp.where(r_i <= c_i, 1.0, 0.0).astype(BF16)

    icol = ifc_ref[0, span, :]
    irow = ifr_ref[0, :, span]
    ch, cm, cl = _split3(icol)
    bcum_col = _dot(tril, ch) + _dot(tril, cm) + _dot(tril, cl)
    rh, rm, rl = _split3(irow)
    bcum_row = _dot(rh, triu) + _dot(rm, triu) + _dot(rl, triu)

    lane = lax.broadcasted_iota(jnp.int32, (1, LANES), 1)
    lo_half = lane < M_DQK
    hi_half = jnp.logical_not(lo_half)
    ones = jnp.ones((L, LANES), BF16)
    pair = lambda h: slice(LANES * (h // 2), LANES * (h // 2 + 1))

    qmask =[jnp.where(lo_half if h % 2 == 0 else hi_half, q[:, pair(h)], 0.0) for h in heads]
    vaug = [jnp.concatenate([vb[:, M_DV * h:M_DV * (h + 1)], ones], axis=1) for h in heads]
    bc_col = [bcum_col[:, M_HEADS + h:M_HEADS + h + 1] for h in heads]
    bc_row = [bcum_row[M_HEADS + h:M_HEADS + h + 1, :] for h in heads]
    i_row = [irow[h:h + 1, :] for h in heads]
    b_last = [bc_row[h][:, L - 1:L] for h in heads]

    s = [_dot(qmask[h].astype(BF16), kb[pair(h), :]) for h in heads]

    dlog = [jnp.where(causal, bc_col[h] - bc_row[h] + i_row[h], NEG_INF) for h in heads]
    m_inter = [bc_col[h] + m_prev[h] for h in heads]
    m_row = [jnp.maximum(m_inter[h], jnp.max(dlog[h], axis=-1, keepdims=True)) for h in heads]
    sc = [(s[h] * jnp.exp(dlog[h] - m_row[h])).astype(BF16) for h in heads]
    qi = [(jnp.exp(m_inter[h] - m_row[h]) * qmask[h]).astype(BF16) for h in heads]

    num = [_dot(sc[h], vaug[h]) + _dot(qi[h], cst[h].astype(BF16)) for h in heads]
    for h in heads:
        den = num[h][:, M_DV:]
        hh = num[h][:, :M_DV] / jnp.maximum(jnp.abs(den), jnp.exp(-m_row[h]))
        hn = _rms(hh) * wn_ref[:, M_DV * h:M_DV * (h + 1)]
        og = _sigmoid(o_ref[0, span, M_DV * h:M_DV * (h + 1)].astype(F32))
        out_ref[0, span, M_DV * h:M_DV * (h + 1)] = (hn * og).astype(BF16)

    w_state = [b_last[h] - bc_row[h] + i_row[h] for h in heads]
    m_loc = [jnp.max(w_state[h], axis=-1, keepdims=True) for h in heads]
    ak = [(jnp.exp(w_state[h] - m_loc[h]) * kf[pair(h), :]).astype(BF16) for h in heads]
    c_loc = [_dot(ak[h], vaug[h]) for h in heads]
    m_new = [jnp.maximum(b_last[h] + m_prev[h], m_loc[h]) for h in heads]
    c_new = [jnp.exp(b_last[h] + m_prev[h] - m_new[h]) * cst[h] + jnp.exp(m_loc[h] - m_new[h]) * c_loc[h]
             for h in heads]
    return xq[L - pad:L, :], xk, c_new, m_new


def _mlstm_call(q, kt, v, o, ifc, ifr, cwq, cbq, cwk, cbk, w_mnorm):
    b, s, _ = q.shape
    L = MLSTM_CHUNK
    span = MLSTM_CHUNKS_PER_STEP * L
    tok = lambda w_: pl.BlockSpec((1, span, w_), lambda i, j: (i, j, 0))
    rows = lambda n: pl.BlockSpec((1, n, span), lambda i, j: (i, 0, j))
    full = lambda a: pl.BlockSpec(a.shape, lambda i, j: (0,) * a.ndim)
    return pl.pallas_call(
        _mlstm_kernel,
        grid=(b, s // span),
        in_specs=[tok(M_QK_W), rows(M_QK_W), tok(M_V_W), tok(M_V_W), tok(LANES), rows(2 * M_HEADS),
                  full(cwq), full(cbq), full(cwk), full(cbk), full(w_mnorm)],
        out_specs=tok(M_V_W),
        out_shape=jax.ShapeDtypeStruct((b, s, M_V_W), BF16),
        scratch_shapes=[pltpu.VMEM((2 * SUBLANES, M_QK_W), BF16),
                        pltpu.VMEM((M_QK_W, L), BF16),
                        pltpu.VMEM((M_HEADS, LANES, 2 * M_DV), F32),
                        pltpu.VMEM((M_HEADS, 1, LANES), F32)],
        compiler_params=_params("arbitrary", "arbitrary"),
        name="mlstm",
    )(q, kt, v, o, ifc, ifr, cwq, cbq, cwk, cbk, w_mnorm)


def _swa_kernel(sink_ref, q_ref, kvp_ref, kvc_ref, bias_ref, out_ref):
    W = WINDOW
    first = pl.program_id(1) == 0
    kv = [kvp_ref[0]] + [kvc_ref[0, W * u:W * (u + 1), :] for u in range(SWA_BLOCKS)]
    lane = lax.broadcasted_iota(jnp.int32, (1, LANES), 1)
    lo_half = lane < A_HEAD_DIM
    hi_half = jnp.logical_not(lo_half)
    keys, v_half = [], []
    for u in range(SWA_BLOCKS):
        keys.append(jnp.concatenate([kv[u][:, :A_KV_W], kv[u + 1][:, :A_KV_W]], axis=0))
        vals = jnp.concatenate([kv[u][:, A_KV_W:], kv[u + 1][:, A_KV_W:]], axis=0)
        zero = jnp.zeros_like(vals)
        v_half.append((jnp.where(lo_half, vals, zero), jnp.where(hi_half, vals, zero)))
    table = [jnp.where(first, 1, 0)] + [0] * (SWA_BLOCKS - 1)

    tiles = range(A_HEADS // 2)
    slots = [(u, j, p) for u in range(SWA_BLOCKS) for j in tiles for p in range(2)]
    head = {ujp: A_HEAD_ORDER[2 * ujp[1] + ujp[2]] for ujp in slots}
    scale = jnp.asarray(A_HEAD_DIM ** -0.5, BF16)
    qt = {(u, j): q_ref[0, W * u:W * (u + 1), LANES * j:LANES * (j + 1)] * scale
          for u in range(SWA_BLOCKS) for j in tiles}
    qm = {(u, j, p): jnp.where(lo_half if p == 0 else hi_half, qt[u, j], jnp.zeros_like(qt[u, j]))
          for u, j, p in slots}
    s = {ujp: _dot(qm[ujp], keys[ujp[0]], _NT) + bias_ref[table[ujp[0]], head[ujp]] for ujp in slots}
    m = {ujp: jnp.maximum(jnp.max(s[ujp], axis=-1, keepdims=True), sink_ref[head[ujp]]) for ujp in slots}
    e = {ujp: jnp.exp(s[ujp] - m[ujp]) for ujp in slots}
    denom = {ujp: jnp.sum(e[ujp], axis=-1, keepdims=True) + jnp.exp(sink_ref[head[ujp]] - m[ujp])
             for ujp in slots}
    pv = {(u, j, p): _dot(e[u, j, p].astype(BF16), v_half[u][p]) for u, j, p in slots}
    for u in range(SWA_BLOCKS):
        for j in tiles:
            out = pv[u, j, 0] / denom[u, j, 0] + pv[u, j, 1] / denom[u, j, 1]
            out_ref[0, W * u:W * (u + 1), LANES * j:LANES * (j + 1)] = out.astype(BF16)


def _swa_call(sinks, qa, kva, bias):
    b, s, _ = qa.shape
    W = WINDOW
    rows = SWA_BLOCKS * W
    return pl.pallas_call(
        _swa_kernel,
        grid=(b, s // rows),
        in_specs=[pl.BlockSpec(memory_space=pltpu.SMEM),
                  pl.BlockSpec((1, rows, A_Q_W), lambda i, j: (i, j, 0)),
                  pl.BlockSpec((1, W, 2 * A_KV_W), lambda i, j: (i, jnp.maximum(SWA_BLOCKS * j - 1, 0), 0)),
                  pl.BlockSpec((1, rows, 2 * A_KV_W), lambda i, j: (i, j, 0)),
                  pl.BlockSpec(bias.shape, lambda i, j: (0,) * bias.ndim)],
        out_specs=pl.BlockSpec((1, rows, A_Q_W), lambda i, j: (i, j, 0)),
        out_shape=jax.ShapeDtypeStruct((b, s, A_Q_W), BF16),
        compiler_params=_params("arbitrary", "arbitrary"),
        name="swa",
    )(sinks, qa, kva, kva, bias)


def _post_kernel(hm_ref, ha_ref, g_ref, x_ref, mod_ref, wbm_ref, wba_ref, wo_ref, wn2_ref,
                 wrh_ref, wrl_ref, rb_ref,
                 x1_ref, pay_ref, cls_ref, rank_ref, cnt_ref, carry):
    tm = x_ref.shape[1]

    @pl.when((pl.program_id(0) == 0) & (pl.program_id(1) == 0))
    def _():
        carry[...] = jnp.zeros(carry.shape, F32)

    g = g_ref[0]
    pm = _dot(hm_ref[0], wbm_ref[...])
    pa = _dot(ha_ref[0], wba_ref[...])
    merged = (_sigmoid(g[:, :D_MODEL]) * pm.astype(BF16)
              + _sigmoid(g[:, D_MODEL:]) * pa.astype(BF16))
    mo = _dot(merged, wo_ref[...])
    x1 = x_ref[0] + mod_ref[0, 2:3, :] * mo
    x1_ref[0] = x1
    h2 = _rms(x1) * (wn2_ref[...] * (1.0 + mod_ref[0, 4:5, :])) + mod_ref[0, 3:4, :]
    pay_ref[0, :, :D_MODEL] = h2

    hh, hl = _split2(h2)
    wrh = wrh_ref[...]
    logits = _dot(wrh, hh, _NT) + _dot(wrh, hl, _NT) + _dot(wrl_ref[...], hh, _NT)
    scores = _sigmoid(logits)
    sel = scores + rb_ref[...]

    def row(a, r):
        return a[r:r + 1, :]

    grp = []
    for gi in range(N_GROUPS):
        v = [row(sel, gi * EXPERTS_PER_GROUP + k) for k in range(EXPERTS_PER_GROUP)]
        best = v[0] + v[1]
        for a, b in _PAIRS[1:]:
            best = jnp.maximum(best, v[a] + v[b])
        grp.append(best)
    gbest = grp[0]
    gsel = jnp.zeros_like(gbest)
    for gi in range(1, N_GROUPS):
        take = grp[gi] > gbest
        gbest = jnp.where(take, grp[gi], gbest)
        gsel = jnp.where(take, float(gi), gsel)

    sv, gv = [], []
    for k in range(EXPERTS_PER_GROUP):
        s_k = row(sel, k)
        g_k = row(scores, k)
        for gi in range(1, N_GROUPS):
            hit = gsel == float(gi)
            s_k = jnp.where(hit, row(sel, gi * EXPERTS_PER_GROUP + k), s_k)
            g_k = jnp.where(hit, row(scores, gi * EXPERTS_PER_GROUP + k), g_k)
        sv.append(s_k)
        gv.append(g_k)

    def argmax4(vals):
        bv, bi = vals[0], jnp.zeros_like(vals[0])
        for k in range(1, EXPERTS_PER_GROUP):
            take = vals[k] > bv
            bv = jnp.where(take, vals[k], bv)
            bi = jnp.where(take, float(k), bi)
        return bi

    i1 = argmax4(sv)
    i2 = argmax4([jnp.where(i1 == float(k), -jnp.inf, sv[k]) for k in range(EXPERTS_PER_GROUP)])
    w1 = jnp.zeros_like(i1)
    w2 = jnp.zeros_like(i1)
    for k in range(EXPERTS_PER_GROUP):
        w1 = jnp.where(i1 == float(k), gv[k], w1)
        w2 = jnp.where(i2 == float(k), gv[k], w2)
    wsum = w1 + w2
    w1 = w1 / wsum
    w2 = w2 / wsum
    code = i1 * float(EXPERTS_PER_GROUP) + i2
    pair_idx = jnp.zeros_like(code)
    gate_a, gate_b = w1, w2
    for k, (a, b) in enumerate(_CLASS_PAIRS):
        fwd = code == float(a * EXPERTS_PER_GROUP + b)
        rev = code == float(b * EXPERTS_PER_GROUP + a)
        pair_idx = jnp.where(fwd, float(k), jnp.where(rev, float(k), pair_idx))
        gate_a = jnp.where(rev, w2, gate_a)
        gate_b = jnp.where(rev, w1, gate_b)
    cls_f = gsel * float(PAIRS_PER_GROUP) + pair_idx
    cls_ref[0] = cls_f.astype(jnp.int32)

    grow = lax.broadcasted_iota(jnp.int32, (LANES, tm), 0)
    gmat = jnp.where(grow == 0, gate_a, jnp.where(grow == 1, gate_b, 0.0))
    pay_ref[0, :, D_MODEL:] = gmat.T

    crow = lax.broadcasted_iota(jnp.int32, (CLASS_ROWS, tm), 0).astype(F32)
    onehot = crow == cls_f
    r_i = lax.broadcasted_iota(jnp.int32, (tm, tm), 0)
    c_i = lax.broadcasted_iota(jnp.int32, (tm, tm), 1)
    upper = jnp.where(r_i <= c_i, 1.0, 0.0).astype(BF16)
    cum = _dot(jnp.where(onehot, 1.0, 0.0).astype(BF16), upper)
    before = carry[:, 0:1]
    rank = jnp.sum(jnp.where(onehot, cum - 1.0 + before, 0.0), axis=0, keepdims=True)
    rank_ref[0] = rank.astype(jnp.int32)
    total = before + cum[:, tm - 1:tm]
    carry[...] = jnp.broadcast_to(total, carry.shape)
    cnt_ref[...] = jnp.broadcast_to(total, cnt_ref.shape)


def _post_call(hm, ha, g, x, mod, wbm, wba, wo, wn2, wrh, wrl, rb):
    b, s, d = x.shape
    tm = ROUTER_TILE
    tok = lambda w_: pl.BlockSpec((1, tm, w_), lambda i, j: (i, j, 0))
    full = lambda a: pl.BlockSpec(a.shape, lambda i, j: (0,) * a.ndim)
    lanes = pl.BlockSpec((1, 1, tm), lambda i, j: (i, 0, j))
    return pl.pallas_call(
        _post_kernel,
        grid=(b, s // tm),
        in_specs=[tok(M_V_W), tok(A_Q_W), tok(2 * d), tok(d),
                  pl.BlockSpec((1, 6, d), lambda i, j: (i, 0, 0)),
                  full(wbm), full(wba), full(wo), full(wn2), full(wrh), full(wrl), full(rb)],
        out_specs=[tok(d), tok(PAYLOAD_W), lanes, lanes,
                   pl.BlockSpec((CLASS_ROWS, LANES), lambda i, j: (0, 0))],
        out_shape=[jax.ShapeDtypeStruct((b, s, d), F32),
                   jax.ShapeDtypeStruct((b, s, PAYLOAD_W), F32),
                   jax.ShapeDtypeStruct((b, 1, s), jnp.int32),
                   jax.ShapeDtypeStruct((b, 1, s), jnp.int32),
                   jax.ShapeDtypeStruct((CLASS_ROWS, LANES), F32)],
        scratch_shapes=[pltpu.VMEM((CLASS_ROWS, LANES), F32)],
        compiler_params=_params("arbitrary", "arbitrary"),
        name="merge_outproj_router",
    )(hm, ha, g, x, mod, wbm, wba, wo, wn2, wrh, wrl, rb)


def _zero_fill(off_s, cnt_s, nu_s, xs_out, zbuf, zsem, *, wait):
    chunk = zbuf.shape[0]

    def zero_rows(first, n):
        cp = pltpu.make_async_copy(zbuf.at[pl.ds(0, n), :], xs_out.at[pl.ds(first, n), :], zsem)
        cp.wait() if wait else cp.start()

    def per_class(c, carry):
        n = cnt_s[c]
        fill = (-n) & (EXPERT_TILE - 1)
        head = fill & (SUBLANES - 1)
        for k in range(SUBLANES - 1):
            pl.when(k < head)(lambda k=k: zero_rows(off_s[c] + n + k, 1))
        cur = off_s[c] + n + head
        p = chunk
        while p >= SUBLANES:
            pl.when((fill & p) != 0)(lambda cur=cur, p=p: zero_rows(pl.multiple_of(cur, SUBLANES), p))
            cur = cur + (fill & p)
            p //= 2
        return carry

    lax.fori_loop(0, N_CLASSES, per_class, 0)

    def per_chunk(r, carry):
        zero_rows(pl.multiple_of(r * chunk, SUBLANES), chunk)
        return carry

    per_tile = EXPERT_TILE // chunk
    lax.fori_loop(nu_s[0] * per_tile, (xs_out.shape[0] // EXPERT_TILE) * per_tile, per_chunk, 0)


def _dispatch_kernel(pos_s, off_s, cnt_s, nu_s, pay_ref, xs_out, zbuf, sem, zsem):
    @pl.when(pl.program_id(0) == 0)
    def _():
        zbuf[...] = jnp.zeros(zbuf.shape, F32)
        _zero_fill(off_s, cnt_s, nu_s, xs_out, zbuf, zsem, wait=False)
        _zero_fill(off_s, cnt_s, nu_s, xs_out, zbuf, zsem, wait=True)

    first_token = pl.program_id(0) * pay_ref.shape[0] * SUBLANES
    for wait in (False, True):
        _row_copies(pos_s, first_token, xs_out, pay_ref, sem, to_sorted=True, wait=wait)


def _dispatch_call(pos, off, cnt, n_used, payload, npad):
    t8, _, width = payload.shape
    tiles = ROW_TILE // SUBLANES
    return pl.pallas_call(
        _dispatch_kernel,
        grid_spec=pltpu.PrefetchScalarGridSpec(
            num_scalar_prefetch=4,
            grid=(t8 // tiles,),
            in_specs=[pl.BlockSpec((tiles, SUBLANES, width), lambda i, *_: (i, 0, 0))],
            out_specs=pl.BlockSpec(memory_space=pl.ANY),
            scratch_shapes=[pltpu.VMEM((EXPERT_TILE // 2, width), F32),
                            pltpu.SemaphoreType.DMA(()), pltpu.SemaphoreType.DMA(())],
        ),
        out_shape=jax.ShapeDtypeStruct((npad, width), F32),
        compiler_params=_params("arbitrary"),
        name="moe_dispatch",
    )(pos, off, cnt, n_used, payload)


def _moe_kernel(ea_s, eb_s, nu_s, x_ref, wga, wua, wda, wgb, wub, wdb, y_ref):
    del ea_s, eb_s

    @pl.when(pl.program_id(0) < nu_s[0])
    def _():
        x = x_ref[:, :D_MODEL].astype(BF16)
        gate_a = x_ref[:, D_MODEL:D_MODEL + 1]
        gate_b = x_ref[:, D_MODEL + 1:D_MODEL + 2]

        up = [(_dot(x, wg[0]), _dot(x, wu[0])) for wg, wu in ((wga, wua), (wgb, wub))]
        act = [(hg * _sigmoid(hg) * hu).astype(BF16) for hg, hu in up]
        y_ref[...] = gate_a * _dot(act[0], wda[0]) + gate_b * _dot(act[1], wdb[0])

    @pl.when(pl.program_id(0) >= nu_s[0])
    def _():
        y_ref[...] = jnp.zeros(y_ref.shape, F32)


def _moe_call(tile_ea, tile_eb, n_used, xs, wg, wu, wd, layer):
    npad = xs.shape[0]
    te = EXPERT_TILE
    last = lambda i, nu: jnp.minimum(i, nu[0] - 1)
    first = layer * N_EXPERTS
    pick = lambda tab: (lambda i, ea, eb, nu: (first + (ea, eb)[tab][i], 0, 0))
    up = lambda tab: pl.BlockSpec((1, D_MODEL, D_EXPERT), pick(tab))
    down = lambda tab: pl.BlockSpec((1, D_EXPERT, D_MODEL), pick(tab))
    return pl.pallas_call(
        _moe_kernel,
        grid_spec=pltpu.PrefetchScalarGridSpec(
            num_scalar_prefetch=3,
            grid=(npad // te,),
            in_specs=[pl.BlockSpec((te, PAYLOAD_W), lambda i, ea, eb, nu: (last(i, nu), 0)),
                      up(0), up(0), down(0), up(1), up(1), down(1)],
            out_specs=pl.BlockSpec((te, D_MODEL), lambda i, ea, eb, nu: (i, 0)),
        ),
        out_shape=jax.ShapeDtypeStruct((npad, D_MODEL), F32),
        compiler_params=_params("arbitrary"),
        name="moe_experts",
    )(tile_ea, tile_eb, n_used, xs, wg, wu, wd, wg, wu, wd)


def _combine_kernel(pos_s, ys_hbm, x1_ref, mod_ref, wf_ref, out_ref, ybuf, sem):
    x2 = _moe_residual(pos_s, ys_hbm, x1_ref, mod_ref, ybuf, sem)
    out_ref[0] = _rms(x2) * wf_ref[...]


def _combine_call(pos, ys, x1, mod, wf):
    b, s, d = x1.shape
    rows = ROW_TILE
    return pl.pallas_call(
        _combine_kernel,
        grid_spec=pltpu.PrefetchScalarGridSpec(
            num_scalar_prefetch=1,
            grid=(b, s // rows),
            in_specs=[pl.BlockSpec(memory_space=pl.ANY),
                      pl.BlockSpec((1, rows, d), lambda i, j, *_: (i, j, 0)),
                      pl.BlockSpec((1, 6, d), lambda i, j, *_: (i, 0, 0)),
                      pl.BlockSpec((1, d), lambda i, j, *_: (0, 0))],
            out_specs=pl.BlockSpec((1, rows, d), lambda i, j, *_: (i, j, 0)),
            scratch_shapes=[pltpu.VMEM((2, rows // SUBLANES, SUBLANES, d), F32),
                            pltpu.SemaphoreType.DMA((2,))],
        ),
        out_shape=jax.ShapeDtypeStruct((b, s, d), F32),
        compiler_params=_params("arbitrary", "arbitrary"),
        name="moe_combine",
    )(pos, ys, x1, mod, wf)


def _t5_bucket(n):
    max_exact = N_BUCKETS // 2
    large = max_exact + (np.log(np.maximum(n, 1) / max_exact)
                         / np.log(MAX_DISTANCE / max_exact)
                         * (N_BUCKETS - max_exact)).astype(np.int32)
    large = np.minimum(large, N_BUCKETS - 1)
    return np.where(n < max_exact, n, large).astype(np.int32)


def _bias_table(rel_bias):
    dist = np.arange(WINDOW)[:, None] + WINDOW - np.arange(2 * WINDOW)[None, :]
    bucket = _t5_bucket(np.maximum(dist, 0)).reshape(1, -1)
    onehot = (jnp.arange(N_BUCKETS, dtype=jnp.int32)[:, None] == jnp.asarray(bucket)).astype(F32)
    tab = jnp.dot(rel_bias.astype(F32).T, onehot, precision=lax.Precision.HIGHEST)
    tab = tab.reshape(A_HEADS, WINDOW, 2 * WINDOW)
    in_window = (dist >= 0) & (dist < WINDOW)
    has_key = np.arange(2 * WINDOW)[None, :] >= WINDOW
    return jnp.stack([jnp.where(jnp.asarray(in_window), tab, NEG_INF),
                      jnp.where(jnp.asarray(in_window & has_key), tab, NEG_INF)])


def _layout_w_in(w_in_l):
    pts = np.cumsum([0, M_QK_W, M_QK_W, M_V_W, M_V_W, M_HEADS, M_HEADS,
                     A_Q_W, A_KV_W, A_KV_W, D_MODEL, D_MODEL])
    wb = w_in_l.astype(BF16)
    col = lambda k: wb[:, pts[k]:pts[k + 1]]
    head = lambda h: wb[:, pts[6] + A_HEAD_DIM * h:pts[6] + A_HEAD_DIM * (h + 1)]
    w_if = jnp.concatenate([col(4), col(5)], axis=1)
    w_if_pad = jnp.pad(w_if, ((0, 0), (0, LANES - 2 * M_HEADS)))
    w = jnp.concatenate([col(0), col(2), col(3)] + [head(h) for h in A_HEAD_ORDER]
                        + [col(7), col(8), col(9), col(10), w_if_pad], axis=1)
    w_t = jnp.concatenate([col(1), w_if], axis=1).T
    return w, jnp.pad(w_t, ((0, -w_t.shape[0] % (2 * SUBLANES)), (0, 0)))


def _tile_tables(counts, n_tiles):
    te = EXPERT_TILE
    tiles = (counts + te - 1) // te
    ends = jnp.cumsum(tiles)
    off = (ends - tiles) * te
    n_used = ends[-1]
    tile_idx = jnp.minimum(jnp.arange(n_tiles, dtype=jnp.int32), n_used - 1)
    tile_cls = jnp.sum((ends[None, :] <= tile_idx[:, None]).astype(jnp.int32), axis=1)
    tile_cls = jnp.minimum(tile_cls, N_CLASSES - 1)
    off_pad = jnp.pad(off, (0, CLASS_ROWS - N_CLASSES)).astype(jnp.int32)
    return (off_pad, jnp.asarray(_CLASS_EA)[tile_cls], jnp.asarray(_CLASS_EB)[tile_cls],
            n_used.reshape(1).astype(jnp.int32))


def kernel(x, c, w_ada, b_ada, w_norm1, w_in, conv_w, conv_b, b_igate, b_fgate, w_mnorm, sinks, rel_bias,
           w_br_m, w_br_a, w_out, w_norm2, w_router, router_bias, w_gate_e, w_up_e, w_down_e, w_final):
    b, s, d = x.shape
    depth = w_ada.shape[0]
    t = b * s
    n_tiles = t // EXPERT_TILE + N_CLASSES
    npad = n_tiles * EXPERT_TILE

    mod_all = _ada_call(jnp.pad(c, ((0, 8 - b), (0, 0))), w_ada, b_ada)[:, :b]
    bias_tab = _bias_table(rel_bias)
    wrt = w_router.T
    wrh = wrt.astype(BF16)
    wrl = (wrt - wrh.astype(F32)).astype(BF16)
    rb = router_bias.reshape(N_EXPERTS, 1).astype(F32)
    row = lambda v: v.reshape(1, -1).astype(F32)

    wg_all, wu_all, wd_all = _cast_call(w_gate_e), _cast_call(w_up_e), _cast_call(w_down_e)

    moe = None
    for l in range(depth):
        mod = mod_all[l].reshape(b, 6, d)
        w_l, wift = _layout_w_in(w_in[l])
        gate_bias = jnp.concatenate([b_igate[l], b_fgate[l]]).astype(F32)
        bcol = jnp.pad(gate_bias, (0, LANES - 2 * M_HEADS)).reshape(1, LANES)
        brow = gate_bias.reshape(2 * M_HEADS, 1)

        proj_args = (mod, row(w_norm1[l]), w_l, wift, bcol, brow)
        if moe is None:
            q, kt, v, o, qa, kva, g, ifc, ifr = _inproj_call(x, *proj_args)
        else:
            x, q, kt, v, o, qa, kva, g, ifc, ifr = _combine_inproj_call(*moe, *proj_args)
        cw, cb = conv_w[l].astype(F32), conv_b[l].astype(F32)
        cwk = jnp.broadcast_to(cw[:, M_QK_W:, None], (M_CONV, M_QK_W, MLSTM_CHUNK))
        cbk = jnp.broadcast_to(cb[M_QK_W:, None], (M_QK_W, MLSTM_CHUNK))
        hm = _mlstm_call(q, kt, v, o, ifc, ifr, cw[:, :M_QK_W], row(cb[:M_QK_W]), cwk, cbk, row(w_mnorm[l]))
        ha = _swa_call(sinks[l].astype(F32), qa, kva, bias_tab)

        wba = jnp.concatenate([w_br_a[l][A_HEAD_DIM * h:A_HEAD_DIM * (h + 1)] for h in A_HEAD_ORDER])
        x1, payload, cls, rank, cnt = _post_call(
            hm, ha, g, x, mod, w_br_m[l].astype(BF16), wba.astype(BF16), w_out[l].astype(BF16),
            row(w_norm2[l]), wrh, wrl, rb)

        counts = cnt[:, 0].astype(jnp.int32)
        off, tile_ea, tile_eb, n_used = _tile_tables(counts[:N_CLASSES], n_tiles)
        hit = cls.reshape(t, 1) == jnp.arange(CLASS_ROWS, dtype=jnp.int32)[None, :]
        pos = jnp.sum(jnp.where(hit, off[None, :], 0), axis=1) + rank.reshape(t)
        xs = _dispatch_call(pos, off, counts, n_used,
                            payload.reshape(t // SUBLANES, SUBLANES, PAYLOAD_W), npad)
        ys = _moe_call(tile_ea, tile_eb, n_used, xs, wg_all, wu_all, wd_all, layer=l)
        moe = (pos, ys, x1, mod)
    return _combine_call(*moe, row(w_final))
```

```python
import numpy as np
import jax
import jax.numpy as jnp
from jax import lax
from jax.experimental import pallas as pl
from jax.experimental.pallas import tpu as pltpu

F32 = jnp.float32
BF16 = jnp.bfloat16

D_MODEL = 1024
M_HEADS = 4
M_DQK = 64
M_DV = 128
M_CONV = 4
M_QK_W = M_HEADS * M_DQK
M_V_W = M_HEADS * M_DV
A_HEADS = 8
A_KV_HEADS = 2
A_HEAD_DIM = 64
A_Q_W = A_HEADS * A_HEAD_DIM
A_KV_W = A_KV_HEADS * A_HEAD_DIM
WINDOW = 128
N_BUCKETS = 32
MAX_DISTANCE = 128
N_EXPERTS = 16
N_GROUPS = 4
EXPERTS_PER_GROUP = N_EXPERTS // N_GROUPS
D_EXPERT = 512
EPS = 1e-6
NEG_INF = -1e30

LANES = 128
SUBLANES = 8
MLSTM_CHUNK = 128
MLSTM_CHUNKS_PER_STEP = 4
SWA_BLOCKS = 4
TOKEN_TILE = 512
ROUTER_TILE = 512
EXPERT_TILE = 256
ROW_TILE = 1024
PAIRS_PER_GROUP = 6
N_CLASSES = N_GROUPS * PAIRS_PER_GROUP
CLASS_ROWS = 32
PAYLOAD_W = D_MODEL + LANES
VMEM_LIMIT = 48 * 1024 * 1024

C_Q = 0
C_V = C_Q + M_QK_W
C_O = C_V + M_V_W
C_QA = C_O + M_V_W
C_KVA = C_QA + A_Q_W
C_G = C_KVA + 2 * A_KV_W
C_IF = C_G + 2 * D_MODEL
C_END = C_IF + LANES

A_HEAD_ORDER = (0, 4, 1, 5, 2, 6, 3, 7)

_PAIRS = [(a, b) for a in range(EXPERTS_PER_GROUP) for b in range(a + 1, EXPERTS_PER_GROUP)]
_CLASS_PAIRS = ((0, 1), (2, 1), (2, 0), (3, 0), (3, 1), (3, 2))
assert sorted(tuple(sorted(p)) for p in _CLASS_PAIRS) == _PAIRS
_CLASS_EA = np.array([g * EXPERTS_PER_GROUP + a for g in range(N_GROUPS) for a, _ in _CLASS_PAIRS], np.int32)
_CLASS_EB = np.array([g * EXPERTS_PER_GROUP + b for g in range(N_GROUPS) for _, b in _CLASS_PAIRS], np.int32)

_NT = (((1,), (1,)), ((), ()))


def _sigmoid(x):
    return 0.5 * jnp.tanh(0.5 * x) + 0.5


def _log_sigmoid(x):
    return -(jnp.maximum(-x, 0.0) + jnp.log1p(jnp.exp(-jnp.abs(x))))


def _dot(a, b, dims=None):
    if dims is None:
        return jnp.dot(a, b, preferred_element_type=F32)
    return lax.dot_general(a, b, dims, preferred_element_type=F32)


def _split2(a):
    hi = a.astype(BF16)
    lo = (a - hi.astype(F32)).astype(BF16)
    return hi, lo


def _split3(a):
    hi = a.astype(BF16)
    r = a - hi.astype(F32)
    mid = r.astype(BF16)
    lo = (r - mid.astype(F32)).astype(BF16)
    return hi, mid, lo


def _dot_hi(a, b, dims=None):
    ah, al = _split2(a)
    bh, bl = _split2(b)
    return _dot(ah, bh, dims) + _dot(ah, bl, dims) + _dot(al, bh, dims)


def _rms(x):
    return x * lax.rsqrt(jnp.mean(x * x, axis=-1, keepdims=True) + EPS)


def _params(*sem):
    return pltpu.CompilerParams(dimension_semantics=sem, vmem_limit_bytes=VMEM_LIMIT)


def _cast_kernel(w_ref, o_ref):
    o_ref[...] = w_ref[...].astype(BF16)


def _cast_call(w):
    w = w.reshape((-1,) + w.shape[2:])
    e, a, b = w.shape
    blk = 2
    spec = pl.BlockSpec((blk, a, b), lambda i: (i, 0, 0))
    return pl.pallas_call(
        _cast_kernel,
        grid=(e // blk,),
        in_specs=[spec],
        out_specs=spec,
        out_shape=jax.ShapeDtypeStruct(w.shape, BF16),
        compiler_params=_params("arbitrary"),
        name="expert_weights_bf16",
    )(w)


def _ada_kernel(c_ref, w_ref, b_ref, o_ref):
    c = c_ref[...]
    cond = c * _sigmoid(c)
    o_ref[0] = _dot_hi(cond, w_ref[0]) + b_ref[0]


def _ada_call(c8, w_ada, b_ada):
    depth = w_ada.shape[0]
    return pl.pallas_call(
        _ada_kernel,
        grid=(depth, 6),
        in_specs=[
            pl.BlockSpec((8, D_MODEL), lambda l, j: (0, 0)),
            pl.BlockSpec((1, D_MODEL, D_MODEL), lambda l, j: (l, 0, j)),
            pl.BlockSpec((1, 1, D_MODEL), lambda l, j: (l, 0, j)),
        ],
        out_specs=pl.BlockSpec((1, 8, D_MODEL), lambda l, j: (l, 0, j)),
        out_shape=jax.ShapeDtypeStruct((depth, 8, 6 * D_MODEL), F32),
        compiler_params=_params("arbitrary", "arbitrary"),
        name="adaln_mod",
    )(c8, w_ada, b_ada.reshape(depth, 1, 6 * D_MODEL))


def _project(x, mod_ref, wn_ref, w_ref, wift_ref, bcol_ref, brow_ref, outs):
    q_ref, kt_ref, v_ref, o_ref, qa_ref, kva_ref, g_ref, ifc_ref, ifr_ref = outs
    h = _rms(x) * (wn_ref[...] * (1.0 + mod_ref[0, 1:2, :])) + mod_ref[0, 0:1, :]
    hb = h.astype(BF16)

    def seg(a, b):
        return _dot(hb, w_ref[:, a:b])

    q_ref[0] = seg(C_Q, C_V).astype(BF16)
    v_ref[0] = seg(C_V, C_O).astype(BF16)
    o_ref[0] = seg(C_O, C_QA).astype(BF16)
    qa_ref[0] = seg(C_QA, C_KVA).astype(BF16)
    kva_ref[0] = seg(C_KVA, C_G).astype(BF16)
    g_ref[0] = seg(C_G, C_IF).astype(BF16)
    zc = seg(C_IF, C_END) + bcol_ref[...]
    lane = lax.broadcasted_iota(jnp.int32, zc.shape, 1)
    ifc_ref[0] = jnp.where((lane >= M_HEADS) & (lane < 2 * M_HEADS), _log_sigmoid(zc), zc)
    zt = _dot(wift_ref[...], hb, _NT)
    kt_ref[0] = zt[:M_QK_W].astype(BF16)
    zr = zt[M_QK_W:M_QK_W + 2 * M_HEADS] + brow_ref[...]
    row = lax.broadcasted_iota(jnp.int32, zr.shape, 0)
    ifr_ref[0] = jnp.where(row >= M_HEADS, _log_sigmoid(zr), zr)


def _inproj_kernel(x_ref, mod_ref, wn_ref, w_ref, wift_ref, bcol_ref, brow_ref, *outs):
    _project(x_ref[0], mod_ref, wn_ref, w_ref, wift_ref, bcol_ref, brow_ref, outs)


def _row_copies(pos_s, first_token, sorted_hbm, tiles, sem, *, to_sorted, wait):
    def copy(i, k, pos):
        pair = (tiles.at[i, pl.ds(k, 1), :], sorted_hbm.at[pl.ds(pos, 1), :])
        return pltpu.make_async_copy(*(pair if to_sorted else pair[::-1]), sem)

    if wait:
        def body(i, carry):
            for k in range(SUBLANES):
                copy(0, k, 0).wait()
            return carry

        lax.fori_loop(0, tiles.shape[0], body, 0, unroll=2)
        return
    for i in range(tiles.shape[0]):
        for k in range(SUBLANES):
            copy(i, k, pos_s[first_token + i * SUBLANES + k]).start(priority=k % 2)


def _moe_residual(pos_s, ys_hbm, x1_ref, mod_ref, ybuf, sem):
    rows = x1_ref.shape[1]
    step = pl.program_id(0) * pl.num_programs(1) + pl.program_id(1)
    n_steps = pl.num_programs(0) * pl.num_programs(1)
    slot = lax.rem(step, 2)

    def fetch(first_token, slot_, wait):
        _row_copies(pos_s, first_token, ys_hbm, ybuf.at[slot_], sem.at[slot_], to_sorted=False, wait=wait)

    @pl.when(step == 0)
    def _():
        fetch(0, 0, False)

    @pl.when(step + 1 < n_steps)
    def _():
        fetch((step + 1) * rows, 1 - slot, False)

    fetch(0, slot, True)
    y = ybuf[slot].reshape(rows, ybuf.shape[-1])
    return x1_ref[0] + mod_ref[0, 5:6, :] * y


def _combine_inproj_kernel(pos_s, ys_hbm, x1_ref, modp_ref, mod_ref, wn_ref, w_ref,
                           wift_ref, bcol_ref, brow_ref, xo_ref, *rest):
    outs, (ybuf, sem) = rest[:-2], rest[-2:]
    x = _moe_residual(pos_s, ys_hbm, x1_ref, modp_ref, ybuf, sem)
    xo_ref[0] = x
    _project(x, mod_ref, wn_ref, w_ref, wift_ref, bcol_ref, brow_ref, outs)


def _inproj_specs(b, s, tm):
    tok = lambda w_: pl.BlockSpec((1, tm, w_), lambda i, j, *_: (i, j, 0))
    sds = lambda w_, dt: jax.ShapeDtypeStruct((b, s, w_), dt)
    rows = lambda n: pl.BlockSpec((1, n, tm), lambda i, j, *_: (i, 0, j))
    out_specs = [tok(M_QK_W), rows(M_QK_W), tok(M_V_W), tok(M_V_W), tok(A_Q_W), tok(2 * A_KV_W),
                 tok(2 * D_MODEL), tok(LANES), rows(2 * M_HEADS)]
    out_shape = [sds(M_QK_W, BF16), jax.ShapeDtypeStruct((b, M_QK_W, s), BF16), sds(M_V_W, BF16),
                 sds(M_V_W, BF16), sds(A_Q_W, BF16), sds(2 * A_KV_W, BF16), sds(2 * D_MODEL, BF16),
                 sds(LANES, F32), jax.ShapeDtypeStruct((b, 2 * M_HEADS, s), F32)]
    return tok, out_specs, out_shape


def _inproj_call(x, mod, wn, w, wift, bcol, brow):
    b, s, d = x.shape
    tm = TOKEN_TILE
    tok, out_specs, out_shape = _inproj_specs(b, s, tm)
    full = lambda a: pl.BlockSpec(a.shape, lambda i, j: (0,) * a.ndim)
    return pl.pallas_call(
        _inproj_kernel,
        grid=(b, s // tm),
        in_specs=[tok(d), pl.BlockSpec((1, 6, d), lambda i, j: (i, 0, 0)), full(wn), full(w),
                  full(wift), full(bcol), full(brow)],
        out_specs=out_specs,
        out_shape=out_shape,
        compiler_params=_params("arbitrary", "arbitrary"),
        name="norm1_inproj",
    )(x, mod, wn, w, wift, bcol, brow)


def _combine_inproj_call(pos, ys, x1, mod_prev, mod, wn, w, wift, bcol, brow):
    b, s, d = x1.shape
    tm = TOKEN_TILE
    tok, out_specs, out_shape = _inproj_specs(b, s, tm)
    full = lambda a: pl.BlockSpec(a.shape, lambda i, j, *_: (0,) * a.ndim)
    modspec = pl.BlockSpec((1, 6, d), lambda i, j, *_: (i, 0, 0))
    return pl.pallas_call(
        _combine_inproj_kernel,
        grid_spec=pltpu.PrefetchScalarGridSpec(
            num_scalar_prefetch=1,
            grid=(b, s // tm),
            in_specs=[pl.BlockSpec(memory_space=pl.ANY), tok(d), modspec, modspec, full(wn), full(w),
                      full(wift), full(bcol), full(brow)],
            out_specs=[tok(d)] + out_specs,
            scratch_shapes=[pltpu.VMEM((2, tm // SUBLANES, SUBLANES, d), F32),
                            pltpu.SemaphoreType.DMA((2,))],
        ),
        out_shape=[jax.ShapeDtypeStruct((b, s, d), F32)] + out_shape,
        compiler_params=_params("arbitrary", "arbitrary"),
        name="combine_norm1_inproj",
    )(pos, ys, x1, mod_prev, mod, wn, w, wift, bcol, brow)


def _mlstm_kernel(q_ref, kt_ref, v_ref, o_ref, ifc_ref, ifr_ref, cwq_ref, cbq_ref, cwk_ref, cbk_ref, wn_ref,
                  out_ref, qbuf, kbuf, cstate, mstate):
    @pl.when(pl.program_id(1) == 0)
    def _():
        qbuf[...] = jnp.zeros(qbuf.shape, qbuf.dtype)
        kbuf[...] = jnp.zeros(kbuf.shape, kbuf.dtype)
        cstate[...] = jnp.zeros(cstate.shape, F32)
        mstate[...] = jnp.zeros(mstate.shape, F32)

    heads = range(M_HEADS)
    carry = (qbuf[...], kbuf[...], [cstate[h] for h in heads], [mstate[h][:, 0:1] for h in heads])
    for c in range(q_ref.shape[1] // MLSTM_CHUNK):
        carry = _mlstm_chunk(c, carry, q_ref, kt_ref, v_ref, o_ref, ifc_ref, ifr_ref, cwq_ref, cbq_ref,
                             cwk_ref, cbk_ref, wn_ref, out_ref)
    qbuf[...], kbuf[...] = carry[0], carry[1]
    for h in heads:
        cstate[h] = carry[2][h]
        mstate[h] = jnp.broadcast_to(carry[3][h], (1, LANES))


def _mlstm_chunk(c, carry, q_ref, kt_ref, v_ref, o_ref, ifc_ref, ifr_ref, cwq_ref, cbq_ref, cwk_ref, cbk_ref,
                 wn_ref, out_ref):
    L = MLSTM_CHUNK
    span = slice(L * c, L * (c + 1))
    q_tail, k_prev, cst, m_prev = carry
    pad = q_tail.shape[0]
    heads = range(M_HEADS)

    xq = q_ref[0, span, :]
    q_all = jnp.concatenate([q_tail, xq], axis=0)
    xk = kt_ref[0, :, span]
    k_all = jnp.concatenate([k_prev, xk], axis=1)
    r_s = lax.broadcasted_iota(jnp.int32, (L, L + pad), 0)
    c_s = lax.broadcasted_iota(jnp.int32, (L, L + pad), 1)
    r_t = lax.broadcasted_iota(jnp.int32, (2 * L, L), 0)
    c_t = lax.broadcasted_iota(jnp.int32, (2 * L, L), 1)
    acc_q = cbq_ref[...] + cwq_ref[M_CONV - 1:M_CONV, :] * xq.astype(F32)
    acc_k = cbk_ref[...] + cwk_ref[M_CONV - 1] * xk.astype(F32)
    for j in range(M_CONV - 1):
        back = M_CONV - 1 - j
        shift_q = jnp.where(c_s == r_s + (pad - back), 1.0, 0.0).astype(BF16)
        acc_q = acc_q + cwq_ref[j:j + 1, :] * _dot(shift_q, q_all)
        shift_k = jnp.where(r_t == c_t + (L - back), 1.0, 0.0).astype(BF16)
        acc_k = acc_k + cwk_ref[j] * _dot(k_all, shift_k)
    q = acc_q * _sigmoid(acc_q) * (M_DQK ** -0.5)
    kf = acc_k * _sigmoid(acc_k)
    kb = kf.astype(BF16)
    vb = v_ref[0, span, :]

    r_i = lax.broadcasted_iota(jnp.int32, (L, L), 0)
    c_i = lax.broadcasted_iota(jnp.int32, (L, L), 1)
    causal = c_i <= r_i
    tril = jnp.where(causal, 1.0, 0.0).astype(BF16)
    triu = jnp.where(r_i <= c_i, 1.0, 0.0).astype(BF16)

    icol = ifc_ref[0, span, :]
    irow = ifr_ref[0, :, span]
    ch, cm, cl = _split3(icol)
    bcum_col = _dot(tril, ch) + _dot(tril, cm) + _dot(tril, cl)
    rh, rm, rl = _split3(irow)
    bcum_row = _dot(rh, triu) + _dot(rm, triu) + _dot(rl, triu)

    lane = lax.broadcasted_iota(jnp.int32, (1, LANES), 1)
    lo_half = lane < M_DQK
    hi_half = jnp.logical_not(lo_half)
    ones = jnp.ones((L, LANES), BF16)
    pair = lambda h: slice(LANES * (h // 2), LANES * (h // 2 + 1))

    qmask =[jnp.where(lo_half if h % 2 == 0 else hi_half, q[:, pair(h)], 0.0) for h in heads]
    vaug = [jnp.concatenate([vb[:, M_DV * h:M_DV * (h + 1)], ones], axis=1) for h in heads]
    bc_col = [bcum_col[:, M_HEADS + h:M_HEADS + h + 1] for h in heads]
    bc_row = [bcum_row[M_HEADS + h:M_HEADS + h + 1, :] for h in heads]
    i_row = [irow[h:h + 1, :] for h in heads]
    b_last = [bc_row[h][:, L - 1:L] for h in heads]

    s = [_dot(qmask[h].astype(BF16), kb[pair(h), :]) for h in heads]

    dlog = [jnp.where(causal, bc_col[h] - bc_row[h] + i_row[h], NEG_INF) for h in heads]
    m_inter = [bc_col[h] + m_prev[h] for h in heads]
    m_row = [jnp.maximum(m_inter[h], jnp.max(dlog[h], axis=-1, keepdims=True)) for h in heads]
    sc = [(s[h] * jnp.exp(dlog[h] - m_row[h])).astype(BF16) for h in heads]
    qi = [(jnp.exp(m_inter[h] - m_row[h]) * qmask[h]).astype(BF16) for h in heads]

    num = [_dot(sc[h], vaug[h]) + _dot(qi[h], cst[h].astype(BF16)) for h in heads]
    for h in heads:
        den = num[h][:, M_DV:]
        hh = num[h][:, :M_DV] / jnp.maximum(jnp.abs(den), jnp.exp(-m_row[h]))
        hn = _rms(hh) * wn_ref[:, M_DV * h:M_DV * (h + 1)]
        og = _sigmoid(o_ref[0, span, M_DV * h:M_DV * (h + 1)].astype(F32))
        out_ref[0, span, M_DV * h:M_DV * (h + 1)] = (hn * og).astype(BF16)

    w_state = [b_last[h] - bc_row[h] + i_row[h] for h in heads]
    m_loc = [jnp.max(w_state[h], axis=-1, keepdims=True) for h in heads]
    ak = [(jnp.exp(w_state[h] - m_loc[h]) * kf[pair(h), :]).astype(BF16) for h in heads]
    c_loc = [_dot(ak[h], vaug[h]) for h in heads]
    m_new = [jnp.maximum(b_last[h] + m_prev[h], m_loc[h]) for h in heads]
    c_new = [jnp.exp(b_last[h] + m_prev[h] - m_new[h]) * cst[h] + jnp.exp(m_loc[h] - m_new[h]) * c_loc[h]
             for h in heads]
    return xq[L - pad:L, :], xk, c_new, m_new


def _mlstm_call(q, kt, v, o, ifc, ifr, cwq, cbq, cwk, cbk, w_mnorm):
    b, s, _ = q.shape
    L = MLSTM_CHUNK
    span = MLSTM_CHUNKS_PER_STEP * L
    tok = lambda w_: pl.BlockSpec((1, span, w_), lambda i, j: (i, j, 0))
    rows = lambda n: pl.BlockSpec((1, n, span), lambda i, j: (i, 0, j))
    full = lambda a: pl.BlockSpec(a.shape, lambda i, j: (0,) * a.ndim)
    return pl.pallas_call(
        _mlstm_kernel,
        grid=(b, s // span),
        in_specs=[tok(M_QK_W), rows(M_QK_W), tok(M_V_W), tok(M_V_W), tok(LANES), rows(2 * M_HEADS),
                  full(cwq), full(cbq), full(cwk), full(cbk), full(w_mnorm)],
        out_specs=tok(M_V_W),
        out_shape=jax.ShapeDtypeStruct((b, s, M_V_W), BF16),
        scratch_shapes=[pltpu.VMEM((2 * SUBLANES, M_QK_W), BF16),
                        pltpu.VMEM((M_QK_W, L), BF16),
                        pltpu.VMEM((M_HEADS, LANES, 2 * M_DV), F32),
                        pltpu.VMEM((M_HEADS, 1, LANES), F32)],
        compiler_params=_params("arbitrary", "arbitrary"),
        name="mlstm",
    )(q, kt, v, o, ifc, ifr, cwq, cbq, cwk, cbk, w_mnorm)


def _swa_kernel(sink_ref, q_ref, kvp_ref, kvc_ref, bias_ref, out_ref):
    W = WINDOW
    first = pl.program_id(1) == 0
    kv = [kvp_ref[0]] + [kvc_ref[0, W * u:W * (u + 1), :] for u in range(SWA_BLOCKS)]
    lane = lax.broadcasted_iota(jnp.int32, (1, LANES), 1)
    lo_half = lane < A_HEAD_DIM
    hi_half = jnp.logical_not(lo_half)
    keys, v_half = [], []
    for u in range(SWA_BLOCKS):
        keys.append(jnp.concatenate([kv[u][:, :A_KV_W], kv[u + 1][:, :A_KV_W]], axis=0))
        vals = jnp.concatenate([kv[u][:, A_KV_W:], kv[u + 1][:, A_KV_W:]], axis=0)
        zero = jnp.zeros_like(vals)
        v_half.append((jnp.where(lo_half, vals, zero), jnp.where(hi_half, vals, zero)))
    table = [jnp.where(first, 1, 0)] + [0] * (SWA_BLOCKS - 1)

    tiles = range(A_HEADS // 2)
    slots = [(u, j, p) for u in range(SWA_BLOCKS) for j in tiles for p in range(2)]
    head = {ujp: A_HEAD_ORDER[2 * ujp[1] + ujp[2]] for ujp in slots}
    scale = jnp.asarray(A_HEAD_DIM ** -0.5, BF16)
    qt = {(u, j): q_ref[0, W * u:W * (u + 1), LANES * j:LANES * (j + 1)] * scale
          for u in range(SWA_BLOCKS) for j in tiles}
    qm = {(u, j, p): jnp.where(lo_half if p == 0 else hi_half, qt[u, j], jnp.zeros_like(qt[u, j]))
          for u, j, p in slots}
    s = {ujp: _dot(qm[ujp], keys[ujp[0]], _NT) + bias_ref[table[ujp[0]], head[ujp]] for ujp in slots}
    m = {ujp: jnp.maximum(jnp.max(s[ujp], axis=-1, keepdims=True), sink_ref[head[ujp]]) for ujp in slots}
    e = {ujp: jnp.exp(s[ujp] - m[ujp]) for ujp in slots}
    denom = {ujp: jnp.sum(e[ujp], axis=-1, keepdims=True) + jnp.exp(sink_ref[head[ujp]] - m[ujp])
             for ujp in slots}
    pv = {(u, j, p): _dot(e[u, j, p].astype(BF16), v_half[u][p]) for u, j, p in slots}
    for u in range(SWA_BLOCKS):
        for j in tiles:
            out = pv[u, j, 0] / denom[u, j, 0] + pv[u, j, 1] / denom[u, j, 1]
            out_ref[0, W * u:W * (u + 1), LANES * j:LANES * (j + 1)] = out.astype(BF16)


def _swa_call(sinks, qa, kva, bias):
    b, s, _ = qa.shape
    W = WINDOW
    rows = SWA_BLOCKS * W
    return pl.pallas_call(
        _swa_kernel,
        grid=(b, s // rows),
        in_specs=[pl.BlockSpec(memory_space=pltpu.SMEM),
                  pl.BlockSpec((1, rows, A_Q_W), lambda i, j: (i, j, 0)),
                  pl.BlockSpec((1, W, 2 * A_KV_W), lambda i, j: (i, jnp.maximum(SWA_BLOCKS * j - 1, 0), 0)),
                  pl.BlockSpec((1, rows, 2 * A_KV_W), lambda i, j: (i, j, 0)),
                  pl.BlockSpec(bias.shape, lambda i, j: (0,) * bias.ndim)],
        out_specs=pl.BlockSpec((1, rows, A_Q_W), lambda i, j: (i, j, 0)),
        out_shape=jax.ShapeDtypeStruct((b, s, A_Q_W), BF16),
        compiler_params=_params("arbitrary", "arbitrary"),
        name="swa",
    )(sinks, qa, kva, kva, bias)


def _post_kernel(hm_ref, ha_ref, g_ref, x_ref, mod_ref, wbm_ref, wba_ref, wo_ref, wn2_ref,
                 wrh_ref, wrl_ref, rb_ref,
                 x1_ref, pay_ref, cls_ref, rank_ref, cnt_ref, carry):
    tm = x_ref.shape[1]

    @pl.when((pl.program_id(0) == 0) & (pl.program_id(1) == 0))
    def _():
        carry[...] = jnp.zeros(carry.shape, F32)

    g = g_ref[0]
    pm = _dot(hm_ref[0], wbm_ref[...])
    pa = _dot(ha_ref[0], wba_ref[...])
    merged = (_sigmoid(g[:, :D_MODEL]) * pm.astype(BF16)
              + _sigmoid(g[:, D_MODEL:]) * pa.astype(BF16))
    mo = _dot(merged, wo_ref[...])
    x1 = x_ref[0] + mod_ref[0, 2:3, :] * mo
    x1_ref[0] = x1
    h2 = _rms(x1) * (wn2_ref[...] * (1.0 + mod_ref[0, 4:5, :])) + mod_ref[0, 3:4, :]
    pay_ref[0, :, :D_MODEL] = h2

    hh, hl = _split2(h2)
    wrh = wrh_ref[...]
    logits = _dot(wrh, hh, _NT) + _dot(wrh, hl, _NT) + _dot(wrl_ref[...], hh, _NT)
    scores = _sigmoid(logits)
    sel = scores + rb_ref[...]

    def row(a, r):
        return a[r:r + 1, :]

    grp = []
    for gi in range(N_GROUPS):
        v = [row(sel, gi * EXPERTS_PER_GROUP + k) for k in range(EXPERTS_PER_GROUP)]
        best = v[0] + v[1]
        for a, b in _PAIRS[1:]:
            best = jnp.maximum(best, v[a] + v[b])
        grp.append(best)
    gbest = grp[0]
    gsel = jnp.zeros_like(gbest)
    for gi in range(1, N_GROUPS):
        take = grp[gi] > gbest
        gbest = jnp.where(take, grp[gi], gbest)
        gsel = jnp.where(take, float(gi), gsel)

    sv, gv = [], []
    for k in range(EXPERTS_PER_GROUP):
        s_k = row(sel, k)
        g_k = row(scores, k)
        for gi in range(1, N_GROUPS):
            hit = gsel == float(gi)
            s_k = jnp.where(hit, row(sel, gi * EXPERTS_PER_GROUP + k), s_k)
            g_k = jnp.where(hit, row(scores, gi * EXPERTS_PER_GROUP + k), g_k)
        sv.append(s_k)
        gv.append(g_k)

    def argmax4(vals):
        bv, bi = vals[0], jnp.zeros_like(vals[0])
        for k in range(1, EXPERTS_PER_GROUP):
            take = vals[k] > bv
            bv = jnp.where(take, vals[k], bv)
            bi = jnp.where(take, float(k), bi)
        return bi

    i1 = argmax4(sv)
    i2 = argmax4([jnp.where(i1 == float(k), -jnp.inf, sv[k]) for k in range(EXPERTS_PER_GROUP)])
    w1 = jnp.zeros_like(i1)
    w2 = jnp.zeros_like(i1)
    for k in range(EXPERTS_PER_GROUP):
        w1 = jnp.where(i1 == float(k), gv[k], w1)
        w2 = jnp.where(i2 == float(k), gv[k], w2)
    wsum = w1 + w2
    w1 = w1 / wsum
    w2 = w2 / wsum
    code = i1 * float(EXPERTS_PER_GROUP) + i2
    pair_idx = jnp.zeros_like(code)
    gate_a, gate_b = w1, w2
    for k, (a, b) in enumerate(_CLASS_PAIRS):
        fwd = code == float(a * EXPERTS_PER_GROUP + b)
        rev = code == float(b * EXPERTS_PER_GROUP + a)
        pair_idx = jnp.where(fwd, float(k), jnp.where(rev, float(k), pair_idx))
        gate_a = jnp.where(rev, w2, gate_a)
        gate_b = jnp.where(rev, w1, gate_b)
    cls_f = gsel * float(PAIRS_PER_GROUP) + pair_idx
    cls_ref[0] = cls_f.astype(jnp.int32)

    grow = lax.broadcasted_iota(jnp.int32, (LANES, tm), 0)
    gmat = jnp.where(grow == 0, gate_a, jnp.where(grow == 1, gate_b, 0.0))
    pay_ref[0, :, D_MODEL:] = gmat.T

    crow = lax.broadcasted_iota(jnp.int32, (CLASS_ROWS, tm), 0).astype(F32)
    onehot = crow == cls_f
    r_i = lax.broadcasted_iota(jnp.int32, (tm, tm), 0)
    c_i = lax.broadcasted_iota(jnp.int32, (tm, tm), 1)
    upper = jnp.where(r_i <= c_i, 1.0, 0.0).astype(BF16)
    cum = _dot(jnp.where(onehot, 1.0, 0.0).astype(BF16), upper)
    before = carry[:, 0:1]
    rank = jnp.sum(jnp.where(onehot, cum - 1.0 + before, 0.0), axis=0, keepdims=True)
    rank_ref[0] = rank.astype(jnp.int32)
    total = before + cum[:, tm - 1:tm]
    carry[...] = jnp.broadcast_to(total, carry.shape)
    cnt_ref[...] = jnp.broadcast_to(total, cnt_ref.shape)


def _post_call(hm, ha, g, x, mod, wbm, wba, wo, wn2, wrh, wrl, rb):
    b, s, d = x.shape
    tm = ROUTER_TILE
    tok = lambda w_: pl.BlockSpec((1, tm, w_), lambda i, j: (i, j, 0))
    full = lambda a: pl.BlockSpec(a.shape, lambda i, j: (0,) * a.ndim)
    lanes = pl.BlockSpec((1, 1, tm), lambda i, j: (i, 0, j))
    return pl.pallas_call(
        _post_kernel,
        grid=(b, s // tm),
        in_specs=[tok(M_V_W), tok(A_Q_W), tok(2 * d), tok(d),
                  pl.BlockSpec((1, 6, d), lambda i, j: (i, 0, 0)),
                  full(wbm), full(wba), full(wo), full(wn2), full(wrh), full(wrl), full(rb)],
        out_specs=[tok(d), tok(PAYLOAD_W), lanes, lanes,
                   pl.BlockSpec((CLASS_ROWS, LANES), lambda i, j: (0, 0))],
        out_shape=[jax.ShapeDtypeStruct((b, s, d), F32),
                   jax.ShapeDtypeStruct((b, s, PAYLOAD_W), F32),
                   jax.ShapeDtypeStruct((b, 1, s), jnp.int32),
                   jax.ShapeDtypeStruct((b, 1, s), jnp.int32),
                   jax.ShapeDtypeStruct((CLASS_ROWS, LANES), F32)],
        scratch_shapes=[pltpu.VMEM((CLASS_ROWS, LANES), F32)],
        compiler_params=_params("arbitrary", "arbitrary"),
        name="merge_outproj_router",
    )(hm, ha, g, x, mod, wbm, wba, wo, wn2, wrh, wrl, rb)


def _zero_fill(off_s, cnt_s, nu_s, xs_out, zbuf, zsem, *, wait):
    chunk = zbuf.shape[0]

    def zero_rows(first, n):
        cp = pltpu.make_async_copy(zbuf.at[pl.ds(0, n), :], xs_out.at[pl.ds(first, n), :], zsem)
        cp.wait() if wait else cp.start()

    def per_class(c, carry):
        n = cnt_s[c]
        fill = (-n) & (EXPERT_TILE - 1)
        head = fill & (SUBLANES - 1)
        for k in range(SUBLANES - 1):
            pl.when(k < head)(lambda k=k: zero_rows(off_s[c] + n + k, 1))
        cur = off_s[c] + n + head
        p = chunk
        while p >= SUBLANES:
            pl.when((fill & p) != 0)(lambda cur=cur, p=p: zero_rows(pl.multiple_of(cur, SUBLANES), p))
            cur = cur + (fill & p)
            p //= 2
        return carry

    lax.fori_loop(0, N_CLASSES, per_class, 0)

    def per_chunk(r, carry):
        zero_rows(pl.multiple_of(r * chunk, SUBLANES), chunk)
        return carry

    per_tile = EXPERT_TILE // chunk
    lax.fori_loop(nu_s[0] * per_tile, (xs_out.shape[0] // EXPERT_TILE) * per_tile, per_chunk, 0)


def _dispatch_kernel(pos_s, off_s, cnt_s, nu_s, pay_ref, xs_out, zbuf, sem, zsem):
    @pl.when(pl.program_id(0) == 0)
    def _():
        zbuf[...] = jnp.zeros(zbuf.shape, F32)
        _zero_fill(off_s, cnt_s, nu_s, xs_out, zbuf, zsem, wait=False)
        _zero_fill(off_s, cnt_s, nu_s, xs_out, zbuf, zsem, wait=True)

    first_token = pl.program_id(0) * pay_ref.shape[0] * SUBLANES
    for wait in (False, True):
        _row_copies(pos_s, first_token, xs_out, pay_ref, sem, to_sorted=True, wait=wait)


def _dispatch_call(pos, off, cnt, n_used, payload, npad):
    t8, _, width = payload.shape
    tiles = ROW_TILE // SUBLANES
    return pl.pallas_call(
        _dispatch_kernel,
        grid_spec=pltpu.PrefetchScalarGridSpec(
            num_scalar_prefetch=4,
            grid=(t8 // tiles,),
            in_specs=[pl.BlockSpec((tiles, SUBLANES, width), lambda i, *_: (i, 0, 0))],
            out_specs=pl.BlockSpec(memory_space=pl.ANY),
            scratch_shapes=[pltpu.VMEM((EXPERT_TILE // 2, width), F32),
                            pltpu.SemaphoreType.DMA(()), pltpu.SemaphoreType.DMA(())],
        ),
        out_shape=jax.ShapeDtypeStruct((npad, width), F32),
        compiler_params=_params("arbitrary"),
        name="moe_dispatch",
    )(pos, off, cnt, n_used, payload)


def _moe_kernel(ea_s, eb_s, nu_s, x_ref, wga, wua, wda, wgb, wub, wdb, y_ref):
    del ea_s, eb_s

    @pl.when(pl.program_id(0) < nu_s[0])
    def _():
        x = x_ref[:, :D_MODEL].astype(BF16)
        gate_a = x_ref[:, D_MODEL:D_MODEL + 1]
        gate_b = x_ref[:, D_MODEL + 1:D_MODEL + 2]

        up = [(_dot(x, wg[0]), _dot(x, wu[0])) for wg, wu in ((wga, wua), (wgb, wub))]
        act = [(hg * _sigmoid(hg) * hu).astype(BF16) for hg, hu in up]
        y_ref[...] = gate_a * _dot(act[0], wda[0]) + gate_b * _dot(act[1], wdb[0])

    @pl.when(pl.program_id(0) >= nu_s[0])
    def _():
        y_ref[...] = jnp.zeros(y_ref.shape, F32)


def _moe_call(tile_ea, tile_eb, n_used, xs, wg, wu, wd, layer):
    npad = xs.shape[0]
    te = EXPERT_TILE
    last = lambda i, nu: jnp.minimum(i, nu[0] - 1)
    first = layer * N_EXPERTS
    pick = lambda tab: (lambda i, ea, eb, nu: (first + (ea, eb)[tab][i], 0, 0))
    up = lambda tab: pl.BlockSpec((1, D_MODEL, D_EXPERT), pick(tab))
    down = lambda tab: pl.BlockSpec((1, D_EXPERT, D_MODEL), pick(tab))
    return pl.pallas_call(
        _moe_kernel,
        grid_spec=pltpu.PrefetchScalarGridSpec(
            num_scalar_prefetch=3,
            grid=(npad // te,),
            in_specs=[pl.BlockSpec((te, PAYLOAD_W), lambda i, ea, eb, nu: (last(i, nu), 0)),
                      up(0), up(0), down(0), up(1), up(1), down(1)],
            out_specs=pl.BlockSpec((te, D_MODEL), lambda i, ea, eb, nu: (i, 0)),
        ),
        out_shape=jax.ShapeDtypeStruct((npad, D_MODEL), F32),
        compiler_params=_params("arbitrary"),
        name="moe_experts",
    )(tile_ea, tile_eb, n_used, xs, wg, wu, wd, wg, wu, wd)


def _combine_kernel(pos_s, ys_hbm, x1_ref, mod_ref, wf_ref, out_ref, ybuf, sem):
    x2 = _moe_residual(pos_s, ys_hbm, x1_ref, mod_ref, ybuf, sem)
    out_ref[0] = _rms(x2) * wf_ref[...]


def _combine_call(pos, ys, x1, mod, wf):
    b, s, d = x1.shape
    rows = ROW_TILE
    return pl.pallas_call(
        _combine_kernel,
        grid_spec=pltpu.PrefetchScalarGridSpec(
            num_scalar_prefetch=1,
            grid=(b, s // rows),
            in_specs=[pl.BlockSpec(memory_space=pl.ANY),
                      pl.BlockSpec((1, rows, d), lambda i, j, *_: (i, j, 0)),
                      pl.BlockSpec((1, 6, d), lambda i, j, *_: (i, 0, 0)),
                      pl.BlockSpec((1, d), lambda i, j, *_: (0, 0))],
            out_specs=pl.BlockSpec((1, rows, d), lambda i, j, *_: (i, j, 0)),
            scratch_shapes=[pltpu.VMEM((2, rows // SUBLANES, SUBLANES, d), F32),
                            pltpu.SemaphoreType.DMA((2,))],
        ),
        out_shape=jax.ShapeDtypeStruct((b, s, d), F32),
        compiler_params=_params("arbitrary", "arbitrary"),
        name="moe_combine",
    )(pos, ys, x1, mod, wf)


def _t5_bucket(n):
    max_exact = N_BUCKETS // 2
    large = max_exact + (np.log(np.maximum(n, 1) / max_exact)
                         / np.log(MAX_DISTANCE / max_exact)
                         * (N_BUCKETS - max_exact)).astype(np.int32)
    large = np.minimum(large, N_BUCKETS - 1)
    return np.where(n < max_exact, n, large).astype(np.int32)


def _bias_table(rel_bias):
    dist = np.arange(WINDOW)[:, None] + WINDOW - np.arange(2 * WINDOW)[None, :]
    bucket = _t5_bucket(np.maximum(dist, 0)).reshape(1, -1)
    onehot = (jnp.arange(N_BUCKETS, dtype=jnp.int32)[:, None] == jnp.asarray(bucket)).astype(F32)
    tab = jnp.dot(rel_bias.astype(F32).T, onehot, precision=lax.Precision.HIGHEST)
    tab = tab.reshape(A_HEADS, WINDOW, 2 * WINDOW)
    in_window = (dist >= 0) & (dist < WINDOW)
    has_key = np.arange(2 * WINDOW)[None, :] >= WINDOW
    return jnp.stack([jnp.where(jnp.asarray(in_window), tab, NEG_INF),
                      jnp.where(jnp.asarray(in_window & has_key), tab, NEG_INF)])


def _layout_w_in(w_in_l):
    pts = np.cumsum([0, M_QK_W, M_QK_W, M_V_W, M_V_W, M_HEADS, M_HEADS,
                     A_Q_W, A_KV_W, A_KV_W, D_MODEL, D_MODEL])
    wb = w_in_l.astype(BF16)
    col = lambda k: wb[:, pts[k]:pts[k + 1]]
    head = lambda h: wb[:, pts[6] + A_HEAD_DIM * h:pts[6] + A_HEAD_DIM * (h + 1)]
    w_if = jnp.concatenate([col(4), col(5)], axis=1)
    w_if_pad = jnp.pad(w_if, ((0, 0), (0, LANES - 2 * M_HEADS)))
    w = jnp.concatenate([col(0), col(2), col(3)] + [head(h) for h in A_HEAD_ORDER]
                        + [col(7), col(8), col(9), col(10), w_if_pad], axis=1)
    w_t = jnp.concatenate([col(1), w_if], axis=1).T
    return w, jnp.pad(w_t, ((0, -w_t.shape[0] % (2 * SUBLANES)), (0, 0)))


def _tile_tables(counts, n_tiles):
    te = EXPERT_TILE
    tiles = (counts + te - 1) // te
    ends = jnp.cumsum(tiles)
    off = (ends - tiles) * te
    n_used = ends[-1]
    tile_idx = jnp.minimum(jnp.arange(n_tiles, dtype=jnp.int32), n_used - 1)
    tile_cls = jnp.sum((ends[None, :] <= tile_idx[:, None]).astype(jnp.int32), axis=1)
    tile_cls = jnp.minimum(tile_cls, N_CLASSES - 1)
    off_pad = jnp.pad(off, (0, CLASS_ROWS - N_CLASSES)).astype(jnp.int32)
    return (off_pad, jnp.asarray(_CLASS_EA)[tile_cls], jnp.asarray(_CLASS_EB)[tile_cls],
            n_used.reshape(1).astype(jnp.int32))


def kernel(x, c, w_ada, b_ada, w_norm1, w_in, conv_w, conv_b, b_igate, b_fgate, w_mnorm, sinks, rel_bias,
           w_br_m, w_br_a, w_out, w_norm2, w_router, router_bias, w_gate_e, w_up_e, w_down_e, w_final):
    b, s, d = x.shape
    depth = w_ada.shape[0]
    t = b * s
    n_tiles = t // EXPERT_TILE + N_CLASSES
    npad = n_tiles * EXPERT_TILE

    mod_all = _ada_call(jnp.pad(c, ((0, 8 - b), (0, 0))), w_ada, b_ada)[:, :b]
    bias_tab = _bias_table(rel_bias)
    wrt = w_router.T
    wrh = wrt.astype(BF16)
    wrl = (wrt - wrh.astype(F32)).astype(BF16)
    rb = router_bias.reshape(N_EXPERTS, 1).astype(F32)
    row = lambda v: v.reshape(1, -1).astype(F32)

    wg_all, wu_all, wd_all = _cast_call(w_gate_e), _cast_call(w_up_e), _cast_call(w_down_e)

    moe = None
    for l in range(depth):
        mod = mod_all[l].reshape(b, 6, d)
        w_l, wift = _layout_w_in(w_in[l])
        gate_bias = jnp.concatenate([b_igate[l], b_fgate[l]]).astype(F32)
        bcol = jnp.pad(gate_bias, (0, LANES - 2 * M_HEADS)).reshape(1, LANES)
        brow = gate_bias.reshape(2 * M_HEADS, 1)

        proj_args = (mod, row(w_norm1[l]), w_l, wift, bcol, brow)
        if moe is None:
            q, kt, v, o, qa, kva, g, ifc, ifr = _inproj_call(x, *proj_args)
        else:
            x, q, kt, v, o, qa, kva, g, ifc, ifr = _combine_inproj_call(*moe, *proj_args)
        cw, cb = conv_w[l].astype(F32), conv_b[l].astype(F32)
        cwk = jnp.broadcast_to(cw[:, M_QK_W:, None], (M_CONV, M_QK_W, MLSTM_CHUNK))
        cbk = jnp.broadcast_to(cb[M_QK_W:, None], (M_QK_W, MLSTM_CHUNK))
        hm = _mlstm_call(q, kt, v, o, ifc, ifr, cw[:, :M_QK_W], row(cb[:M_QK_W]), cwk, cbk, row(w_mnorm[l]))
        ha = _swa_call(sinks[l].astype(F32), qa, kva, bias_tab)

        wba = jnp.concatenate([w_br_a[l][A_HEAD_DIM * h:A_HEAD_DIM * (h + 1)] for h in A_HEAD_ORDER])
        x1, payload, cls, rank, cnt = _post_call(
            hm, ha, g, x, mod, w_br_m[l].astype(BF16), wba.astype(BF16), w_out[l].astype(BF16),
            row(w_norm2[l]), wrh, wrl, rb)

        counts = cnt[:, 0].astype(jnp.int32)
        off, tile_ea, tile_eb, n_used = _tile_tables(counts[:N_CLASSES], n_tiles)
        hit = cls.reshape(t, 1) == jnp.arange(CLASS_ROWS, dtype=jnp.int32)[None, :]
        pos = jnp.sum(jnp.where(hit, off[None, :], 0), axis=1) + rank.reshape(t)
        xs = _dispatch_call(pos, off, counts, n_used,
                            payload.reshape(t // SUBLANES, SUBLANES, PAYLOAD_W), npad)
        ys = _moe_call(tile_ea, tile_eb, n_used, xs, wg_all, wu_all, wd_all, layer=l)
        moe = (pos, ys, x1, mod)
    return _combine_call(*moe, row(w_final))
```

```python
import numpy as np
import jax
import jax.numpy as jnp
from jax import lax
from jax.experimental import pallas as pl
from jax.experimental.pallas import tpu as pltpu

F32 = jnp.float32
BF16 = jnp.bfloat16

D_MODEL = 1024
M_HEADS = 4
M_DQK = 64
M_DV = 128
M_CONV = 4
M_QK_W = M_HEADS * M_DQK
M_V_W = M_HEADS * M_DV
A_HEADS = 8
A_KV_HEADS = 2
A_HEAD_DIM = 64
A_Q_W = A_HEADS * A_HEAD_DIM
A_KV_W = A_KV_HEADS * A_HEAD_DIM
WINDOW = 128
N_BUCKETS = 32
MAX_DISTANCE = 128
N_EXPERTS = 16
N_GROUPS = 4
EXPERTS_PER_GROUP = N_EXPERTS // N_GROUPS
D_EXPERT = 512
EPS = 1e-6
NEG_INF = -1e30

LANES = 128
SUBLANES = 8
MLSTM_CHUNK = 128
MLSTM_CHUNKS_PER_STEP = 8
SWA_BLOCKS = 8
TOKEN_TILE = 512
ROUTER_TILE = 1024
EXPERT_TILE = 256
ROW_TILE = 1024
PAIRS_PER_GROUP = 6
N_CLASSES = N_GROUPS * PAIRS_PER_GROUP
CLASS_ROWS = 32
PAYLOAD_W = D_MODEL + LANES
VMEM_LIMIT = 48 * 1024 * 1024

C_Q = 0
C_V = C_Q + M_QK_W
C_O = C_V + M_V_W
C_QA = C_O + M_V_W
C_KVA = C_QA + A_Q_W
C_G = C_KVA + 2 * A_KV_W
C_IF = C_G + 2 * D_MODEL
C_END = C_IF + LANES

A_HEAD_ORDER = (0, 4, 1, 5, 2, 6, 3, 7)

_PAIRS = [(a, b) for a in range(EXPERTS_PER_GROUP) for b in range(a + 1, EXPERTS_PER_GROUP)]
_CLASS_PAIRS = ((0, 1), (2, 1), (2, 0), (3, 0), (3, 1), (3, 2))
assert sorted(tuple(sorted(p)) for p in _CLASS_PAIRS) == _PAIRS
_CLASS_EA = np.array([g * EXPERTS_PER_GROUP + a for g in range(N_GROUPS) for a, _ in _CLASS_PAIRS], np.int32)
_CLASS_EB = np.array([g * EXPERTS_PER_GROUP + b for g in range(N_GROUPS) for _, b in _CLASS_PAIRS], np.int32)

_NT = (((1,), (1,)), ((), ()))


def _sigmoid(x):
    return 0.5 * jnp.tanh(0.5 * x) + 0.5


def _log_sigmoid(x):
    return -(jnp.maximum(-x, 0.0) + jnp.log1p(jnp.exp(-jnp.abs(x))))


def _dot(a, b, dims=None):
    if dims is None:
        return jnp.dot(a, b, preferred_element_type=F32)
    return lax.dot_general(a, b, dims, preferred_element_type=F32)


def _split2(a):
    hi = a.astype(BF16)
    lo = (a - hi.astype(F32)).astype(BF16)
    return hi, lo


def _split3(a):
    hi = a.astype(BF16)
    r = a - hi.astype(F32)
    mid = r.astype(BF16)
    lo = (r - mid.astype(F32)).astype(BF16)
    return hi, mid, lo


def _dot_hi(a, b, dims=None):
    ah, al = _split2(a)
    bh, bl = _split2(b)
    return _dot(ah, bh, dims) + _dot(ah, bl, dims) + _dot(al, bh, dims)


def _rms(x):
    return x * lax.rsqrt(jnp.mean(x * x, axis=-1, keepdims=True) + EPS)


def _params(*sem):
    return pltpu.CompilerParams(dimension_semantics=sem, vmem_limit_bytes=VMEM_LIMIT)


def _cast_kernel(w_ref, o_ref):
    o_ref[...] = w_ref[...].astype(BF16)


def _cast_call(w):
    w = w.reshape((-1,) + w.shape[2:])
    e, a, b = w.shape
    blk = 2
    spec = pl.BlockSpec((blk, a, b), lambda i: (i, 0, 0))
    return pl.pallas_call(
        _cast_kernel,
        grid=(e // blk,),
        in_specs=[spec],
        out_specs=spec,
        out_shape=jax.ShapeDtypeStruct(w.shape, BF16),
        compiler_params=_params("arbitrary"),
        name="expert_weights_bf16",
    )(w)


def _ada_kernel(c_ref, w_ref, b_ref, o_ref):
    c = c_ref[...]
    cond = c * _sigmoid(c)
    o_ref[0] = _dot_hi(cond, w_ref[0]) + b_ref[0]


def _ada_call(c8, w_ada, b_ada):
    depth = w_ada.shape[0]
    return pl.pallas_call(
        _ada_kernel,
        grid=(depth, 6),
        in_specs=[
            pl.BlockSpec((8, D_MODEL), lambda l, j: (0, 0)),
            pl.BlockSpec((1, D_MODEL, D_MODEL), lambda l, j: (l, 0, j)),
            pl.BlockSpec((1, 1, D_MODEL), lambda l, j: (l, 0, j)),
        ],
        out_specs=pl.BlockSpec((1, 8, D_MODEL), lambda l, j: (l, 0, j)),
        out_shape=jax.ShapeDtypeStruct((depth, 8, 6 * D_MODEL), F32),
        compiler_params=_params("arbitrary", "arbitrary"),
        name="adaln_mod",
    )(c8, w_ada, b_ada.reshape(depth, 1, 6 * D_MODEL))


def _project(x, mod_ref, wn_ref, w_ref, wift_ref, bcol_ref, brow_ref, outs):
    q_ref, kt_ref, v_ref, o_ref, qa_ref, kva_ref, g_ref, ifc_ref, ifr_ref = outs
    h = _rms(x) * (wn_ref[...] * (1.0 + mod_ref[0, 1:2, :])) + mod_ref[0, 0:1, :]
    hb = h.astype(BF16)

    def seg(a, b):
        return _dot(hb, w_ref[:, a:b])

    q_ref[0] = seg(C_Q, C_V).astype(BF16)
    v_ref[0] = seg(C_V, C_O).astype(BF16)
    o_ref[0] = seg(C_O, C_QA).astype(BF16)
    qa_ref[0] = seg(C_QA, C_KVA).astype(BF16)
    kva_ref[0] = seg(C_KVA, C_G).astype(BF16)
    g_ref[0] = seg(C_G, C_IF).astype(BF16)
    zc = seg(C_IF, C_END) + bcol_ref[...]
    lane = lax.broadcasted_iota(jnp.int32, zc.shape, 1)
    ifc_ref[0] = jnp.where((lane >= M_HEADS) & (lane < 2 * M_HEADS), _log_sigmoid(zc), zc)
    zt = _dot(wift_ref[...], hb, _NT)
    kt_ref[0] = zt[:M_QK_W].astype(BF16)
    zr = zt[M_QK_W:M_QK_W + 2 * M_HEADS] + brow_ref[...]
    row = lax.broadcasted_iota(jnp.int32, zr.shape, 0)
    ifr_ref[0] = jnp.where(row >= M_HEADS, _log_sigmoid(zr), zr)


def _inproj_kernel(x_ref, mod_ref, wn_ref, w_ref, wift_ref, bcol_ref, brow_ref, *outs):
    _project(x_ref[0], mod_ref, wn_ref, w_ref, wift_ref, bcol_ref, brow_ref, outs)


def _row_copies(pos_s, first_token, sorted_hbm, tiles, sem, *, to_sorted, wait):
    def copy(i, k, pos):
        pair = (tiles.at[i, pl.ds(k, 1), :], sorted_hbm.at[pl.ds(pos, 1), :])
        return pltpu.make_async_copy(*(pair if to_sorted else pair[::-1]), sem)

    if wait:
        def body(i, carry):
            for k in range(SUBLANES):
                copy(0, k, 0).wait()
            return carry

        lax.fori_loop(0, tiles.shape[0], body, 0, unroll=2)
        return
    for i in range(tiles.shape[0]):
        for k in range(SUBLANES):
            copy(i, k, pos_s[first_token + i * SUBLANES + k]).start(priority=k % 2)


def _moe_residual(pos_s, ys_hbm, x1_ref, mod_ref, ybuf, sem):
    rows = x1_ref.shape[1]
    step = pl.program_id(0) * pl.num_programs(1) + pl.program_id(1)
    n_steps = pl.num_programs(0) * pl.num_programs(1)
    slot = lax.rem(step, 2)

    def fetch(first_token, slot_, wait):
        _row_copies(pos_s, first_token, ys_hbm, ybuf.at[slot_], sem.at[slot_], to_sorted=False, wait=wait)

    @pl.when(step == 0)
    def _():
        fetch(0, 0, False)

    @pl.when(step + 1 < n_steps)
    def _():
        fetch((step + 1) * rows, 1 - slot, False)

    fetch(0, slot, True)
    y = ybuf[slot].reshape(rows, ybuf.shape[-1])
    return x1_ref[0] + mod_ref[0, 5:6, :] * y


def _combine_inproj_kernel(pos_s, ys_hbm, x1_ref, modp_ref, mod_ref, wn_ref, w_ref,
                           wift_ref, bcol_ref, brow_ref, xo_ref, *rest):
    outs, (ybuf, sem) = rest[:-2], rest[-2:]
    x = _moe_residual(pos_s, ys_hbm, x1_ref, modp_ref, ybuf, sem)
    xo_ref[0] = x
    _project(x, mod_ref, wn_ref, w_ref, wift_ref, bcol_ref, brow_ref, outs)


def _inproj_specs(b, s, tm):
    tok = lambda w_: pl.BlockSpec((1, tm, w_), lambda i, j, *_: (i, j, 0))
    sds = lambda w_, dt: jax.ShapeDtypeStruct((b, s, w_), dt)
    rows = lambda n: pl.BlockSpec((1, n, tm), lambda i, j, *_: (i, 0, j))
    out_specs = [tok(M_QK_W), rows(M_QK_W), tok(M_V_W), tok(M_V_W), tok(A_Q_W), tok(2 * A_KV_W),
                 tok(2 * D_MODEL), tok(LANES), rows(2 * M_HEADS)]
    out_shape = [sds(M_QK_W, BF16), jax.ShapeDtypeStruct((b, M_QK_W, s), BF16), sds(M_V_W, BF16),
                 sds(M_V_W, BF16), sds(A_Q_W, BF16), sds(2 * A_KV_W, BF16), sds(2 * D_MODEL, BF16),
                 sds(LANES, F32), jax.ShapeDtypeStruct((b, 2 * M_HEADS, s), F32)]
    return tok, out_specs, out_shape


def _inproj_call(x, mod, wn, w, wift, bcol, brow):
    b, s, d = x.shape
    tm = TOKEN_TILE
    tok, out_specs, out_shape = _inproj_specs(b, s, tm)
    full = lambda a: pl.BlockSpec(a.shape, lambda i, j: (0,) * a.ndim)
    return pl.pallas_call(
        _inproj_kernel,
        grid=(b, s // tm),
        in_specs=[tok(d), pl.BlockSpec((1, 6, d), lambda i, j: (i, 0, 0)), full(wn), full(w),
                  full(wift), full(bcol), full(brow)],
        out_specs=out_specs,
        out_shape=out_shape,
        compiler_params=_params("arbitrary", "arbitrary"),
        name="norm1_inproj",
    )(x, mod, wn, w, wift, bcol, brow)


def _combine_inproj_call(pos, ys, x1, mod_prev, mod, wn, w, wift, bcol, brow):
    b, s, d = x1.shape
    tm = TOKEN_TILE
    tok, out_specs, out_shape = _inproj_specs(b, s, tm)
    full = lambda a: pl.BlockSpec(a.shape, lambda i, j, *_: (0,) * a.ndim)
    modspec = pl.BlockSpec((1, 6, d), lambda i, j, *_: (i, 0, 0))
    return pl.pallas_call(
        _combine_inproj_kernel,
        grid_spec=pltpu.PrefetchScalarGridSpec(
            num_scalar_prefetch=1,
            grid=(b, s // tm),
            in_specs=[pl.BlockSpec(memory_space=pl.ANY), tok(d), modspec, modspec, full(wn), full(w),
                      full(wift), full(bcol), full(brow)],
            out_specs=[tok(d)] + out_specs,
            scratch_shapes=[pltpu.VMEM((2, tm // SUBLANES, SUBLANES, d), F32),
                            pltpu.SemaphoreType.DMA((2,))],
        ),
        out_shape=[jax.ShapeDtypeStruct((b, s, d), F32)] + out_shape,
        compiler_params=_params("arbitrary", "arbitrary"),
        name="combine_norm1_inproj",
    )(pos, ys, x1, mod_prev, mod, wn, w, wift, bcol, brow)


def _mlstm_kernel(q_ref, kt_ref, v_ref, o_ref, ifc_ref, ifr_ref, cwq_ref, cbq_ref, cwk_ref, cbk_ref, wn_ref,
                  out_ref, qbuf, kbuf, cstate, mstate):
    @pl.when(pl.program_id(1) == 0)
    def _():
        qbuf[...] = jnp.zeros(qbuf.shape, qbuf.dtype)
        kbuf[...] = jnp.zeros(kbuf.shape, kbuf.dtype)
        cstate[...] = jnp.zeros(cstate.shape, F32)
        mstate[...] = jnp.zeros(mstate.shape, F32)

    heads = range(M_HEADS)
    carry = (qbuf[...], kbuf[...], [cstate[h] for h in heads], [mstate[h][:, 0:1] for h in heads])
    for c in range(q_ref.shape[1] // MLSTM_CHUNK):
        carry = _mlstm_chunk(c, carry, q_ref, kt_ref, v_ref, o_ref, ifc_ref, ifr_ref, cwq_ref, cbq_ref,
                             cwk_ref, cbk_ref, wn_ref, out_ref)
    qbuf[...], kbuf[...] = carry[0], carry[1]
    for h in heads:
        cstate[h] = carry[2][h]
        mstate[h] = jnp.broadcast_to(carry[3][h], (1, LANES))


def _mlstm_chunk(c, carry, q_ref, kt_ref, v_ref, o_ref, ifc_ref, ifr_ref, cwq_ref, cbq_ref, cwk_ref, cbk_ref,
                 wn_ref, out_ref):
    L = MLSTM_CHUNK
    span = slice(L * c, L * (c + 1))
    q_tail, k_prev, cst, m_prev = carry
    pad = q_tail.shape[0]
    heads = range(M_HEADS)

    xq = q_ref[0, span, :]
    q_all = jnp.concatenate([q_tail, xq], axis=0)
    xk = kt_ref[0, :, span]
    k_all = jnp.concatenate([k_prev, xk], axis=1)
    r_s = lax.broadcasted_iota(jnp.int32, (L, L + pad), 0)
    c_s = lax.broadcasted_iota(jnp.int32, (L, L + pad), 1)
    r_t = lax.broadcasted_iota(jnp.int32, (2 * L, L), 0)
    c_t = lax.broadcasted_iota(jnp.int32, (2 * L, L), 1)
    acc_q = cbq_ref[...] + cwq_ref[M_CONV - 1:M_CONV, :] * xq.astype(F32)
    acc_k = cbk_ref[...] + cwk_ref[M_CONV - 1] * xk.astype(F32)
    for j in range(M_CONV - 1):
        back = M_CONV - 1 - j
        shift_q = jnp.where(c_s == r_s + (pad - back), 1.0, 0.0).astype(BF16)
        acc_q = acc_q + cwq_ref[j:j + 1, :] * _dot(shift_q, q_all)
        shift_k = jnp.where(r_t == c_t + (L - back), 1.0, 0.0).astype(BF16)
        acc_k = acc_k + cwk_ref[j] * _dot(k_all, shift_k)
    q = acc_q * _sigmoid(acc_q) * (M_DQK ** -0.5)
    kf = acc_k * _sigmoid(acc_k)
    kb = kf.astype(BF16)
    vb = v_ref[0, span, :]

    r_i = lax.broadcasted_iota(jnp.int32, (L, L), 0)
    c_i = lax.broadcasted_iota(jnp.int32, (L, L), 1)
    causal = c_i <= r_i
    tril = jnp.where(causal, 1.0, 0.0).astype(BF16)
    triu = jnp.where(r_i <= c_i, 1.0, 0.0).astype(BF16)

    icol = ifc_ref[0, span, :]
    irow = ifr_ref[0, :, span]
    ch, cm, cl = _split3(icol)
    bcum_col = _dot(tril, ch) + _dot(tril, cm) + _dot(tril, cl)
    rh, rm, rl = _split3(irow)
    bcum_row = _dot(rh, triu) + _dot(rm, triu) + _dot(rl, triu)

    lane = lax.broadcasted_iota(jnp.int32, (1, LANES), 1)
    lo_half = lane < M_DQK
    hi_half = jnp.logical_not(lo_half)
    ones = jnp.ones((L, LANES), BF16)
    pair = lambda h: slice(LANES * (h // 2), LANES * (h // 2 + 1))

    qmask =[jnp.where(lo_half if h % 2 == 0 else hi_half, q[:, pair(h)], 0.0) for h in heads]
    vaug = [jnp.concatenate([vb[:, M_DV * h:M_DV * (h + 1)], ones], axis=1) for h in heads]
    bc_col = [bcum_col[:, M_HEADS + h:M_HEADS + h + 1] for h in heads]
    bc_row = [bcum_row[M_HEADS + h:M_HEADS + h + 1, :] for h in heads]
    i_row = [irow[h:h + 1, :] for h in heads]
    b_last = [bc_row[h][:, L - 1:L] for h in heads]

    s = [_dot(qmask[h].astype(BF16), kb[pair(h), :]) for h in heads]

    dlog = [jnp.where(causal, bc_col[h] - bc_row[h] + i_row[h], NEG_INF) for h in heads]
    m_inter = [bc_col[h] + m_prev[h] for h in heads]
    m_row = [jnp.maximum(m_inter[h], jnp.max(dlog[h], axis=-1, keepdims=True)) for h in heads]
    sc = [(s[h] * jnp.exp(dlog[h] - m_row[h])).astype(BF16) for h in heads]
    qi = [(jnp.exp(m_inter[h] - m_row[h]) * qmask[h]).astype(BF16) for h in heads]

    num = [_dot(sc[h], vaug[h]) + _dot(qi[h], cst[h].astype(BF16)) for h in heads]
    for h in heads:
        den = num[h][:, M_DV:]
        hh = num[h][:, :M_DV] / jnp.maximum(jnp.abs(den), jnp.exp(-m_row[h]))
        hn = _rms(hh) * wn_ref[:, M_DV * h:M_DV * (h + 1)]
        og = _sigmoid(o_ref[0, span, M_DV * h:M_DV * (h + 1)].astype(F32))
        out_ref[0, span, M_DV * h:M_DV * (h + 1)] = (hn * og).astype(BF16)

    w_state = [b_last[h] - bc_row[h] + i_row[h] for h in heads]
    m_loc = [jnp.max(w_state[h], axis=-1, keepdims=True) for h in heads]
    ak = [(jnp.exp(w_state[h] - m_loc[h]) * kf[pair(h), :]).astype(BF16) for h in heads]
    c_loc = [_dot(ak[h], vaug[h]) for h in heads]
    m_new = [jnp.maximum(b_last[h] + m_prev[h], m_loc[h]) for h in heads]
    c_new = [jnp.exp(b_last[h] + m_prev[h] - m_new[h]) * cst[h] + jnp.exp(m_loc[h] - m_new[h]) * c_loc[h]
             for h in heads]
    return xq[L - pad:L, :], xk, c_new, m_new


def _mlstm_call(q, kt, v, o, ifc, ifr, cwq, cbq, cwk, cbk, w_mnorm):
    b, s, _ = q.shape
    L = MLSTM_CHUNK
    span = MLSTM_CHUNKS_PER_STEP * L
    tok = lambda w_: pl.BlockSpec((1, span, w_), lambda i, j: (i, j, 0))
    rows = lambda n: pl.BlockSpec((1, n, span), lambda i, j: (i, 0, j))
    full = lambda a: pl.BlockSpec(a.shape, lambda i, j: (0,) * a.ndim)
    return pl.pallas_call(
        _mlstm_kernel,
        grid=(b, s // span),
        in_specs=[tok(M_QK_W), rows(M_QK_W), tok(M_V_W), tok(M_V_W), tok(LANES), rows(2 * M_HEADS),
                  full(cwq), full(cbq), full(cwk), full(cbk), full(w_mnorm)],
        out_specs=tok(M_V_W),
        out_shape=jax.ShapeDtypeStruct((b, s, M_V_W), BF16),
        scratch_shapes=[pltpu.VMEM((2 * SUBLANES, M_QK_W), BF16),
                        pltpu.VMEM((M_QK_W, L), BF16),
                        pltpu.VMEM((M_HEADS, LANES, 2 * M_DV), F32),
                        pltpu.VMEM((M_HEADS, 1, LANES), F32)],
        compiler_params=_params("arbitrary", "arbitrary"),
        name="mlstm",
    )(q, kt, v, o, ifc, ifr, cwq, cbq, cwk, cbk, w_mnorm)


def _swa_kernel(sink_ref, q_ref, kvp_ref, kvc_ref, bias_ref, out_ref):
    W = WINDOW
    first = pl.program_id(1) == 0
    kv = [kvp_ref[0]] + [kvc_ref[0, W * u:W * (u + 1), :] for u in range(SWA_BLOCKS)]
    lane = lax.broadcasted_iota(jnp.int32, (1, LANES), 1)
    lo_half = lane < A_HEAD_DIM
    hi_half = jnp.logical_not(lo_half)
    keys, v_half = [], []
    for u in range(SWA_BLOCKS):
        keys.append(jnp.concatenate([kv[u][:, :A_KV_W], kv[u + 1][:, :A_KV_W]], axis=0))
        vals = jnp.concatenate([kv[u][:, A_KV_W:], kv[u + 1][:, A_KV_W:]], axis=0)
        zero = jnp.zeros_like(vals)
        v_half.append((jnp.where(lo_half, vals, zero), jnp.where(hi_half, vals, zero)))
    table = [jnp.where(first, 1, 0)] + [0] * (SWA_BLOCKS - 1)

    tiles = range(A_HEADS // 2)
    slots = [(u, j, p) for u in range(SWA_BLOCKS) for j in tiles for p in range(2)]
    head = {ujp: A_HEAD_ORDER[2 * ujp[1] + ujp[2]] for ujp in slots}
    scale = jnp.asarray(A_HEAD_DIM ** -0.5, BF16)
    qt = {(u, j): q_ref[0, W * u:W * (u + 1), LANES * j:LANES * (j + 1)] * scale
          for u in range(SWA_BLOCKS) for j in tiles}
    qm = {(u, j, p): jnp.where(lo_half if p == 0 else hi_half, qt[u, j], jnp.zeros_like(qt[u, j]))
          for u, j, p in slots}
    s = {ujp: _dot(qm[ujp], keys[ujp[0]], _NT) + bias_ref[table[ujp[0]], head[ujp]] for ujp in slots}
    m = {ujp: jnp.maximum(jnp.max(s[ujp], axis=-1, keepdims=True), sink_ref[head[ujp]]) for ujp in slots}
    e = {ujp: jnp.exp(s[ujp] - m[ujp]) for ujp in slots}
    denom = {ujp: jnp.sum(e[ujp], axis=-1, keepdims=True) + jnp.exp(sink_ref[head[ujp]] - m[ujp])
             for ujp in slots}
    pv = {(u, j, p): _dot(e[u, j, p].astype(BF16), v_half[u][p]) for u, j, p in slots}
    for u in range(SWA_BLOCKS):
        for j in tiles:
            out = pv[u, j, 0] / denom[u, j, 0] + pv[u, j, 1] / denom[u, j, 1]
            out_ref[0, W * u:W * (u + 1), LANES * j:LANES * (j + 1)] = out.astype(BF16)


def _swa_call(sinks, qa, kva, bias):
    b, s, _ = qa.shape
    W = WINDOW
    rows = SWA_BLOCKS * W
    return pl.pallas_call(
        _swa_kernel,
        grid=(b, s // rows),
        in_specs=[pl.BlockSpec(memory_space=pltpu.SMEM),
                  pl.BlockSpec((1, rows, A_Q_W), lambda i, j: (i, j, 0)),
                  pl.BlockSpec((1, W, 2 * A_KV_W), lambda i, j: (i, jnp.maximum(SWA_BLOCKS * j - 1, 0), 0)),
                  pl.BlockSpec((1, rows, 2 * A_KV_W), lambda i, j: (i, j, 0)),
                  pl.BlockSpec(bias.shape, lambda i, j: (0,) * bias.ndim)],
        out_specs=pl.BlockSpec((1, rows, A_Q_W), lambda i, j: (i, j, 0)),
        out_shape=jax.ShapeDtypeStruct((b, s, A_Q_W), BF16),
        compiler_params=_params("arbitrary", "arbitrary"),
        name="swa",
    )(sinks, qa, kva, kva, bias)


def _post_kernel(hm_ref, ha_ref, g_ref, x_ref, mod_ref, wbm_ref, wba_ref, wo_ref, wn2_ref,
                 wrh_ref, wrl_ref, rb_ref,
                 x1_ref, pay_ref, cls_ref, rank_ref, cnt_ref, carry):
    tm = x_ref.shape[1]

    @pl.when((pl.program_id(0) == 0) & (pl.program_id(1) == 0))
    def _():
        carry[...] = jnp.zeros(carry.shape, F32)

    g = g_ref[0]
    pm = _dot(hm_ref[0], wbm_ref[...])
    pa = _dot(ha_ref[0], wba_ref[...])
    merged = (_sigmoid(g[:, :D_MODEL]) * pm.astype(BF16)
              + _sigmoid(g[:, D_MODEL:]) * pa.astype(BF16))
    mo = _dot(merged, wo_ref[...])
    x1 = x_ref[0] + mod_ref[0, 2:3, :] * mo
    x1_ref[0] = x1
    h2 = _rms(x1) * (wn2_ref[...] * (1.0 + mod_ref[0, 4:5, :])) + mod_ref[0, 3:4, :]
    pay_ref[0, :, :D_MODEL] = h2

    hh, hl = _split2(h2)
    wrh = wrh_ref[...]
    logits = _dot(wrh, hh, _NT) + _dot(wrh, hl, _NT) + _dot(wrl_ref[...], hh, _NT)
    scores = _sigmoid(logits)
    sel = scores + rb_ref[...]

    def row(a, r):
        return a[r:r + 1, :]

    grp = []
    for gi in range(N_GROUPS):
        v = [row(sel, gi * EXPERTS_PER_GROUP + k) for k in range(EXPERTS_PER_GROUP)]
        best = v[0] + v[1]
        for a, b in _PAIRS[1:]:
            best = jnp.maximum(best, v[a] + v[b])
        grp.append(best)
    gbest = grp[0]
    gsel = jnp.zeros_like(gbest)
    for gi in range(1, N_GROUPS):
        take = grp[gi] > gbest
        gbest = jnp.where(take, grp[gi], gbest)
        gsel = jnp.where(take, float(gi), gsel)

    sv, gv = [], []
    for k in range(EXPERTS_PER_GROUP):
        s_k = row(sel, k)
        g_k = row(scores, k)
        for gi in range(1, N_GROUPS):
            hit = gsel == float(gi)
            s_k = jnp.where(hit, row(sel, gi * EXPERTS_PER_GROUP + k), s_k)
            g_k = jnp.where(hit, row(scores, gi * EXPERTS_PER_GROUP + k), g_k)
        sv.append(s_k)
        gv.append(g_k)

    def argmax4(vals):
        bv, bi = vals[0], jnp.zeros_like(vals[0])
        for k in range(1, EXPERTS_PER_GROUP):
            take = vals[k] > bv
            bv = jnp.where(take, vals[k], bv)
            bi = jnp.where(take, float(k), bi)
        return bi

    i1 = argmax4(sv)
    i2 = argmax4([jnp.where(i1 == float(k), -jnp.inf, sv[k]) for k in range(EXPERTS_PER_GROUP)])
    w1 = jnp.zeros_like(i1)
    w2 = jnp.zeros_like(i1)
    for k in range(EXPERTS_PER_GROUP):
        w1 = jnp.where(i1 == float(k), gv[k], w1)
        w2 = jnp.where(i2 == float(k), gv[k], w2)
    wsum = w1 + w2
    w1 = w1 / wsum
    w2 = w2 / wsum
    code = i1 * float(EXPERTS_PER_GROUP) + i2
    pair_idx = jnp.zeros_like(code)
    gate_a, gate_b = w1, w2
    for k, (a, b) in enumerate(_CLASS_PAIRS):
        fwd = code == float(a * EXPERTS_PER_GROUP + b)
        rev = code == float(b * EXPERTS_PER_GROUP + a)
        pair_idx = jnp.where(fwd, float(k), jnp.where(rev, float(k), pair_idx))
        gate_a = jnp.where(rev, w2, gate_a)
        gate_b = jnp.where(rev, w1, gate_b)
    cls_f = gsel * float(PAIRS_PER_GROUP) + pair_idx
    cls_ref[0] = cls_f.astype(jnp.int32)

    grow = lax.broadcasted_iota(jnp.int32, (LANES, tm), 0)
    gmat = jnp.where(grow == 0, gate_a, jnp.where(grow == 1, gate_b, 0.0))
    pay_ref[0, :, D_MODEL:] = gmat.T

    crow = lax.broadcasted_iota(jnp.int32, (CLASS_ROWS, tm), 0).astype(F32)
    onehot = crow == cls_f
    r_i = lax.broadcasted_iota(jnp.int32, (tm, tm), 0)
    c_i = lax.broadcasted_iota(jnp.int32, (tm, tm), 1)
    upper = jnp.where(r_i <= c_i, 1.0, 0.0).astype(BF16)
    cum = _dot(jnp.where(onehot, 1.0, 0.0).astype(BF16), upper)
    before = carry[:, 0:1]
    rank = jnp.sum(jnp.where(onehot, cum - 1.0 + before, 0.0), axis=0, keepdims=True)
    rank_ref[0] = rank.astype(jnp.int32)
    total = before + cum[:, tm - 1:tm]
    carry[...] = jnp.broadcast_to(total, carry.shape)
    cnt_ref[...] = jnp.broadcast_to(total, cnt_ref.shape)


def _post_call(hm, ha, g, x, mod, wbm, wba, wo, wn2, wrh, wrl, rb):
    b, s, d = x.shape
    tm = ROUTER_TILE
    tok = lambda w_: pl.BlockSpec((1, tm, w_), lambda i, j: (i, j, 0))
    full = lambda a: pl.BlockSpec(a.shape, lambda i, j: (0,) * a.ndim)
    lanes = pl.BlockSpec((1, 1, tm), lambda i, j: (i, 0, j))
    return pl.pallas_call(
        _post_kernel,
        grid=(b, s // tm),
        in_specs=[tok(M_V_W), tok(A_Q_W), tok(2 * d), tok(d),
                  pl.BlockSpec((1, 6, d), lambda i, j: (i, 0, 0)),
                  full(wbm), full(wba), full(wo), full(wn2), full(wrh), full(wrl), full(rb)],
        out_specs=[tok(d), tok(PAYLOAD_W), lanes, lanes,
                   pl.BlockSpec((CLASS_ROWS, LANES), lambda i, j: (0, 0))],
        out_shape=[jax.ShapeDtypeStruct((b, s, d), F32),
                   jax.ShapeDtypeStruct((b, s, PAYLOAD_W), F32),
                   jax.ShapeDtypeStruct((b, 1, s), jnp.int32),
                   jax.ShapeDtypeStruct((b, 1, s), jnp.int32),
                   jax.ShapeDtypeStruct((CLASS_ROWS, LANES), F32)],
        scratch_shapes=[pltpu.VMEM((CLASS_ROWS, LANES), F32)],
        compiler_params=_params("arbitrary", "arbitrary"),
        name="merge_outproj_router",
    )(hm, ha, g, x, mod, wbm, wba, wo, wn2, wrh, wrl, rb)


def _zero_fill(off_s, cnt_s, nu_s, xs_out, zbuf, zsem, *, wait):
    chunk = zbuf.shape[0]

    def zero_rows(first, n):
        cp = pltpu.make_async_copy(zbuf.at[pl.ds(0, n), :], xs_out.at[pl.ds(first, n), :], zsem)
        cp.wait() if wait else cp.start()

    def per_class(c, carry):
        n = cnt_s[c]
        fill = (-n) & (EXPERT_TILE - 1)
        head = fill & (SUBLANES - 1)
        for k in range(SUBLANES - 1):
            pl.when(k < head)(lambda k=k: zero_rows(off_s[c] + n + k, 1))
        cur = off_s[c] + n + head
        p = chunk
        while p >= SUBLANES:
            pl.when((fill & p) != 0)(lambda cur=cur, p=p: zero_rows(pl.multiple_of(cur, SUBLANES), p))
            cur = cur + (fill & p)
            p //= 2
        return carry

    lax.fori_loop(0, N_CLASSES, per_class, 0)

    def per_chunk(r, carry):
        zero_rows(pl.multiple_of(r * chunk, SUBLANES), chunk)
        return carry

    per_tile = EXPERT_TILE // chunk
    lax.fori_loop(nu_s[0] * per_tile, (xs_out.shape[0] // EXPERT_TILE) * per_tile, per_chunk, 0)


def _dispatch_kernel(pos_s, off_s, cnt_s, nu_s, pay_ref, xs_out, zbuf, sem, zsem):
    @pl.when(pl.program_id(0) == 0)
    def _():
        zbuf[...] = jnp.zeros(zbuf.shape, F32)
        _zero_fill(off_s, cnt_s, nu_s, xs_out, zbuf, zsem, wait=False)
        _zero_fill(off_s, cnt_s, nu_s, xs_out, zbuf, zsem, wait=True)

    first_token = pl.program_id(0) * pay_ref.shape[0] * SUBLANES
    for wait in (False, True):
        _row_copies(pos_s, first_token, xs_out, pay_ref, sem, to_sorted=True, wait=wait)


def _dispatch_call(pos, off, cnt, n_used, payload, npad):
    t8, _, width = payload.shape
    tiles = ROW_TILE // SUBLANES
    return pl.pallas_call(
        _dispatch_kernel,
        grid_spec=pltpu.PrefetchScalarGridSpec(
            num_scalar_prefetch=4,
            grid=(t8 // tiles,),
            in_specs=[pl.BlockSpec((tiles, SUBLANES, width), lambda i, *_: (i, 0, 0))],
            out_specs=pl.BlockSpec(memory_space=pl.ANY),
            scratch_shapes=[pltpu.VMEM((EXPERT_TILE // 2, width), F32),
                            pltpu.SemaphoreType.DMA(()), pltpu.SemaphoreType.DMA(())],
        ),
        out_shape=jax.ShapeDtypeStruct((npad, width), F32),
        compiler_params=_params("arbitrary"),
        name="moe_dispatch",
    )(pos, off, cnt, n_used, payload)


def _moe_kernel(ea_s, eb_s, nu_s, x_ref, wga, wua, wda, wgb, wub, wdb, y_ref):
    del ea_s, eb_s

    @pl.when(pl.program_id(0) < nu_s[0])
    def _():
        x = x_ref[:, :D_MODEL].astype(BF16)
        gate_a = x_ref[:, D_MODEL:D_MODEL + 1]
        gate_b = x_ref[:, D_MODEL + 1:D_MODEL + 2]

        up = [(_dot(x, wg[0]), _dot(x, wu[0])) for wg, wu in ((wga, wua), (wgb, wub))]
        act = [(hg * _sigmoid(hg) * hu).astype(BF16) for hg, hu in up]
        y_ref[...] = gate_a * _dot(act[0], wda[0]) + gate_b * _dot(act[1], wdb[0])

    @pl.when(pl.program_id(0) >= nu_s[0])
    def _():
        y_ref[...] = jnp.zeros(y_ref.shape, F32)


def _moe_call(tile_ea, tile_eb, n_used, xs, wg, wu, wd, layer):
    npad = xs.shape[0]
    te = EXPERT_TILE
    last = lambda i, nu: jnp.minimum(i, nu[0] - 1)
    first = layer * N_EXPERTS
    pick = lambda tab: (lambda i, ea, eb, nu: (first + (ea, eb)[tab][i], 0, 0))
    up = lambda tab: pl.BlockSpec((1, D_MODEL, D_EXPERT), pick(tab))
    down = lambda tab: pl.BlockSpec((1, D_EXPERT, D_MODEL), pick(tab))
    return pl.pallas_call(
        _moe_kernel,
        grid_spec=pltpu.PrefetchScalarGridSpec(
            num_scalar_prefetch=3,
            grid=(npad // te,),
            in_specs=[pl.BlockSpec((te, PAYLOAD_W), lambda i, ea, eb, nu: (last(i, nu), 0)),
                      up(0), up(0), down(0), up(1), up(1), down(1)],
            out_specs=pl.BlockSpec((te, D_MODEL), lambda i, ea, eb, nu: (i, 0)),
        ),
        out_shape=jax.ShapeDtypeStruct((npad, D_MODEL), F32),
        compiler_params=_params("arbitrary"),
        name="moe_experts",
    )(tile_ea, tile_eb, n_used, xs, wg, wu, wd, wg, wu, wd)


def _combine_kernel(pos_s, ys_hbm, x1_ref, mod_ref, wf_ref, out_ref, ybuf, sem):
    x2 = _moe_residual(pos_s, ys_hbm, x1_ref, mod_ref, ybuf, sem)
    out_ref[0] = _rms(x2) * wf_ref[...]


def _combine_call(pos, ys, x1, mod, wf):
    b, s, d = x1.shape
    rows = ROW_TILE
    return pl.pallas_call(
        _combine_kernel,
        grid_spec=pltpu.PrefetchScalarGridSpec(
            num_scalar_prefetch=1,
            grid=(b, s // rows),
            in_specs=[pl.BlockSpec(memory_space=pl.ANY),
                      pl.BlockSpec((1, rows, d), lambda i, j, *_: (i, j, 0)),
                      pl.BlockSpec((1, 6, d), lambda i, j, *_: (i, 0, 0)),
                      pl.BlockSpec((1, d), lambda i, j, *_: (0, 0))],
            out_specs=pl.BlockSpec((1, rows, d), lambda i, j, *_: (i, j, 0)),
            scratch_shapes=[pltpu.VMEM((2, rows // SUBLANES, SUBLANES, d), F32),
                            pltpu.SemaphoreType.DMA((2,))],
        ),
        out_shape=jax.ShapeDtypeStruct((b, s, d), F32),
        compiler_params=_params("arbitrary", "arbitrary"),
        name="moe_combine",
    )(pos, ys, x1, mod, wf)


def _t5_bucket(n):
    max_exact = N_BUCKETS // 2
    large = max_exact + (np.log(np.maximum(n, 1) / max_exact)
                         / np.log(MAX_DISTANCE / max_exact)
                         * (N_BUCKETS - max_exact)).astype(np.int32)
    large = np.minimum(large, N_BUCKETS - 1)
    return np.where(n < max_exact, n, large).astype(np.int32)


def _bias_table(rel_bias):
    dist = np.arange(WINDOW)[:, None] + WINDOW - np.arange(2 * WINDOW)[None, :]
    bucket = _t5_bucket(np.maximum(dist, 0)).reshape(1, -1)
    onehot = (jnp.arange(N_BUCKETS, dtype=jnp.int32)[:, None] == jnp.asarray(bucket)).astype(F32)
    tab = jnp.dot(rel_bias.astype(F32).T, onehot, precision=lax.Precision.HIGHEST)
    tab = tab.reshape(A_HEADS, WINDOW, 2 * WINDOW)
    in_window = (dist >= 0) & (dist < WINDOW)
    has_key = np.arange(2 * WINDOW)[None, :] >= WINDOW
    return jnp.stack([jnp.where(jnp.asarray(in_window), tab, NEG_INF),
                      jnp.where(jnp.asarray(in_window & has_key), tab, NEG_INF)])


def _layout_w_in(w_in_l):
    pts = np.cumsum([0, M_QK_W, M_QK_W, M_V_W, M_V_W, M_HEADS, M_HEADS,
                     A_Q_W, A_KV_W, A_KV_W, D_MODEL, D_MODEL])
    wb = w_in_l.astype(BF16)
    col = lambda k: wb[:, pts[k]:pts[k + 1]]
    head = lambda h: wb[:, pts[6] + A_HEAD_DIM * h:pts[6] + A_HEAD_DIM * (h + 1)]
    w_if = jnp.concatenate([col(4), col(5)], axis=1)
    w_if_pad = jnp.pad(w_if, ((0, 0), (0, LANES - 2 * M_HEADS)))
    w = jnp.concatenate([col(0), col(2), col(3)] + [head(h) for h in A_HEAD_ORDER]
                        + [col(7), col(8), col(9), col(10), w_if_pad], axis=1)
    w_t = jnp.concatenate([col(1), w_if], axis=1).T
    return w, jnp.pad(w_t, ((0, -w_t.shape[0] % (2 * SUBLANES)), (0, 0)))


def _tile_tables(counts, n_tiles):
    te = EXPERT_TILE
    tiles = (counts + te - 1) // te
    ends = jnp.cumsum(tiles)
    off = (ends - tiles) * te
    n_used = ends[-1]
    tile_idx = jnp.minimum(jnp.arange(n_tiles, dtype=jnp.int32), n_used - 1)
    tile_cls = jnp.sum((ends[None, :] <= tile_idx[:, None]).astype(jnp.int32), axis=1)
    tile_cls = jnp.minimum(tile_cls, N_CLASSES - 1)
    off_pad = jnp.pad(off, (0, CLASS_ROWS - N_CLASSES)).astype(jnp.int32)
    return (off_pad, jnp.asarray(_CLASS_EA)[tile_cls], jnp.asarray(_CLASS_EB)[tile_cls],
            n_used.reshape(1).astype(jnp.int32))


def kernel(x, c, w_ada, b_ada, w_norm1, w_in, conv_w, conv_b, b_igate, b_fgate, w_mnorm, sinks, rel_bias,
           w_br_m, w_br_a, w_out, w_norm2, w_router, router_bias, w_gate_e, w_up_e, w_down_e, w_final):
    b, s, d = x.shape
    depth = w_ada.shape[0]
    t = b * s
    n_tiles = t // EXPERT_TILE + N_CLASSES
    npad = n_tiles * EXPERT_TILE

    mod_all = _ada_call(jnp.pad(c, ((0, 8 - b), (0, 0))), w_ada, b_ada)[:, :b]
    bias_tab = _bias_table(rel_bias)
    wrt = w_router.T
    wrh = wrt.astype(BF16)
    wrl = (wrt - wrh.astype(F32)).astype(BF16)
    rb = router_bias.reshape(N_EXPERTS, 1).astype(F32)
    row = lambda v: v.reshape(1, -1).astype(F32)

    wg_all, wu_all, wd_all = _cast_call(w_gate_e), _cast_call(w_up_e), _cast_call(w_down_e)

    moe = None
    for l in range(depth):
        mod = mod_all[l].reshape(b, 6, d)
        w_l, wift = _layout_w_in(w_in[l])
        gate_bias = jnp.concatenate([b_igate[l], b_fgate[l]]).astype(F32)
        bcol = jnp.pad(gate_bias, (0, LANES - 2 * M_HEADS)).reshape(1, LANES)
        brow = gate_bias.reshape(2 * M_HEADS, 1)

        proj_args = (mod, row(w_norm1[l]), w_l, wift, bcol, brow)
        if moe is None:
            q, kt, v, o, qa, kva, g, ifc, ifr = _inproj_call(x, *proj_args)
        else:
            x, q, kt, v, o, qa, kva, g, ifc, ifr = _combine_inproj_call(*moe, *proj_args)
        cw, cb = conv_w[l].astype(F32), conv_b[l].astype(F32)
        cwk = jnp.broadcast_to(cw[:, M_QK_W:, None], (M_CONV, M_QK_W, MLSTM_CHUNK))
        cbk = jnp.broadcast_to(cb[M_QK_W:, None], (M_QK_W, MLSTM_CHUNK))
        hm = _mlstm_call(q, kt, v, o, ifc, ifr, cw[:, :M_QK_W], row(cb[:M_QK_W]), cwk, cbk, row(w_mnorm[l]))
        ha = _swa_call(sinks[l].astype(F32), qa, kva, bias_tab)

        wba = jnp.concatenate([w_br_a[l][A_HEAD_DIM * h:A_HEAD_DIM * (h + 1)] for h in A_HEAD_ORDER])
        x1, payload, cls, rank, cnt = _post_call(
            hm, ha, g, x, mod, w_br_m[l].astype(BF16), wba.astype(BF16), w_out[l].astype(BF16),
            row(w_norm2[l]), wrh, wrl, rb)

        counts = cnt[:, 0].astype(jnp.int32)
        off, tile_ea, tile_eb, n_used = _tile_tables(counts[:N_CLASSES], n_tiles)
        hit = cls.reshape(t, 1) == jnp.arange(CLASS_ROWS, dtype=jnp.int32)[None, :]
        pos = jnp.sum(jnp.where(hit, off[None, :], 0), axis=1) + rank.reshape(t)
        xs = _dispatch_call(pos, off, counts, n_used,
                            payload.reshape(t // SUBLANES, SUBLANES, PAYLOAD_W), npad)
        ys = _moe_call(tile_ea, tile_eb, n_used, xs, wg_all, wu_all, wd_all, layer=l)
        moe = (pos, ys, x1, mod)
    return _combine_call(*moe, row(w_final))
```

```python
import numpy as np
import jax
import jax.numpy as jnp
from jax import lax
from jax.experimental import pallas as pl
from jax.experimental.pallas import tpu as pltpu

F32 = jnp.float32
BF16 = jnp.bfloat16

D_MODEL = 1024
M_HEADS = 4
M_DQK = 64
M_DV = 128
M_CONV = 4
M_QK_W = M_HEADS * M_DQK
M_V_W = M_HEADS * M_DV
A_HEADS = 8
A_KV_HEADS = 2
A_HEAD_DIM = 64
A_Q_W = A_HEADS * A_HEAD_DIM
A_KV_W = A_KV_HEADS * A_HEAD_DIM
WINDOW = 128
N_BUCKETS = 32
MAX_DISTANCE = 128
N_EXPERTS = 16
N_GROUPS = 4
EXPERTS_PER_GROUP = N_EXPERTS // N_GROUPS
D_EXPERT = 512
EPS = 1e-6
NEG_INF = -1e30

LANES = 128
SUBLANES = 8
MLSTM_CHUNK = 128
MLSTM_CHUNKS_PER_STEP = 8
SWA_BLOCKS = 8
TOKEN_TILE = 512
ROUTER_TILE = 1024
EXPERT_TILE = 256
ROW_TILE = 1024
PAIRS_PER_GROUP = 6
N_CLASSES = N_GROUPS * PAIRS_PER_GROUP
CLASS_ROWS = 32
PAYLOAD_W = D_MODEL + LANES
VMEM_LIMIT = 48 * 1024 * 1024

C_Q = 0
C_V = C_Q + M_QK_W
C_O = C_V + M_V_W
C_QA = C_O + M_V_W
C_KVA = C_QA + A_Q_W
C_G = C_KVA + 2 * A_KV_W
C_IF = C_G + 2 * D_MODEL
C_END = C_IF + LANES

A_HEAD_ORDER = (0, 4, 1, 5, 2, 6, 3, 7)

_PAIRS = [(a, b) for a in range(EXPERTS_PER_GROUP) for b in range(a + 1, EXPERTS_PER_GROUP)]
_CLASS_PAIRS = ((0, 1), (2, 1), (2, 0), (3, 0), (3, 1), (3, 2))
assert sorted(tuple(sorted(p)) for p in _CLASS_PAIRS) == _PAIRS
_CLASS_EA = np.array([g * EXPERTS_PER_GROUP + a for g in range(N_GROUPS) for a, _ in _CLASS_PAIRS], np.int32)
_CLASS_EB = np.array([g * EXPERTS_PER_GROUP + b for g in range(N_GROUPS) for _, b in _CLASS_PAIRS], np.int32)

_NT = (((1,), (1,)), ((), ()))


def _sigmoid(x):
    return 0.5 * jnp.tanh(0.5 * x) + 0.5


def _log_sigmoid(x):
    return -(jnp.maximum(-x, 0.0) + jnp.log1p(jnp.exp(-jnp.abs(x))))


def _dot(a, b, dims=None):
    if dims is None:
        return jnp.dot(a, b, preferred_element_type=F32)
    return lax.dot_general(a, b, dims, preferred_element_type=F32)


def _split2(a):
    hi = a.astype(BF16)
    lo = (a - hi.astype(F32)).astype(BF16)
    return hi, lo


def _split3(a):
    hi = a.astype(BF16)
    r = a - hi.astype(F32)
    mid = r.astype(BF16)
    lo = (r - mid.astype(F32)).astype(BF16)
    return hi, mid, lo


def _dot_hi(a, b, dims=None):
    ah, al = _split2(a)
    bh, bl = _split2(b)
    return _dot(ah, bh, dims) + _dot(ah, bl, dims) + _dot(al, bh, dims)


def _rms(x):
    return x * lax.rsqrt(jnp.mean(x * x, axis=-1, keepdims=True) + EPS)


def _params(*sem):
    return pltpu.CompilerParams(dimension_semantics=sem, vmem_limit_bytes=VMEM_LIMIT)


def _cast_kernel(w_ref, o_ref):
    o_ref[...] = w_ref[...].astype(BF16)


def _cast_call(w):
    w = w.reshape((-1,) + w.shape[2:])
    e, a, b = w.shape
    blk = 2
    spec = pl.BlockSpec((blk, a, b), lambda i: (i, 0, 0))
    return pl.pallas_call(
        _cast_kernel,
        grid=(e // blk,),
        in_specs=[spec],
        out_specs=spec,
        out_shape=jax.ShapeDtypeStruct(w.shape, BF16),
        compiler_params=_params("arbitrary"),
        name="expert_weights_bf16",
    )(w)


def _ada_kernel(c_ref, w_ref, b_ref, o_ref):
    c = c_ref[...]
    cond = c * _sigmoid(c)
    o_ref[0] = _dot_hi(cond, w_ref[0]) + b_ref[0]


def _ada_call(c8, w_ada, b_ada):
    depth = w_ada.shape[0]
    return pl.pallas_call(
        _ada_kernel,
        grid=(depth, 6),
        in_specs=[
            pl.BlockSpec((8, D_MODEL), lambda l, j: (0, 0)),
            pl.BlockSpec((1, D_MODEL, D_MODEL), lambda l, j: (l, 0, j)),
            pl.BlockSpec((1, 1, D_MODEL), lambda l, j: (l, 0, j)),
        ],
        out_specs=pl.BlockSpec((1, 8, D_MODEL), lambda l, j: (l, 0, j)),
        out_shape=jax.ShapeDtypeStruct((depth, 8, 6 * D_MODEL), F32),
        compiler_params=_params("arbitrary", "arbitrary"),
        name="adaln_mod",
    )(c8, w_ada, b_ada.reshape(depth, 1, 6 * D_MODEL))


def _project(x, mod_ref, wn_ref, w_ref, wift_ref, bcol_ref, brow_ref, outs):
    q_ref, kt_ref, v_ref, o_ref, qa_ref, kva_ref, g_ref, ifc_ref, ifr_ref = outs
    h = _rms(x) * (wn_ref[...] * (1.0 + mod_ref[0, 1:2, :])) + mod_ref[0, 0:1, :]
    hb = h.astype(BF16)

    def seg(a, b):
        return _dot(hb, w_ref[:, a:b])

    q_ref[0] = seg(C_Q, C_V).astype(BF16)
    v_ref[0] = seg(C_V, C_O).astype(BF16)
    o_ref[0] = seg(C_O, C_QA).astype(BF16)
    qa_ref[0] = seg(C_QA, C_KVA).astype(BF16)
    kva_ref[0] = seg(C_KVA, C_G).astype(BF16)
    g_ref[0] = seg(C_G, C_IF).astype(BF16)
    zc = seg(C_IF, C_END) + bcol_ref[...]
    lane = lax.broadcasted_iota(jnp.int32, zc.shape, 1)
    ifc_ref[0] = jnp.where((lane >= M_HEADS) & (lane < 2 * M_HEADS), _log_sigmoid(zc), zc)
    zt = _dot(wift_ref[...], hb, _NT)
    kt_ref[0] = zt[:M_QK_W].astype(BF16)
    zr = zt[M_QK_W:M_QK_W + 2 * M_HEADS] + brow_ref[...]
    row = lax.broadcasted_iota(jnp.int32, zr.shape, 0)
    ifr_ref[0] = jnp.where(row >= M_HEADS, _log_sigmoid(zr), zr)


def _inproj_kernel(x_ref, mod_ref, wn_ref, w_ref, wift_ref, bcol_ref, brow_ref, *outs):
    _project(x_ref[0], mod_ref, wn_ref, w_ref, wift_ref, bcol_ref, brow_ref, outs)


def _row_copies(pos_s, first_token, sorted_hbm, tiles, sem, *, to_sorted, wait):
    def copy(i, k, pos):
        pair = (tiles.at[i, pl.ds(k, 1), :], sorted_hbm.at[pl.ds(pos, 1), :])
        return pltpu.make_async_copy(*(pair if to_sorted else pair[::-1]), sem)

    if wait:
        def body(i, carry):
            for k in range(SUBLANES):
                copy(0, k, 0).wait()
            return carry

        lax.fori_loop(0, tiles.shape[0], body, 0, unroll=2)
        return
    for i in range(tiles.shape[0]):
        for k in range(SUBLANES):
            copy(i, k, pos_s[first_token + i * SUBLANES + k]).start(priority=k % 2)


def _moe_residual(pos_s, ys_hbm, x1_ref, mod_ref, ybuf, sem):
    rows = x1_ref.shape[1]
    step = pl.program_id(0) * pl.num_programs(1) + pl.program_id(1)
    n_steps = pl.num_programs(0) * pl.num_programs(1)
    slot = lax.rem(step, 2)

    def fetch(first_token, slot_, wait):
        _row_copies(pos_s, first_token, ys_hbm, ybuf.at[slot_], sem.at[slot_], to_sorted=False, wait=wait)

    @pl.when(step == 0)
    def _():
        fetch(0, 0, False)

    @pl.when(step + 1 < n_steps)
    def _():
        fetch((step + 1) * rows, 1 - slot, False)

    fetch(0, slot, True)
    y = ybuf[slot].reshape(rows, ybuf.shape[-1])
    return x1_ref[0] + mod_ref[0, 5:6, :] * y


def _combine_inproj_kernel(pos_s, ys_hbm, x1_ref, modp_ref, mod_ref, wn_ref, w_ref,
                           wift_ref, bcol_ref, brow_ref, xo_ref, *rest):
    outs, (ybuf, sem) = rest[:-2], rest[-2:]
    x = _moe_residual(pos_s, ys_hbm, x1_ref, modp_ref, ybuf, sem)
    xo_ref[0] = x
    _project(x, mod_ref, wn_ref, w_ref, wift_ref, bcol_ref, brow_ref, outs)


def _inproj_specs(b, s, tm):
    tok = lambda w_: pl.BlockSpec((1, tm, w_), lambda i, j, *_: (i, j, 0))
    sds = lambda w_, dt: jax.ShapeDtypeStruct((b, s, w_), dt)
    rows = lambda n: pl.BlockSpec((1, n, tm), lambda i, j, *_: (i, 0, j))
    out_specs = [tok(M_QK_W), rows(M_QK_W), tok(M_V_W), tok(M_V_W), tok(A_Q_W), tok(2 * A_KV_W),
                 tok(2 * D_MODEL), tok(LANES), rows(2 * M_HEADS)]
    out_shape = [sds(M_QK_W, BF16), jax.ShapeDtypeStruct((b, M_QK_W, s), BF16), sds(M_V_W, BF16),
                 sds(M_V_W, BF16), sds(A_Q_W, BF16), sds(2 * A_KV_W, BF16), sds(2 * D_MODEL, BF16),
                 sds(LANES, F32), jax.ShapeDtypeStruct((b, 2 * M_HEADS, s), F32)]
    return tok, out_specs, out_shape


def _inproj_call(x, mod, wn, w, wift, bcol, brow):
    b, s, d = x.shape
    tm = TOKEN_TILE
    tok, out_specs, out_shape = _inproj_specs(b, s, tm)
    full = lambda a: pl.BlockSpec(a.shape, lambda i, j: (0,) * a.ndim)
    return pl.pallas_call(
        _inproj_kernel,
        grid=(b, s // tm),
        in_specs=[tok(d), pl.BlockSpec((1, 6, d), lambda i, j: (i, 0, 0)), full(wn), full(w),
                  full(wift), full(bcol), full(brow)],
        out_specs=out_specs,
        out_shape=out_shape,
        compiler_params=_params("arbitrary", "arbitrary"),
        name="norm1_inproj",
    )(x, mod, wn, w, wift, bcol, brow)


def _combine_inproj_call(pos, ys, x1, mod_prev, mod, wn, w, wift, bcol, brow):
    b, s, d = x1.shape
    tm = TOKEN_TILE
    tok, out_specs, out_shape = _inproj_specs(b, s, tm)
    full = lambda a: pl.BlockSpec(a.shape, lambda i, j, *_: (0,) * a.ndim)
    modspec = pl.BlockSpec((1, 6, d), lambda i, j, *_: (i, 0, 0))
    return pl.pallas_call(
        _combine_inproj_kernel,
        grid_spec=pltpu.PrefetchScalarGridSpec(
            num_scalar_prefetch=1,
            grid=(b, s // tm),
            in_specs=[pl.BlockSpec(memory_space=pl.ANY), tok(d), modspec, modspec, full(wn), full(w),
                      full(wift), full(bcol), full(brow)],
            out_specs=[tok(d)] + out_specs,
            scratch_shapes=[pltpu.VMEM((2, tm // SUBLANES, SUBLANES, d), F32),
                            pltpu.SemaphoreType.DMA((2,))],
        ),
        out_shape=[jax.ShapeDtypeStruct((b, s, d), F32)] + out_shape,
        compiler_params=_params("arbitrary", "arbitrary"),
        name="combine_norm1_inproj",
    )(pos, ys, x1, mod_prev, mod, wn, w, wift, bcol, brow)


def _mlstm_kernel(q_ref, kt_ref, v_ref, o_ref, ifc_ref, ifr_ref, cwq_ref, cbq_ref, cwk_ref, cbk_ref, wn_ref,
                  out_ref, qbuf, kbuf, cstate, mstate):
    @pl.when(pl.program_id(1) == 0)
    def _():
        qbuf[...] = jnp.zeros(qbuf.shape, qbuf.dtype)
        kbuf[...] = jnp.zeros(kbuf.shape, kbuf.dtype)
        cstate[...] = jnp.zeros(cstate.shape, F32)
        mstate[...] = jnp.zeros(mstate.shape, F32)

    heads = range(M_HEADS)
    carry = (qbuf[...], kbuf[...], [cstate[h] for h in heads], [mstate[h][:, 0:1] for h in heads])
    for c in range(q_ref.shape[1] // MLSTM_CHUNK):
        carry = _mlstm_chunk(c, carry, q_ref, kt_ref, v_ref, o_ref, ifc_ref, ifr_ref, cwq_ref, cbq_ref,
                             cwk_ref, cbk_ref, wn_ref, out_ref)
    qbuf[...], kbuf[...] = carry[0], carry[1]
    for h in heads:
        cstate[h] = carry[2][h]
        mstate[h] = jnp.broadcast_to(carry[3][h], (1, LANES))


def _mlstm_chunk(c, carry, q_ref, kt_ref, v_ref, o_ref, ifc_ref, ifr_ref, cwq_ref, cbq_ref, cwk_ref, cbk_ref,
                 wn_ref, out_ref):
    L = MLSTM_CHUNK
    span = slice(L * c, L * (c + 1))
    q_tail, k_prev, cst, m_prev = carry
    pad = q_tail.shape[0]
    heads = range(M_HEADS)

    xq = q_ref[0, span, :]
    q_all = jnp.concatenate([q_tail, xq], axis=0)
    xk = kt_ref[0, :, span]
    k_all = jnp.concatenate([k_prev, xk], axis=1)
    r_s = lax.broadcasted_iota(jnp.int32, (L, L + pad), 0)
    c_s = lax.broadcasted_iota(jnp.int32, (L, L + pad), 1)
    r_t = lax.broadcasted_iota(jnp.int32, (2 * L, L), 0)
    c_t = lax.broadcasted_iota(jnp.int32, (2 * L, L), 1)
    acc_q = cbq_ref[...] + cwq_ref[M_CONV - 1:M_CONV, :] * xq.astype(F32)
    acc_k = cbk_ref[...] + cwk_ref[M_CONV - 1] * xk.astype(F32)
    for j in range(M_CONV - 1):
        back = M_CONV - 1 - j
        shift_q = jnp.where(c_s == r_s + (pad - back), 1.0, 0.0).astype(BF16)
        acc_q = acc_q + cwq_ref[j:j + 1, :] * _dot(shift_q, q_all)
        shift_k = jnp.where(r_t == c_t + (L - back), 1.0, 0.0).astype(BF16)
        acc_k = acc_k + cwk_ref[j] * _dot(k_all, shift_k)
    q = acc_q * _sigmoid(acc_q) * (M_DQK ** -0.5)
    kf = acc_k * _sigmoid(acc_k)
    kb = kf.astype(BF16)
    vb = v_ref[0, span, :]

    r_i = lax.broadcasted_iota(jnp.int32, (L, L), 0)
    c_i = lax.broadcasted_iota(jnp.int32, (L, L), 1)
    causal = c_i <= r_i
    tril = jnp.where(causal, 1.0, 0.0).astype(BF16)
    triu = jnp.where(r_i <= c_i, 1.0, 0.0).astype(BF16)

    icol = ifc_ref[0, span, :]
    irow = ifr_ref[0, :, span]
    ch, cm, cl = _split3(icol)
    bcum_col = _dot(tril, ch) + _dot(tril, cm) + _dot(tril, cl)
    rh, rm, rl = _split3(irow)
    bcum_row = _dot(rh, triu) + _dot(rm, triu) + _dot(rl, triu)

    lane = lax.broadcasted_iota(jnp.int32, (1, LANES), 1)
    lo_half = lane < M_DQK
    hi_half = jnp.logical_not(lo_half)
    ones = jnp.ones((L, LANES), BF16)
    pair = lambda h: slice(LANES * (h // 2), LANES * (h // 2 + 1))

    qmask =[jnp.where(lo_half if h % 2 == 0 else hi_half, q[:, pair(h)], 0.0) for h in heads]
    vaug = [jnp.concatenate([vb[:, M_DV * h:M_DV * (h + 1)], ones], axis=1) for h in heads]
    bc_col = [bcum_col[:, M_HEADS + h:M_HEADS + h + 1] for h in heads]
    bc_row = [bcum_row[M_HEADS + h:M_HEADS + h + 1, :] for h in heads]
    i_row = [irow[h:h + 1, :] for h in heads]
    b_last = [bc_row[h][:, L - 1:L] for h in heads]

    s = [_dot(qmask[h].astype(BF16), kb[pair(h), :]) for h in heads]

    dlog = [jnp.where(causal, bc_col[h] - bc_row[h] + i_row[h], NEG_INF) for h in heads]
    m_inter = [bc_col[h] + m_prev[h] for h in heads]
    m_row = [jnp.maximum(m_inter[h], jnp.max(dlog[h], axis=-1, keepdims=True)) for h in heads]
    sc = [(s[h] * jnp.exp(dlog[h] - m_row[h])).astype(BF16) for h in heads]
    qi = [(jnp.exp(m_inter[h] - m_row[h]) * qmask[h]).astype(BF16) for h in heads]

    num = [_dot(sc[h], vaug[h]) + _dot(qi[h], cst[h].astype(BF16)) for h in heads]
    for h in heads:
        den = num[h][:, M_DV:]
        hh = num[h][:, :M_DV] / jnp.maximum(jnp.abs(den), jnp.exp(-m_row[h]))
        hn = _rms(hh) * wn_ref[:, M_DV * h:M_DV * (h + 1)]
        og = _sigmoid(o_ref[0, span, M_DV * h:M_DV * (h + 1)].astype(F32))
        out_ref[0, span, M_DV * h:M_DV * (h + 1)] = (hn * og).astype(BF16)

    w_state = [b_last[h] - bc_row[h] + i_row[h] for h in heads]
    m_loc = [jnp.max(w_state[h], axis=-1, keepdims=True) for h in heads]
    ak = [(jnp.exp(w_state[h] - m_loc[h]) * kf[pair(h), :]).astype(BF16) for h in heads]
    c_loc = [_dot(ak[h], vaug[h]) for h in heads]
    m_new = [jnp.maximum(b_last[h] + m_prev[h], m_loc[h]) for h in heads]
    c_new = [jnp.exp(b_last[h] + m_prev[h] - m_new[h]) * cst[h] + jnp.exp(m_loc[h] - m_new[h]) * c_loc[h]
             for h in heads]
    return xq[L - pad:L, :], xk, c_new, m_new


def _mlstm_call(q, kt, v, o, ifc, ifr, cwq, cbq, cwk, cbk, w_mnorm):
    b, s, _ = q.shape
    L = MLSTM_CHUNK
    span = MLSTM_CHUNKS_PER_STEP * L
    tok = lambda w_: pl.BlockSpec((1, span, w_), lambda i, j: (i, j, 0))
    rows = lambda n: pl.BlockSpec((1, n, span), lambda i, j: (i, 0, j))
    full = lambda a: pl.BlockSpec(a.shape, lambda i, j: (0,) * a.ndim)
    return pl.pallas_call(
        _mlstm_kernel,
        grid=(b, s // span),
        in_specs=[tok(M_QK_W), rows(M_QK_W), tok(M_V_W), tok(M_V_W), tok(LANES), rows(2 * M_HEADS),
                  full(cwq), full(cbq), full(cwk), full(cbk), full(w_mnorm)],
        out_specs=tok(M_V_W),
        out_shape=jax.ShapeDtypeStruct((b, s, M_V_W), BF16),
        scratch_shapes=[pltpu.VMEM((2 * SUBLANES, M_QK_W), BF16),
                        pltpu.VMEM((M_QK_W, L), BF16),
                        pltpu.VMEM((M_HEADS, LANES, 2 * M_DV), F32),
                        pltpu.VMEM((M_HEADS, 1, LANES), F32)],
        compiler_params=_params("arbitrary", "arbitrary"),
        name="mlstm",
    )(q, kt, v, o, ifc, ifr, cwq, cbq, cwk, cbk, w_mnorm)


def _swa_kernel(sink_ref, q_ref, kvp_ref, kvc_ref, bias_ref, out_ref):
    W = WINDOW
    first = pl.program_id(1) == 0
    kv = [kvp_ref[0]] + [kvc_ref[0, W * u:W * (u + 1), :] for u in range(SWA_BLOCKS)]
    lane = lax.broadcasted_iota(jnp.int32, (1, LANES), 1)
    lo_half = lane < A_HEAD_DIM
    hi_half = jnp.logical_not(lo_half)
    keys, v_half = [], []
    for u in range(SWA_BLOCKS):
        keys.append(jnp.concatenate([kv[u][:, :A_KV_W], kv[u + 1][:, :A_KV_W]], axis=0))
        vals = jnp.concatenate([kv[u][:, A_KV_W:], kv[u + 1][:, A_KV_W:]], axis=0)
        zero = jnp.zeros_like(vals)
        v_half.append((jnp.where(lo_half, vals, zero), jnp.where(hi_half, vals, zero)))
    table = [jnp.where(first, 1, 0)] + [0] * (SWA_BLOCKS - 1)

    tiles = range(A_HEADS // 2)
    slots = [(u, j, p) for u in range(SWA_BLOCKS) for j in tiles for p in range(2)]
    head = {ujp: A_HEAD_ORDER[2 * ujp[1] + ujp[2]] for ujp in slots}
    scale = jnp.asarray(A_HEAD_DIM ** -0.5, BF16)
    qt = {(u, j): q_ref[0, W * u:W * (u + 1), LANES * j:LANES * (j + 1)] * scale
          for u in range(SWA_BLOCKS) for j in tiles}
    qm = {(u, j, p): jnp.where(lo_half if p == 0 else hi_half, qt[u, j], jnp.zeros_like(qt[u, j]))
          for u, j, p in slots}
    s = {ujp: _dot(qm[ujp], keys[ujp[0]], _NT) + bias_ref[table[ujp[0]], head[ujp]] for ujp in slots}
    m = {ujp: jnp.maximum(jnp.max(s[ujp], axis=-1, keepdims=True), sink_ref[head[ujp]]) for ujp in slots}
    e = {ujp: jnp.exp(s[ujp] - m[ujp]) for ujp in slots}
    denom = {ujp: jnp.sum(e[ujp], axis=-1, keepdims=True) + jnp.exp(sink_ref[head[ujp]] - m[ujp])
             for ujp in slots}
    pv = {(u, j, p): _dot(e[u, j, p].astype(BF16), v_half[u][p]) for u, j, p in slots}
    for u in range(SWA_BLOCKS):
        for j in tiles:
            out = pv[u, j, 0] / denom[u, j, 0] + pv[u, j, 1] / denom[u, j, 1]
            out_ref[0, W * u:W * (u + 1), LANES * j:LANES * (j + 1)] = out.astype(BF16)


def _swa_call(sinks, qa, kva, bias):
    b, s, _ = qa.shape
    W = WINDOW
    rows = SWA_BLOCKS * W
    return pl.pallas_call(
        _swa_kernel,
        grid=(b, s // rows),
        in_specs=[pl.BlockSpec(memory_space=pltpu.SMEM),
                  pl.BlockSpec((1, rows, A_Q_W), lambda i, j: (i, j, 0)),
                  pl.BlockSpec((1, W, 2 * A_KV_W), lambda i, j: (i, jnp.maximum(SWA_BLOCKS * j - 1, 0), 0)),
                  pl.BlockSpec((1, rows, 2 * A_KV_W), lambda i, j: (i, j, 0)),
                  pl.BlockSpec(bias.shape, lambda i, j: (0,) * bias.ndim)],
        out_specs=pl.BlockSpec((1, rows, A_Q_W), lambda i, j: (i, j, 0)),
        out_shape=jax.ShapeDtypeStruct((b, s, A_Q_W), BF16),
        compiler_params=_params("arbitrary", "arbitrary"),
        name="swa",
    )(sinks, qa, kva, kva, bias)


def _post_kernel(hm_ref, ha_ref, g_ref, x_ref, mod_ref, wbm_ref, wba_ref, wo_ref, wn2_ref,
                 wrh_ref, wrl_ref, rb_ref,
                 x1_ref, pay_ref, cls_ref, rank_ref, cnt_ref, carry):
    tm = x_ref.shape[1]

    @pl.when((pl.program_id(0) == 0) & (pl.program_id(1) == 0))
    def _():
        carry[...] = jnp.zeros(carry.shape, F32)

    g = g_ref[0]
    pm = _dot(hm_ref[0], wbm_ref[...])
    pa = _dot(ha_ref[0], wba_ref[...])
    merged = (_sigmoid(g[:, :D_MODEL]) * pm.astype(BF16)
              + _sigmoid(g[:, D_MODEL:]) * pa.astype(BF16))
    mo = _dot(merged, wo_ref[...])
    x1 = x_ref[0] + mod_ref[0, 2:3, :] * mo
    x1_ref[0] = x1
    h2 = _rms(x1) * (wn2_ref[...] * (1.0 + mod_ref[0, 4:5, :])) + mod_ref[0, 3:4, :]
    pay_ref[0, :, :D_MODEL] = h2

    hh, hl = _split2(h2)
    wrh = wrh_ref[...]
    logits = _dot(wrh, hh, _NT) + _dot(wrh, hl, _NT) + _dot(wrl_ref[...], hh, _NT)
    scores = 1.0 / (1.0 + jnp.exp(-logits))
    sel = scores + rb_ref[...]

    def row(a, r):
        return a[r:r + 1, :]

    grp = []
    for gi in range(N_GROUPS):
        v = [row(sel, gi * EXPERTS_PER_GROUP + k) for k in range(EXPERTS_PER_GROUP)]
        best = v[0] + v[1]
        for a, b in _PAIRS[1:]:
            best = jnp.maximum(best, v[a] + v[b])
        grp.append(best)
    gbest = grp[0]
    gsel = jnp.zeros_like(gbest)
    for gi in range(1, N_GROUPS):
        take = grp[gi] > gbest
        gbest = jnp.where(take, grp[gi], gbest)
        gsel = jnp.where(take, float(gi), gsel)

    sv, gv = [], []
    for k in range(EXPERTS_PER_GROUP):
        s_k = row(sel, k)
        g_k = row(scores, k)
        for gi in range(1, N_GROUPS):
            hit = gsel == float(gi)
            s_k = jnp.where(hit, row(sel, gi * EXPERTS_PER_GROUP + k), s_k)
            g_k = jnp.where(hit, row(scores, gi * EXPERTS_PER_GROUP + k), g_k)
        sv.append(s_k)
        gv.append(g_k)

    def argmax4(vals):
        bv, bi = vals[0], jnp.zeros_like(vals[0])
        for k in range(1, EXPERTS_PER_GROUP):
            take = vals[k] > bv
            bv = jnp.where(take, vals[k], bv)
            bi = jnp.where(take, float(k), bi)
        return bi

    i1 = argmax4(sv)
    i2 = argmax4([jnp.where(i1 == float(k), -jnp.inf, sv[k]) for k in range(EXPERTS_PER_GROUP)])
    w1 = jnp.zeros_like(i1)
    w2 = jnp.zeros_like(i1)
    for k in range(EXPERTS_PER_GROUP):
        w1 = jnp.where(i1 == float(k), gv[k], w1)
        w2 = jnp.where(i2 == float(k), gv[k], w2)
    wsum = w1 + w2
    w1 = w1 / wsum
    w2 = w2 / wsum
    code = i1 * float(EXPERTS_PER_GROUP) + i2
    pair_idx = jnp.zeros_like(code)
    gate_a, gate_b = w1, w2
    for k, (a, b) in enumerate(_CLASS_PAIRS):
        fwd = code == float(a * EXPERTS_PER_GROUP + b)
        rev = code == float(b * EXPERTS_PER_GROUP + a)
        pair_idx = jnp.where(fwd, float(k), jnp.where(rev, float(k), pair_idx))
        gate_a = jnp.where(rev, w2, gate_a)
        gate_b = jnp.where(rev, w1, gate_b)
    cls_f = gsel * float(PAIRS_PER_GROUP) + pair_idx
    cls_ref[0] = cls_f.astype(jnp.int32)

    grow = lax.broadcasted_iota(jnp.int32, (LANES, tm), 0)
    gmat = jnp.where(grow == 0, gate_a, jnp.where(grow == 1, gate_b, 0.0))
    pay_ref[0, :, D_MODEL:] = gmat.T

    crow = lax.broadcasted_iota(jnp.int32, (CLASS_ROWS, tm), 0).astype(F32)
    onehot = crow == cls_f
    r_i = lax.broadcasted_iota(jnp.int32, (tm, tm), 0)
    c_i = lax.broadcasted_iota(jnp.int32, (tm, tm), 1)
    upper = jnp.where(r_i <= c_i, 1.0, 0.0).astype(BF16)
    cum = _dot(jnp.where(onehot, 1.0, 0.0).astype(BF16), upper)
    before = carry[:, 0:1]
    rank = jnp.sum(jnp.where(onehot, cum - 1.0 + before, 0.0), axis=0, keepdims=True)
    rank_ref[0] = rank.astype(jnp.int32)
    total = before + cum[:, tm - 1:tm]
    carry[...] = jnp.broadcast_to(total, carry.shape)
    cnt_ref[...] = jnp.broadcast_to(total, cnt_ref.shape)


def _post_call(hm, ha, g, x, mod, wbm, wba, wo, wn2, wrh, wrl, rb):
    b, s, d = x.shape
    tm = ROUTER_TILE
    tok = lambda w_: pl.BlockSpec((1, tm, w_), lambda i, j: (i, j, 0))
    full = lambda a: pl.BlockSpec(a.shape, lambda i, j: (0,) * a.ndim)
    lanes = pl.BlockSpec((1, 1, tm), lambda i, j: (i, 0, j))
    return pl.pallas_call(
        _post_kernel,
        grid=(b, s // tm),
        in_specs=[tok(M_V_W), tok(A_Q_W), tok(2 * d), tok(d),
                  pl.BlockSpec((1, 6, d), lambda i, j: (i, 0, 0)),
                  full(wbm), full(wba), full(wo), full(wn2), full(wrh), full(wrl), full(rb)],
        out_specs=[tok(d), tok(PAYLOAD_W), lanes, lanes,
                   pl.BlockSpec((CLASS_ROWS, LANES), lambda i, j: (0, 0))],
        out_shape=[jax.ShapeDtypeStruct((b, s, d), F32),
                   jax.ShapeDtypeStruct((b, s, PAYLOAD_W), F32),
                   jax.ShapeDtypeStruct((b, 1, s), jnp.int32),
                   jax.ShapeDtypeStruct((b, 1, s), jnp.int32),
                   jax.ShapeDtypeStruct((CLASS_ROWS, LANES), F32)],
        scratch_shapes=[pltpu.VMEM((CLASS_ROWS, LANES), F32)],
        compiler_params=_params("arbitrary", "arbitrary"),
        name="merge_outproj_router",
    )(hm, ha, g, x, mod, wbm, wba, wo, wn2, wrh, wrl, rb)


def _zero_fill(off_s, cnt_s, nu_s, xs_out, zbuf, zsem, *, wait):
    chunk = zbuf.shape[0]

    def zero_rows(first, n):
        cp = pltpu.make_async_copy(zbuf.at[pl.ds(0, n), :], xs_out.at[pl.ds(first, n), :], zsem)
        cp.wait() if wait else cp.start()

    def per_class(c, carry):
        n = cnt_s[c]
        fill = (-n) & (EXPERT_TILE - 1)
        head = fill & (SUBLANES - 1)
        for k in range(SUBLANES - 1):
            pl.when(k < head)(lambda k=k: zero_rows(off_s[c] + n + k, 1))
        cur = off_s[c] + n + head
        p = chunk
        while p >= SUBLANES:
            pl.when((fill & p) != 0)(lambda cur=cur, p=p: zero_rows(pl.multiple_of(cur, SUBLANES), p))
            cur = cur + (fill & p)
            p //= 2
        return carry

    lax.fori_loop(0, N_CLASSES, per_class, 0)

    def per_chunk(r, carry):
        zero_rows(pl.multiple_of(r * chunk, SUBLANES), chunk)
        return carry

    per_tile = EXPERT_TILE // chunk
    lax.fori_loop(nu_s[0] * per_tile, (xs_out.shape[0] // EXPERT_TILE) * per_tile, per_chunk, 0)


def _dispatch_kernel(pos_s, off_s, cnt_s, nu_s, pay_ref, xs_out, zbuf, sem, zsem):
    @pl.when(pl.program_id(0) == 0)
    def _():
        zbuf[...] = jnp.zeros(zbuf.shape, F32)
        _zero_fill(off_s, cnt_s, nu_s, xs_out, zbuf, zsem, wait=False)
        _zero_fill(off_s, cnt_s, nu_s, xs_out, zbuf, zsem, wait=True)

    first_token = pl.program_id(0) * pay_ref.shape[0] * SUBLANES
    for wait in (False, True):
        _row_copies(pos_s, first_token, xs_out, pay_ref, sem, to_sorted=True, wait=wait)


def _dispatch_call(pos, off, cnt, n_used, payload, npad):
    t8, _, width = payload.shape
    tiles = ROW_TILE // SUBLANES
    return pl.pallas_call(
        _dispatch_kernel,
        grid_spec=pltpu.PrefetchScalarGridSpec(
            num_scalar_prefetch=4,
            grid=(t8 // tiles,),
            in_specs=[pl.BlockSpec((tiles, SUBLANES, width), lambda i, *_: (i, 0, 0))],
            out_specs=pl.BlockSpec(memory_space=pl.ANY),
            scratch_shapes=[pltpu.VMEM((EXPERT_TILE // 2, width), F32),
                            pltpu.SemaphoreType.DMA(()), pltpu.SemaphoreType.DMA(())],
        ),
        out_shape=jax.ShapeDtypeStruct((npad, width), F32),
        compiler_params=_params("arbitrary"),
        name="moe_dispatch",
    )(pos, off, cnt, n_used, payload)


def _moe_kernel(ea_s, eb_s, nu_s, x_ref, wga, wua, wda, wgb, wub, wdb, y_ref):
    del ea_s, eb_s

    @pl.when(pl.program_id(0) < nu_s[0])
    def _():
        x = x_ref[:, :D_MODEL].astype(BF16)
        gate_a = x_ref[:, D_MODEL:D_MODEL + 1]
        gate_b = x_ref[:, D_MODEL + 1:D_MODEL + 2]

        up = [(_dot(x, wg[0]), _dot(x, wu[0])) for wg, wu in ((wga, wua), (wgb, wub))]
        act = [(hg * _sigmoid(hg) * hu).astype(BF16) for hg, hu in up]
        y_ref[...] = gate_a * _dot(act[0], wda[0]) + gate_b * _dot(act[1], wdb[0])

    @pl.when(pl.program_id(0) >= nu_s[0])
    def _():
        y_ref[...] = jnp.zeros(y_ref.shape, F32)


def _moe_call(tile_ea, tile_eb, n_used, xs, wg, wu, wd, layer):
    npad = xs.shape[0]
    te = EXPERT_TILE
    last = lambda i, nu: jnp.minimum(i, nu[0] - 1)
    first = layer * N_EXPERTS
    pick = lambda tab: (lambda i, ea, eb, nu: (first + (ea, eb)[tab][i], 0, 0))
    up = lambda tab: pl.BlockSpec((1, D_MODEL, D_EXPERT), pick(tab))
    down = lambda tab: pl.BlockSpec((1, D_EXPERT, D_MODEL), pick(tab))
    return pl.pallas_call(
        _moe_kernel,
        grid_spec=pltpu.PrefetchScalarGridSpec(
            num_scalar_prefetch=3,
            grid=(npad // te,),
            in_specs=[pl.BlockSpec((te, PAYLOAD_W), lambda i, ea, eb, nu: (last(i, nu), 0)),
                      up(0), up(0), down(0), up(1), up(1), down(1)],
            out_specs=pl.BlockSpec((te, D_MODEL), lambda i, ea, eb, nu: (i, 0)),
        ),
        out_shape=jax.ShapeDtypeStruct((npad, D_MODEL), F32),
        compiler_params=_params("arbitrary"),
        name="moe_experts",
    )(tile_ea, tile_eb, n_used, xs, wg, wu, wd, wg, wu, wd)


def _combine_kernel(pos_s, ys_hbm, x1_ref, mod_ref, wf_ref, out_ref, ybuf, sem):
    x2 = _moe_residual(pos_s, ys_hbm, x1_ref, mod_ref, ybuf, sem)
    out_ref[0] = _rms(x2) * wf_ref[...]


def _combine_call(pos, ys, x1, mod, wf):
    b, s, d = x1.shape
    rows = ROW_TILE
    return pl.pallas_call(
        _combine_kernel,
        grid_spec=pltpu.PrefetchScalarGridSpec(
            num_scalar_prefetch=1,
            grid=(b, s // rows),
            in_specs=[pl.BlockSpec(memory_space=pl.ANY),
                      pl.BlockSpec((1, rows, d), lambda i, j, *_: (i, j, 0)),
                      pl.BlockSpec((1, 6, d), lambda i, j, *_: (i, 0, 0)),
                      pl.BlockSpec((1, d), lambda i, j, *_: (0, 0))],
            out_specs=pl.BlockSpec((1, rows, d), lambda i, j, *_: (i, j, 0)),
            scratch_shapes=[pltpu.VMEM((2, rows // SUBLANES, SUBLANES, d), F32),
                            pltpu.SemaphoreType.DMA((2,))],
        ),
        out_shape=jax.ShapeDtypeStruct((b, s, d), F32),
        compiler_params=_params("arbitrary", "arbitrary"),
        name="moe_combine",
    )(pos, ys, x1, mod, wf)


def _t5_bucket(n):
    max_exact = N_BUCKETS // 2
    large = max_exact + (np.log(np.maximum(n, 1) / max_exact)
                         / np.log(MAX_DISTANCE / max_exact)
                         * (N_BUCKETS - max_exact)).astype(np.int32)
    large = np.minimum(large, N_BUCKETS - 1)
    return np.where(n < max_exact, n, large).astype(np.int32)


def _bias_table(rel_bias):
    dist = np.arange(WINDOW)[:, None] + WINDOW - np.arange(2 * WINDOW)[None, :]
    bucket = _t5_bucket(np.maximum(dist, 0)).reshape(1, -1)
    onehot = (jnp.arange(N_BUCKETS, dtype=jnp.int32)[:, None] == jnp.asarray(bucket)).astype(F32)
    tab = jnp.dot(rel_bias.astype(F32).T, onehot, precision=lax.Precision.HIGHEST)
    tab = tab.reshape(A_HEADS, WINDOW, 2 * WINDOW)
    in_window = (dist >= 0) & (dist < WINDOW)
    has_key = np.arange(2 * WINDOW)[None, :] >= WINDOW
    return jnp.stack([jnp.where(jnp.asarray(in_window), tab, NEG_INF),
                      jnp.where(jnp.asarray(in_window & has_key), tab, NEG_INF)])


def _layout_w_in(w_in_l):
    pts = np.cumsum([0, M_QK_W, M_QK_W, M_V_W, M_V_W, M_HEADS, M_HEADS,
                     A_Q_W, A_KV_W, A_KV_W, D_MODEL, D_MODEL])
    wb = w_in_l.astype(BF16)
    col = lambda k: wb[:, pts[k]:pts[k + 1]]
    head = lambda h: wb[:, pts[6] + A_HEAD_DIM * h:pts[6] + A_HEAD_DIM * (h + 1)]
    w_if = jnp.concatenate([col(4), col(5)], axis=1)
    w_if_pad = jnp.pad(w_if, ((0, 0), (0, LANES - 2 * M_HEADS)))
    w = jnp.concatenate([col(0), col(2), col(3)] + [head(h) for h in A_HEAD_ORDER]
                        + [col(7), col(8), col(9), col(10), w_if_pad], axis=1)
    w_t = jnp.concatenate([col(1), w_if], axis=1).T
    return w, jnp.pad(w_t, ((0, -w_t.shape[0] % (2 * SUBLANES)), (0, 0)))


def _tile_tables(counts, n_tiles):
    te = EXPERT_TILE
    tiles = (counts + te - 1) // te
    ends = jnp.cumsum(tiles)
    off = (ends - tiles) * te
    n_used = ends[-1]
    tile_idx = jnp.minimum(jnp.arange(n_tiles, dtype=jnp.int32), n_used - 1)
    tile_cls = jnp.sum((ends[None, :] <= tile_idx[:, None]).astype(jnp.int32), axis=1)
    tile_cls = jnp.minimum(tile_cls, N_CLASSES - 1)
    off_pad = jnp.pad(off, (0, CLASS_ROWS - N_CLASSES)).astype(jnp.int32)
    return (off_pad, jnp.asarray(_CLASS_EA)[tile_cls], jnp.asarray(_CLASS_EB)[tile_cls],
            n_used.reshape(1).astype(jnp.int32))


def kernel(x, c, w_ada, b_ada, w_norm1, w_in, conv_w, conv_b, b_igate, b_fgate, w_mnorm, sinks, rel_bias,
           w_br_m, w_br_a, w_out, w_norm2, w_router, router_bias, w_gate_e, w_up_e, w_down_e, w_final):
    b, s, d = x.shape
    depth = w_ada.shape[0]
    t = b * s
    n_tiles = t // EXPERT_TILE + N_CLASSES
    npad = n_tiles * EXPERT_TILE

    mod_all = _ada_call(jnp.pad(c, ((0, 8 - b), (0, 0))), w_ada, b_ada)[:, :b]
    bias_tab = _bias_table(rel_bias)
    wrt = w_router.T
    wrh = wrt.astype(BF16)
    wrl = (wrt - wrh.astype(F32)).astype(BF16)
    rb = router_bias.reshape(N_EXPERTS, 1).astype(F32)
    row = lambda v: v.reshape(1, -1).astype(F32)

    wg_all, wu_all, wd_all = _cast_call(w_gate_e), _cast_call(w_up_e), _cast_call(w_down_e)

    moe = None
    for l in range(depth):
        mod = mod_all[l].reshape(b, 6, d)
        w_l, wift = _layout_w_in(w_in[l])
        gate_bias = jnp.concatenate([b_igate[l], b_fgate[l]]).astype(F32)
        bcol = jnp.pad(gate_bias, (0, LANES - 2 * M_HEADS)).reshape(1, LANES)
        brow = gate_bias.reshape(2 * M_HEADS, 1)

        proj_args = (mod, row(w_norm1[l]), w_l, wift, bcol, brow)
        if moe is None:
            q, kt, v, o, qa, kva, g, ifc, ifr = _inproj_call(x, *proj_args)
        else:
            x, q, kt, v, o, qa, kva, g, ifc, ifr = _combine_inproj_call(*moe, *proj_args)
        cw, cb = conv_w[l].astype(F32), conv_b[l].astype(F32)
        cwk = jnp.broadcast_to(cw[:, M_QK_W:, None], (M_CONV, M_QK_W, MLSTM_CHUNK))
        cbk = jnp.broadcast_to(cb[M_QK_W:, None], (M_QK_W, MLSTM_CHUNK))
        hm = _mlstm_call(q, kt, v, o, ifc, ifr, cw[:, :M_QK_W], row(cb[:M_QK_W]), cwk, cbk, row(w_mnorm[l]))
        ha = _swa_call(sinks[l].astype(F32), qa, kva, bias_tab)

        wba = jnp.concatenate([w_br_a[l][A_HEAD_DIM * h:A_HEAD_DIM * (h + 1)] for h in A_HEAD_ORDER])
        x1, payload, cls, rank, cnt = _post_call(
            hm, ha, g, x, mod, w_br_m[l].astype(BF16), wba.astype(BF16), w_out[l].astype(BF16),
            row(w_norm2[l]), wrh, wrl, rb)

        counts = cnt[:, 0].astype(jnp.int32)
        off, tile_ea, tile_eb, n_used = _tile_tables(counts[:N_CLASSES], n_tiles)
        hit = cls.reshape(t, 1) == jnp.arange(CLASS_ROWS, dtype=jnp.int32)[None, :]
        pos = jnp.sum(jnp.where(hit, off[None, :], 0), axis=1) + rank.reshape(t)
        xs = _dispatch_call(pos, off, counts, n_used,
                            payload.reshape(t // SUBLANES, SUBLANES, PAYLOAD_W), npad)
        ys = _moe_call(tile_ea, tile_eb, n_used, xs, wg_all, wu_all, wd_all, layer=l)
        moe = (pos, ys, x1, mod)
    return _combine_call(*moe, row(w_final))
```

```python
import numpy as np
import jax
import jax.numpy as jnp
from jax import lax
from jax.experimental import pallas as pl
from jax.experimental.pallas import tpu as pltpu

F32 = jnp.float32
BF16 = jnp.bfloat16

D_MODEL = 1024
M_HEADS = 4
M_DQK = 64
M_DV = 128
M_CONV = 4
M_QK_W = M_HEADS * M_DQK
M_V_W = M_HEADS * M_DV
A_HEADS = 8
A_KV_HEADS = 2
A_HEAD_DIM = 64
A_Q_W = A_HEADS * A_HEAD_DIM
A_KV_W = A_KV_HEADS * A_HEAD_DIM
WINDOW = 128
N_BUCKETS = 32
MAX_DISTANCE = 128
N_EXPERTS = 16
N_GROUPS = 4
EXPERTS_PER_GROUP = N_EXPERTS // N_GROUPS
D_EXPERT = 512
EPS = 1e-6
NEG_INF = -1e30

LANES = 128
SUBLANES = 8
MLSTM_CHUNK = 128
MLSTM_CHUNKS_PER_STEP = 8
SWA_BLOCKS = 8
TOKEN_TILE = 512
ROUTER_TILE = 1024
EXPERT_TILE = 256
ROW_TILE = 1024
PAIRS_PER_GROUP = 6
N_CLASSES = N_GROUPS * PAIRS_PER_GROUP
CLASS_ROWS = 32
PAYLOAD_W = D_MODEL + LANES
VMEM_LIMIT = 48 * 1024 * 1024

C_Q = 0
C_V = C_Q + M_QK_W
C_O = C_V + M_V_W
C_QA = C_O + M_V_W
C_KVA = C_QA + A_Q_W
C_G = C_KVA + 2 * A_KV_W
C_IF = C_G + 2 * D_MODEL
C_END = C_IF + LANES

A_HEAD_ORDER = (0, 4, 1, 5, 2, 6, 3, 7)

_PAIRS = [(a, b) for a in range(EXPERTS_PER_GROUP) for b in range(a + 1, EXPERTS_PER_GROUP)]
_CLASS_PAIRS = ((0, 1), (2, 1), (2, 0), (3, 0), (3, 1), (3, 2))
assert sorted(tuple(sorted(p)) for p in _CLASS_PAIRS) == _PAIRS
_CLASS_EA = np.array([g * EXPERTS_PER_GROUP + a for g in range(N_GROUPS) for a, _ in _CLASS_PAIRS], np.int32)
_CLASS_EB = np.array([g * EXPERTS_PER_GROUP + b for g in range(N_GROUPS) for _, b in _CLASS_PAIRS], np.int32)

_NT = (((1,), (1,)), ((), ()))


def _sigmoid(x):
    return 0.5 * jnp.tanh(0.5 * x) + 0.5


def _log_sigmoid(x):
    return -(jnp.maximum(-x, 0.0) + jnp.log1p(jnp.exp(-jnp.abs(x))))


def _dot(a, b, dims=None):
    if dims is None:
        return jnp.dot(a, b, preferred_element_type=F32)
    return lax.dot_general(a, b, dims, preferred_element_type=F32)


def _split2(a):
    hi = a.astype(BF16)
    lo = (a - hi.astype(F32)).astype(BF16)
    return hi, lo


def _split3(a):
    hi = a.astype(BF16)
    r = a - hi.astype(F32)
    mid = r.astype(BF16)
    lo = (r - mid.astype(F32)).astype(BF16)
    return hi, mid, lo


def _dot_hi(a, b, dims=None):
    ah, al = _split2(a)
    bh, bl = _split2(b)
    return _dot(ah, bh, dims) + _dot(ah, bl, dims) + _dot(al, bh, dims)


def _rms(x):
    return x * lax.rsqrt(jnp.mean(x * x, axis=-1, keepdims=True) + EPS)


def _params(*sem):
    return pltpu.CompilerParams(dimension_semantics=sem, vmem_limit_bytes=VMEM_LIMIT)


def _ada_kernel(c_ref, w_ref, b_ref, o_ref):
    c = c_ref[...]
    cond = c * _sigmoid(c)
    o_ref[0] = _dot_hi(cond, w_ref[0]) + b_ref[0]


def _ada_call(c8, w_ada, b_ada):
    depth = w_ada.shape[0]
    return pl.pallas_call(
        _ada_kernel,
        grid=(depth, 6),
        in_specs=[
            pl.BlockSpec((8, D_MODEL), lambda l, j: (0, 0)),
            pl.BlockSpec((1, D_MODEL, D_MODEL), lambda l, j: (l, 0, j)),
            pl.BlockSpec((1, 1, D_MODEL), lambda l, j: (l, 0, j)),
        ],
        out_specs=pl.BlockSpec((1, 8, D_MODEL), lambda l, j: (l, 0, j)),
        out_shape=jax.ShapeDtypeStruct((depth, 8, 6 * D_MODEL), F32),
        compiler_params=_params("arbitrary", "arbitrary"),
        name="adaln_mod",
    )(c8, w_ada, b_ada.reshape(depth, 1, 6 * D_MODEL))


def _project(x, mod_ref, wn_ref, w_ref, wift_ref, bcol_ref, brow_ref, outs):
    q_ref, kt_ref, v_ref, o_ref, qa_ref, kva_ref, g_ref, ifc_ref, ifr_ref = outs
    h = _rms(x) * (wn_ref[...] * (1.0 + mod_ref[0, 1:2, :])) + mod_ref[0, 0:1, :]
    hb = h.astype(BF16)

    def seg(a, b):
        return _dot(hb, w_ref[:, a:b])

    q_ref[0] = seg(C_Q, C_V).astype(BF16)
    v_ref[0] = seg(C_V, C_O).astype(BF16)
    o_ref[0] = seg(C_O, C_QA).astype(BF16)
    qa_ref[0] = seg(C_QA, C_KVA).astype(BF16)
    kva_ref[0] = seg(C_KVA, C_G).astype(BF16)
    g_ref[0] = seg(C_G, C_IF).astype(BF16)
    zc = seg(C_IF, C_END) + bcol_ref[...]
    lane = lax.broadcasted_iota(jnp.int32, zc.shape, 1)
    ifc_ref[0] = jnp.where((lane >= M_HEADS) & (lane < 2 * M_HEADS), _log_sigmoid(zc), zc)
    zt = _dot(wift_ref[...], hb, _NT)
    kt_ref[0] = zt[:M_QK_W].astype(BF16)
    zr = zt[M_QK_W:M_QK_W + 2 * M_HEADS] + brow_ref[...]
    row = lax.broadcasted_iota(jnp.int32, zr.shape, 0)
    ifr_ref[0] = jnp.where(row >= M_HEADS, _log_sigmoid(zr), zr)


def _inproj_kernel(x_ref, mod_ref, wn_ref, w_ref, wift_ref, bcol_ref, brow_ref, *outs):
    _project(x_ref[0], mod_ref, wn_ref, w_ref, wift_ref, bcol_ref, brow_ref, outs)


def _row_copies(pos_s, first_token, sorted_hbm, tiles, sem, *, to_sorted, wait):
    def copy(i, k, pos):
        pair = (tiles.at[i, pl.ds(k, 1), :], sorted_hbm.at[pl.ds(pos, 1), :])
        return pltpu.make_async_copy(*(pair if to_sorted else pair[::-1]), sem)

    if wait:
        def body(i, carry):
            for k in range(SUBLANES):
                copy(0, k, 0).wait()
            return carry

        lax.fori_loop(0, tiles.shape[0], body, 0, unroll=2)
        return
    for i in range(tiles.shape[0]):
        for k in range(SUBLANES):
            copy(i, k, pos_s[first_token + i * SUBLANES + k]).start(priority=k % 2)


def _moe_residual(pos_s, ys_hbm, x1_ref, mod_ref, ybuf, sem):
    rows = x1_ref.shape[1]
    step = pl.program_id(0) * pl.num_programs(1) + pl.program_id(1)
    n_steps = pl.num_programs(0) * pl.num_programs(1)
    slot = lax.rem(step, 2)

    def fetch(first_token, slot_, wait):
        _row_copies(pos_s, first_token, ys_hbm, ybuf.at[slot_], sem.at[slot_], to_sorted=False, wait=wait)

    @pl.when(step == 0)
    def _():
        fetch(0, 0, False)

    @pl.when(step + 1 < n_steps)
    def _():
        fetch((step + 1) * rows, 1 - slot, False)

    fetch(0, slot, True)
    y = ybuf[slot].reshape(rows, ybuf.shape[-1])
    return x1_ref[0] + mod_ref[0, 5:6, :] * y


def _combine_inproj_kernel(pos_s, ys_hbm, x1_ref, modp_ref, mod_ref, wn_ref, w_ref,
                           wift_ref, bcol_ref, brow_ref, xo_ref, *rest):
    outs, (ybuf, sem) = rest[:-2], rest[-2:]
    x = _moe_residual(pos_s, ys_hbm, x1_ref, modp_ref, ybuf, sem)
    xo_ref[0] = x
    _project(x, mod_ref, wn_ref, w_ref, wift_ref, bcol_ref, brow_ref, outs)


def _inproj_specs(b, s, tm):
    tok = lambda w_: pl.BlockSpec((1, tm, w_), lambda i, j, *_: (i, j, 0))
    sds = lambda w_, dt: jax.ShapeDtypeStruct((b, s, w_), dt)
    rows = lambda n: pl.BlockSpec((1, n, tm), lambda i, j, *_: (i, 0, j))
    out_specs = [tok(M_QK_W), rows(M_QK_W), tok(M_V_W), tok(M_V_W), tok(A_Q_W), tok(2 * A_KV_W),
                 tok(2 * D_MODEL), tok(LANES), rows(2 * M_HEADS)]
    out_shape = [sds(M_QK_W, BF16), jax.ShapeDtypeStruct((b, M_QK_W, s), BF16), sds(M_V_W, BF16),
                 sds(M_V_W, BF16), sds(A_Q_W, BF16), sds(2 * A_KV_W, BF16), sds(2 * D_MODEL, BF16),
                 sds(LANES, F32), jax.ShapeDtypeStruct((b, 2 * M_HEADS, s), F32)]
    return tok, out_specs, out_shape


def _inproj_call(x, mod, wn, w, wift, bcol, brow):
    b, s, d = x.shape
    tm = TOKEN_TILE
    tok, out_specs, out_shape = _inproj_specs(b, s, tm)
    full = lambda a: pl.BlockSpec(a.shape, lambda i, j: (0,) * a.ndim)
    return pl.pallas_call(
        _inproj_kernel,
        grid=(b, s // tm),
        in_specs=[tok(d), pl.BlockSpec((1, 6, d), lambda i, j: (i, 0, 0)), full(wn), full(w),
                  full(wift), full(bcol), full(brow)],
        out_specs=out_specs,
        out_shape=out_shape,
        compiler_params=_params("arbitrary", "arbitrary"),
        name="norm1_inproj",
    )(x, mod, wn, w, wift, bcol, brow)


def _combine_inproj_call(pos, ys, x1, mod_prev, mod, wn, w, wift, bcol, brow):
    b, s, d = x1.shape
    tm = TOKEN_TILE
    tok, out_specs, out_shape = _inproj_specs(b, s, tm)
    full = lambda a: pl.BlockSpec(a.shape, lambda i, j, *_: (0,) * a.ndim)
    modspec = pl.BlockSpec((1, 6, d), lambda i, j, *_: (i, 0, 0))
    return pl.pallas_call(
        _combine_inproj_kernel,
        grid_spec=pltpu.PrefetchScalarGridSpec(
            num_scalar_prefetch=1,
            grid=(b, s // tm),
            in_specs=[pl.BlockSpec(memory_space=pl.ANY), tok(d), modspec, modspec, full(wn), full(w),
                      full(wift), full(bcol), full(brow)],
            out_specs=[tok(d)] + out_specs,
            scratch_shapes=[pltpu.VMEM((2, tm // SUBLANES, SUBLANES, d), F32),
                            pltpu.SemaphoreType.DMA((2,))],
        ),
        out_shape=[jax.ShapeDtypeStruct((b, s, d), F32)] + out_shape,
        compiler_params=_params("arbitrary", "arbitrary"),
        name="combine_norm1_inproj",
    )(pos, ys, x1, mod_prev, mod, wn, w, wift, bcol, brow)


def _mlstm_kernel(q_ref, kt_ref, v_ref, o_ref, ifc_ref, ifr_ref, cwq_ref, cbq_ref, cwk_ref, cbk_ref, wn_ref,
                  out_ref, qbuf, kbuf, cstate, mstate):
    @pl.when(pl.program_id(1) == 0)
    def _():
        qbuf[...] = jnp.zeros(qbuf.shape, qbuf.dtype)
        kbuf[...] = jnp.zeros(kbuf.shape, kbuf.dtype)
        cstate[...] = jnp.zeros(cstate.shape, F32)
        mstate[...] = jnp.zeros(mstate.shape, F32)

    heads = range(M_HEADS)
    carry = (qbuf[...], kbuf[...], [cstate[h] for h in heads], [mstate[h][:, 0:1] for h in heads])
    for c in range(q_ref.shape[1] // MLSTM_CHUNK):
        carry = _mlstm_chunk(c, carry, q_ref, kt_ref, v_ref, o_ref, ifc_ref, ifr_ref, cwq_ref, cbq_ref,
                             cwk_ref, cbk_ref, wn_ref, out_ref)
    qbuf[...], kbuf[...] = carry[0], carry[1]
    for h in heads:
        cstate[h] = carry[2][h]
        mstate[h] = jnp.broadcast_to(carry[3][h], (1, LANES))


def _mlstm_chunk(c, carry, q_ref, kt_ref, v_ref, o_ref, ifc_ref, ifr_ref, cwq_ref, cbq_ref, cwk_ref, cbk_ref,
                 wn_ref, out_ref):
    L = MLSTM_CHUNK
    span = slice(L * c, L * (c + 1))
    q_tail, k_prev, cst, m_prev = carry
    pad = q_tail.shape[0]
    heads = range(M_HEADS)

    xq = q_ref[0, span, :]
    q_all = jnp.concatenate([q_tail, xq], axis=0)
    xk = kt_ref[0, :, span]
    k_all = jnp.concatenate([k_prev, xk], axis=1)
    r_s = lax.broadcasted_iota(jnp.int32, (L, L + pad), 0)
    c_s = lax.broadcasted_iota(jnp.int32, (L, L + pad), 1)
    r_t = lax.broadcasted_iota(jnp.int32, (2 * L, L), 0)
    c_t = lax.broadcasted_iota(jnp.int32, (2 * L, L), 1)
    acc_q = cbq_ref[...] + cwq_ref[M_CONV - 1:M_CONV, :] * xq.astype(F32)
    acc_k = cbk_ref[...] + cwk_ref[M_CONV - 1] * xk.astype(F32)
    for j in range(M_CONV - 1):
        back = M_CONV - 1 - j
        shift_q = jnp.where(c_s == r_s + (pad - back), 1.0, 0.0).astype(BF16)
        acc_q = acc_q + cwq_ref[j:j + 1, :] * _dot(shift_q, q_all)
        shift_k = jnp.where(r_t == c_t + (L - back), 1.0, 0.0).astype(BF16)
        acc_k = acc_k + cwk_ref[j] * _dot(k_all, shift_k)
    q = acc_q * _sigmoid(acc_q) * (M_DQK ** -0.5)
    kf = acc_k * _sigmoid(acc_k)
    kb = kf.astype(BF16)
    vb = v_ref[0, span, :]

    r_i = lax.broadcasted_iota(jnp.int32, (L, L), 0)
    c_i = lax.broadcasted_iota(jnp.int32, (L, L), 1)
    causal = c_i <= r_i
    tril = jnp.where(causal, 1.0, 0.0).astype(BF16)
    triu = jnp.where(r_i <= c_i, 1.0, 0.0).astype(BF16)

    icol = ifc_ref[0, span, :]
    irow = ifr_ref[0, :, span]
    ch, cm, cl = _split3(icol)
    bcum_col = _dot(tril, ch) + _dot(tril, cm) + _dot(tril, cl)
    rh, rm, rl = _split3(irow)
    bcum_row = _dot(rh, triu) + _dot(rm, triu) + _dot(rl, triu)

    lane = lax.broadcasted_iota(jnp.int32, (1, LANES), 1)
    lo_half = lane < M_DQK
    hi_half = jnp.logical_not(lo_half)
    ones = jnp.ones((L, LANES), BF16)
    pair = lambda h: slice(LANES * (h // 2), LANES * (h // 2 + 1))

    qmask =[jnp.where(lo_half if h % 2 == 0 else hi_half, q[:, pair(h)], 0.0) for h in heads]
    vaug = [jnp.concatenate([vb[:, M_DV * h:M_DV * (h + 1)], ones], axis=1) for h in heads]
    bc_col = [bcum_col[:, M_HEADS + h:M_HEADS + h + 1] for h in heads]
    bc_row = [bcum_row[M_HEADS + h:M_HEADS + h + 1, :] for h in heads]
    i_row = [irow[h:h + 1, :] for h in heads]
    b_last = [bc_row[h][:, L - 1:L] for h in heads]

    s = [_dot(qmask[h].astype(BF16), kb[pair(h), :]) for h in heads]

    dlog = [jnp.where(causal, bc_col[h] - bc_row[h] + i_row[h], NEG_INF) for h in heads]
    m_inter = [bc_col[h] + m_prev[h] for h in heads]
    m_row = [jnp.maximum(m_inter[h], jnp.max(dlog[h], axis=-1, keepdims=True)) for h in heads]
    sc = [(s[h] * jnp.exp(dlog[h] - m_row[h])).astype(BF16) for h in heads]
    qi = [(jnp.exp(m_inter[h] - m_row[h]) * qmask[h]).astype(BF16) for h in heads]

    num = [_dot(sc[h], vaug[h]) + _dot(qi[h], cst[h].astype(BF16)) for h in heads]
    for h in heads:
        den = num[h][:, M_DV:]
        hh = num[h][:, :M_DV] / jnp.maximum(jnp.abs(den), jnp.exp(-m_row[h]))
        hn = _rms(hh) * wn_ref[:, M_DV * h:M_DV * (h + 1)]
        og = _sigmoid(o_ref[0, span, M_DV * h:M_DV * (h + 1)].astype(F32))
        out_ref[0, span, M_DV * h:M_DV * (h + 1)] = (hn * og).astype(BF16)

    w_state = [b_last[h] - bc_row[h] + i_row[h] for h in heads]
    m_loc = [jnp.max(w_state[h], axis=-1, keepdims=True) for h in heads]
    ak = [(jnp.exp(w_state[h] - m_loc[h]) * kf[pair(h), :]).astype(BF16) for h in heads]
    c_loc = [_dot(ak[h], vaug[h]) for h in heads]
    m_new = [jnp.maximum(b_last[h] + m_prev[h], m_loc[h]) for h in heads]
    c_new = [jnp.exp(b_last[h] + m_prev[h] - m_new[h]) * cst[h] + jnp.exp(m_loc[h] - m_new[h]) * c_loc[h]
             for h in heads]
    return xq[L - pad:L, :], xk, c_new, m_new


def _mlstm_call(q, kt, v, o, ifc, ifr, cwq, cbq, cwk, cbk, w_mnorm):
    b, s, _ = q.shape
    L = MLSTM_CHUNK
    span = MLSTM_CHUNKS_PER_STEP * L
    tok = lambda w_: pl.BlockSpec((1, span, w_), lambda i, j: (i, j, 0))
    rows = lambda n: pl.BlockSpec((1, n, span), lambda i, j: (i, 0, j))
    full = lambda a: pl.BlockSpec(a.shape, lambda i, j: (0,) * a.ndim)
    return pl.pallas_call(
        _mlstm_kernel,
        grid=(b, s // span),
        in_specs=[tok(M_QK_W), rows(M_QK_W), tok(M_V_W), tok(M_V_W), tok(LANES), rows(2 * M_HEADS),
                  full(cwq), full(cbq), full(cwk), full(cbk), full(w_mnorm)],
        out_specs=tok(M_V_W),
        out_shape=jax.ShapeDtypeStruct((b, s, M_V_W), BF16),
        scratch_shapes=[pltpu.VMEM((2 * SUBLANES, M_QK_W), BF16),
                        pltpu.VMEM((M_QK_W, L), BF16),
                        pltpu.VMEM((M_HEADS, LANES, 2 * M_DV), F32),
                        pltpu.VMEM((M_HEADS, 1, LANES), F32)],
        compiler_params=_params("arbitrary", "arbitrary"),
        name="mlstm",
    )(q, kt, v, o, ifc, ifr, cwq, cbq, cwk, cbk, w_mnorm)


def _swa_kernel(sink_ref, q_ref, kvp_ref, kvc_ref, bias_ref, out_ref):
    W = WINDOW
    first = pl.program_id(1) == 0
    kv = [kvp_ref[0]] + [kvc_ref[0, W * u:W * (u + 1), :] for u in range(SWA_BLOCKS)]
    lane = lax.broadcasted_iota(jnp.int32, (1, LANES), 1)
    lo_half = lane < A_HEAD_DIM
    hi_half = jnp.logical_not(lo_half)
    keys, v_half = [], []
    for u in range(SWA_BLOCKS):
        keys.append(jnp.concatenate([kv[u][:, :A_KV_W], kv[u + 1][:, :A_KV_W]], axis=0))
        vals = jnp.concatenate([kv[u][:, A_KV_W:], kv[u + 1][:, A_KV_W:]], axis=0)
        zero = jnp.zeros_like(vals)
        v_half.append((jnp.where(lo_half, vals, zero), jnp.where(hi_half, vals, zero)))
    table = [jnp.where(first, 1, 0)] + [0] * (SWA_BLOCKS - 1)

    tiles = range(A_HEADS // 2)
    slots = [(u, j, p) for u in range(SWA_BLOCKS) for j in tiles for p in range(2)]
    head = {ujp: A_HEAD_ORDER[2 * ujp[1] + ujp[2]] for ujp in slots}
    scale = jnp.asarray(A_HEAD_DIM ** -0.5, BF16)
    qt = {(u, j): q_ref[0, W * u:W * (u + 1), LANES * j:LANES * (j + 1)] * scale
          for u in range(SWA_BLOCKS) for j in tiles}
    qm = {(u, j, p): jnp.where(lo_half if p == 0 else hi_half, qt[u, j], jnp.zeros_like(qt[u, j]))
          for u, j, p in slots}
    s = {ujp: _dot(qm[ujp], keys[ujp[0]], _NT) + bias_ref[table[ujp[0]], head[ujp]] for ujp in slots}
    m = {ujp: jnp.maximum(jnp.max(s[ujp], axis=-1, keepdims=True), sink_ref[head[ujp]]) for ujp in slots}
    e = {ujp: jnp.exp(s[ujp] - m[ujp]) for ujp in slots}
    denom = {ujp: jnp.sum(e[ujp], axis=-1, keepdims=True) + jnp.exp(sink_ref[head[ujp]] - m[ujp])
             for ujp in slots}
    pv = {(u, j, p): _dot(e[u, j, p].astype(BF16), v_half[u][p]) for u, j, p in slots}
    for u in range(SWA_BLOCKS):
        for j in tiles:
            out = pv[u, j, 0] / denom[u, j, 0] + pv[u, j, 1] / denom[u, j, 1]
            out_ref[0, W * u:W * (u + 1), LANES * j:LANES * (j + 1)] = out.astype(BF16)


def _swa_call(sinks, qa, kva, bias):
    b, s, _ = qa.shape
    W = WINDOW
    rows = SWA_BLOCKS * W
    return pl.pallas_call(
        _swa_kernel,
        grid=(b, s // rows),
        in_specs=[pl.BlockSpec(memory_space=pltpu.SMEM),
                  pl.BlockSpec((1, rows, A_Q_W), lambda i, j: (i, j, 0)),
                  pl.BlockSpec((1, W, 2 * A_KV_W), lambda i, j: (i, jnp.maximum(SWA_BLOCKS * j - 1, 0), 0)),
                  pl.BlockSpec((1, rows, 2 * A_KV_W), lambda i, j: (i, j, 0)),
                  pl.BlockSpec(bias.shape, lambda i, j: (0,) * bias.ndim)],
        out_specs=pl.BlockSpec((1, rows, A_Q_W), lambda i, j: (i, j, 0)),
        out_shape=jax.ShapeDtypeStruct((b, s, A_Q_W), BF16),
        compiler_params=_params("arbitrary", "arbitrary"),
        name="swa",
    )(sinks, qa, kva, kva, bias)


def _post_kernel(hm_ref, ha_ref, g_ref, x_ref, mod_ref, wbm_ref, wba_ref, wo_ref, wn2_ref,
                 wrh_ref, wrl_ref, rb_ref,
                 x1_ref, pay_ref, cls_ref, rank_ref, cnt_ref, carry):
    tm = x_ref.shape[1]

    @pl.when((pl.program_id(0) == 0) & (pl.program_id(1) == 0))
    def _():
        carry[...] = jnp.zeros(carry.shape, F32)

    g = g_ref[0]
    pm = _dot(hm_ref[0], wbm_ref[...])
    pa = _dot(ha_ref[0], wba_ref[...])
    merged = (_sigmoid(g[:, :D_MODEL]) * pm.astype(BF16)
              + _sigmoid(g[:, D_MODEL:]) * pa.astype(BF16))
    mo = _dot(merged, wo_ref[...])
    x1 = x_ref[0] + mod_ref[0, 2:3, :] * mo
    x1_ref[0] = x1
    h2 = _rms(x1) * (wn2_ref[...] * (1.0 + mod_ref[0, 4:5, :])) + mod_ref[0, 3:4, :]
    pay_ref[0, :, :D_MODEL] = h2

    hh, hl = _split2(h2)
    wrh = wrh_ref[...]
    logits = _dot(wrh, hh, _NT) + _dot(wrh, hl, _NT) + _dot(wrl_ref[...], hh, _NT)
    scores = 1.0 / (1.0 + jnp.exp(-logits))
    sel = scores + rb_ref[...]

    def row(a, r):
        return a[r:r + 1, :]

    grp = []
    for gi in range(N_GROUPS):
        v = [row(sel, gi * EXPERTS_PER_GROUP + k) for k in range(EXPERTS_PER_GROUP)]
        best = v[0] + v[1]
        for a, b in _PAIRS[1:]:
            best = jnp.maximum(best, v[a] + v[b])
        grp.append(best)
    gbest = grp[0]
    gsel = jnp.zeros_like(gbest)
    for gi in range(1, N_GROUPS):
        take = grp[gi] > gbest
        gbest = jnp.where(take, grp[gi], gbest)
        gsel = jnp.where(take, float(gi), gsel)

    sv, gv = [], []
    for k in range(EXPERTS_PER_GROUP):
        s_k = row(sel, k)
        g_k = row(scores, k)
        for gi in range(1, N_GROUPS):
            hit = gsel == float(gi)
            s_k = jnp.where(hit, row(sel, gi * EXPERTS_PER_GROUP + k), s_k)
            g_k = jnp.where(hit, row(scores, gi * EXPERTS_PER_GROUP + k), g_k)
        sv.append(s_k)
        gv.append(g_k)

    def argmax4(vals):
        bv, bi = vals[0], jnp.zeros_like(vals[0])
        for k in range(1, EXPERTS_PER_GROUP):
            take = vals[k] > bv
            bv = jnp.where(take, vals[k], bv)
            bi = jnp.where(take, float(k), bi)
        return bi

    i1 = argmax4(sv)
    i2 = argmax4([jnp.where(i1 == float(k), -jnp.inf, sv[k]) for k in range(EXPERTS_PER_GROUP)])
    w1 = jnp.zeros_like(i1)
    w2 = jnp.zeros_like(i1)
    for k in range(EXPERTS_PER_GROUP):
        w1 = jnp.where(i1 == float(k), gv[k], w1)
        w2 = jnp.where(i2 == float(k), gv[k], w2)
    wsum = w1 + w2
    w1 = w1 / wsum
    w2 = w2 / wsum
    code = i1 * float(EXPERTS_PER_GROUP) + i2
    pair_idx = jnp.zeros_like(code)
    gate_a, gate_b = w1, w2
    for k, (a, b) in enumerate(_CLASS_PAIRS):
        fwd = code == float(a * EXPERTS_PER_GROUP + b)
        rev = code == float(b * EXPERTS_PER_GROUP + a)
        pair_idx = jnp.where(fwd, float(k), jnp.where(rev, float(k), pair_idx))
        gate_a = jnp.where(rev, w2, gate_a)
        gate_b = jnp.where(rev, w1, gate_b)
    cls_f = gsel * float(PAIRS_PER_GROUP) + pair_idx
    cls_ref[0] = cls_f.astype(jnp.int32)

    grow = lax.broadcasted_iota(jnp.int32, (LANES, tm), 0)
    gmat = jnp.where(grow == 0, gate_a, jnp.where(grow == 1, gate_b, 0.0))
    pay_ref[0, :, D_MODEL:] = gmat.T

    crow = lax.broadcasted_iota(jnp.int32, (CLASS_ROWS, tm), 0).astype(F32)
    onehot = crow == cls_f
    r_i = lax.broadcasted_iota(jnp.int32, (tm, tm), 0)
    c_i = lax.broadcasted_iota(jnp.int32, (tm, tm), 1)
    upper = jnp.where(r_i <= c_i, 1.0, 0.0).astype(BF16)
    cum = _dot(jnp.where(onehot, 1.0, 0.0).astype(BF16), upper)
    before = carry[:, 0:1]
    rank = jnp.sum(jnp.where(onehot, cum - 1.0 + before, 0.0), axis=0, keepdims=True)
    rank_ref[0] = rank.astype(jnp.int32)
    total = before + cum[:, tm - 1:tm]
    carry[...] = jnp.broadcast_to(total, carry.shape)
    cnt_ref[...] = jnp.broadcast_to(total, cnt_ref.shape)


def _post_call(hm, ha, g, x, mod, wbm, wba, wo, wn2, wrh, wrl, rb):
    b, s, d = x.shape
    tm = ROUTER_TILE
    tok = lambda w_: pl.BlockSpec((1, tm, w_), lambda i, j: (i, j, 0))
    full = lambda a: pl.BlockSpec(a.shape, lambda i, j: (0,) * a.ndim)
    lanes = pl.BlockSpec((1, 1, tm), lambda i, j: (i, 0, j))
    return pl.pallas_call(
        _post_kernel,
        grid=(b, s // tm),
        in_specs=[tok(M_V_W), tok(A_Q_W), tok(2 * d), tok(d),
                  pl.BlockSpec((1, 6, d), lambda i, j: (i, 0, 0)),
                  full(wbm), full(wba), full(wo), full(wn2), full(wrh), full(wrl), full(rb)],
        out_specs=[tok(d), tok(PAYLOAD_W), lanes, lanes,
                   pl.BlockSpec((CLASS_ROWS, LANES), lambda i, j: (0, 0))],
        out_shape=[jax.ShapeDtypeStruct((b, s, d), F32),
                   jax.ShapeDtypeStruct((b, s, PAYLOAD_W), F32),
                   jax.ShapeDtypeStruct((b, 1, s), jnp.int32),
                   jax.ShapeDtypeStruct((b, 1, s), jnp.int32),
                   jax.ShapeDtypeStruct((CLASS_ROWS, LANES), F32)],
        scratch_shapes=[pltpu.VMEM((CLASS_ROWS, LANES), F32)],
        compiler_params=_params("arbitrary", "arbitrary"),
        name="merge_outproj_router",
    )(hm, ha, g, x, mod, wbm, wba, wo, wn2, wrh, wrl, rb)


def _zero_fill(off_s, cnt_s, nu_s, xs_out, zbuf, zsem, *, wait):
    chunk = zbuf.shape[0]

    def zero_rows(first, n):
        cp = pltpu.make_async_copy(zbuf.at[pl.ds(0, n), :], xs_out.at[pl.ds(first, n), :], zsem)
        cp.wait() if wait else cp.start()

    def per_class(c, carry):
        n = cnt_s[c]
        fill = (-n) & (EXPERT_TILE - 1)
        head = fill & (SUBLANES - 1)
        for k in range(SUBLANES - 1):
            pl.when(k < head)(lambda k=k: zero_rows(off_s[c] + n + k, 1))
        cur = off_s[c] + n + head
        p = chunk
        while p >= SUBLANES:
            pl.when((fill & p) != 0)(lambda cur=cur, p=p: zero_rows(pl.multiple_of(cur, SUBLANES), p))
            cur = cur + (fill & p)
            p //= 2
        return carry

    lax.fori_loop(0, N_CLASSES, per_class, 0)

    def per_chunk(r, carry):
        zero_rows(pl.multiple_of(r * chunk, SUBLANES), chunk)
        return carry

    per_tile = EXPERT_TILE // chunk
    lax.fori_loop(nu_s[0] * per_tile, (xs_out.shape[0] // EXPERT_TILE) * per_tile, per_chunk, 0)


def _dispatch_kernel(pos_s, off_s, cnt_s, nu_s, pay_ref, xs_out, zbuf, sem, zsem):
    @pl.when(pl.program_id(0) == 0)
    def _():
        zbuf[...] = jnp.zeros(zbuf.shape, F32)
        _zero_fill(off_s, cnt_s, nu_s, xs_out, zbuf, zsem, wait=False)
        _zero_fill(off_s, cnt_s, nu_s, xs_out, zbuf, zsem, wait=True)

    first_token = pl.program_id(0) * pay_ref.shape[0] * SUBLANES
    for wait in (False, True):
        _row_copies(pos_s, first_token, xs_out, pay_ref, sem, to_sorted=True, wait=wait)


def _dispatch_call(pos, off, cnt, n_used, payload, npad):
    t8, _, width = payload.shape
    tiles = ROW_TILE // SUBLANES
    return pl.pallas_call(
        _dispatch_kernel,
        grid_spec=pltpu.PrefetchScalarGridSpec(
            num_scalar_prefetch=4,
            grid=(t8 // tiles,),
            in_specs=[pl.BlockSpec((tiles, SUBLANES, width), lambda i, *_: (i, 0, 0))],
            out_specs=pl.BlockSpec(memory_space=pl.ANY),
            scratch_shapes=[pltpu.VMEM((EXPERT_TILE // 2, width), F32),
                            pltpu.SemaphoreType.DMA(()), pltpu.SemaphoreType.DMA(())],
        ),
        out_shape=jax.ShapeDtypeStruct((npad, width), F32),
        compiler_params=_params("arbitrary"),
        name="moe_dispatch",
    )(pos, off, cnt, n_used, payload)


def _moe_kernel(ea_s, eb_s, nu_s, x_ref, wga, wua, wda, wgb, wub, wdb, y_ref):
    del ea_s, eb_s

    @pl.when(pl.program_id(0) < nu_s[0])
    def _():
        x = x_ref[:, :D_MODEL].astype(BF16)
        gate_a = x_ref[:, D_MODEL:D_MODEL + 1]
        gate_b = x_ref[:, D_MODEL + 1:D_MODEL + 2]

        up = [(_dot(x, wg[0].astype(BF16)), _dot(x, wu[0].astype(BF16))) for wg, wu in ((wga, wua), (wgb, wub))]
        act = [(hg * _sigmoid(hg) * hu).astype(BF16) for hg, hu in up]
        y_ref[...] = (gate_a * _dot(act[0], wda[0].astype(BF16))
                      + gate_b * _dot(act[1], wdb[0].astype(BF16)))

    @pl.when(pl.program_id(0) >= nu_s[0])
    def _():
        y_ref[...] = jnp.zeros(y_ref.shape, F32)


def _moe_call(tile_ea, tile_eb, n_used, xs, wg, wu, wd, layer):
    npad = xs.shape[0]
    te = EXPERT_TILE
    last = lambda i, nu: jnp.minimum(i, nu[0] - 1)
    first = layer * N_EXPERTS
    pick = lambda tab: (lambda i, ea, eb, nu: (first + (ea, eb)[tab][i], 0, 0))
    up = lambda tab: pl.BlockSpec((1, D_MODEL, D_EXPERT), pick(tab))
    down = lambda tab: pl.BlockSpec((1, D_EXPERT, D_MODEL), pick(tab))
    return pl.pallas_call(
        _moe_kernel,
        grid_spec=pltpu.PrefetchScalarGridSpec(
            num_scalar_prefetch=3,
            grid=(npad // te,),
            in_specs=[pl.BlockSpec((te, PAYLOAD_W), lambda i, ea, eb, nu: (last(i, nu), 0)),
                      up(0), up(0), down(0), up(1), up(1), down(1)],
            out_specs=pl.BlockSpec((te, D_MODEL), lambda i, ea, eb, nu: (i, 0)),
        ),
        out_shape=jax.ShapeDtypeStruct((npad, D_MODEL), F32),
        compiler_params=_params("arbitrary"),
        name="moe_experts",
    )(tile_ea, tile_eb, n_used, xs, wg, wu, wd, wg, wu, wd)


def _combine_kernel(pos_s, ys_hbm, x1_ref, mod_ref, wf_ref, out_ref, ybuf, sem):
    x2 = _moe_residual(pos_s, ys_hbm, x1_ref, mod_ref, ybuf, sem)
    out_ref[0] = _rms(x2) * wf_ref[...]


def _combine_call(pos, ys, x1, mod, wf):
    b, s, d = x1.shape
    rows = ROW_TILE
    return pl.pallas_call(
        _combine_kernel,
        grid_spec=pltpu.PrefetchScalarGridSpec(
            num_scalar_prefetch=1,
            grid=(b, s // rows),
            in_specs=[pl.BlockSpec(memory_space=pl.ANY),
                      pl.BlockSpec((1, rows, d), lambda i, j, *_: (i, j, 0)),
                      pl.BlockSpec((1, 6, d), lambda i, j, *_: (i, 0, 0)),
                      pl.BlockSpec((1, d), lambda i, j, *_: (0, 0))],
            out_specs=pl.BlockSpec((1, rows, d), lambda i, j, *_: (i, j, 0)),
            scratch_shapes=[pltpu.VMEM((2, rows // SUBLANES, SUBLANES, d), F32),
                            pltpu.SemaphoreType.DMA((2,))],
        ),
        out_shape=jax.ShapeDtypeStruct((b, s, d), F32),
        compiler_params=_params("arbitrary", "arbitrary"),
        name="moe_combine",
    )(pos, ys, x1, mod, wf)


def _t5_bucket(n):
    max_exact = N_BUCKETS // 2
    large = max_exact + (np.log(np.maximum(n, 1) / max_exact)
                         / np.log(MAX_DISTANCE / max_exact)
                         * (N_BUCKETS - max_exact)).astype(np.int32)
    large = np.minimum(large, N_BUCKETS - 1)
    return np.where(n < max_exact, n, large).astype(np.int32)


def _bias_table(rel_bias):
    dist = np.arange(WINDOW)[:, None] + WINDOW - np.arange(2 * WINDOW)[None, :]
    bucket = _t5_bucket(np.maximum(dist, 0)).reshape(1, -1)
    onehot = (jnp.arange(N_BUCKETS, dtype=jnp.int32)[:, None] == jnp.asarray(bucket)).astype(F32)
    tab = jnp.dot(rel_bias.astype(F32).T, onehot, precision=lax.Precision.HIGHEST)
    tab = tab.reshape(A_HEADS, WINDOW, 2 * WINDOW)
    in_window = (dist >= 0) & (dist < WINDOW)
    has_key = np.arange(2 * WINDOW)[None, :] >= WINDOW
    return jnp.stack([jnp.where(jnp.asarray(in_window), tab, NEG_INF),
                      jnp.where(jnp.asarray(in_window & has_key), tab, NEG_INF)])


def _layout_w_in(w_in_l):
    pts = np.cumsum([0, M_QK_W, M_QK_W, M_V_W, M_V_W, M_HEADS, M_HEADS,
                     A_Q_W, A_KV_W, A_KV_W, D_MODEL, D_MODEL])
    wb = w_in_l.astype(BF16)
    col = lambda k: wb[:, pts[k]:pts[k + 1]]
    head = lambda h: wb[:, pts[6] + A_HEAD_DIM * h:pts[6] + A_HEAD_DIM * (h + 1)]
    w_if = jnp.concatenate([col(4), col(5)], axis=1)
    w_if_pad = jnp.pad(w_if, ((0, 0), (0, LANES - 2 * M_HEADS)))
    w = jnp.concatenate([col(0), col(2), col(3)] + [head(h) for h in A_HEAD_ORDER]
                        + [col(7), col(8), col(9), col(10), w_if_pad], axis=1)
    w_t = jnp.concatenate([col(1), w_if], axis=1).T
    return w, jnp.pad(w_t, ((0, -w_t.shape[0] % (2 * SUBLANES)), (0, 0)))


def _tile_tables(counts, n_tiles):
    te = EXPERT_TILE
    tiles = (counts + te - 1) // te
    ends = jnp.cumsum(tiles)
    off = (ends - tiles) * te
    n_used = ends[-1]
    tile_idx = jnp.minimum(jnp.arange(n_tiles, dtype=jnp.int32), n_used - 1)
    tile_cls = jnp.sum((ends[None, :] <= tile_idx[:, None]).astype(jnp.int32), axis=1)
    tile_cls = jnp.minimum(tile_cls, N_CLASSES - 1)
    off_pad = jnp.pad(off, (0, CLASS_ROWS - N_CLASSES)).astype(jnp.int32)
    return (off_pad, jnp.asarray(_CLASS_EA)[tile_cls], jnp.asarray(_CLASS_EB)[tile_cls],
            n_used.reshape(1).astype(jnp.int32))


def kernel(x, c, w_ada, b_ada, w_norm1, w_in, conv_w, conv_b, b_igate, b_fgate, w_mnorm, sinks, rel_bias,
           w_br_m, w_br_a, w_out, w_norm2, w_router, router_bias, w_gate_e, w_up_e, w_down_e, w_final):
    b, s, d = x.shape
    depth = w_ada.shape[0]
    t = b * s
    n_tiles = t // EXPERT_TILE + N_CLASSES
    npad = n_tiles * EXPERT_TILE

    mod_all = _ada_call(jnp.pad(c, ((0, 8 - b), (0, 0))), w_ada, b_ada)[:, :b]
    bias_tab = _bias_table(rel_bias)
    wrt = w_router.T
    wrh = wrt.astype(BF16)
    wrl = (wrt - wrh.astype(F32)).astype(BF16)
    rb = router_bias.reshape(N_EXPERTS, 1).astype(F32)
    row = lambda v: v.reshape(1, -1).astype(F32)

    wg_all, wu_all, wd_all = [w.reshape((-1,) + w.shape[2:]) for w in (w_gate_e, w_up_e, w_down_e)]

    moe = None
    for l in range(depth):
        mod = mod_all[l].reshape(b, 6, d)
        w_l, wift = _layout_w_in(w_in[l])
        gate_bias = jnp.concatenate([b_igate[l], b_fgate[l]]).astype(F32)
        bcol = jnp.pad(gate_bias, (0, LANES - 2 * M_HEADS)).reshape(1, LANES)
        brow = gate_bias.reshape(2 * M_HEADS, 1)

        proj_args = (mod, row(w_norm1[l]), w_l, wift, bcol, brow)
        if moe is None:
            q, kt, v, o, qa, kva, g, ifc, ifr = _inproj_call(x, *proj_args)
        else:
            x, q, kt, v, o, qa, kva, g, ifc, ifr = _combine_inproj_call(*moe, *proj_args)
        cw, cb = conv_w[l].astype(F32), conv_b[l].astype(F32)
        cwk = jnp.broadcast_to(cw[:, M_QK_W:, None], (M_CONV, M_QK_W, MLSTM_CHUNK))
        cbk = jnp.broadcast_to(cb[M_QK_W:, None], (M_QK_W, MLSTM_CHUNK))
        hm = _mlstm_call(q, kt, v, o, ifc, ifr, cw[:, :M_QK_W], row(cb[:M_QK_W]), cwk, cbk, row(w_mnorm[l]))
        ha = _swa_call(sinks[l].astype(F32), qa, kva, bias_tab)

        wba = jnp.concatenate([w_br_a[l][A_HEAD_DIM * h:A_HEAD_DIM * (h + 1)] for h in A_HEAD_ORDER])
        x1, payload, cls, rank, cnt = _post_call(
            hm, ha, g, x, mod, w_br_m[l].astype(BF16), wba.astype(BF16), w_out[l].astype(BF16),
            row(w_norm2[l]), wrh, wrl, rb)

        counts = cnt[:, 0].astype(jnp.int32)
        off, tile_ea, tile_eb, n_used = _tile_tables(counts[:N_CLASSES], n_tiles)
        hit = cls.reshape(t, 1) == jnp.arange(CLASS_ROWS, dtype=jnp.int32)[None, :]
        pos = jnp.sum(jnp.where(hit, off[None, :], 0), axis=1) + rank.reshape(t)
        xs = _dispatch_call(pos, off, counts, n_used,
                            payload.reshape(t // SUBLANES, SUBLANES, PAYLOAD_W), npad)
        ys = _moe_call(tile_ea, tile_eb, n_used, xs, wg_all, wu_all, wd_all, layer=l)
        moe = (pos, ys, x1, mod)
    return _combine_call(*moe, row(w_final))
```

```python
import numpy as np
import jax
import jax.numpy as jnp
from jax import lax
from jax.experimental import pallas as pl
from jax.experimental.pallas import tpu as pltpu

F32 = jnp.float32
BF16 = jnp.bfloat16

D_MODEL = 1024
M_HEADS = 4
M_DQK = 64
M_DV = 128
M_CONV = 4
M_QK_W = M_HEADS * M_DQK
M_V_W = M_HEADS * M_DV
A_HEADS = 8
A_KV_HEADS = 2
A_HEAD_DIM = 64
A_Q_W = A_HEADS * A_HEAD_DIM
A_KV_W = A_KV_HEADS * A_HEAD_DIM
WINDOW = 128
N_BUCKETS = 32
MAX_DISTANCE = 128
N_EXPERTS = 16
N_GROUPS = 4
EXPERTS_PER_GROUP = N_EXPERTS // N_GROUPS
D_EXPERT = 512
EPS = 1e-6
NEG_INF = -1e30

LANES = 128
SUBLANES = 8
MLSTM_CHUNK = 128
MLSTM_CHUNKS_PER_STEP = 8
SWA_BLOCKS = 8
TOKEN_TILE = 512
ROUTER_TILE = 1024
EXPERT_TILE = 256
ROW_TILE = 1024
PAIRS_PER_GROUP = 6
N_CLASSES = N_GROUPS * PAIRS_PER_GROUP
CLASS_ROWS = 32
PAYLOAD_W = D_MODEL + LANES
VMEM_LIMIT = 48 * 1024 * 1024

C_Q = 0
C_V = C_Q + M_QK_W
C_O = C_V + M_V_W
C_QA = C_O + M_V_W
C_KVA = C_QA + A_Q_W
C_G = C_KVA + 2 * A_KV_W
C_IF = C_G + 2 * D_MODEL
C_END = C_IF + LANES

A_HEAD_ORDER = (0, 4, 1, 5, 2, 6, 3, 7)

_PAIRS = [(a, b) for a in range(EXPERTS_PER_GROUP) for b in range(a + 1, EXPERTS_PER_GROUP)]
_CLASS_PAIRS = ((0, 1), (2, 1), (2, 0), (3, 0), (3, 1), (3, 2))
assert sorted(tuple(sorted(p)) for p in _CLASS_PAIRS) == _PAIRS
_CLASS_EA = np.array([g * EXPERTS_PER_GROUP + a for g in range(N_GROUPS) for a, _ in _CLASS_PAIRS], np.int32)
_CLASS_EB = np.array([g * EXPERTS_PER_GROUP + b for g in range(N_GROUPS) for _, b in _CLASS_PAIRS], np.int32)

_NT = (((1,), (1,)), ((), ()))


def _sigmoid(x):
    return 0.5 * jnp.tanh(0.5 * x) + 0.5


def _log_sigmoid(x):
    return -(jnp.maximum(-x, 0.0) + jnp.log1p(jnp.exp(-jnp.abs(x))))


def _dot(a, b, dims=None):
    if dims is None:
        return jnp.dot(a, b, preferred_element_type=F32)
    return lax.dot_general(a, b, dims, preferred_element_type=F32)


def _split2(a):
    hi = a.astype(BF16)
    lo = (a - hi.astype(F32)).astype(BF16)
    return hi, lo


def _split3(a):
    hi = a.astype(BF16)
    r = a - hi.astype(F32)
    mid = r.astype(BF16)
    lo = (r - mid.astype(F32)).astype(BF16)
    return hi, mid, lo


def _dot_hi(a, b, dims=None):
    ah, al = _split2(a)
    bh, bl = _split2(b)
    return _dot(ah, bh, dims) + _dot(ah, bl, dims) + _dot(al, bh, dims)


def _rms(x):
    return x * lax.rsqrt(jnp.mean(x * x, axis=-1, keepdims=True) + EPS)


def _params(*sem):
    return pltpu.CompilerParams(dimension_semantics=sem, vmem_limit_bytes=VMEM_LIMIT)


def _ada_kernel(c_ref, w_ref, b_ref, o_ref):
    c = c_ref[...]
    cond = c * _sigmoid(c)
    o_ref[0] = _dot_hi(cond, w_ref[0]) + b_ref[0]


def _ada_call(c8, w_ada, b_ada):
    depth = w_ada.shape[0]
    return pl.pallas_call(
        _ada_kernel,
        grid=(depth, 6),
        in_specs=[
            pl.BlockSpec((8, D_MODEL), lambda l, j: (0, 0)),
            pl.BlockSpec((1, D_MODEL, D_MODEL), lambda l, j: (l, 0, j)),
            pl.BlockSpec((1, 1, D_MODEL), lambda l, j: (l, 0, j)),
        ],
        out_specs=pl.BlockSpec((1, 8, D_MODEL), lambda l, j: (l, 0, j)),
        out_shape=jax.ShapeDtypeStruct((depth, 8, 6 * D_MODEL), F32),
        compiler_params=_params("arbitrary", "arbitrary"),
        name="adaln_mod",
    )(c8, w_ada, b_ada.reshape(depth, 1, 6 * D_MODEL))


def _project(x, mod_ref, wn_ref, w_ref, wift_ref, bcol_ref, brow_ref, outs):
    q_ref, kt_ref, v_ref, o_ref, qa_ref, kva_ref, g_ref, ifc_ref, ifr_ref = outs
    h = _rms(x) * (wn_ref[...] * (1.0 + mod_ref[0, 1:2, :])) + mod_ref[0, 0:1, :]
    hb = h.astype(BF16)

    def seg(a, b):
        return _dot(hb, w_ref[:, a:b])

    q_ref[0] = seg(C_Q, C_V).astype(BF16)
    v_ref[0] = seg(C_V, C_O).astype(BF16)
    o_ref[0] = seg(C_O, C_QA).astype(BF16)
    qa_ref[0] = seg(C_QA, C_KVA).astype(BF16)
    kva_ref[0] = seg(C_KVA, C_G).astype(BF16)
    g_ref[0] = seg(C_G, C_IF).astype(BF16)
    zc = seg(C_IF, C_END) + bcol_ref[...]
    lane = lax.broadcasted_iota(jnp.int32, zc.shape, 1)
    ifc_ref[0] = jnp.where((lane >= M_HEADS) & (lane < 2 * M_HEADS), _log_sigmoid(zc), zc)
    zt = _dot(wift_ref[...], hb, _NT)
    kt_ref[0] = zt[:M_QK_W].astype(BF16)
    zr = zt[M_QK_W:M_QK_W + 2 * M_HEADS] + brow_ref[...]
    row = lax.broadcasted_iota(jnp.int32, zr.shape, 0)
    ifr_ref[0] = jnp.where(row >= M_HEADS, _log_sigmoid(zr), zr)


def _inproj_kernel(x_ref, mod_ref, wn_ref, w_ref, wift_ref, bcol_ref, brow_ref, *outs):
    _project(x_ref[0], mod_ref, wn_ref, w_ref, wift_ref, bcol_ref, brow_ref, outs)


def _row_copies(pos_s, first_token, sorted_hbm, tiles, sem, *, to_sorted, wait):
    def copy(i, k, pos):
        pair = (tiles.at[i, pl.ds(k, 1), :], sorted_hbm.at[pl.ds(pos, 1), :])
        return pltpu.make_async_copy(*(pair if to_sorted else pair[::-1]), sem)

    if wait:
        def body(i, carry):
            for k in range(SUBLANES):
                copy(0, k, 0).wait()
            return carry

        lax.fori_loop(0, tiles.shape[0], body, 0, unroll=2)
        return
    for i in range(tiles.shape[0]):
        for k in range(SUBLANES):
            copy(i, k, pos_s[first_token + i * SUBLANES + k]).start(priority=k % 2)


def _moe_residual(pos_s, ys_hbm, x1_ref, mod_ref, ybuf, sem):
    rows = x1_ref.shape[1]
    step = pl.program_id(0) * pl.num_programs(1) + pl.program_id(1)
    n_steps = pl.num_programs(0) * pl.num_programs(1)
    slot = lax.rem(step, 2)

    def fetch(first_token, slot_, wait):
        _row_copies(pos_s, first_token, ys_hbm, ybuf.at[slot_], sem.at[slot_], to_sorted=False, wait=wait)

    @pl.when(step == 0)
    def _():
        fetch(0, 0, False)

    @pl.when(step + 1 < n_steps)
    def _():
        fetch((step + 1) * rows, 1 - slot, False)

    fetch(0, slot, True)
    y = ybuf[slot].reshape(rows, ybuf.shape[-1])
    return x1_ref[0] + mod_ref[0, 5:6, :] * y


def _combine_inproj_kernel(pos_s, ys_hbm, x1_ref, modp_ref, mod_ref, wn_ref, w_ref,
                           wift_ref, bcol_ref, brow_ref, xo_ref, *rest):
    outs, (ybuf, sem) = rest[:-2], rest[-2:]
    x = _moe_residual(pos_s, ys_hbm, x1_ref, modp_ref, ybuf, sem)
    xo_ref[0] = x
    _project(x, mod_ref, wn_ref, w_ref, wift_ref, bcol_ref, brow_ref, outs)


def _inproj_specs(b, s, tm):
    tok = lambda w_: pl.BlockSpec((1, tm, w_), lambda i, j, *_: (i, j, 0))
    sds = lambda w_, dt: jax.ShapeDtypeStruct((b, s, w_), dt)
    rows = lambda n: pl.BlockSpec((1, n, tm), lambda i, j, *_: (i, 0, j))
    out_specs = [tok(M_QK_W), rows(M_QK_W), tok(M_V_W), tok(M_V_W), tok(A_Q_W), tok(2 * A_KV_W),
                 tok(2 * D_MODEL), tok(LANES), rows(2 * M_HEADS)]
    out_shape = [sds(M_QK_W, BF16), jax.ShapeDtypeStruct((b, M_QK_W, s), BF16), sds(M_V_W, BF16),
                 sds(M_V_W, BF16), sds(A_Q_W, BF16), sds(2 * A_KV_W, BF16), sds(2 * D_MODEL, BF16),
                 sds(LANES, F32), jax.ShapeDtypeStruct((b, 2 * M_HEADS, s), F32)]
    return tok, out_specs, out_shape


def _inproj_call(x, mod, wn, w, wift, bcol, brow):
    b, s, d = x.shape
    tm = TOKEN_TILE
    tok, out_specs, out_shape = _inproj_specs(b, s, tm)
    full = lambda a: pl.BlockSpec(a.shape, lambda i, j: (0,) * a.ndim)
    return pl.pallas_call(
        _inproj_kernel,
        grid=(b, s // tm),
        in_specs=[tok(d), pl.BlockSpec((1, 6, d), lambda i, j: (i, 0, 0)), full(wn), full(w),
                  full(wift), full(bcol), full(brow)],
        out_specs=out_specs,
        out_shape=out_shape,
        compiler_params=_params("arbitrary", "arbitrary"),
        name="norm1_inproj",
    )(x, mod, wn, w, wift, bcol, brow)


def _combine_inproj_call(pos, ys, x1, mod_prev, mod, wn, w, wift, bcol, brow):
    b, s, d = x1.shape
    tm = TOKEN_TILE
    tok, out_specs, out_shape = _inproj_specs(b, s, tm)
    full = lambda a: pl.BlockSpec(a.shape, lambda i, j, *_: (0,) * a.ndim)
    modspec = pl.BlockSpec((1, 6, d), lambda i, j, *_: (i, 0, 0))
    return pl.pallas_call(
        _combine_inproj_kernel,
        grid_spec=pltpu.PrefetchScalarGridSpec(
            num_scalar_prefetch=1,
            grid=(b, s // tm),
            in_specs=[pl.BlockSpec(memory_space=pl.ANY), tok(d), modspec, modspec, full(wn), full(w),
                      full(wift), full(bcol), full(brow)],
            out_specs=[tok(d)] + out_specs,
            scratch_shapes=[pltpu.VMEM((2, tm // SUBLANES, SUBLANES, d), F32),
                            pltpu.SemaphoreType.DMA((2,))],
        ),
        out_shape=[jax.ShapeDtypeStruct((b, s, d), F32)] + out_shape,
        compiler_params=_params("arbitrary", "arbitrary"),
        name="combine_norm1_inproj",
    )(pos, ys, x1, mod_prev, mod, wn, w, wift, bcol, brow)


def _mlstm_kernel(q_ref, kt_ref, v_ref, o_ref, ifc_ref, ifr_ref, cwq_ref, cbq_ref, cwk_ref, cbk_ref, wn_ref,
                  out_ref, qbuf, kbuf, cstate, mstate):
    @pl.when(pl.program_id(1) == 0)
    def _():
        qbuf[...] = jnp.zeros(qbuf.shape, qbuf.dtype)
        kbuf[...] = jnp.zeros(kbuf.shape, kbuf.dtype)
        cstate[...] = jnp.zeros(cstate.shape, F32)
        mstate[...] = jnp.zeros(mstate.shape, F32)

    heads = range(M_HEADS)
    carry = (qbuf[...], kbuf[...], [cstate[h] for h in heads], [mstate[h][:, 0:1] for h in heads])
    for c in range(q_ref.shape[1] // MLSTM_CHUNK):
        carry = _mlstm_chunk(c, carry, q_ref, kt_ref, v_ref, o_ref, ifc_ref, ifr_ref, cwq_ref, cbq_ref,
                             cwk_ref, cbk_ref, wn_ref, out_ref)
    qbuf[...], kbuf[...] = carry[0], carry[1]
    for h in heads:
        cstate[h] = carry[2][h]
        mstate[h] = jnp.broadcast_to(carry[3][h], (1, LANES))


def _mlstm_chunk(c, carry, q_ref, kt_ref, v_ref, o_ref, ifc_ref, ifr_ref, cwq_ref, cbq_ref, cwk_ref, cbk_ref,
                 wn_ref, out_ref):
    L = MLSTM_CHUNK
    span = slice(L * c, L * (c + 1))
    q_tail, k_prev, cst, m_prev = carry
    pad = q_tail.shape[0]
    heads = range(M_HEADS)

    xq = q_ref[0, span, :]
    q_all = jnp.concatenate([q_tail, xq], axis=0)
    xk = kt_ref[0, :, span]
    k_all = jnp.concatenate([k_prev, xk], axis=1)
    r_s = lax.broadcasted_iota(jnp.int32, (L, L + pad), 0)
    c_s = lax.broadcasted_iota(jnp.int32, (L, L + pad), 1)
    r_t = lax.broadcasted_iota(jnp.int32, (2 * L, L), 0)
    c_t = lax.broadcasted_iota(jnp.int32, (2 * L, L), 1)
    acc_q = cbq_ref[...] + cwq_ref[M_CONV - 1:M_CONV, :] * xq.astype(F32)
    acc_k = cbk_ref[...] + cwk_ref[M_CONV - 1] * xk.astype(F32)
    for j in range(M_CONV - 1):
        back = M_CONV - 1 - j
        shift_q = jnp.where(c_s == r_s + (pad - back), 1.0, 0.0).astype(BF16)
        acc_q = acc_q + cwq_ref[j:j + 1, :] * _dot(shift_q, q_all)
        shift_k = jnp.where(r_t == c_t + (L - back), 1.0, 0.0).astype(BF16)
        acc_k = acc_k + cwk_ref[j] * _dot(k_all, shift_k)
    q = acc_q * _sigmoid(acc_q) * (M_DQK ** -0.5)
    kf = acc_k * _sigmoid(acc_k)
    kb = kf.astype(BF16)
    vb = v_ref[0, span, :]

    r_i = lax.broadcasted_iota(jnp.int32, (L, L), 0)
    c_i = lax.broadcasted_iota(jnp.int32, (L, L), 1)
    causal = c_i <= r_i
    tril = jnp.where(causal, 1.0, 0.0).astype(BF16)
    triu = jnp.where(r_i <= c_i, 1.0, 0.0).astype(BF16)

    icol = ifc_ref[0, span, :]
    irow = ifr_ref[0, :, span]
    ch, cm, cl = _split3(icol)
    bcum_col = _dot(tril, ch) + _dot(tril, cm) + _dot(tril, cl)
    rh, rm, rl = _split3(irow)
    bcum_row = _dot(rh, triu) + _dot(rm, triu) + _dot(rl, triu)

    lane = lax.broadcasted_iota(jnp.int32, (1, LANES), 1)
    lo_half = lane < M_DQK
    hi_half = jnp.logical_not(lo_half)
    ones = jnp.ones((L, LANES), BF16)
    pair = lambda h: slice(LANES * (h // 2), LANES * (h // 2 + 1))

    qmask =[jnp.where(lo_half if h % 2 == 0 else hi_half, q[:, pair(h)], 0.0) for h in heads]
    vaug = [jnp.concatenate([vb[:, M_DV * h:M_DV * (h + 1)], ones], axis=1) for h in heads]
    bc_col = [bcum_col[:, M_HEADS + h:M_HEADS + h + 1] for h in heads]
    bc_row = [bcum_row[M_HEADS + h:M_HEADS + h + 1, :] for h in heads]
    i_row = [irow[h:h + 1, :] for h in heads]
    b_last = [bc_row[h][:, L - 1:L] for h in heads]

    s = [_dot(qmask[h].astype(BF16), kb[pair(h), :]) for h in heads]

    dlog = [jnp.where(causal, bc_col[h] - bc_row[h] + i_row[h], NEG_INF) for h in heads]
    m_inter = [bc_col[h] + m_prev[h] for h in heads]
    m_row = [jnp.maximum(m_inter[h], jnp.max(dlog[h], axis=-1, keepdims=True)) for h in heads]
    sc = [(s[h] * jnp.exp(dlog[h] - m_row[h])).astype(BF16) for h in heads]
    qi = [(jnp.exp(m_inter[h] - m_row[h]) * qmask[h]).astype(BF16) for h in heads]

    num = [_dot(sc[h], vaug[h]) + _dot(qi[h], cst[h].astype(BF16)) for h in heads]
    for h in heads:
        den = num[h][:, M_DV:]
        hh = num[h][:, :M_DV] / jnp.maximum(jnp.abs(den), jnp.exp(-m_row[h]))
        hn = _rms(hh) * wn_ref[:, M_DV * h:M_DV * (h + 1)]
        og = _sigmoid(o_ref[0, span, M_DV * h:M_DV * (h + 1)].astype(F32))
        out_ref[0, span, M_DV * h:M_DV * (h + 1)] = (hn * og).astype(BF16)

    w_state = [b_last[h] - bc_row[h] + i_row[h] for h in heads]
    m_loc = [jnp.max(w_state[h], axis=-1, keepdims=True) for h in heads]
    ak = [(jnp.exp(w_state[h] - m_loc[h]) * kf[pair(h), :]).astype(BF16) for h in heads]
    c_loc = [_dot(ak[h], vaug[h]) for h in heads]
    m_new = [jnp.maximum(b_last[h] + m_prev[h], m_loc[h]) for h in heads]
    c_new = [jnp.exp(b_last[h] + m_prev[h] - m_new[h]) * cst[h] + jnp.exp(m_loc[h] - m_new[h]) * c_loc[h]
             for h in heads]
    return xq[L - pad:L, :], xk, c_new, m_new


def _mlstm_call(q, kt, v, o, ifc, ifr, cwq, cbq, cwk, cbk, w_mnorm):
    b, s, _ = q.shape
    L = MLSTM_CHUNK
    span = MLSTM_CHUNKS_PER_STEP * L
    tok = lambda w_: pl.BlockSpec((1, span, w_), lambda i, j: (i, j, 0))
    rows = lambda n: pl.BlockSpec((1, n, span), lambda i, j: (i, 0, j))
    full = lambda a: pl.BlockSpec(a.shape, lambda i, j: (0,) * a.ndim)
    return pl.pallas_call(
        _mlstm_kernel,
        grid=(b, s // span),
        in_specs=[tok(M_QK_W), rows(M_QK_W), tok(M_V_W), tok(M_V_W), tok(LANES), rows(2 * M_HEADS),
                  full(cwq), full(cbq), full(cwk), full(cbk), full(w_mnorm)],
        out_specs=tok(M_V_W),
        out_shape=jax.ShapeDtypeStruct((b, s, M_V_W), BF16),
        scratch_shapes=[pltpu.VMEM((2 * SUBLANES, M_QK_W), BF16),
                        pltpu.VMEM((M_QK_W, L), BF16),
                        pltpu.VMEM((M_HEADS, LANES, 2 * M_DV), F32),
                        pltpu.VMEM((M_HEADS, 1, LANES), F32)],
        compiler_params=_params("arbitrary", "arbitrary"),
        name="mlstm",
    )(q, kt, v, o, ifc, ifr, cwq, cbq, cwk, cbk, w_mnorm)


def _swa_kernel(sink_ref, q_ref, kvp_ref, kvc_ref, bias_ref, out_ref):
    W = WINDOW
    first = pl.program_id(1) == 0
    kv = [kvp_ref[0]] + [kvc_ref[0, W * u:W * (u + 1), :] for u in range(SWA_BLOCKS)]
    lane = lax.broadcasted_iota(jnp.int32, (1, LANES), 1)
    lo_half = lane < A_HEAD_DIM
    hi_half = jnp.logical_not(lo_half)
    keys, v_half = [], []
    for u in range(SWA_BLOCKS):
        keys.append(jnp.concatenate([kv[u][:, :A_KV_W], kv[u + 1][:, :A_KV_W]], axis=0))
        vals = jnp.concatenate([kv[u][:, A_KV_W:], kv[u + 1][:, A_KV_W:]], axis=0)
        zero = jnp.zeros_like(vals)
        v_half.append((jnp.where(lo_half, vals, zero), jnp.where(hi_half, vals, zero)))
    table = [jnp.where(first, 1, 0)] + [0] * (SWA_BLOCKS - 1)

    tiles = range(A_HEADS // 2)
    slots = [(u, j, p) for u in range(SWA_BLOCKS) for j in tiles for p in range(2)]
    head = {ujp: A_HEAD_ORDER[2 * ujp[1] + ujp[2]] for ujp in slots}
    scale = jnp.asarray(A_HEAD_DIM ** -0.5, BF16)
    qt = {(u, j): q_ref[0, W * u:W * (u + 1), LANES * j:LANES * (j + 1)] * scale
          for u in range(SWA_BLOCKS) for j in tiles}
    qm = {(u, j, p): jnp.where(lo_half if p == 0 else hi_half, qt[u, j], jnp.zeros_like(qt[u, j]))
          for u, j, p in slots}
    s = {ujp: _dot(qm[ujp], keys[ujp[0]], _NT) + bias_ref[table[ujp[0]], head[ujp]] for ujp in slots}
    m = {ujp: jnp.maximum(jnp.max(s[ujp], axis=-1, keepdims=True), sink_ref[head[ujp]]) for ujp in slots}
    e = {ujp: jnp.exp(s[ujp] - m[ujp]) for ujp in slots}
    denom = {ujp: jnp.sum(e[ujp], axis=-1, keepdims=True) + jnp.exp(sink_ref[head[ujp]] - m[ujp])
             for ujp in slots}
    pv = {(u, j, p): _dot(e[u, j, p].astype(BF16), v_half[u][p]) for u, j, p in slots}
    for u in range(SWA_BLOCKS):
        for j in tiles:
            out = pv[u, j, 0] / denom[u, j, 0] + pv[u, j, 1] / denom[u, j, 1]
            out_ref[0, W * u:W * (u + 1), LANES * j:LANES * (j + 1)] = out.astype(BF16)


def _swa_call(sinks, qa, kva, bias):
    b, s, _ = qa.shape
    W = WINDOW
    rows = SWA_BLOCKS * W
    return pl.pallas_call(
        _swa_kernel,
        grid=(b, s // rows),
        in_specs=[pl.BlockSpec(memory_space=pltpu.SMEM),
                  pl.BlockSpec((1, rows, A_Q_W), lambda i, j: (i, j, 0)),
                  pl.BlockSpec((1, W, 2 * A_KV_W), lambda i, j: (i, jnp.maximum(SWA_BLOCKS * j - 1, 0), 0)),
                  pl.BlockSpec((1, rows, 2 * A_KV_W), lambda i, j: (i, j, 0)),
                  pl.BlockSpec(bias.shape, lambda i, j: (0,) * bias.ndim)],
        out_specs=pl.BlockSpec((1, rows, A_Q_W), lambda i, j: (i, j, 0)),
        out_shape=jax.ShapeDtypeStruct((b, s, A_Q_W), BF16),
        compiler_params=_params("arbitrary", "arbitrary"),
        name="swa",
    )(sinks, qa, kva, kva, bias)


def _post_kernel(hm_ref, ha_ref, g_ref, x_ref, mod_ref, wbm_ref, wba_ref, wo_ref, wn2_ref,
                 wrh_ref, wrl_ref, rb_ref,
                 x1_ref, pay_ref, cls_ref, rank_ref, cnt_ref, carry):
    tm = x_ref.shape[1]

    @pl.when((pl.program_id(0) == 0) & (pl.program_id(1) == 0))
    def _():
        carry[...] = jnp.zeros(carry.shape, F32)

    g = g_ref[0]
    pm = _dot(hm_ref[0], wbm_ref[...])
    pa = _dot(ha_ref[0], wba_ref[...])
    merged = (_sigmoid(g[:, :D_MODEL]) * pm.astype(BF16)
              + _sigmoid(g[:, D_MODEL:]) * pa.astype(BF16))
    mo = _dot(merged, wo_ref[...])
    x1 = x_ref[0] + mod_ref[0, 2:3, :] * mo
    x1_ref[0] = x1
    h2 = _rms(x1) * (wn2_ref[...] * (1.0 + mod_ref[0, 4:5, :])) + mod_ref[0, 3:4, :]
    pay_ref[0, :, :D_MODEL] = h2

    hh, hl = _split2(h2)
    wrh = wrh_ref[...]
    logits = _dot(wrh, hh, _NT) + _dot(wrh, hl, _NT) + _dot(wrl_ref[...], hh, _NT)
    scores = 1.0 / (1.0 + jnp.exp(-logits))
    sel = scores + rb_ref[...]

    def row(a, r):
        return a[r:r + 1, :]

    grp = []
    for gi in range(N_GROUPS):
        v = [row(sel, gi * EXPERTS_PER_GROUP + k) for k in range(EXPERTS_PER_GROUP)]
        best = v[0] + v[1]
        for a, b in _PAIRS[1:]:
            best = jnp.maximum(best, v[a] + v[b])
        grp.append(best)
    gbest = grp[0]
    gsel = jnp.zeros_like(gbest)
    for gi in range(1, N_GROUPS):
        take = grp[gi] > gbest
        gbest = jnp.where(take, grp[gi], gbest)
        gsel = jnp.where(take, float(gi), gsel)

    sv, gv = [], []
    for k in range(EXPERTS_PER_GROUP):
        s_k = row(sel, k)
        g_k = row(scores, k)
        for gi in range(1, N_GROUPS):
            hit = gsel == float(gi)
            s_k = jnp.where(hit, row(sel, gi * EXPERTS_PER_GROUP + k), s_k)
            g_k = jnp.where(hit, row(scores, gi * EXPERTS_PER_GROUP + k), g_k)
        sv.append(s_k)
        gv.append(g_k)

    def argmax4(vals):
        bv, bi = vals[0], jnp.zeros_like(vals[0])
        for k in range(1, EXPERTS_PER_GROUP):
            take = vals[k] > bv
            bv = jnp.where(take, vals[k], bv)
            bi = jnp.where(take, float(k), bi)
        return bi

    i1 = argmax4(sv)
    i2 = argmax4([jnp.where(i1 == float(k), -jnp.inf, sv[k]) for k in range(EXPERTS_PER_GROUP)])
    w1 = jnp.zeros_like(i1)
    w2 = jnp.zeros_like(i1)
    for k in range(EXPERTS_PER_GROUP):
        w1 = jnp.where(i1 == float(k), gv[k], w1)
        w2 = jnp.where(i2 == float(k), gv[k], w2)
    wsum = w1 + w2
    w1 = w1 / wsum
    w2 = w2 / wsum
    code = i1 * float(EXPERTS_PER_GROUP) + i2
    pair_idx = jnp.zeros_like(code)
    gate_a, gate_b = w1, w2
    for k, (a, b) in enumerate(_CLASS_PAIRS):
        fwd = code == float(a * EXPERTS_PER_GROUP + b)
        rev = code == float(b * EXPERTS_PER_GROUP + a)
        pair_idx = jnp.where(fwd, float(k), jnp.where(rev, float(k), pair_idx))
        gate_a = jnp.where(rev, w2, gate_a)
        gate_b = jnp.where(rev, w1, gate_b)
    cls_f = gsel * float(PAIRS_PER_GROUP) + pair_idx
    cls_ref[0] = cls_f.astype(jnp.int32)

    grow = lax.broadcasted_iota(jnp.int32, (LANES, tm), 0)
    gmat = jnp.where(grow == 0, gate_a, jnp.where(grow == 1, gate_b, 0.0))
    pay_ref[0, :, D_MODEL:] = gmat.T

    crow = lax.broadcasted_iota(jnp.int32, (CLASS_ROWS, tm), 0).astype(F32)
    onehot = crow == cls_f
    r_i = lax.broadcasted_iota(jnp.int32, (tm, tm), 0)
    c_i = lax.broadcasted_iota(jnp.int32, (tm, tm), 1)
    upper = jnp.where(r_i <= c_i, 1.0, 0.0).astype(BF16)
    cum = _dot(jnp.where(onehot, 1.0, 0.0).astype(BF16), upper)
    before = carry[:, 0:1]
    rank = jnp.sum(jnp.where(onehot, cum - 1.0 + before, 0.0), axis=0, keepdims=True)
    rank_ref[0] = rank.astype(jnp.int32)
    total = before + cum[:, tm - 1:tm]
    carry[...] = jnp.broadcast_to(total, carry.shape)
    cnt_ref[...] = jnp.broadcast_to(total, cnt_ref.shape)


def _post_call(hm, ha, g, x, mod, wbm, wba, wo, wn2, wrh, wrl, rb):
    b, s, d = x.shape
    tm = ROUTER_TILE
    tok = lambda w_: pl.BlockSpec((1, tm, w_), lambda i, j: (i, j, 0))
    full = lambda a: pl.BlockSpec(a.shape, lambda i, j: (0,) * a.ndim)
    lanes = pl.BlockSpec((1, 1, tm), lambda i, j: (i, 0, j))
    return pl.pallas_call(
        _post_kernel,
        grid=(b, s // tm),
        in_specs=[tok(M_V_W), tok(A_Q_W), tok(2 * d), tok(d),
                  pl.BlockSpec((1, 6, d), lambda i, j: (i, 0, 0)),
                  full(wbm), full(wba), full(wo), full(wn2), full(wrh), full(wrl), full(rb)],
        out_specs=[tok(d), tok(PAYLOAD_W), lanes, lanes,
                   pl.BlockSpec((CLASS_ROWS, LANES), lambda i, j: (0, 0))],
        out_shape=[jax.ShapeDtypeStruct((b, s, d), F32),
                   jax.ShapeDtypeStruct((b, s, PAYLOAD_W), F32),
                   jax.ShapeDtypeStruct((b, 1, s), jnp.int32),
                   jax.ShapeDtypeStruct((b, 1, s), jnp.int32),
                   jax.ShapeDtypeStruct((CLASS_ROWS, LANES), F32)],
        scratch_shapes=[pltpu.VMEM((CLASS_ROWS, LANES), F32)],
        compiler_params=_params("arbitrary", "arbitrary"),
        name="merge_outproj_router",
    )(hm, ha, g, x, mod, wbm, wba, wo, wn2, wrh, wrl, rb)


def _zero_fill(off_s, cnt_s, nu_s, xs_out, zbuf, zsem, *, wait):
    chunk = zbuf.shape[0]

    def zero_rows(first, n):
        cp = pltpu.make_async_copy(zbuf.at[pl.ds(0, n), :], xs_out.at[pl.ds(first, n), :], zsem)
        cp.wait() if wait else cp.start()

    def per_class(c, carry):
        n = cnt_s[c]
        fill = (-n) & (EXPERT_TILE - 1)
        head = fill & (SUBLANES - 1)
        for k in range(SUBLANES - 1):
            pl.when(k < head)(lambda k=k: zero_rows(off_s[c] + n + k, 1))
        cur = off_s[c] + n + head
        p = chunk
        while p >= SUBLANES:
            pl.when((fill & p) != 0)(lambda cur=cur, p=p: zero_rows(pl.multiple_of(cur, SUBLANES), p))
            cur = cur + (fill & p)
            p //= 2
        return carry

    lax.fori_loop(0, N_CLASSES, per_class, 0)

    def per_chunk(r, carry):
        zero_rows(pl.multiple_of(r * chunk, SUBLANES), chunk)
        return carry

    per_tile = EXPERT_TILE // chunk
    lax.fori_loop(nu_s[0] * per_tile, (xs_out.shape[0] // EXPERT_TILE) * per_tile, per_chunk, 0)


def _dispatch_kernel(pos_s, off_s, cnt_s, nu_s, pay_ref, xs_out, zbuf, sem, zsem):
    @pl.when(pl.program_id(0) == 0)
    def _():
        zbuf[...] = jnp.zeros(zbuf.shape, F32)
        _zero_fill(off_s, cnt_s, nu_s, xs_out, zbuf, zsem, wait=False)
        _zero_fill(off_s, cnt_s, nu_s, xs_out, zbuf, zsem, wait=True)

    first_token = pl.program_id(0) * pay_ref.shape[0] * SUBLANES
    for wait in (False, True):
        _row_copies(pos_s, first_token, xs_out, pay_ref, sem, to_sorted=True, wait=wait)


def _dispatch_call(pos, off, cnt, n_used, payload, npad):
    t8, _, width = payload.shape
    tiles = ROW_TILE // SUBLANES
    return pl.pallas_call(
        _dispatch_kernel,
        grid_spec=pltpu.PrefetchScalarGridSpec(
            num_scalar_prefetch=4,
            grid=(t8 // tiles,),
            in_specs=[pl.BlockSpec((tiles, SUBLANES, width), lambda i, *_: (i, 0, 0))],
            out_specs=pl.BlockSpec(memory_space=pl.ANY),
            scratch_shapes=[pltpu.VMEM((EXPERT_TILE // 2, width), F32),
                            pltpu.SemaphoreType.DMA(()), pltpu.SemaphoreType.DMA(())],
        ),
        out_shape=jax.ShapeDtypeStruct((npad, width), F32),
        compiler_params=_params("arbitrary"),
        name="moe_dispatch",
    )(pos, off, cnt, n_used, payload)


def _moe_kernel(load_s, buf_s, ahead_s, ahead_buf_s, wait_s, nu_s, x_ref, wg_hbm, wu_hbm, wd_hbm, y_ref,
                wg_c, wu_c, wd_c, sem):
    i = pl.program_id(0)

    def copies(slot, expert, buf):
        return [pltpu.make_async_copy(hbm.at[expert], cache.at[slot, buf], sem.at[slot, buf])
                for hbm, cache in ((wg_hbm, wg_c), (wu_hbm, wu_c), (wd_hbm, wd_c))]

    for slot in range(2):
        @pl.when(i == 0)
        def _(slot=slot):
            for cp in copies(slot, load_s[slot, 0], buf_s[slot, 0]):
                cp.start()

        @pl.when(ahead_s[slot, i] >= 0)
        def _(slot=slot):
            for cp in copies(slot, ahead_s[slot, i], ahead_buf_s[slot, i]):
                cp.start()

        @pl.when(wait_s[slot, i] == 1)
        def _(slot=slot):
            for cp in copies(slot, 0, buf_s[slot, i]):
                cp.wait()

    @pl.when(i < nu_s[0])
    def _():
        x = x_ref[:, :D_MODEL].astype(BF16)
        gates = (x_ref[:, D_MODEL:D_MODEL + 1], x_ref[:, D_MODEL + 1:D_MODEL + 2])
        w = [[cache[slot, buf_s[slot, i]].astype(BF16) for cache in (wg_c, wu_c, wd_c)] for slot in range(2)]
        up = [(_dot(x, wg), _dot(x, wu)) for wg, wu, _ in w]
        act = [(hg * _sigmoid(hg) * hu).astype(BF16) for hg, hu in up]
        y_ref[...] = gates[0] * _dot(act[0], w[0][2]) + gates[1] * _dot(act[1], w[1][2])

    @pl.when(i >= nu_s[0])
    def _():
        y_ref[...] = jnp.zeros(y_ref.shape, F32)


def _moe_call(schedule, n_used, xs, wg, wu, wd):
    npad = xs.shape[0]
    te = EXPERT_TILE
    last = lambda i, *s: jnp.minimum(i, s[-1][0] - 1)
    hbm = pl.BlockSpec(memory_space=pl.ANY)
    return pl.pallas_call(
        _moe_kernel,
        grid_spec=pltpu.PrefetchScalarGridSpec(
            num_scalar_prefetch=len(schedule) + 1,
            grid=(npad // te,),
            in_specs=[pl.BlockSpec((te, PAYLOAD_W), lambda i, *s: (last(i, *s), 0)), hbm, hbm, hbm],
            out_specs=pl.BlockSpec((te, D_MODEL), lambda i, *s: (i, 0)),
            scratch_shapes=[pltpu.VMEM((2, 2, D_MODEL, D_EXPERT), F32),
                            pltpu.VMEM((2, 2, D_MODEL, D_EXPERT), F32),
                            pltpu.VMEM((2, 2, D_EXPERT, D_MODEL), F32),
                            pltpu.SemaphoreType.DMA((2, 2))],
        ),
        out_shape=jax.ShapeDtypeStruct((npad, D_MODEL), F32),
        compiler_params=_params("arbitrary"),
        name="moe_experts",
    )(*schedule, n_used, xs, wg, wu, wd)


def _combine_kernel(pos_s, ys_hbm, x1_ref, mod_ref, wf_ref, out_ref, ybuf, sem):
    x2 = _moe_residual(pos_s, ys_hbm, x1_ref, mod_ref, ybuf, sem)
    out_ref[0] = _rms(x2) * wf_ref[...]


def _combine_call(pos, ys, x1, mod, wf):
    b, s, d = x1.shape
    rows = ROW_TILE
    return pl.pallas_call(
        _combine_kernel,
        grid_spec=pltpu.PrefetchScalarGridSpec(
            num_scalar_prefetch=1,
            grid=(b, s // rows),
            in_specs=[pl.BlockSpec(memory_space=pl.ANY),
                      pl.BlockSpec((1, rows, d), lambda i, j, *_: (i, j, 0)),
                      pl.BlockSpec((1, 6, d), lambda i, j, *_: (i, 0, 0)),
                      pl.BlockSpec((1, d), lambda i, j, *_: (0, 0))],
            out_specs=pl.BlockSpec((1, rows, d), lambda i, j, *_: (i, j, 0)),
            scratch_shapes=[pltpu.VMEM((2, rows // SUBLANES, SUBLANES, d), F32),
                            pltpu.SemaphoreType.DMA((2,))],
        ),
        out_shape=jax.ShapeDtypeStruct((b, s, d), F32),
        compiler_params=_params("arbitrary", "arbitrary"),
        name="moe_combine",
    )(pos, ys, x1, mod, wf)


def _t5_bucket(n):
    max_exact = N_BUCKETS // 2
    large = max_exact + (np.log(np.maximum(n, 1) / max_exact)
                         / np.log(MAX_DISTANCE / max_exact)
                         * (N_BUCKETS - max_exact)).astype(np.int32)
    large = np.minimum(large, N_BUCKETS - 1)
    return np.where(n < max_exact, n, large).astype(np.int32)


def _bias_table(rel_bias):
    dist = np.arange(WINDOW)[:, None] + WINDOW - np.arange(2 * WINDOW)[None, :]
    bucket = _t5_bucket(np.maximum(dist, 0)).reshape(1, -1)
    onehot = (jnp.arange(N_BUCKETS, dtype=jnp.int32)[:, None] == jnp.asarray(bucket)).astype(F32)
    tab = jnp.dot(rel_bias.astype(F32).T, onehot, precision=lax.Precision.HIGHEST)
    tab = tab.reshape(A_HEADS, WINDOW, 2 * WINDOW)
    in_window = (dist >= 0) & (dist < WINDOW)
    has_key = np.arange(2 * WINDOW)[None, :] >= WINDOW
    return jnp.stack([jnp.where(jnp.asarray(in_window), tab, NEG_INF),
                      jnp.where(jnp.asarray(in_window & has_key), tab, NEG_INF)])


def _layout_w_in(w_in_l):
    pts = np.cumsum([0, M_QK_W, M_QK_W, M_V_W, M_V_W, M_HEADS, M_HEADS,
                     A_Q_W, A_KV_W, A_KV_W, D_MODEL, D_MODEL])
    wb = w_in_l.astype(BF16)
    col = lambda k: wb[:, pts[k]:pts[k + 1]]
    head = lambda h: wb[:, pts[6] + A_HEAD_DIM * h:pts[6] + A_HEAD_DIM * (h + 1)]
    w_if = jnp.concatenate([col(4), col(5)], axis=1)
    w_if_pad = jnp.pad(w_if, ((0, 0), (0, LANES - 2 * M_HEADS)))
    w = jnp.concatenate([col(0), col(2), col(3)] + [head(h) for h in A_HEAD_ORDER]
                        + [col(7), col(8), col(9), col(10), w_if_pad], axis=1)
    w_t = jnp.concatenate([col(1), w_if], axis=1).T
    return w, jnp.pad(w_t, ((0, -w_t.shape[0] % (2 * SUBLANES)), (0, 0)))


def _tile_tables(counts, n_tiles, layer):
    te = EXPERT_TILE
    tiles = (counts + te - 1) // te
    ends = jnp.cumsum(tiles)
    off = (ends - tiles) * te
    n_used = ends[-1]
    tile_idx = jnp.minimum(jnp.arange(n_tiles, dtype=jnp.int32), n_used - 1)
    tile_cls = jnp.sum((ends[None, :] <= tile_idx[:, None]).astype(jnp.int32), axis=1)
    tile_cls = jnp.minimum(tile_cls, N_CLASSES - 1)
    off_pad = jnp.pad(off, (0, CLASS_ROWS - N_CLASSES)).astype(jnp.int32)

    cls = jnp.arange(N_CLASSES, dtype=jnp.int32)
    nonempty = tiles > 0
    later = (cls[None, :] > cls[:, None]) & nonempty[None, :]
    nxt = jnp.min(jnp.where(later, cls[None, :], N_CLASSES), axis=1)
    earlier = (cls[None, :] < cls[:, None]) & nonempty[None, :]
    prv = jnp.max(jnp.where(earlier, cls[None, :], -1), axis=1)
    nxt_c = jnp.minimum(nxt, N_CLASSES - 1)
    steps = jnp.arange(n_tiles, dtype=jnp.int32)
    first_tile = (steps < n_used) & (steps == (ends - tiles)[tile_cls])
    rows = []
    for experts in (_CLASS_EA, _CLASS_EB):
        e = jnp.asarray(experts) + layer * N_EXPERTS
        new = nonempty & ((prv < 0) | (e != e[jnp.maximum(prv, 0)]))
        buf = (jnp.cumsum(new.astype(jnp.int32)) + 1) % 2
        ahead = jnp.where((nxt < N_CLASSES) & new[nxt_c], e[nxt_c], -1)
        rows.append((e[tile_cls], buf[tile_cls], jnp.where(first_tile, ahead[tile_cls], -1), buf[nxt_c][tile_cls],
                     (first_tile & new[tile_cls]).astype(jnp.int32)))
    schedule = tuple(jnp.stack(pair).astype(jnp.int32) for pair in zip(*rows))
    return off_pad, schedule, n_used.reshape(1).astype(jnp.int32)


def kernel(x, c, w_ada, b_ada, w_norm1, w_in, conv_w, conv_b, b_igate, b_fgate, w_mnorm, sinks, rel_bias,
           w_br_m, w_br_a, w_out, w_norm2, w_router, router_bias, w_gate_e, w_up_e, w_down_e, w_final):
    b, s, d = x.shape
    depth = w_ada.shape[0]
    t = b * s
    n_tiles = t // EXPERT_TILE + N_CLASSES
    npad = n_tiles * EXPERT_TILE

    mod_all = _ada_call(jnp.pad(c, ((0, 8 - b), (0, 0))), w_ada, b_ada)[:, :b]
    bias_tab = _bias_table(rel_bias)
    wrt = w_router.T
    wrh = wrt.astype(BF16)
    wrl = (wrt - wrh.astype(F32)).astype(BF16)
    rb = router_bias.reshape(N_EXPERTS, 1).astype(F32)
    row = lambda v: v.reshape(1, -1).astype(F32)

    wg_all, wu_all, wd_all = [w.reshape((-1,) + w.shape[2:]) for w in (w_gate_e, w_up_e, w_down_e)]

    moe = None
    for l in range(depth):
        mod = mod_all[l].reshape(b, 6, d)
        w_l, wift = _layout_w_in(w_in[l])
        gate_bias = jnp.concatenate([b_igate[l], b_fgate[l]]).astype(F32)
        bcol = jnp.pad(gate_bias, (0, LANES - 2 * M_HEADS)).reshape(1, LANES)
        brow = gate_bias.reshape(2 * M_HEADS, 1)

        proj_args = (mod, row(w_norm1[l]), w_l, wift, bcol, brow)
        if moe is None:
            q, kt, v, o, qa, kva, g, ifc, ifr = _inproj_call(x, *proj_args)
        else:
            x, q, kt, v, o, qa, kva, g, ifc, ifr = _combine_inproj_call(*moe, *proj_args)
        cw, cb = conv_w[l].astype(F32), conv_b[l].astype(F32)
        cwk = jnp.broadcast_to(cw[:, M_QK_W:, None], (M_CONV, M_QK_W, MLSTM_CHUNK))
        cbk = jnp.broadcast_to(cb[M_QK_W:, None], (M_QK_W, MLSTM_CHUNK))
        hm = _mlstm_call(q, kt, v, o, ifc, ifr, cw[:, :M_QK_W], row(cb[:M_QK_W]), cwk, cbk, row(w_mnorm[l]))
        ha = _swa_call(sinks[l].astype(F32), qa, kva, bias_tab)

        wba = jnp.concatenate([w_br_a[l][A_HEAD_DIM * h:A_HEAD_DIM * (h + 1)] for h in A_HEAD_ORDER])
        x1, payload, cls, rank, cnt = _post_call(
            hm, ha, g, x, mod, w_br_m[l].astype(BF16), wba.astype(BF16), w_out[l].astype(BF16),
            row(w_norm2[l]), wrh, wrl, rb)

        counts = cnt[:, 0].astype(jnp.int32)
        off, schedule, n_used = _tile_tables(counts[:N_CLASSES], n_tiles, l)
        hit = cls.reshape(t, 1) == jnp.arange(CLASS_ROWS, dtype=jnp.int32)[None, :]
        pos = jnp.sum(jnp.where(hit, off[None, :], 0), axis=1) + rank.reshape(t)
        xs = _dispatch_call(pos, off, counts, n_used,
                            payload.reshape(t // SUBLANES, SUBLANES, PAYLOAD_W), npad)
        ys = _moe_call(schedule, n_used, xs, wg_all, wu_all, wd_all)
        moe = (pos, ys, x1, mod)
    return _combine_call(*moe, row(w_final))
```

```python
import numpy as np
import jax
import jax.numpy as jnp
from jax import lax
from jax.experimental import pallas as pl
from jax.experimental.pallas import tpu as pltpu

F32 = jnp.float32
BF16 = jnp.bfloat16

D_MODEL = 1024
M_HEADS = 4
M_DQK = 64
M_DV = 128
M_CONV = 4
M_QK_W = M_HEADS * M_DQK
M_V_W = M_HEADS * M_DV
A_HEADS = 8
A_KV_HEADS = 2
A_HEAD_DIM = 64
A_Q_W = A_HEADS * A_HEAD_DIM
A_KV_W = A_KV_HEADS * A_HEAD_DIM
WINDOW = 128
N_BUCKETS = 32
MAX_DISTANCE = 128
N_EXPERTS = 16
N_GROUPS = 4
EXPERTS_PER_GROUP = N_EXPERTS // N_GROUPS
D_EXPERT = 512
EPS = 1e-6
NEG_INF = -1e30

LANES = 128
SUBLANES = 8
MLSTM_CHUNK = 128
MLSTM_CHUNKS_PER_STEP = 8
SWA_BLOCKS = 8
TOKEN_TILE = 512
ROUTER_TILE = 1024
EXPERT_TILE = 256
ROW_TILE = 1024
PAIRS_PER_GROUP = 6
N_CLASSES = N_GROUPS * PAIRS_PER_GROUP
CLASS_ROWS = 32
PAYLOAD_W = D_MODEL + LANES
VMEM_LIMIT = 48 * 1024 * 1024

C_Q = 0
C_V = C_Q + M_QK_W
C_O = C_V + M_V_W
C_QA = C_O + M_V_W
C_KVA = C_QA + A_Q_W
C_G = C_KVA + 2 * A_KV_W
C_IF = C_G + 2 * D_MODEL
C_END = C_IF + LANES

A_HEAD_ORDER = (0, 4, 1, 5, 2, 6, 3, 7)

_PAIRS = [(a, b) for a in range(EXPERTS_PER_GROUP) for b in range(a + 1, EXPERTS_PER_GROUP)]
_CLASS_PAIRS = ((0, 1), (2, 1), (2, 0), (3, 0), (3, 1), (3, 2))
assert sorted(tuple(sorted(p)) for p in _CLASS_PAIRS) == _PAIRS
_CLASS_EA = np.array([g * EXPERTS_PER_GROUP + a for g in range(N_GROUPS) for a, _ in _CLASS_PAIRS], np.int32)
_CLASS_EB = np.array([g * EXPERTS_PER_GROUP + b for g in range(N_GROUPS) for _, b in _CLASS_PAIRS], np.int32)

_NT = (((1,), (1,)), ((), ()))


def _sigmoid(x):
    return 0.5 * jnp.tanh(0.5 * x) + 0.5


def _log_sigmoid(x):
    return -(jnp.maximum(-x, 0.0) + jnp.log1p(jnp.exp(-jnp.abs(x))))


def _dot(a, b, dims=None):
    if dims is None:
        return jnp.dot(a, b, preferred_element_type=F32)
    return lax.dot_general(a, b, dims, preferred_element_type=F32)


def _split2(a):
    hi = a.astype(BF16)
    lo = (a - hi.astype(F32)).astype(BF16)
    return hi, lo


def _split3(a):
    hi = a.astype(BF16)
    r = a - hi.astype(F32)
    mid = r.astype(BF16)
    lo = (r - mid.astype(F32)).astype(BF16)
    return hi, mid, lo


def _dot_hi(a, b, dims=None):
    ah, al = _split2(a)
    bh, bl = _split2(b)
    return _dot(ah, bh, dims) + _dot(ah, bl, dims) + _dot(al, bh, dims)


def _rms(x):
    return x * lax.rsqrt(jnp.mean(x * x, axis=-1, keepdims=True) + EPS)


def _params(*sem):
    return pltpu.CompilerParams(dimension_semantics=sem, vmem_limit_bytes=VMEM_LIMIT)


def _ada_kernel(c_ref, w_ref, b_ref, o_ref):
    c = c_ref[...]
    cond = c * _sigmoid(c)
    o_ref[0] = _dot_hi(cond, w_ref[0]) + b_ref[0]


def _ada_call(c8, w_ada, b_ada):
    depth = w_ada.shape[0]
    return pl.pallas_call(
        _ada_kernel,
        grid=(depth, 6),
        in_specs=[
            pl.BlockSpec((8, D_MODEL), lambda l, j: (0, 0)),
            pl.BlockSpec((1, D_MODEL, D_MODEL), lambda l, j: (l, 0, j)),
            pl.BlockSpec((1, 1, D_MODEL), lambda l, j: (l, 0, j)),
        ],
        out_specs=pl.BlockSpec((1, 8, D_MODEL), lambda l, j: (l, 0, j)),
        out_shape=jax.ShapeDtypeStruct((depth, 8, 6 * D_MODEL), F32),
        compiler_params=_params("arbitrary", "arbitrary"),
        name="adaln_mod",
    )(c8, w_ada, b_ada.reshape(depth, 1, 6 * D_MODEL))


def _project(x, mod_ref, wn_ref, w_ref, wift_ref, bcol_ref, brow_ref, outs):
    q_ref, kt_ref, v_ref, o_ref, qa_ref, kva_ref, g_ref, ifc_ref, ifr_ref = outs
    h = _rms(x) * (wn_ref[...] * (1.0 + mod_ref[0, 1:2, :])) + mod_ref[0, 0:1, :]
    hb = h.astype(BF16)

    def seg(a, b):
        return _dot(hb, w_ref[:, a:b])

    q_ref[0] = seg(C_Q, C_V).astype(BF16)
    v_ref[0] = seg(C_V, C_O).astype(BF16)
    o_ref[0] = seg(C_O, C_QA).astype(BF16)
    qa_ref[0] = seg(C_QA, C_KVA).astype(BF16)
    kva_ref[0] = seg(C_KVA, C_G).astype(BF16)
    g_ref[0] = seg(C_G, C_IF).astype(BF16)
    zc = seg(C_IF, C_END) + bcol_ref[...]
    lane = lax.broadcasted_iota(jnp.int32, zc.shape, 1)
    ifc_ref[0] = jnp.where((lane >= M_HEADS) & (lane < 2 * M_HEADS), _log_sigmoid(zc), zc)
    zt = _dot(wift_ref[...], hb, _NT)
    kt_ref[0] = zt[:M_QK_W].astype(BF16)
    zr = zt[M_QK_W:M_QK_W + 2 * M_HEADS] + brow_ref[...]
    row = lax.broadcasted_iota(jnp.int32, zr.shape, 0)
    ifr_ref[0] = jnp.where(row >= M_HEADS, _log_sigmoid(zr), zr)


def _inproj_kernel(x_ref, mod_ref, wn_ref, w_ref, wift_ref, bcol_ref, brow_ref, *outs):
    _project(x_ref[0], mod_ref, wn_ref, w_ref, wift_ref, bcol_ref, brow_ref, outs)


def _row_copies(pos_s, first_token, sorted_hbm, tiles, sem, *, to_sorted, wait):
    def copy(i, k, pos):
        pair = (tiles.at[i, pl.ds(k, 1), :], sorted_hbm.at[pl.ds(pos, 1), :])
        return pltpu.make_async_copy(*(pair if to_sorted else pair[::-1]), sem)

    if wait:
        def body(i, carry):
            for k in range(SUBLANES):
                copy(0, k, 0).wait()
            return carry

        lax.fori_loop(0, tiles.shape[0], body, 0, unroll=2)
        return
    for i in range(tiles.shape[0]):
        for k in range(SUBLANES):
            copy(i, k, pos_s[first_token + i * SUBLANES + k]).start(priority=k % 2)


def _moe_residual(pos_s, ys_hbm, x1_ref, mod_ref, ybuf, sem):
    rows = x1_ref.shape[1]
    step = pl.program_id(0) * pl.num_programs(1) + pl.program_id(1)
    n_steps = pl.num_programs(0) * pl.num_programs(1)
    slot = lax.rem(step, 2)

    def fetch(first_token, slot_, wait):
        _row_copies(pos_s, first_token, ys_hbm, ybuf.at[slot_], sem.at[slot_], to_sorted=False, wait=wait)

    @pl.when(step == 0)
    def _():
        fetch(0, 0, False)

    @pl.when(step + 1 < n_steps)
    def _():
        fetch((step + 1) * rows, 1 - slot, False)

    fetch(0, slot, True)
    y = ybuf[slot].reshape(rows, ybuf.shape[-1])
    return x1_ref[0] + mod_ref[0, 5:6, :] * y


def _combine_inproj_kernel(pos_s, ys_hbm, x1_ref, modp_ref, mod_ref, wn_ref, w_ref,
                           wift_ref, bcol_ref, brow_ref, xo_ref, *rest):
    outs, (ybuf, sem) = rest[:-2], rest[-2:]
    x = _moe_residual(pos_s, ys_hbm, x1_ref, modp_ref, ybuf, sem)
    xo_ref[0] = x
    _project(x, mod_ref, wn_ref, w_ref, wift_ref, bcol_ref, brow_ref, outs)


def _inproj_specs(b, s, tm):
    tok = lambda w_: pl.BlockSpec((1, tm, w_), lambda i, j, *_: (i, j, 0))
    sds = lambda w_, dt: jax.ShapeDtypeStruct((b, s, w_), dt)
    rows = lambda n: pl.BlockSpec((1, n, tm), lambda i, j, *_: (i, 0, j))
    out_specs = [tok(M_QK_W), rows(M_QK_W), tok(M_V_W), tok(M_V_W), tok(A_Q_W), tok(2 * A_KV_W),
                 tok(2 * D_MODEL), tok(LANES), rows(2 * M_HEADS)]
    out_shape = [sds(M_QK_W, BF16), jax.ShapeDtypeStruct((b, M_QK_W, s), BF16), sds(M_V_W, BF16),
                 sds(M_V_W, BF16), sds(A_Q_W, BF16), sds(2 * A_KV_W, BF16), sds(2 * D_MODEL, BF16),
                 sds(LANES, F32), jax.ShapeDtypeStruct((b, 2 * M_HEADS, s), F32)]
    return tok, out_specs, out_shape


def _inproj_call(x, mod, wn, w, wift, bcol, brow):
    b, s, d = x.shape
    tm = TOKEN_TILE
    tok, out_specs, out_shape = _inproj_specs(b, s, tm)
    full = lambda a: pl.BlockSpec(a.shape, lambda i, j: (0,) * a.ndim)
    return pl.pallas_call(
        _inproj_kernel,
        grid=(b, s // tm),
        in_specs=[tok(d), pl.BlockSpec((1, 6, d), lambda i, j: (i, 0, 0)), full(wn), full(w),
                  full(wift), full(bcol), full(brow)],
        out_specs=out_specs,
        out_shape=out_shape,
        compiler_params=_params("arbitrary", "arbitrary"),
        name="norm1_inproj",
    )(x, mod, wn, w, wift, bcol, brow)


def _combine_inproj_call(pos, ys, x1, mod_prev, mod, wn, w, wift, bcol, brow):
    b, s, d = x1.shape
    tm = TOKEN_TILE
    tok, out_specs, out_shape = _inproj_specs(b, s, tm)
    full = lambda a: pl.BlockSpec(a.shape, lambda i, j, *_: (0,) * a.ndim)
    modspec = pl.BlockSpec((1, 6, d), lambda i, j, *_: (i, 0, 0))
    return pl.pallas_call(
        _combine_inproj_kernel,
        grid_spec=pltpu.PrefetchScalarGridSpec(
            num_scalar_prefetch=1,
            grid=(b, s // tm),
            in_specs=[pl.BlockSpec(memory_space=pl.ANY), tok(d), modspec, modspec, full(wn), full(w),
                      full(wift), full(bcol), full(brow)],
            out_specs=[tok(d)] + out_specs,
            scratch_shapes=[pltpu.VMEM((2, tm // SUBLANES, SUBLANES, d), F32),
                            pltpu.SemaphoreType.DMA((2,))],
        ),
        out_shape=[jax.ShapeDtypeStruct((b, s, d), F32)] + out_shape,
        compiler_params=_params("arbitrary", "arbitrary"),
        name="combine_norm1_inproj",
    )(pos, ys, x1, mod_prev, mod, wn, w, wift, bcol, brow)


def _mlstm_kernel(q_ref, kt_ref, v_ref, o_ref, ifc_ref, ifr_ref, cwq_ref, cbq_ref, cwk_ref, cbk_ref, wn_ref,
                  out_ref, qbuf, kbuf, cstate, mstate):
    @pl.when(pl.program_id(1) == 0)
    def _():
        qbuf[...] = jnp.zeros(qbuf.shape, qbuf.dtype)
        kbuf[...] = jnp.zeros(kbuf.shape, kbuf.dtype)
        cstate[...] = jnp.zeros(cstate.shape, F32)
        mstate[...] = jnp.zeros(mstate.shape, F32)

    heads = range(M_HEADS)
    carry = (qbuf[...], kbuf[...], [cstate[h] for h in heads], [mstate[h][:, 0:1] for h in heads])
    for c in range(q_ref.shape[1] // MLSTM_CHUNK):
        carry = _mlstm_chunk(c, carry, q_ref, kt_ref, v_ref, o_ref, ifc_ref, ifr_ref, cwq_ref, cbq_ref,
                             cwk_ref, cbk_ref, wn_ref, out_ref)
    qbuf[...], kbuf[...] = carry[0], carry[1]
    for h in heads:
        cstate[h] = carry[2][h]
        mstate[h] = jnp.broadcast_to(carry[3][h], (1, LANES))


def _mlstm_chunk(c, carry, q_ref, kt_ref, v_ref, o_ref, ifc_ref, ifr_ref, cwq_ref, cbq_ref, cwk_ref, cbk_ref,
                 wn_ref, out_ref):
    L = MLSTM_CHUNK
    span = slice(L * c, L * (c + 1))
    q_tail, k_prev, cst, m_prev = carry
    pad = q_tail.shape[0]
    heads = range(M_HEADS)

    xq = q_ref[0, span, :]
    q_all = jnp.concatenate([q_tail, xq], axis=0)
    xk = kt_ref[0, :, span]
    k_all = jnp.concatenate([k_prev, xk], axis=1)
    r_s = lax.broadcasted_iota(jnp.int32, (L, L + pad), 0)
    c_s = lax.broadcasted_iota(jnp.int32, (L, L + pad), 1)
    r_t = lax.broadcasted_iota(jnp.int32, (2 * L, L), 0)
    c_t = lax.broadcasted_iota(jnp.int32, (2 * L, L), 1)
    acc_q = cbq_ref[...] + cwq_ref[M_CONV - 1:M_CONV, :] * xq.astype(F32)
    acc_k = cbk_ref[...] + cwk_ref[M_CONV - 1] * xk.astype(F32)
    for j in range(M_CONV - 1):
        back = M_CONV - 1 - j
        shift_q = jnp.where(c_s == r_s + (pad - back), 1.0, 0.0).astype(BF16)
        acc_q = acc_q + cwq_ref[j:j + 1, :] * _dot(shift_q, q_all)
        shift_k = jnp.where(r_t == c_t + (L - back), 1.0, 0.0).astype(BF16)
        acc_k = acc_k + cwk_ref[j] * _dot(k_all, shift_k)
    q = acc_q * _sigmoid(acc_q) * (M_DQK ** -0.5)
    kf = acc_k * _sigmoid(acc_k)
    kb = kf.astype(BF16)
    vb = v_ref[0, span, :]

    r_i = lax.broadcasted_iota(jnp.int32, (L, L), 0)
    c_i = lax.broadcasted_iota(jnp.int32, (L, L), 1)
    causal = c_i <= r_i
    tril = jnp.where(causal, 1.0, 0.0).astype(BF16)
    triu = jnp.where(r_i <= c_i, 1.0, 0.0).astype(BF16)

    icol = ifc_ref[0, span, :]
    irow = ifr_ref[0, :, span]
    ch, cm, cl = _split3(icol)
    bcum_col = _dot(tril, ch) + _dot(tril, cm) + _dot(tril, cl)
    rh, rm, rl = _split3(irow)
    bcum_row = _dot(rh, triu) + _dot(rm, triu) + _dot(rl, triu)

    lane = lax.broadcasted_iota(jnp.int32, (1, LANES), 1)
    lo_half = lane < M_DQK
    hi_half = jnp.logical_not(lo_half)
    ones = jnp.ones((L, LANES), BF16)
    pair = lambda h: slice(LANES * (h // 2), LANES * (h // 2 + 1))

    qmask =[jnp.where(lo_half if h % 2 == 0 else hi_half, q[:, pair(h)], 0.0) for h in heads]
    vaug = [jnp.concatenate([vb[:, M_DV * h:M_DV * (h + 1)], ones], axis=1) for h in heads]
    bc_col = [bcum_col[:, M_HEADS + h:M_HEADS + h + 1] for h in heads]
    bc_row = [bcum_row[M_HEADS + h:M_HEADS + h + 1, :] for h in heads]
    i_row = [irow[h:h + 1, :] for h in heads]
    b_last = [bc_row[h][:, L - 1:L] for h in heads]

    s = [_dot(qmask[h].astype(BF16), kb[pair(h), :]) for h in heads]

    dlog = [jnp.where(causal, bc_col[h] - bc_row[h] + i_row[h], NEG_INF) for h in heads]
    m_inter = [bc_col[h] + m_prev[h] for h in heads]
    m_row = [jnp.maximum(m_inter[h], jnp.max(dlog[h], axis=-1, keepdims=True)) for h in heads]
    sc = [(s[h] * jnp.exp(dlog[h] - m_row[h])).astype(BF16) for h in heads]
    qi = [(jnp.exp(m_inter[h] - m_row[h]) * qmask[h]).astype(BF16) for h in heads]

    num = [_dot(sc[h], vaug[h]) + _dot(qi[h], cst[h].astype(BF16)) for h in heads]
    for h in heads:
        den = num[h][:, M_DV:]
        hh = num[h][:, :M_DV] / jnp.maximum(jnp.abs(den), jnp.exp(-m_row[h]))
        hn = _rms(hh) * wn_ref[:, M_DV * h:M_DV * (h + 1)]
        og = _sigmoid(o_ref[0, span, M_DV * h:M_DV * (h + 1)].astype(F32))
        out_ref[0, span, M_DV * h:M_DV * (h + 1)] = (hn * og).astype(BF16)

    w_state = [b_last[h] - bc_row[h] + i_row[h] for h in heads]
    m_loc = [jnp.max(w_state[h], axis=-1, keepdims=True) for h in heads]
    ak = [(jnp.exp(w_state[h] - m_loc[h]) * kf[pair(h), :]).astype(BF16) for h in heads]
    c_loc = [_dot(ak[h], vaug[h]) for h in heads]
    m_new = [jnp.maximum(b_last[h] + m_prev[h], m_loc[h]) for h in heads]
    c_new = [jnp.exp(b_last[h] + m_prev[h] - m_new[h]) * cst[h] + jnp.exp(m_loc[h] - m_new[h]) * c_loc[h]
             for h in heads]
    return xq[L - pad:L, :], xk, c_new, m_new


def _mlstm_call(q, kt, v, o, ifc, ifr, cwq, cbq, cwk, cbk, w_mnorm):
    b, s, _ = q.shape
    L = MLSTM_CHUNK
    span = MLSTM_CHUNKS_PER_STEP * L
    tok = lambda w_: pl.BlockSpec((1, span, w_), lambda i, j: (i, j, 0))
    rows = lambda n: pl.BlockSpec((1, n, span), lambda i, j: (i, 0, j))
    full = lambda a: pl.BlockSpec(a.shape, lambda i, j: (0,) * a.ndim)
    return pl.pallas_call(
        _mlstm_kernel,
        grid=(b, s // span),
        in_specs=[tok(M_QK_W), rows(M_QK_W), tok(M_V_W), tok(M_V_W), tok(LANES), rows(2 * M_HEADS),
                  full(cwq), full(cbq), full(cwk), full(cbk), full(w_mnorm)],
        out_specs=tok(M_V_W),
        out_shape=jax.ShapeDtypeStruct((b, s, M_V_W), BF16),
        scratch_shapes=[pltpu.VMEM((2 * SUBLANES, M_QK_W), BF16),
                        pltpu.VMEM((M_QK_W, L), BF16),
                        pltpu.VMEM((M_HEADS, LANES, 2 * M_DV), F32),
                        pltpu.VMEM((M_HEADS, 1, LANES), F32)],
        compiler_params=_params("arbitrary", "arbitrary"),
        name="mlstm",
    )(q, kt, v, o, ifc, ifr, cwq, cbq, cwk, cbk, w_mnorm)


def _swa_kernel(sink_ref, q_ref, kvp_ref, kvc_ref, bias_ref, out_ref):
    W = WINDOW
    first = pl.program_id(1) == 0
    kv = [kvp_ref[0]] + [kvc_ref[0, W * u:W * (u + 1), :] for u in range(SWA_BLOCKS)]
    lane = lax.broadcasted_iota(jnp.int32, (1, LANES), 1)
    lo_half = lane < A_HEAD_DIM
    hi_half = jnp.logical_not(lo_half)
    keys, v_half = [], []
    for u in range(SWA_BLOCKS):
        keys.append(jnp.concatenate([kv[u][:, :A_KV_W], kv[u + 1][:, :A_KV_W]], axis=0))
        vals = jnp.concatenate([kv[u][:, A_KV_W:], kv[u + 1][:, A_KV_W:]], axis=0)
        zero = jnp.zeros_like(vals)
        v_half.append((jnp.where(lo_half, vals, zero), jnp.where(hi_half, vals, zero)))
    table = [jnp.where(first, 1, 0)] + [0] * (SWA_BLOCKS - 1)

    tiles = range(A_HEADS // 2)
    slots = [(u, j, p) for u in range(SWA_BLOCKS) for j in tiles for p in range(2)]
    head = {ujp: A_HEAD_ORDER[2 * ujp[1] + ujp[2]] for ujp in slots}
    scale = jnp.asarray(A_HEAD_DIM ** -0.5, BF16)
    qt = {(u, j): q_ref[0, W * u:W * (u + 1), LANES * j:LANES * (j + 1)] * scale
          for u in range(SWA_BLOCKS) for j in tiles}
    qm = {(u, j, p): jnp.where(lo_half if p == 0 else hi_half, qt[u, j], jnp.zeros_like(qt[u, j]))
          for u, j, p in slots}
    s = {ujp: _dot(qm[ujp], keys[ujp[0]], _NT) + bias_ref[table[ujp[0]], head[ujp]] for ujp in slots}
    m = {ujp: jnp.maximum(jnp.max(s[ujp], axis=-1, keepdims=True), sink_ref[head[ujp]]) for ujp in slots}
    e = {ujp: jnp.exp(s[ujp] - m[ujp]) for ujp in slots}
    denom = {ujp: jnp.sum(e[ujp], axis=-1, keepdims=True) + jnp.exp(sink_ref[head[ujp]] - m[ujp])
             for ujp in slots}
    pv = {(u, j, p): _dot(e[u, j, p].astype(BF16), v_half[u][p]) for u, j, p in slots}
    for u in range(SWA_BLOCKS):
        for j in tiles:
            out = pv[u, j, 0] / denom[u, j, 0] + pv[u, j, 1] / denom[u, j, 1]
            out_ref[0, W * u:W * (u + 1), LANES * j:LANES * (j + 1)] = out.astype(BF16)


def _swa_call(sinks, qa, kva, bias):
    b, s, _ = qa.shape
    W = WINDOW
    rows = SWA_BLOCKS * W
    return pl.pallas_call(
        _swa_kernel,
        grid=(b, s // rows),
        in_specs=[pl.BlockSpec(memory_space=pltpu.SMEM),
                  pl.BlockSpec((1, rows, A_Q_W), lambda i, j: (i, j, 0)),
                  pl.BlockSpec((1, W, 2 * A_KV_W), lambda i, j: (i, jnp.maximum(SWA_BLOCKS * j - 1, 0), 0)),
                  pl.BlockSpec((1, rows, 2 * A_KV_W), lambda i, j: (i, j, 0)),
                  pl.BlockSpec(bias.shape, lambda i, j: (0,) * bias.ndim)],
        out_specs=pl.BlockSpec((1, rows, A_Q_W), lambda i, j: (i, j, 0)),
        out_shape=jax.ShapeDtypeStruct((b, s, A_Q_W), BF16),
        compiler_params=_params("arbitrary", "arbitrary"),
        name="swa",
    )(sinks, qa, kva, kva, bias)


def _post_kernel(hm_ref, ha_ref, g_ref, x_ref, mod_ref, wbm_ref, wba_ref, wo_ref, wn2_ref,
                 wrh_ref, wrl_ref, rb_ref,
                 x1_ref, pay_ref, cls_ref, rank_ref, cnt_ref, carry):
    tm = x_ref.shape[1]

    @pl.when((pl.program_id(0) == 0) & (pl.program_id(1) == 0))
    def _():
        carry[...] = jnp.zeros(carry.shape, F32)

    g = g_ref[0]
    pm = _dot(hm_ref[0], wbm_ref[...])
    pa = _dot(ha_ref[0], wba_ref[...])
    merged = (_sigmoid(g[:, :D_MODEL]) * pm.astype(BF16)
              + _sigmoid(g[:, D_MODEL:]) * pa.astype(BF16))
    mo = _dot(merged, wo_ref[...])
    x1 = x_ref[0] + mod_ref[0, 2:3, :] * mo
    x1_ref[0] = x1
    h2 = _rms(x1) * (wn2_ref[...] * (1.0 + mod_ref[0, 4:5, :])) + mod_ref[0, 3:4, :]
    pay_ref[0, :, :D_MODEL] = h2

    hh, hl = _split2(h2)
    wrh = wrh_ref[...]
    logits = _dot(wrh, hh, _NT) + _dot(wrh, hl, _NT) + _dot(wrl_ref[...], hh, _NT)
    scores = 1.0 / (1.0 + jnp.exp(-logits))
    sel = scores + rb_ref[...]

    def row(a, r):
        return a[r:r + 1, :]

    grp = []
    for gi in range(N_GROUPS):
        v = [row(sel, gi * EXPERTS_PER_GROUP + k) for k in range(EXPERTS_PER_GROUP)]
        best = v[0] + v[1]
        for a, b in _PAIRS[1:]:
            best = jnp.maximum(best, v[a] + v[b])
        grp.append(best)
    gbest = grp[0]
    gsel = jnp.zeros_like(gbest)
    for gi in range(1, N_GROUPS):
        take = grp[gi] > gbest
        gbest = jnp.where(take, grp[gi], gbest)
        gsel = jnp.where(take, float(gi), gsel)

    sv, gv = [], []
    for k in range(EXPERTS_PER_GROUP):
        s_k = row(sel, k)
        g_k = row(scores, k)
        for gi in range(1, N_GROUPS):
            hit = gsel == float(gi)
            s_k = jnp.where(hit, row(sel, gi * EXPERTS_PER_GROUP + k), s_k)
            g_k = jnp.where(hit, row(scores, gi * EXPERTS_PER_GROUP + k), g_k)
        sv.append(s_k)
        gv.append(g_k)

    def argmax4(vals):
        bv, bi = vals[0], jnp.zeros_like(vals[0])
        for k in range(1, EXPERTS_PER_GROUP):
            take = vals[k] > bv
            bv = jnp.where(take, vals[k], bv)
            bi = jnp.where(take, float(k), bi)
        return bi

    i1 = argmax4(sv)
    i2 = argmax4([jnp.where(i1 == float(k), -jnp.inf, sv[k]) for k in range(EXPERTS_PER_GROUP)])
    w1 = jnp.zeros_like(i1)
    w2 = jnp.zeros_like(i1)
    for k in range(EXPERTS_PER_GROUP):
        w1 = jnp.where(i1 == float(k), gv[k], w1)
        w2 = jnp.where(i2 == float(k), gv[k], w2)
    wsum = w1 + w2
    w1 = w1 / wsum
    w2 = w2 / wsum
    code = i1 * float(EXPERTS_PER_GROUP) + i2
    pair_idx = jnp.zeros_like(code)
    gate_a, gate_b = w1, w2
    for k, (a, b) in enumerate(_CLASS_PAIRS):
        fwd = code == float(a * EXPERTS_PER_GROUP + b)
        rev = code == float(b * EXPERTS_PER_GROUP + a)
        pair_idx = jnp.where(fwd, float(k), jnp.where(rev, float(k), pair_idx))
        gate_a = jnp.where(rev, w2, gate_a)
        gate_b = jnp.where(rev, w1, gate_b)
    cls_f = gsel * float(PAIRS_PER_GROUP) + pair_idx
    cls_ref[0] = cls_f.astype(jnp.int32)

    grow = lax.broadcasted_iota(jnp.int32, (LANES, tm), 0)
    gmat = jnp.where(grow == 0, gate_a, jnp.where(grow == 1, gate_b, 0.0))
    pay_ref[0, :, D_MODEL:] = gmat.T

    crow = lax.broadcasted_iota(jnp.int32, (CLASS_ROWS, tm), 0).astype(F32)
    onehot = crow == cls_f
    r_i = lax.broadcasted_iota(jnp.int32, (tm, tm), 0)
    c_i = lax.broadcasted_iota(jnp.int32, (tm, tm), 1)
    upper = jnp.where(r_i <= c_i, 1.0, 0.0).astype(BF16)
    cum = _dot(jnp.where(onehot, 1.0, 0.0).astype(BF16), upper)
    before = carry[:, 0:1]
    rank = jnp.sum(jnp.where(onehot, cum - 1.0 + before, 0.0), axis=0, keepdims=True)
    rank_ref[0] = rank.astype(jnp.int32)
    total = before + cum[:, tm - 1:tm]
    carry[...] = jnp.broadcast_to(total, carry.shape)
    cnt_ref[...] = jnp.broadcast_to(total, cnt_ref.shape)


def _post_call(hm, ha, g, x, mod, wbm, wba, wo, wn2, wrh, wrl, rb):
    b, s, d = x.shape
    tm = ROUTER_TILE
    tok = lambda w_: pl.BlockSpec((1, tm, w_), lambda i, j: (i, j, 0))
    full = lambda a: pl.BlockSpec(a.shape, lambda i, j: (0,) * a.ndim)
    lanes = pl.BlockSpec((1, 1, tm), lambda i, j: (i, 0, j))
    return pl.pallas_call(
        _post_kernel,
        grid=(b, s // tm),
        in_specs=[tok(M_V_W), tok(A_Q_W), tok(2 * d), tok(d),
                  pl.BlockSpec((1, 6, d), lambda i, j: (i, 0, 0)),
                  full(wbm), full(wba), full(wo), full(wn2), full(wrh), full(wrl), full(rb)],
        out_specs=[tok(d), tok(PAYLOAD_W), lanes, lanes,
                   pl.BlockSpec((CLASS_ROWS, LANES), lambda i, j: (0, 0))],
        out_shape=[jax.ShapeDtypeStruct((b, s, d), F32),
                   jax.ShapeDtypeStruct((b, s, PAYLOAD_W), F32),
                   jax.ShapeDtypeStruct((b, 1, s), jnp.int32),
                   jax.ShapeDtypeStruct((b, 1, s), jnp.int32),
                   jax.ShapeDtypeStruct((CLASS_ROWS, LANES), F32)],
        scratch_shapes=[pltpu.VMEM((CLASS_ROWS, LANES), F32)],
        compiler_params=_params("arbitrary", "arbitrary"),
        name="merge_outproj_router",
    )(hm, ha, g, x, mod, wbm, wba, wo, wn2, wrh, wrl, rb)


def _zero_fill(off_s, cnt_s, nu_s, xs_out, zbuf, zsem, *, wait):
    chunk = zbuf.shape[0]

    def zero_rows(first, n):
        cp = pltpu.make_async_copy(zbuf.at[pl.ds(0, n), :], xs_out.at[pl.ds(first, n), :], zsem)
        cp.wait() if wait else cp.start()

    def per_class(c, carry):
        n = cnt_s[c]
        fill = (-n) & (EXPERT_TILE - 1)
        head = fill & (SUBLANES - 1)
        for k in range(SUBLANES - 1):
            pl.when(k < head)(lambda k=k: zero_rows(off_s[c] + n + k, 1))
        cur = off_s[c] + n + head
        p = chunk
        while p >= SUBLANES:
            pl.when((fill & p) != 0)(lambda cur=cur, p=p: zero_rows(pl.multiple_of(cur, SUBLANES), p))
            cur = cur + (fill & p)
            p //= 2
        return carry

    lax.fori_loop(0, N_CLASSES, per_class, 0)

    def per_chunk(r, carry):
        zero_rows(pl.multiple_of(r * chunk, SUBLANES), chunk)
        return carry

    per_tile = EXPERT_TILE // chunk
    lax.fori_loop(nu_s[0] * per_tile, (xs_out.shape[0] // EXPERT_TILE) * per_tile, per_chunk, 0)


def _dispatch_kernel(pos_s, off_s, cnt_s, nu_s, pay_ref, xs_out, zbuf, sem, zsem):
    @pl.when(pl.program_id(0) == 0)
    def _():
        zbuf[...] = jnp.zeros(zbuf.shape, F32)
        _zero_fill(off_s, cnt_s, nu_s, xs_out, zbuf, zsem, wait=False)
        _zero_fill(off_s, cnt_s, nu_s, xs_out, zbuf, zsem, wait=True)

    first_token = pl.program_id(0) * pay_ref.shape[0] * SUBLANES
    for wait in (False, True):
        _row_copies(pos_s, first_token, xs_out, pay_ref, sem, to_sorted=True, wait=wait)


def _dispatch_call(pos, off, cnt, n_used, payload, npad):
    t8, _, width = payload.shape
    tiles = ROW_TILE // SUBLANES
    return pl.pallas_call(
        _dispatch_kernel,
        grid_spec=pltpu.PrefetchScalarGridSpec(
            num_scalar_prefetch=4,
            grid=(t8 // tiles,),
            in_specs=[pl.BlockSpec((tiles, SUBLANES, width), lambda i, *_: (i, 0, 0))],
            out_specs=pl.BlockSpec(memory_space=pl.ANY),
            scratch_shapes=[pltpu.VMEM((EXPERT_TILE // 2, width), F32),
                            pltpu.SemaphoreType.DMA(()), pltpu.SemaphoreType.DMA(())],
        ),
        out_shape=jax.ShapeDtypeStruct((npad, width), F32),
        compiler_params=_params("arbitrary"),
        name="moe_dispatch",
    )(pos, off, cnt, n_used, payload)


def _moe_kernel(load_s, buf_s, ahead_s, ahead_buf_s, wait_s, nu_s, x_ref, wg_hbm, wu_hbm, wd_hbm, y_ref,
                wg_c, wu_c, wd_c, sem):
    i = pl.program_id(0)

    def copies(slot, expert, buf):
        return [pltpu.make_async_copy(hbm.at[expert], cache.at[slot, buf], sem.at[slot, buf])
                for hbm, cache in ((wg_hbm, wg_c), (wu_hbm, wu_c), (wd_hbm, wd_c))]

    for slot in range(2):
        @pl.when(i == 0)
        def _(slot=slot):
            for cp in copies(slot, load_s[slot, 0], buf_s[slot, 0]):
                cp.start()

        @pl.when(ahead_s[slot, i] >= 0)
        def _(slot=slot):
            for cp in copies(slot, ahead_s[slot, i], ahead_buf_s[slot, i]):
                cp.start()

        @pl.when(wait_s[slot, i] == 1)
        def _(slot=slot):
            for cp in copies(slot, 0, buf_s[slot, i]):
                cp.wait()

    @pl.when(i < nu_s[0])
    def _():
        x = x_ref[:, :D_MODEL].astype(BF16)
        gates = (x_ref[:, D_MODEL:D_MODEL + 1], x_ref[:, D_MODEL + 1:D_MODEL + 2])
        w = [[cache[slot, buf_s[slot, i]].astype(BF16) for cache in (wg_c, wu_c, wd_c)] for slot in range(2)]
        up = [(_dot(x, wg), _dot(x, wu)) for wg, wu, _ in w]
        act = [(hg * _sigmoid(hg) * hu).astype(BF16) for hg, hu in up]
        y_ref[...] = gates[0] * _dot(act[0], w[0][2]) + gates[1] * _dot(act[1], w[1][2])

    @pl.when(i >= nu_s[0])
    def _():
        y_ref[...] = jnp.zeros(y_ref.shape, F32)


def _moe_call(schedule, n_used, xs, wg, wu, wd):
    npad = xs.shape[0]
    te = EXPERT_TILE
    last = lambda i, *s: jnp.minimum(i, s[-1][0] - 1)
    hbm = pl.BlockSpec(memory_space=pl.ANY)
    return pl.pallas_call(
        _moe_kernel,
        grid_spec=pltpu.PrefetchScalarGridSpec(
            num_scalar_prefetch=len(schedule) + 1,
            grid=(npad // te,),
            in_specs=[pl.BlockSpec((te, PAYLOAD_W), lambda i, *s: (last(i, *s), 0)), hbm, hbm, hbm],
            out_specs=pl.BlockSpec((te, D_MODEL), lambda i, *s: (i, 0)),
            scratch_shapes=[pltpu.VMEM((2, 2, D_MODEL, D_EXPERT), F32),
                            pltpu.VMEM((2, 2, D_MODEL, D_EXPERT), F32),
                            pltpu.VMEM((2, 2, D_EXPERT, D_MODEL), F32),
                            pltpu.SemaphoreType.DMA((2, 2))],
        ),
        out_shape=jax.ShapeDtypeStruct((npad, D_MODEL), F32),
        compiler_params=_params("arbitrary"),
        name="moe_experts",
    )(*schedule, n_used, xs, wg, wu, wd)


def _combine_kernel(pos_s, ys_hbm, x1_ref, mod_ref, wf_ref, out_ref, ybuf, sem):
    x2 = _moe_residual(pos_s, ys_hbm, x1_ref, mod_ref, ybuf, sem)
    out_ref[0] = _rms(x2) * wf_ref[...]


def _combine_call(pos, ys, x1, mod, wf):
    b, s, d = x1.shape
    rows = ROW_TILE
    return pl.pallas_call(
        _combine_kernel,
        grid_spec=pltpu.PrefetchScalarGridSpec(
            num_scalar_prefetch=1,
            grid=(b, s // rows),
            in_specs=[pl.BlockSpec(memory_space=pl.ANY),
                      pl.BlockSpec((1, rows, d), lambda i, j, *_: (i, j, 0)),
                      pl.BlockSpec((1, 6, d), lambda i, j, *_: (i, 0, 0)),
                      pl.BlockSpec((1, d), lambda i, j, *_: (0, 0))],
            out_specs=pl.BlockSpec((1, rows, d), lambda i, j, *_: (i, j, 0)),
            scratch_shapes=[pltpu.VMEM((2, rows // SUBLANES, SUBLANES, d), F32),
                            pltpu.SemaphoreType.DMA((2,))],
        ),
        out_shape=jax.ShapeDtypeStruct((b, s, d), F32),
        compiler_params=_params("arbitrary", "arbitrary"),
        name="moe_combine",
    )(pos, ys, x1, mod, wf)


def _t5_bucket(n):
    max_exact = N_BUCKETS // 2
    large = max_exact + (np.log(np.maximum(n, 1) / max_exact)
                         / np.log(MAX_DISTANCE / max_exact)
                         * (N_BUCKETS - max_exact)).astype(np.int32)
    large = np.minimum(large, N_BUCKETS - 1)
    return np.where(n < max_exact, n, large).astype(np.int32)


def _bias_table(rel_bias):
    dist = np.arange(WINDOW)[:, None] + WINDOW - np.arange(2 * WINDOW)[None, :]
    bucket = _t5_bucket(np.maximum(dist, 0)).reshape(1, -1)
    onehot = (jnp.arange(N_BUCKETS, dtype=jnp.int32)[:, None] == jnp.asarray(bucket)).astype(F32)
    tab = jnp.dot(rel_bias.astype(F32).T, onehot, precision=lax.Precision.HIGHEST)
    tab = tab.reshape(A_HEADS, WINDOW, 2 * WINDOW)
    in_window = (dist >= 0) & (dist < WINDOW)
    has_key = np.arange(2 * WINDOW)[None, :] >= WINDOW
    return jnp.stack([jnp.where(jnp.asarray(in_window), tab, NEG_INF),
                      jnp.where(jnp.asarray(in_window & has_key), tab, NEG_INF)])


def _layout_w_in(w_in_l):
    pts = np.cumsum([0, M_QK_W, M_QK_W, M_V_W, M_V_W, M_HEADS, M_HEADS,
                     A_Q_W, A_KV_W, A_KV_W, D_MODEL, D_MODEL])
    wb = w_in_l.astype(BF16)
    col = lambda k: wb[:, pts[k]:pts[k + 1]]
    head = lambda h: wb[:, pts[6] + A_HEAD_DIM * h:pts[6] + A_HEAD_DIM * (h + 1)]
    w_if = jnp.concatenate([col(4), col(5)], axis=1)
    w_if_pad = jnp.pad(w_if, ((0, 0), (0, LANES - 2 * M_HEADS)))
    w = jnp.concatenate([col(0), col(2), col(3)] + [head(h) for h in A_HEAD_ORDER]
                        + [col(7), col(8), col(9), col(10), w_if_pad], axis=1)
    w_t = jnp.concatenate([col(1), w_if], axis=1).T
    return w, jnp.pad(w_t, ((0, -w_t.shape[0] % (2 * SUBLANES)), (0, 0)))


def _lookup(idx, table):
    hit = idx[:, None] == jnp.arange(table.shape[0], dtype=jnp.int32)[None, :]
    return jnp.sum(jnp.where(hit[:, :, None], table[None], 0), axis=1)


def _tile_tables(counts, n_tiles, layer):
    te = EXPERT_TILE
    tiles = (counts + te - 1) // te
    ends = jnp.cumsum(tiles)
    off = (ends - tiles) * te
    n_used = ends[-1]
    tile_idx = jnp.minimum(jnp.arange(n_tiles, dtype=jnp.int32), n_used - 1)
    tile_cls = jnp.sum((ends[None, :] <= tile_idx[:, None]).astype(jnp.int32), axis=1)
    tile_cls = jnp.minimum(tile_cls, N_CLASSES - 1)
    off_pad = jnp.pad(off, (0, CLASS_ROWS - N_CLASSES)).astype(jnp.int32)

    cls = jnp.arange(N_CLASSES, dtype=jnp.int32)
    nonempty = tiles > 0
    later = (cls[None, :] > cls[:, None]) & nonempty[None, :]
    nxt = jnp.min(jnp.where(later, cls[None, :], N_CLASSES), axis=1)
    earlier = (cls[None, :] < cls[:, None]) & nonempty[None, :]
    prv = jnp.max(jnp.where(earlier, cls[None, :], -1), axis=1)
    nxt_c = jnp.minimum(nxt, N_CLASSES - 1)
    e = jnp.asarray(np.stack([_CLASS_EA, _CLASS_EB], axis=1)) + layer * N_EXPERTS
    new = nonempty[:, None] & ((prv < 0)[:, None] | (e != _lookup(jnp.maximum(prv, 0), e)))
    new_i = new.astype(jnp.int32)
    buf = (jnp.cumsum(new_i, axis=0) + 1) % 2
    of_next = _lookup(nxt_c, jnp.concatenate([e, buf, new_i], axis=1))
    ahead = jnp.where((nxt < N_CLASSES)[:, None] & (of_next[:, 4:6] > 0), of_next[:, 0:2], -1)
    per_class = jnp.concatenate([e, buf, ahead, of_next[:, 2:4], new_i, (ends - tiles)[:, None]], axis=1)
    per_step = _lookup(tile_cls, per_class)
    steps = jnp.arange(n_tiles, dtype=jnp.int32)
    first_tile = ((steps < n_used) & (steps == per_step[:, 10]))[:, None]
    schedule = (per_step[:, 0:2], per_step[:, 2:4], jnp.where(first_tile, per_step[:, 4:6], -1),
                per_step[:, 6:8], (first_tile & (per_step[:, 8:10] > 0)).astype(jnp.int32))
    schedule = tuple(t.T.astype(jnp.int32) for t in schedule)
    return off_pad, schedule, n_used.reshape(1).astype(jnp.int32)


def kernel(x, c, w_ada, b_ada, w_norm1, w_in, conv_w, conv_b, b_igate, b_fgate, w_mnorm, sinks, rel_bias,
           w_br_m, w_br_a, w_out, w_norm2, w_router, router_bias, w_gate_e, w_up_e, w_down_e, w_final):
    b, s, d = x.shape
    depth = w_ada.shape[0]
    t = b * s
    n_tiles = t // EXPERT_TILE + N_CLASSES
    npad = n_tiles * EXPERT_TILE

    mod_all = _ada_call(jnp.pad(c, ((0, 8 - b), (0, 0))), w_ada, b_ada)[:, :b]
    bias_tab = _bias_table(rel_bias)
    wrt = w_router.T
    wrh = wrt.astype(BF16)
    wrl = (wrt - wrh.astype(F32)).astype(BF16)
    rb = router_bias.reshape(N_EXPERTS, 1).astype(F32)
    row = lambda v: v.reshape(1, -1).astype(F32)

    wg_all, wu_all, wd_all = [w.reshape((-1,) + w.shape[2:]) for w in (w_gate_e, w_up_e, w_down_e)]

    moe = None
    for l in range(depth):
        mod = mod_all[l].reshape(b, 6, d)
        w_l, wift = _layout_w_in(w_in[l])
        gate_bias = jnp.concatenate([b_igate[l], b_fgate[l]]).astype(F32)
        bcol = jnp.pad(gate_bias, (0, LANES - 2 * M_HEADS)).reshape(1, LANES)
        brow = gate_bias.reshape(2 * M_HEADS, 1)

        proj_args = (mod, row(w_norm1[l]), w_l, wift, bcol, brow)
        if moe is None:
            q, kt, v, o, qa, kva, g, ifc, ifr = _inproj_call(x, *proj_args)
        else:
            x, q, kt, v, o, qa, kva, g, ifc, ifr = _combine_inproj_call(*moe, *proj_args)
        cw, cb = conv_w[l].astype(F32), conv_b[l].astype(F32)
        cwk = jnp.broadcast_to(cw[:, M_QK_W:, None], (M_CONV, M_QK_W, MLSTM_CHUNK))
        cbk = jnp.broadcast_to(cb[M_QK_W:, None], (M_QK_W, MLSTM_CHUNK))
        hm = _mlstm_call(q, kt, v, o, ifc, ifr, cw[:, :M_QK_W], row(cb[:M_QK_W]), cwk, cbk, row(w_mnorm[l]))
        ha = _swa_call(sinks[l].astype(F32), qa, kva, bias_tab)

        wba = jnp.concatenate([w_br_a[l][A_HEAD_DIM * h:A_HEAD_DIM * (h + 1)] for h in A_HEAD_ORDER])
        x1, payload, cls, rank, cnt = _post_call(
            hm, ha, g, x, mod, w_br_m[l].astype(BF16), wba.astype(BF16), w_out[l].astype(BF16),
            row(w_norm2[l]), wrh, wrl, rb)

        counts = cnt[:, 0].astype(jnp.int32)
        off, schedule, n_used = _tile_tables(counts[:N_CLASSES], n_tiles, l)
        hit = cls.reshape(t, 1) == jnp.arange(CLASS_ROWS, dtype=jnp.int32)[None, :]
        pos = jnp.sum(jnp.where(hit, off[None, :], 0), axis=1) + rank.reshape(t)
        xs = _dispatch_call(pos, off, counts, n_used,
                            payload.reshape(t // SUBLANES, SUBLANES, PAYLOAD_W), npad)
        ys = _moe_call(schedule, n_used, xs, wg_all, wu_all, wd_all)
        moe = (pos, ys, x1, mod)
    return _combine_call(*moe, row(w_final))
```

```python
import numpy as np
import jax
import jax.numpy as jnp
from jax import lax
from jax.experimental import pallas as pl
from jax.experimental.pallas import tpu as pltpu

F32 = jnp.float32
BF16 = jnp.bfloat16

D_MODEL = 1024
M_HEADS = 4
M_DQK = 64
M_DV = 128
M_CONV = 4
M_QK_W = M_HEADS * M_DQK
M_V_W = M_HEADS * M_DV
A_HEADS = 8
A_KV_HEADS = 2
A_HEAD_DIM = 64
A_Q_W = A_HEADS * A_HEAD_DIM
A_KV_W = A_KV_HEADS * A_HEAD_DIM
WINDOW = 128
N_BUCKETS = 32
MAX_DISTANCE = 128
N_EXPERTS = 16
N_GROUPS = 4
EXPERTS_PER_GROUP = N_EXPERTS // N_GROUPS
D_EXPERT = 512
EPS = 1e-6
NEG_INF = -1e30

ADA_CHUNKS = 6
LANES = 128
SUBLANES = 8
MLSTM_CHUNK = 128
MLSTM_CHUNKS_PER_STEP = 8
SWA_BLOCKS = 8
TOKEN_TILE = 512
ROUTER_TILE = 1024
EXPERT_TILE = 256
ROW_TILE = 1024
PAIRS_PER_GROUP = 6
N_CLASSES = N_GROUPS * PAIRS_PER_GROUP
CLASS_ROWS = 32
PAYLOAD_W = D_MODEL + LANES
VMEM_LIMIT = 48 * 1024 * 1024

C_Q = 0
C_V = C_Q + M_QK_W
C_O = C_V + M_V_W
C_QA = C_O + M_V_W
C_KVA = C_QA + A_Q_W
C_G = C_KVA + 2 * A_KV_W
C_IF = C_G + 2 * D_MODEL
C_END = C_IF + LANES

A_HEAD_ORDER = (0, 4, 1, 5, 2, 6, 3, 7)

_PAIRS = [(a, b) for a in range(EXPERTS_PER_GROUP) for b in range(a + 1, EXPERTS_PER_GROUP)]
_CLASS_PAIRS = ((0, 1), (2, 1), (2, 0), (3, 0), (3, 1), (3, 2))
assert sorted(tuple(sorted(p)) for p in _CLASS_PAIRS) == _PAIRS
_CLASS_EA = np.array([g * EXPERTS_PER_GROUP + a for g in range(N_GROUPS) for a, _ in _CLASS_PAIRS], np.int32)
_CLASS_EB = np.array([g * EXPERTS_PER_GROUP + b for g in range(N_GROUPS) for _, b in _CLASS_PAIRS], np.int32)

_NT = (((1,), (1,)), ((), ()))


def _sigmoid(x):
    return 0.5 * jnp.tanh(0.5 * x) + 0.5


def _log_sigmoid(x):
    return -(jnp.maximum(-x, 0.0) + jnp.log1p(jnp.exp(-jnp.abs(x))))


def _dot(a, b, dims=None):
    if dims is None:
        return jnp.dot(a, b, preferred_element_type=F32)
    return lax.dot_general(a, b, dims, preferred_element_type=F32)


def _split2(a):
    hi = a.astype(BF16)
    lo = (a - hi.astype(F32)).astype(BF16)
    return hi, lo


def _split3(a):
    hi = a.astype(BF16)
    r = a - hi.astype(F32)
    mid = r.astype(BF16)
    lo = (r - mid.astype(F32)).astype(BF16)
    return hi, mid, lo


def _dot_hi(a, b, dims=None):
    ah, al = _split2(a)
    bh, bl = _split2(b)
    return _dot(ah, bh, dims) + _dot(ah, bl, dims) + _dot(al, bh, dims)


def _rms(x):
    return x * lax.rsqrt(jnp.mean(x * x, axis=-1, keepdims=True) + EPS)


def _params(*sem):
    return pltpu.CompilerParams(dimension_semantics=sem, vmem_limit_bytes=VMEM_LIMIT)


def _ada_kernel(c_ref, w_ref, b_ref, o_ref):
    c = c_ref[...]
    cond = c * _sigmoid(c)
    o_ref[0] = _dot_hi(cond, w_ref[0]) + b_ref[0]


def _ada_call(c8, w_ada, b_ada):
    depth = w_ada.shape[0]
    return pl.pallas_call(
        _ada_kernel,
        grid=(depth, ADA_CHUNKS),
        in_specs=[
            pl.BlockSpec((SUBLANES, D_MODEL), lambda l, j: (0, 0)),
            pl.BlockSpec((1, D_MODEL, D_MODEL), lambda l, j: (l, 0, j)),
            pl.BlockSpec((1, 1, D_MODEL), lambda l, j: (l, 0, j)),
        ],
        out_specs=pl.BlockSpec((1, SUBLANES, D_MODEL), lambda l, j: (l, 0, j)),
        out_shape=jax.ShapeDtypeStruct((depth, SUBLANES, ADA_CHUNKS * D_MODEL), F32),
        compiler_params=_params("arbitrary", "arbitrary"),
        name="adaln_mod",
    )(c8, w_ada, b_ada.reshape(depth, 1, ADA_CHUNKS * D_MODEL))


def _project(x, mod_ref, wn_ref, w_ref, wift_ref, bcol_ref, brow_ref, outs):
    q_ref, kt_ref, v_ref, o_ref, qa_ref, kva_ref, g_ref, ifc_ref, ifr_ref = outs
    h = _rms(x) * (wn_ref[...] * (1.0 + mod_ref[0, 1:2, :])) + mod_ref[0, 0:1, :]
    hb = h.astype(BF16)

    def seg(a, b):
        return _dot(hb, w_ref[:, a:b])

    q_ref[0] = seg(C_Q, C_V).astype(BF16)
    v_ref[0] = seg(C_V, C_O).astype(BF16)
    o_ref[0] = seg(C_O, C_QA).astype(BF16)
    qa_ref[0] = seg(C_QA, C_KVA).astype(BF16)
    kva_ref[0] = seg(C_KVA, C_G).astype(BF16)
    g_ref[0] = seg(C_G, C_IF).astype(BF16)
    zc = seg(C_IF, C_END) + bcol_ref[...]
    lane = lax.broadcasted_iota(jnp.int32, zc.shape, 1)
    ifc_ref[0] = jnp.where((lane >= M_HEADS) & (lane < 2 * M_HEADS), _log_sigmoid(zc), zc)
    zt = _dot(wift_ref[...], hb, _NT)
    kt_ref[0] = zt[:M_QK_W].astype(BF16)
    zr = zt[M_QK_W:M_QK_W + 2 * M_HEADS] + brow_ref[...]
    row = lax.broadcasted_iota(jnp.int32, zr.shape, 0)
    ifr_ref[0] = jnp.where(row >= M_HEADS, _log_sigmoid(zr), zr)


def _inproj_kernel(x_ref, mod_ref, wn_ref, w_ref, wift_ref, bcol_ref, brow_ref, *outs):
    _project(x_ref[0], mod_ref, wn_ref, w_ref, wift_ref, bcol_ref, brow_ref, outs)


def _row_copies(pos_s, first_token, sorted_hbm, tiles, sem, *, to_sorted, wait):
    def copy(i, k, pos):
        pair = (tiles.at[i, pl.ds(k, 1), :], sorted_hbm.at[pl.ds(pos, 1), :])
        return pltpu.make_async_copy(*(pair if to_sorted else pair[::-1]), sem)

    if wait:
        def body(i, carry):
            for k in range(SUBLANES):
                copy(0, k, 0).wait()
            return carry

        lax.fori_loop(0, tiles.shape[0], body, 0, unroll=2)
        return
    for i in range(tiles.shape[0]):
        for k in range(SUBLANES):
            copy(i, k, pos_s[first_token + i * SUBLANES + k]).start(priority=k % 2)


def _moe_residual(pos_s, ys_hbm, x1_ref, mod_ref, ybuf, sem):
    rows = x1_ref.shape[1]
    step = pl.program_id(0) * pl.num_programs(1) + pl.program_id(1)
    n_steps = pl.num_programs(0) * pl.num_programs(1)
    slot = lax.rem(step, 2)

    def fetch(first_token, slot_, wait):
        _row_copies(pos_s, first_token, ys_hbm, ybuf.at[slot_], sem.at[slot_], to_sorted=False, wait=wait)

    @pl.when(step == 0)
    def _():
        fetch(0, 0, False)

    @pl.when(step + 1 < n_steps)
    def _():
        fetch((step + 1) * rows, 1 - slot, False)

    fetch(0, slot, True)
    y = ybuf[slot].reshape(rows, ybuf.shape[-1])
    return x1_ref[0] + mod_ref[0, 5:6, :] * y


def _combine_inproj_kernel(pos_s, ys_hbm, x1_ref, modp_ref, mod_ref, wn_ref, w_ref,
                           wift_ref, bcol_ref, brow_ref, xo_ref, *rest):
    outs, (ybuf, sem) = rest[:-2], rest[-2:]
    x = _moe_residual(pos_s, ys_hbm, x1_ref, modp_ref, ybuf, sem)
    xo_ref[0] = x
    _project(x, mod_ref, wn_ref, w_ref, wift_ref, bcol_ref, brow_ref, outs)


def _inproj_specs(b, s, tm):
    tok = lambda w_: pl.BlockSpec((1, tm, w_), lambda i, j, *_: (i, j, 0))
    sds = lambda w_, dt: jax.ShapeDtypeStruct((b, s, w_), dt)
    rows = lambda n: pl.BlockSpec((1, n, tm), lambda i, j, *_: (i, 0, j))
    out_specs = [tok(M_QK_W), rows(M_QK_W), tok(M_V_W), tok(M_V_W), tok(A_Q_W), tok(2 * A_KV_W),
                 tok(2 * D_MODEL), tok(LANES), rows(2 * M_HEADS)]
    out_shape = [sds(M_QK_W, BF16), jax.ShapeDtypeStruct((b, M_QK_W, s), BF16), sds(M_V_W, BF16),
                 sds(M_V_W, BF16), sds(A_Q_W, BF16), sds(2 * A_KV_W, BF16), sds(2 * D_MODEL, BF16),
                 sds(LANES, F32), jax.ShapeDtypeStruct((b, 2 * M_HEADS, s), F32)]
    return tok, out_specs, out_shape


def _inproj_call(x, mod, wn, w, wift, bcol, brow):
    b, s, d = x.shape
    tm = TOKEN_TILE
    tok, out_specs, out_shape = _inproj_specs(b, s, tm)
    full = lambda a: pl.BlockSpec(a.shape, lambda i, j: (0,) * a.ndim)
    return pl.pallas_call(
        _inproj_kernel,
        grid=(b, s // tm),
        in_specs=[tok(d), pl.BlockSpec((1, ADA_CHUNKS, d), lambda i, j: (i, 0, 0)), full(wn), full(w),
                  full(wift), full(bcol), full(brow)],
        out_specs=out_specs,
        out_shape=out_shape,
        compiler_params=_params("arbitrary", "arbitrary"),
        name="norm1_inproj",
    )(x, mod, wn, w, wift, bcol, brow)


def _combine_inproj_call(pos, ys, x1, mod_prev, mod, wn, w, wift, bcol, brow):
    b, s, d = x1.shape
    tm = TOKEN_TILE
    tok, out_specs, out_shape = _inproj_specs(b, s, tm)
    full = lambda a: pl.BlockSpec(a.shape, lambda i, j, *_: (0,) * a.ndim)
    modspec = pl.BlockSpec((1, ADA_CHUNKS, d), lambda i, j, *_: (i, 0, 0))
    return pl.pallas_call(
        _combine_inproj_kernel,
        grid_spec=pltpu.PrefetchScalarGridSpec(
            num_scalar_prefetch=1,
            grid=(b, s // tm),
            in_specs=[pl.BlockSpec(memory_space=pl.ANY), tok(d), modspec, modspec, full(wn), full(w),
                      full(wift), full(bcol), full(brow)],
            out_specs=[tok(d)] + out_specs,
            scratch_shapes=[pltpu.VMEM((2, tm // SUBLANES, SUBLANES, d), F32),
                            pltpu.SemaphoreType.DMA((2,))],
        ),
        out_shape=[jax.ShapeDtypeStruct((b, s, d), F32)] + out_shape,
        compiler_params=_params("arbitrary", "arbitrary"),
        name="combine_norm1_inproj",
    )(pos, ys, x1, mod_prev, mod, wn, w, wift, bcol, brow)


def _mlstm_kernel(q_ref, kt_ref, v_ref, o_ref, ifc_ref, ifr_ref, cwq_ref, cbq_ref, cwk_ref, cbk_ref, wn_ref,
                  out_ref, qbuf, kbuf, cstate, mstate):
    @pl.when(pl.program_id(1) == 0)
    def _():
        qbuf[...] = jnp.zeros(qbuf.shape, qbuf.dtype)
        kbuf[...] = jnp.zeros(kbuf.shape, kbuf.dtype)
        cstate[...] = jnp.zeros(cstate.shape, F32)
        mstate[...] = jnp.zeros(mstate.shape, F32)

    heads = range(M_HEADS)
    carry = (qbuf[...], kbuf[...], [cstate[h] for h in heads], [mstate[h][:, 0:1] for h in heads])
    for c in range(q_ref.shape[1] // MLSTM_CHUNK):
        carry = _mlstm_chunk(c, carry, q_ref, kt_ref, v_ref, o_ref, ifc_ref, ifr_ref, cwq_ref, cbq_ref,
                             cwk_ref, cbk_ref, wn_ref, out_ref)
    qbuf[...], kbuf[...] = carry[0], carry[1]
    for h in heads:
        cstate[h] = carry[2][h]
        mstate[h] = jnp.broadcast_to(carry[3][h], (1, LANES))


def _mlstm_chunk(c, carry, q_ref, kt_ref, v_ref, o_ref, ifc_ref, ifr_ref, cwq_ref, cbq_ref, cwk_ref, cbk_ref,
                 wn_ref, out_ref):
    L = MLSTM_CHUNK
    span = slice(L * c, L * (c + 1))
    q_tail, k_prev, cst, m_prev = carry
    pad = q_tail.shape[0]
    heads = range(M_HEADS)

    xq = q_ref[0, span, :]
    q_all = jnp.concatenate([q_tail, xq], axis=0)
    xk = kt_ref[0, :, span]
    k_all = jnp.concatenate([k_prev, xk], axis=1)
    r_s = lax.broadcasted_iota(jnp.int32, (L, L + pad), 0)
    c_s = lax.broadcasted_iota(jnp.int32, (L, L + pad), 1)
    r_t = lax.broadcasted_iota(jnp.int32, (2 * L, L), 0)
    c_t = lax.broadcasted_iota(jnp.int32, (2 * L, L), 1)
    acc_q = cbq_ref[...] + cwq_ref[M_CONV - 1:M_CONV, :] * xq.astype(F32)
    acc_k = cbk_ref[...] + cwk_ref[M_CONV - 1] * xk.astype(F32)
    for j in range(M_CONV - 1):
        back = M_CONV - 1 - j
        shift_q = jnp.where(c_s == r_s + (pad - back), 1.0, 0.0).astype(BF16)
        acc_q = acc_q + cwq_ref[j:j + 1, :] * _dot(shift_q, q_all)
        shift_k = jnp.where(r_t == c_t + (L - back), 1.0, 0.0).astype(BF16)
        acc_k = acc_k + cwk_ref[j] * _dot(k_all, shift_k)
    q = acc_q * _sigmoid(acc_q) * (M_DQK ** -0.5)
    kf = acc_k * _sigmoid(acc_k)
    kb = kf.astype(BF16)
    vb = v_ref[0, span, :]

    r_i = lax.broadcasted_iota(jnp.int32, (L, L), 0)
    c_i = lax.broadcasted_iota(jnp.int32, (L, L), 1)
    causal = c_i <= r_i
    tril = jnp.where(causal, 1.0, 0.0).astype(BF16)
    triu = jnp.where(r_i <= c_i, 1.0, 0.0).astype(BF16)

    icol = ifc_ref[0, span, :]
    irow = ifr_ref[0, :, span]
    ch, cm, cl = _split3(icol)
    bcum_col = _dot(tril, ch) + _dot(tril, cm) + _dot(tril, cl)
    rh, rm, rl = _split3(irow)
    bcum_row = _dot(rh, triu) + _dot(rm, triu) + _dot(rl, triu)

    lane = lax.broadcasted_iota(jnp.int32, (1, LANES), 1)
    lo_half = lane < M_DQK
    hi_half = jnp.logical_not(lo_half)
    ones = jnp.ones((L, LANES), BF16)
    pair = lambda h: slice(LANES * (h // 2), LANES * (h // 2 + 1))

    qmask =[jnp.where(lo_half if h % 2 == 0 else hi_half, q[:, pair(h)], 0.0) for h in heads]
    vaug = [jnp.concatenate([vb[:, M_DV * h:M_DV * (h + 1)], ones], axis=1) for h in heads]
    bc_col = [bcum_col[:, M_HEADS + h:M_HEADS + h + 1] for h in heads]
    bc_row = [bcum_row[M_HEADS + h:M_HEADS + h + 1, :] for h in heads]
    i_row = [irow[h:h + 1, :] for h in heads]
    b_last = [bc_row[h][:, L - 1:L] for h in heads]

    s = [_dot(qmask[h].astype(BF16), kb[pair(h), :]) for h in heads]

    dlog = [jnp.where(causal, bc_col[h] - bc_row[h] + i_row[h], NEG_INF) for h in heads]
    m_inter = [bc_col[h] + m_prev[h] for h in heads]
    m_row = [jnp.maximum(m_inter[h], jnp.max(dlog[h], axis=-1, keepdims=True)) for h in heads]
    sc = [(s[h] * jnp.exp(dlog[h] - m_row[h])).astype(BF16) for h in heads]
    qi = [(jnp.exp(m_inter[h] - m_row[h]) * qmask[h]).astype(BF16) for h in heads]

    num = [_dot(sc[h], vaug[h]) + _dot(qi[h], cst[h].astype(BF16)) for h in heads]
    for h in heads:
        den = num[h][:, M_DV:]
        hh = num[h][:, :M_DV] / jnp.maximum(jnp.abs(den), jnp.exp(-m_row[h]))
        hn = _rms(hh) * wn_ref[:, M_DV * h:M_DV * (h + 1)]
        og = _sigmoid(o_ref[0, span, M_DV * h:M_DV * (h + 1)].astype(F32))
        out_ref[0, span, M_DV * h:M_DV * (h + 1)] = (hn * og).astype(BF16)

    w_state = [b_last[h] - bc_row[h] + i_row[h] for h in heads]
    m_loc = [jnp.max(w_state[h], axis=-1, keepdims=True) for h in heads]
    ak = [(jnp.exp(w_state[h] - m_loc[h]) * kf[pair(h), :]).astype(BF16) for h in heads]
    c_loc = [_dot(ak[h], vaug[h]) for h in heads]
    m_new = [jnp.maximum(b_last[h] + m_prev[h], m_loc[h]) for h in heads]
    c_new = [jnp.exp(b_last[h] + m_prev[h] - m_new[h]) * cst[h] + jnp.exp(m_loc[h] - m_new[h]) * c_loc[h]
             for h in heads]
    return xq[L - pad:L, :], xk, c_new, m_new


def _mlstm_call(q, kt, v, o, ifc, ifr, cwq, cbq, cwk, cbk, w_mnorm):
    b, s, _ = q.shape
    L = MLSTM_CHUNK
    span = MLSTM_CHUNKS_PER_STEP * L
    tok = lambda w_: pl.BlockSpec((1, span, w_), lambda i, j: (i, j, 0))
    rows = lambda n: pl.BlockSpec((1, n, span), lambda i, j: (i, 0, j))
    full = lambda a: pl.BlockSpec(a.shape, lambda i, j: (0,) * a.ndim)
    return pl.pallas_call(
        _mlstm_kernel,
        grid=(b, s // span),
        in_specs=[tok(M_QK_W), rows(M_QK_W), tok(M_V_W), tok(M_V_W), tok(LANES), rows(2 * M_HEADS),
                  full(cwq), full(cbq), full(cwk), full(cbk), full(w_mnorm)],
        out_specs=tok(M_V_W),
        out_shape=jax.ShapeDtypeStruct((b, s, M_V_W), BF16),
        scratch_shapes=[pltpu.VMEM((2 * SUBLANES, M_QK_W), BF16),
                        pltpu.VMEM((M_QK_W, L), BF16),
                        pltpu.VMEM((M_HEADS, LANES, 2 * M_DV), F32),
                        pltpu.VMEM((M_HEADS, 1, LANES), F32)],
        compiler_params=_params("arbitrary", "arbitrary"),
        name="mlstm",
    )(q, kt, v, o, ifc, ifr, cwq, cbq, cwk, cbk, w_mnorm)


def _swa_kernel(sink_ref, q_ref, kvp_ref, kvc_ref, bias_ref, out_ref):
    W = WINDOW
    first = pl.program_id(1) == 0
    kv = [kvp_ref[0]] + [kvc_ref[0, W * u:W * (u + 1), :] for u in range(SWA_BLOCKS)]
    lane = lax.broadcasted_iota(jnp.int32, (1, LANES), 1)
    lo_half = lane < A_HEAD_DIM
    hi_half = jnp.logical_not(lo_half)
    keys, v_half = [], []
    for u in range(SWA_BLOCKS):
        keys.append(jnp.concatenate([kv[u][:, :A_KV_W], kv[u + 1][:, :A_KV_W]], axis=0))
        vals = jnp.concatenate([kv[u][:, A_KV_W:], kv[u + 1][:, A_KV_W:]], axis=0)
        zero = jnp.zeros_like(vals)
        v_half.append((jnp.where(lo_half, vals, zero), jnp.where(hi_half, vals, zero)))
    table = [jnp.where(first, 1, 0)] + [0] * (SWA_BLOCKS - 1)

    tiles = range(A_HEADS // 2)
    slots = [(u, j, p) for u in range(SWA_BLOCKS) for j in tiles for p in range(2)]
    head = {ujp: A_HEAD_ORDER[2 * ujp[1] + ujp[2]] for ujp in slots}
    scale = jnp.asarray(A_HEAD_DIM ** -0.5, BF16)
    qt = {(u, j): q_ref[0, W * u:W * (u + 1), LANES * j:LANES * (j + 1)] * scale
          for u in range(SWA_BLOCKS) for j in tiles}
    qm = {(u, j, p): jnp.where(lo_half if p == 0 else hi_half, qt[u, j], jnp.zeros_like(qt[u, j]))
          for u, j, p in slots}
    s = {ujp: _dot(qm[ujp], keys[ujp[0]], _NT) + bias_ref[table[ujp[0]], head[ujp]] for ujp in slots}
    m = {ujp: jnp.maximum(jnp.max(s[ujp], axis=-1, keepdims=True), sink_ref[head[ujp]]) for ujp in slots}
    e = {ujp: jnp.exp(s[ujp] - m[ujp]) for ujp in slots}
    denom = {ujp: jnp.sum(e[ujp], axis=-1, keepdims=True) + jnp.exp(sink_ref[head[ujp]] - m[ujp])
             for ujp in slots}
    pv = {(u, j, p): _dot(e[u, j, p].astype(BF16), v_half[u][p]) for u, j, p in slots}
    for u in range(SWA_BLOCKS):
        for j in tiles:
            out = pv[u, j, 0] / denom[u, j, 0] + pv[u, j, 1] / denom[u, j, 1]
            out_ref[0, W * u:W * (u + 1), LANES * j:LANES * (j + 1)] = out.astype(BF16)


def _swa_call(sinks, qa, kva, bias):
    b, s, _ = qa.shape
    W = WINDOW
    rows = SWA_BLOCKS * W
    return pl.pallas_call(
        _swa_kernel,
        grid=(b, s // rows),
        in_specs=[pl.BlockSpec(memory_space=pltpu.SMEM),
                  pl.BlockSpec((1, rows, A_Q_W), lambda i, j: (i, j, 0)),
                  pl.BlockSpec((1, W, 2 * A_KV_W), lambda i, j: (i, jnp.maximum(SWA_BLOCKS * j - 1, 0), 0)),
                  pl.BlockSpec((1, rows, 2 * A_KV_W), lambda i, j: (i, j, 0)),
                  pl.BlockSpec(bias.shape, lambda i, j: (0,) * bias.ndim)],
        out_specs=pl.BlockSpec((1, rows, A_Q_W), lambda i, j: (i, j, 0)),
        out_shape=jax.ShapeDtypeStruct((b, s, A_Q_W), BF16),
        compiler_params=_params("arbitrary", "arbitrary"),
        name="swa",
    )(sinks, qa, kva, kva, bias)


def _post_kernel(hm_ref, ha_ref, g_ref, x_ref, mod_ref, wbm_ref, wba_ref, wo_ref, wn2_ref,
                 wrh_ref, wrl_ref, rb_ref,
                 x1_ref, pay_ref, cls_ref, rank_ref, cnt_ref, carry):
    tm = x_ref.shape[1]

    @pl.when((pl.program_id(0) == 0) & (pl.program_id(1) == 0))
    def _():
        carry[...] = jnp.zeros(carry.shape, F32)

    g = g_ref[0]
    pm = _dot(hm_ref[0], wbm_ref[...])
    pa = _dot(ha_ref[0], wba_ref[...])
    merged = (_sigmoid(g[:, :D_MODEL]) * pm.astype(BF16)
              + _sigmoid(g[:, D_MODEL:]) * pa.astype(BF16))
    mo = _dot(merged, wo_ref[...])
    x1 = x_ref[0] + mod_ref[0, 2:3, :] * mo
    x1_ref[0] = x1
    h2 = _rms(x1) * (wn2_ref[...] * (1.0 + mod_ref[0, 4:5, :])) + mod_ref[0, 3:4, :]
    pay_ref[0, :, :D_MODEL] = h2

    hh, hl = _split2(h2)
    wrh = wrh_ref[...]
    logits = _dot(wrh, hh, _NT) + _dot(wrh, hl, _NT) + _dot(wrl_ref[...], hh, _NT)
    scores = 1.0 / (1.0 + jnp.exp(-logits))
    sel = scores + rb_ref[...]

    def row(a, r):
        return a[r:r + 1, :]

    grp = []
    for gi in range(N_GROUPS):
        v = [row(sel, gi * EXPERTS_PER_GROUP + k) for k in range(EXPERTS_PER_GROUP)]
        best = v[0] + v[1]
        for a, b in _PAIRS[1:]:
            best = jnp.maximum(best, v[a] + v[b])
        grp.append(best)
    gbest = grp[0]
    gsel = jnp.zeros_like(gbest)
    for gi in range(1, N_GROUPS):
        take = grp[gi] > gbest
        gbest = jnp.where(take, grp[gi], gbest)
        gsel = jnp.where(take, float(gi), gsel)

    sv, gv = [], []
    for k in range(EXPERTS_PER_GROUP):
        s_k = row(sel, k)
        g_k = row(scores, k)
        for gi in range(1, N_GROUPS):
            hit = gsel == float(gi)
            s_k = jnp.where(hit, row(sel, gi * EXPERTS_PER_GROUP + k), s_k)
            g_k = jnp.where(hit, row(scores, gi * EXPERTS_PER_GROUP + k), g_k)
        sv.append(s_k)
        gv.append(g_k)

    def argmax4(vals):
        bv, bi = vals[0], jnp.zeros_like(vals[0])
        for k in range(1, EXPERTS_PER_GROUP):
            take = vals[k] > bv
            bv = jnp.where(take, vals[k], bv)
            bi = jnp.where(take, float(k), bi)
        return bi

    i1 = argmax4(sv)
    i2 = argmax4([jnp.where(i1 == float(k), -jnp.inf, sv[k]) for k in range(EXPERTS_PER_GROUP)])
    w1 = jnp.zeros_like(i1)
    w2 = jnp.zeros_like(i1)
    for k in range(EXPERTS_PER_GROUP):
        w1 = jnp.where(i1 == float(k), gv[k], w1)
        w2 = jnp.where(i2 == float(k), gv[k], w2)
    wsum = w1 + w2
    w1 = w1 / wsum
    w2 = w2 / wsum
    code = i1 * float(EXPERTS_PER_GROUP) + i2
    pair_idx = jnp.zeros_like(code)
    gate_a, gate_b = w1, w2
    for k, (a, b) in enumerate(_CLASS_PAIRS):
        fwd = code == float(a * EXPERTS_PER_GROUP + b)
        rev = code == float(b * EXPERTS_PER_GROUP + a)
        pair_idx = jnp.where(fwd, float(k), jnp.where(rev, float(k), pair_idx))
        gate_a = jnp.where(rev, w2, gate_a)
        gate_b = jnp.where(rev, w1, gate_b)
    cls_f = gsel * float(PAIRS_PER_GROUP) + pair_idx
    cls_ref[0] = cls_f.astype(jnp.int32)

    grow = lax.broadcasted_iota(jnp.int32, (LANES, tm), 0)
    gmat = jnp.where(grow == 0, gate_a, jnp.where(grow == 1, gate_b, 0.0))
    pay_ref[0, :, D_MODEL:] = gmat.T

    crow = lax.broadcasted_iota(jnp.int32, (CLASS_ROWS, tm), 0).astype(F32)
    onehot = crow == cls_f
    r_i = lax.broadcasted_iota(jnp.int32, (tm, tm), 0)
    c_i = lax.broadcasted_iota(jnp.int32, (tm, tm), 1)
    upper = jnp.where(r_i <= c_i, 1.0, 0.0).astype(BF16)
    cum = _dot(jnp.where(onehot, 1.0, 0.0).astype(BF16), upper)
    before = carry[:, 0:1]
    rank = jnp.sum(jnp.where(onehot, cum - 1.0 + before, 0.0), axis=0, keepdims=True)
    rank_ref[0] = rank.astype(jnp.int32)
    total = before + cum[:, tm - 1:tm]
    carry[...] = jnp.broadcast_to(total, carry.shape)
    cnt_ref[...] = jnp.broadcast_to(total, cnt_ref.shape)


def _post_call(hm, ha, g, x, mod, wbm, wba, wo, wn2, wrh, wrl, rb):
    b, s, d = x.shape
    tm = ROUTER_TILE
    tok = lambda w_: pl.BlockSpec((1, tm, w_), lambda i, j: (i, j, 0))
    full = lambda a: pl.BlockSpec(a.shape, lambda i, j: (0,) * a.ndim)
    lanes = pl.BlockSpec((1, 1, tm), lambda i, j: (i, 0, j))
    return pl.pallas_call(
        _post_kernel,
        grid=(b, s // tm),
        in_specs=[tok(M_V_W), tok(A_Q_W), tok(2 * d), tok(d),
                  pl.BlockSpec((1, ADA_CHUNKS, d), lambda i, j: (i, 0, 0)),
                  full(wbm), full(wba), full(wo), full(wn2), full(wrh), full(wrl), full(rb)],
        out_specs=[tok(d), tok(PAYLOAD_W), lanes, lanes,
                   pl.BlockSpec((CLASS_ROWS, LANES), lambda i, j: (0, 0))],
        out_shape=[jax.ShapeDtypeStruct((b, s, d), F32),
                   jax.ShapeDtypeStruct((b, s, PAYLOAD_W), F32),
                   jax.ShapeDtypeStruct((b, 1, s), jnp.int32),
                   jax.ShapeDtypeStruct((b, 1, s), jnp.int32),
                   jax.ShapeDtypeStruct((CLASS_ROWS, LANES), F32)],
        scratch_shapes=[pltpu.VMEM((CLASS_ROWS, LANES), F32)],
        compiler_params=_params("arbitrary", "arbitrary"),
        name="merge_outproj_router",
    )(hm, ha, g, x, mod, wbm, wba, wo, wn2, wrh, wrl, rb)


def _zero_fill(off_s, cnt_s, nu_s, xs_out, zbuf, zsem, *, wait):
    chunk = zbuf.shape[0]

    def zero_rows(first, n):
        cp = pltpu.make_async_copy(zbuf.at[pl.ds(0, n), :], xs_out.at[pl.ds(first, n), :], zsem)
        cp.wait() if wait else cp.start()

    def per_class(c, carry):
        n = cnt_s[c]
        fill = (-n) & (EXPERT_TILE - 1)
        head = fill & (SUBLANES - 1)
        for k in range(SUBLANES - 1):
            pl.when(k < head)(lambda k=k: zero_rows(off_s[c] + n + k, 1))
        cur = off_s[c] + n + head
        p = chunk
        while p >= SUBLANES:
            pl.when((fill & p) != 0)(lambda cur=cur, p=p: zero_rows(pl.multiple_of(cur, SUBLANES), p))
            cur = cur + (fill & p)
            p //= 2
        return carry

    lax.fori_loop(0, N_CLASSES, per_class, 0)

    def per_chunk(r, carry):
        zero_rows(pl.multiple_of(r * chunk, SUBLANES), chunk)
        return carry

    per_tile = EXPERT_TILE // chunk
    lax.fori_loop(nu_s[0] * per_tile, (xs_out.shape[0] // EXPERT_TILE) * per_tile, per_chunk, 0)


def _dispatch_kernel(pos_s, off_s, cnt_s, nu_s, pay_ref, xs_out, zbuf, sem, zsem):
    @pl.when(pl.program_id(0) == 0)
    def _():
        zbuf[...] = jnp.zeros(zbuf.shape, F32)
        _zero_fill(off_s, cnt_s, nu_s, xs_out, zbuf, zsem, wait=False)
        _zero_fill(off_s, cnt_s, nu_s, xs_out, zbuf, zsem, wait=True)

    first_token = pl.program_id(0) * pay_ref.shape[0] * SUBLANES
    for wait in (False, True):
        _row_copies(pos_s, first_token, xs_out, pay_ref, sem, to_sorted=True, wait=wait)


def _dispatch_call(pos, off, cnt, n_used, payload, npad):
    t8, _, width = payload.shape
    tiles = ROW_TILE // SUBLANES
    return pl.pallas_call(
        _dispatch_kernel,
        grid_spec=pltpu.PrefetchScalarGridSpec(
            num_scalar_prefetch=4,
            grid=(t8 // tiles,),
            in_specs=[pl.BlockSpec((tiles, SUBLANES, width), lambda i, *_: (i, 0, 0))],
            out_specs=pl.BlockSpec(memory_space=pl.ANY),
            scratch_shapes=[pltpu.VMEM((EXPERT_TILE // 2, width), F32),
                            pltpu.SemaphoreType.DMA(()), pltpu.SemaphoreType.DMA(())],
        ),
        out_shape=jax.ShapeDtypeStruct((npad, width), F32),
        compiler_params=_params("arbitrary"),
        name="moe_dispatch",
    )(pos, off, cnt, n_used, payload)


def _moe_kernel(load_s, buf_s, ahead_s, ahead_buf_s, wait_s, nu_s, x_ref, wg_hbm, wu_hbm, wd_hbm, y_ref,
                wg_c, wu_c, wd_c, sem):
    i = pl.program_id(0)

    def copies(slot, expert, buf):
        return [pltpu.make_async_copy(hbm.at[expert], cache.at[slot, buf], sem.at[slot, buf])
                for hbm, cache in ((wg_hbm, wg_c), (wu_hbm, wu_c), (wd_hbm, wd_c))]

    for slot in range(2):
        @pl.when(i == 0)
        def _(slot=slot):
            for cp in copies(slot, load_s[slot, 0], buf_s[slot, 0]):
                cp.start()

        @pl.when(ahead_s[slot, i] >= 0)
        def _(slot=slot):
            for cp in copies(slot, ahead_s[slot, i], ahead_buf_s[slot, i]):
                cp.start()

        @pl.when(wait_s[slot, i] == 1)
        def _(slot=slot):
            for cp in copies(slot, 0, buf_s[slot, i]):
                cp.wait()

    @pl.when(i < nu_s[0])
    def _():
        x = x_ref[:, :D_MODEL].astype(BF16)
        gates = (x_ref[:, D_MODEL:D_MODEL + 1], x_ref[:, D_MODEL + 1:D_MODEL + 2])
        w = [[cache[slot, buf_s[slot, i]].astype(BF16) for cache in (wg_c, wu_c, wd_c)] for slot in range(2)]
        up = [(_dot(x, wg), _dot(x, wu)) for wg, wu, _ in w]
        act = [(hg * _sigmoid(hg) * hu).astype(BF16) for hg, hu in up]
        y_ref[...] = gates[0] * _dot(act[0], w[0][2]) + gates[1] * _dot(act[1], w[1][2])

    @pl.when(i >= nu_s[0])
    def _():
        y_ref[...] = jnp.zeros(y_ref.shape, F32)


def _moe_call(schedule, n_used, xs, wg, wu, wd):
    npad = xs.shape[0]
    te = EXPERT_TILE
    last = lambda i, *s: jnp.minimum(i, s[-1][0] - 1)
    hbm = pl.BlockSpec(memory_space=pl.ANY)
    return pl.pallas_call(
        _moe_kernel,
        grid_spec=pltpu.PrefetchScalarGridSpec(
            num_scalar_prefetch=len(schedule) + 1,
            grid=(npad // te,),
            in_specs=[pl.BlockSpec((te, PAYLOAD_W), lambda i, *s: (last(i, *s), 0)), hbm, hbm, hbm],
            out_specs=pl.BlockSpec((te, D_MODEL), lambda i, *s: (i, 0)),
            scratch_shapes=[pltpu.VMEM((2, 2, D_MODEL, D_EXPERT), F32),
                            pltpu.VMEM((2, 2, D_MODEL, D_EXPERT), F32),
                            pltpu.VMEM((2, 2, D_EXPERT, D_MODEL), F32),
                            pltpu.SemaphoreType.DMA((2, 2))],
        ),
        out_shape=jax.ShapeDtypeStruct((npad, D_MODEL), F32),
        compiler_params=_params("arbitrary"),
        name="moe_experts",
    )(*schedule, n_used, xs, wg, wu, wd)


def _combine_kernel(pos_s, ys_hbm, x1_ref, mod_ref, wf_ref, out_ref, ybuf, sem):
    x2 = _moe_residual(pos_s, ys_hbm, x1_ref, mod_ref, ybuf, sem)
    out_ref[0] = _rms(x2) * wf_ref[...]


def _combine_call(pos, ys, x1, mod, wf):
    b, s, d = x1.shape
    rows = ROW_TILE
    return pl.pallas_call(
        _combine_kernel,
        grid_spec=pltpu.PrefetchScalarGridSpec(
            num_scalar_prefetch=1,
            grid=(b, s // rows),
            in_specs=[pl.BlockSpec(memory_space=pl.ANY),
                      pl.BlockSpec((1, rows, d), lambda i, j, *_: (i, j, 0)),
                      pl.BlockSpec((1, ADA_CHUNKS, d), lambda i, j, *_: (i, 0, 0)),
                      pl.BlockSpec((1, d), lambda i, j, *_: (0, 0))],
            out_specs=pl.BlockSpec((1, rows, d), lambda i, j, *_: (i, j, 0)),
            scratch_shapes=[pltpu.VMEM((2, rows // SUBLANES, SUBLANES, d), F32),
                            pltpu.SemaphoreType.DMA((2,))],
        ),
        out_shape=jax.ShapeDtypeStruct((b, s, d), F32),
        compiler_params=_params("arbitrary", "arbitrary"),
        name="moe_combine",
    )(pos, ys, x1, mod, wf)


def _t5_bucket(n):
    max_exact = N_BUCKETS // 2
    large = max_exact + (np.log(np.maximum(n, 1) / max_exact)
                         / np.log(MAX_DISTANCE / max_exact)
                         * (N_BUCKETS - max_exact)).astype(np.int32)
    large = np.minimum(large, N_BUCKETS - 1)
    return np.where(n < max_exact, n, large).astype(np.int32)


def _bias_table(rel_bias):
    dist = np.arange(WINDOW)[:, None] + WINDOW - np.arange(2 * WINDOW)[None, :]
    bucket = _t5_bucket(np.maximum(dist, 0)).reshape(1, -1)
    onehot = (jnp.arange(N_BUCKETS, dtype=jnp.int32)[:, None] == jnp.asarray(bucket)).astype(F32)
    tab = jnp.dot(rel_bias.astype(F32).T, onehot, precision=lax.Precision.HIGHEST)
    tab = tab.reshape(A_HEADS, WINDOW, 2 * WINDOW)
    in_window = (dist >= 0) & (dist < WINDOW)
    has_key = np.arange(2 * WINDOW)[None, :] >= WINDOW
    return jnp.stack([jnp.where(jnp.asarray(in_window), tab, NEG_INF),
                      jnp.where(jnp.asarray(in_window & has_key), tab, NEG_INF)])


def _layout_w_in(w_in_l):
    pts = np.cumsum([0, M_QK_W, M_QK_W, M_V_W, M_V_W, M_HEADS, M_HEADS,
                     A_Q_W, A_KV_W, A_KV_W, D_MODEL, D_MODEL])
    wb = w_in_l.astype(BF16)
    col = lambda k: wb[:, pts[k]:pts[k + 1]]
    head = lambda h: wb[:, pts[6] + A_HEAD_DIM * h:pts[6] + A_HEAD_DIM * (h + 1)]
    w_if = jnp.concatenate([col(4), col(5)], axis=1)
    w_if_pad = jnp.pad(w_if, ((0, 0), (0, LANES - 2 * M_HEADS)))
    w = jnp.concatenate([col(0), col(2), col(3)] + [head(h) for h in A_HEAD_ORDER]
                        + [col(7), col(8), col(9), col(10), w_if_pad], axis=1)
    w_t = jnp.concatenate([col(1), w_if], axis=1).T
    return w, jnp.pad(w_t, ((0, -w_t.shape[0] % (2 * SUBLANES)), (0, 0)))


def _lookup(idx, table):
    hit = idx[:, None] == jnp.arange(table.shape[0], dtype=jnp.int32)[None, :]
    return jnp.sum(jnp.where(hit[:, :, None], table[None], 0), axis=1)


def _tile_tables(counts, n_tiles, layer):
    te = EXPERT_TILE
    tiles = (counts + te - 1) // te
    ends = jnp.cumsum(tiles)
    off = (ends - tiles) * te
    n_used = ends[-1]
    tile_idx = jnp.minimum(jnp.arange(n_tiles, dtype=jnp.int32), n_used - 1)
    tile_cls = jnp.sum((ends[None, :] <= tile_idx[:, None]).astype(jnp.int32), axis=1)
    tile_cls = jnp.minimum(tile_cls, N_CLASSES - 1)
    off_pad = jnp.pad(off, (0, CLASS_ROWS - N_CLASSES)).astype(jnp.int32)

    cls =jnp.arange(N_CLASSES, dtype=jnp.int32)
    nonempty = tiles > 0
    later = (cls[None, :] > cls[:, None]) & nonempty[None, :]
    nxt = jnp.min(jnp.where(later, cls[None, :], N_CLASSES), axis=1)
    earlier = (cls[None, :] < cls[:, None]) & nonempty[None, :]
    prv = jnp.max(jnp.where(earlier, cls[None, :], -1), axis=1)
    nxt_c = jnp.minimum(nxt, N_CLASSES - 1)
    e = jnp.asarray(np.stack([_CLASS_EA, _CLASS_EB], axis=1)) + layer * N_EXPERTS
    new = nonempty[:, None] & ((prv < 0)[:, None] | (e != _lookup(jnp.maximum(prv, 0), e)))
    new_i = new.astype(jnp.int32)
    buf = (jnp.cumsum(new_i, axis=0) + 1) % 2
    of_next = _lookup(nxt_c, jnp.concatenate([e, buf, new_i], axis=1))
    ahead = jnp.where((nxt < N_CLASSES)[:, None] & (of_next[:, 4:6] > 0), of_next[:, 0:2], -1)
    per_class = jnp.concatenate([e, buf, ahead, of_next[:, 2:4], new_i, (ends - tiles)[:, None]], axis=1)
    per_step = _lookup(tile_cls, per_class)
    steps = jnp.arange(n_tiles, dtype=jnp.int32)
    first_tile = ((steps < n_used) & (steps == per_step[:, 10]))[:, None]
    schedule = (per_step[:, 0:2], per_step[:, 2:4], jnp.where(first_tile, per_step[:, 4:6], -1),
                per_step[:, 6:8], (first_tile & (per_step[:, 8:10] > 0)).astype(jnp.int32))
    schedule = tuple(t.T.astype(jnp.int32) for t in schedule)
    return off_pad, schedule, n_used.reshape(1).astype(jnp.int32)


def kernel(x, c, w_ada, b_ada, w_norm1, w_in, conv_w, conv_b, b_igate, b_fgate, w_mnorm, sinks, rel_bias,
           w_br_m, w_br_a, w_out, w_norm2, w_router, router_bias, w_gate_e, w_up_e, w_down_e, w_final):
    b, s, d = x.shape
    depth = w_ada.shape[0]
    t = b * s
    n_tiles = t // EXPERT_TILE + N_CLASSES
    npad = n_tiles * EXPERT_TILE

    mod_all = _ada_call(jnp.pad(c, ((0, SUBLANES - b), (0, 0))), w_ada, b_ada)[:, :b]
    bias_tab = _bias_table(rel_bias)
    wrt = w_router.T
    wrh = wrt.astype(BF16)
    wrl = (wrt - wrh.astype(F32)).astype(BF16)
    rb = router_bias.reshape(N_EXPERTS, 1).astype(F32)
    row = lambda v: v.reshape(1, -1).astype(F32)

    wg_all, wu_all, wd_all = [w.reshape((-1,) + w.shape[2:]) for w in (w_gate_e, w_up_e, w_down_e)]

    moe = None
    for l in range(depth):
        mod = mod_all[l].reshape(b, ADA_CHUNKS, d)
        w_l, wift = _layout_w_in(w_in[l])
        gate_bias = jnp.concatenate([b_igate[l], b_fgate[l]]).astype(F32)
        bcol = jnp.pad(gate_bias, (0, LANES - 2 * M_HEADS)).reshape(1, LANES)
        brow = gate_bias.reshape(2 * M_HEADS, 1)

        proj_args = (mod, row(w_norm1[l]), w_l, wift, bcol, brow)
        if moe is None:
            q, kt, v, o, qa, kva, g, ifc, ifr = _inproj_call(x, *proj_args)
        else:
            x, q, kt, v, o, qa, kva, g, ifc, ifr = _combine_inproj_call(*moe, *proj_args)
        cw, cb = conv_w[l].astype(F32), conv_b[l].astype(F32)
        cwk = jnp.broadcast_to(cw[:, M_QK_W:, None], (M_CONV, M_QK_W, MLSTM_CHUNK))
        cbk = jnp.broadcast_to(cb[M_QK_W:, None], (M_QK_W, MLSTM_CHUNK))
        hm = _mlstm_call(q, kt, v, o, ifc, ifr, cw[:, :M_QK_W], row(cb[:M_QK_W]), cwk, cbk, row(w_mnorm[l]))
        ha = _swa_call(sinks[l].astype(F32), qa, kva, bias_tab)

        wba = jnp.concatenate([w_br_a[l][A_HEAD_DIM * h:A_HEAD_DIM * (h + 1)] for h in A_HEAD_ORDER])
        x1, payload, cls, rank, cnt = _post_call(
            hm, ha, g, x, mod, w_br_m[l].astype(BF16), wba.astype(BF16), w_out[l].astype(BF16),
            row(w_norm2[l]), wrh, wrl, rb)

        counts = cnt[:, 0].astype(jnp.int32)
        off, schedule, n_used = _tile_tables(counts[:N_CLASSES], n_tiles, l)
        hit = cls.reshape(t, 1) == jnp.arange(CLASS_ROWS, dtype=jnp.int32)[None, :]
        pos = jnp.sum(jnp.where(hit, off[None, :], 0), axis=1) + rank.reshape(t)
        xs = _dispatch_call(pos, off, counts, n_used,
                            payload.reshape(t // SUBLANES, SUBLANES, PAYLOAD_W), npad)
        ys = _moe_call(schedule, n_used, xs, wg_all, wu_all, wd_all)
        moe = (pos, ys, x1, mod)
    return _combine_call(*moe, row(w_final))
```

```python
import numpy as np
import jax
import jax.numpy as jnp
from jax import lax
from jax.experimental import pallas as pl
from jax.experimental.pallas import tpu as pltpu

F32 = jnp.float32
BF16 = jnp.bfloat16

D_MODEL = 1024
M_HEADS = 4
M_DQK = 64
M_DV = 128
M_CONV = 4
M_QK_W = M_HEADS * M_DQK
M_V_W = M_HEADS * M_DV
A_HEADS = 8
A_KV_HEADS = 2
A_HEAD_DIM = 64
A_Q_W = A_HEADS * A_HEAD_DIM
A_KV_W = A_KV_HEADS * A_HEAD_DIM
WINDOW = 128
N_BUCKETS = 32
MAX_DISTANCE = 128
N_EXPERTS = 16
N_GROUPS = 4
EXPERTS_PER_GROUP = N_EXPERTS // N_GROUPS
D_EXPERT = 512
EPS = 1e-6
NEG_INF = -1e30

ADA_CHUNKS = 6
LANES = 128
SUBLANES = 8
MLSTM_CHUNK = 128
MLSTM_CHUNKS_PER_STEP = 8
SWA_BLOCKS = 8
TOKEN_TILE = 512
ROUTER_TILE = 1024
EXPERT_TILE = 256
ROW_TILE = 1024
PAIRS_PER_GROUP = 6
N_CLASSES = N_GROUPS * PAIRS_PER_GROUP
CLASS_ROWS = 32
PAYLOAD_W = D_MODEL + LANES
VMEM_LIMIT = 48 * 1024 * 1024

C_Q = 0
C_V = C_Q + M_QK_W
C_O = C_V + M_V_W
C_QA = C_O + M_V_W
C_KVA = C_QA + A_Q_W
C_G = C_KVA + 2 * A_KV_W
C_IF = C_G + 2 * D_MODEL
C_END = C_IF + LANES

A_HEAD_ORDER = (0, 4, 1, 5, 2, 6, 3, 7)

_PAIRS = [(a, b) for a in range(EXPERTS_PER_GROUP) for b in range(a + 1, EXPERTS_PER_GROUP)]
_CLASS_PAIRS = ((0, 1), (2, 1), (2, 0), (3, 0), (3, 1), (3, 2))
assert sorted(tuple(sorted(p)) for p in _CLASS_PAIRS) == _PAIRS
_CLASS_EA = np.array([g * EXPERTS_PER_GROUP + a for g in range(N_GROUPS) for a, _ in _CLASS_PAIRS], np.int32)
_CLASS_EB = np.array([g * EXPERTS_PER_GROUP + b for g in range(N_GROUPS) for _, b in _CLASS_PAIRS], np.int32)

_NT = (((1,), (1,)), ((), ()))


def _sigmoid(x):
    return 0.5 * jnp.tanh(0.5 * x) + 0.5


def _log_sigmoid(x):
    return -(jnp.maximum(-x, 0.0) + jnp.log1p(jnp.exp(-jnp.abs(x))))


def _dot(a, b, dims=None):
    if dims is None:
        return jnp.dot(a, b, preferred_element_type=F32)
    return lax.dot_general(a, b, dims, preferred_element_type=F32)


def _split2(a):
    hi = a.astype(BF16)
    lo = (a - hi.astype(F32)).astype(BF16)
    return hi, lo


def _split3(a):
    hi = a.astype(BF16)
    r = a - hi.astype(F32)
    mid = r.astype(BF16)
    lo = (r - mid.astype(F32)).astype(BF16)
    return hi, mid, lo


def _dot_hi(a, b, dims=None):
    ah, al = _split2(a)
    bh, bl = _split2(b)
    return _dot(ah, bh, dims) + _dot(ah, bl, dims) + _dot(al, bh, dims)


def _rms(x):
    return x * lax.rsqrt(jnp.mean(x * x, axis=-1, keepdims=True) + EPS)


def _params(*sem):
    return pltpu.CompilerParams(dimension_semantics=sem, vmem_limit_bytes=VMEM_LIMIT)


def _ada_kernel(c_ref, w_ref, b_ref, o_ref):
    c = c_ref[...]
    cond = c * _sigmoid(c)
    o_ref[0] = _dot_hi(cond, w_ref[0]) + b_ref[0]


def _ada_call(c8, w_ada, b_ada):
    depth = w_ada.shape[0]
    return pl.pallas_call(
        _ada_kernel,
        grid=(depth, ADA_CHUNKS),
        in_specs=[
            pl.BlockSpec((SUBLANES, D_MODEL), lambda l, j: (0, 0)),
            pl.BlockSpec((1, D_MODEL, D_MODEL), lambda l, j: (l, 0, j)),
            pl.BlockSpec((1, 1, D_MODEL), lambda l, j: (l, 0, j)),
        ],
        out_specs=pl.BlockSpec((1, SUBLANES, D_MODEL), lambda l, j: (l, 0, j)),
        out_shape=jax.ShapeDtypeStruct((depth, SUBLANES, ADA_CHUNKS * D_MODEL), F32),
        compiler_params=_params("arbitrary", "arbitrary"),
        name="adaln_mod",
    )(c8, w_ada, b_ada.reshape(depth, 1, ADA_CHUNKS * D_MODEL))


def _project(x, mod_ref, wn_ref, w_ref, wift_ref, bcol_ref, brow_ref, outs):
    q_ref, kt_ref, v_ref, o_ref, qa_ref, kva_ref, g_ref, ifc_ref, ifr_ref = outs
    h = _rms(x) * (wn_ref[...] * (1.0 + mod_ref[0, 1:2, :])) + mod_ref[0, 0:1, :]
    hb = h.astype(BF16)

    def seg(a, b):
        return _dot(hb, w_ref[:, a:b])

    q_ref[0] = seg(C_Q, C_V).astype(BF16)
    v_ref[0] = seg(C_V, C_O).astype(BF16)
    o_ref[0] = seg(C_O, C_QA).astype(BF16)
    qa_ref[0] = seg(C_QA, C_KVA).astype(BF16)
    kva_ref[0] = seg(C_KVA, C_G).astype(BF16)
    g_ref[0] = seg(C_G, C_IF).astype(BF16)
    zc = seg(C_IF, C_END) + bcol_ref[...]
    lane = lax.broadcasted_iota(jnp.int32, zc.shape, 1)
    ifc_ref[0] = jnp.where((lane >= M_HEADS) & (lane < 2 * M_HEADS), _log_sigmoid(zc), zc)
    zt = _dot(wift_ref[...], hb, _NT)
    kt_ref[0] = zt[:M_QK_W].astype(BF16)
    zr = zt[M_QK_W:M_QK_W + 2 * M_HEADS] + brow_ref[...]
    row = lax.broadcasted_iota(jnp.int32, zr.shape, 0)
    ifr_ref[0] = jnp.where(row >= M_HEADS, _log_sigmoid(zr), zr)


def _inproj_kernel(x_ref, mod_ref, wn_ref, w_ref, wift_ref, bcol_ref, brow_ref, *outs):
    _project(x_ref[0], mod_ref, wn_ref, w_ref, wift_ref, bcol_ref, brow_ref, outs)


def _row_copies(pos_s, first_token, sorted_hbm, tiles, sem, *, to_sorted, wait):
    def copy(i, k, pos):
        pair = (tiles.at[i, pl.ds(k, 1), :], sorted_hbm.at[pl.ds(pos, 1), :])
        return pltpu.make_async_copy(*(pair if to_sorted else pair[::-1]), sem)

    if wait:
        def body(i, carry):
            for k in range(SUBLANES):
                copy(0, k, 0).wait()
            return carry

        lax.fori_loop(0, tiles.shape[0], body, 0, unroll=2)
        return
    for i in range(tiles.shape[0]):
        for k in range(SUBLANES):
            copy(i, k, pos_s[first_token + i * SUBLANES + k]).start(priority=k % 2)


def _moe_residual(pos_s, ys_hbm, x1_ref, mod_ref, ybuf, sem):
    rows = x1_ref.shape[1]
    step = pl.program_id(0) * pl.num_programs(1) + pl.program_id(1)
    n_steps = pl.num_programs(0) * pl.num_programs(1)
    slot = lax.rem(step, 2)

    def fetch(first_token, slot_, wait):
        _row_copies(pos_s, first_token, ys_hbm, ybuf.at[slot_], sem.at[slot_], to_sorted=False, wait=wait)

    @pl.when(step == 0)
    def _():
        fetch(0, 0, False)

    @pl.when(step + 1 < n_steps)
    def _():
        fetch((step + 1) * rows, 1 - slot, False)

    fetch(0, slot, True)
    y = ybuf[slot].reshape(rows, ybuf.shape[-1])
    return x1_ref[0] + mod_ref[0, 5:6, :] * y


def _combine_inproj_kernel(pos_s, ys_hbm, x1_ref, modp_ref, mod_ref, wn_ref, w_ref,
                           wift_ref, bcol_ref, brow_ref, xo_ref, *rest):
    outs, (ybuf, sem) = rest[:-2], rest[-2:]
    x = _moe_residual(pos_s, ys_hbm, x1_ref, modp_ref, ybuf, sem)
    xo_ref[0] = x
    _project(x, mod_ref, wn_ref, w_ref, wift_ref, bcol_ref, brow_ref, outs)


def _inproj_specs(b, s, tm):
    tok = lambda w_: pl.BlockSpec((1, tm, w_), lambda i, j, *_: (i, j, 0))
    sds = lambda w_, dt: jax.ShapeDtypeStruct((b, s, w_), dt)
    rows = lambda n: pl.BlockSpec((1, n, tm), lambda i, j, *_: (i, 0, j))
    out_specs = [tok(M_QK_W), rows(M_QK_W), tok(M_V_W), tok(M_V_W), tok(A_Q_W), tok(2 * A_KV_W),
                 tok(2 * D_MODEL), tok(LANES), rows(2 * M_HEADS)]
    out_shape = [sds(M_QK_W, BF16), jax.ShapeDtypeStruct((b, M_QK_W, s), BF16), sds(M_V_W, BF16),
                 sds(M_V_W, BF16), sds(A_Q_W, BF16), sds(2 * A_KV_W, BF16), sds(2 * D_MODEL, BF16),
                 sds(LANES, F32), jax.ShapeDtypeStruct((b, 2 * M_HEADS, s), F32)]
    return tok, out_specs, out_shape


def _inproj_call(x, mod, wn, w, wift, bcol, brow):
    b, s, d = x.shape
    tm = 2 * TOKEN_TILE
    tok, out_specs, out_shape = _inproj_specs(b, s, tm)
    full = lambda a: pl.BlockSpec(a.shape, lambda i, j: (0,) * a.ndim, pipeline_mode=pl.Buffered(1))
    return pl.pallas_call(
        _inproj_kernel,
        grid=(b, s // tm),
        in_specs=[tok(d), pl.BlockSpec((1, ADA_CHUNKS, d), lambda i, j: (i, 0, 0)), full(wn), full(w),
                  full(wift), full(bcol), full(brow)],
        out_specs=out_specs,
        out_shape=out_shape,
        compiler_params=_params("arbitrary", "arbitrary"),
        name="norm1_inproj",
    )(x, mod, wn, w, wift, bcol, brow)


def _combine_inproj_call(pos, ys, x1, mod_prev, mod, wn, w, wift, bcol, brow):
    b, s, d = x1.shape
    tm = TOKEN_TILE
    tok, out_specs, out_shape = _inproj_specs(b, s, tm)
    full = lambda a: pl.BlockSpec(a.shape, lambda i, j, *_: (0,) * a.ndim)
    modspec = pl.BlockSpec((1, ADA_CHUNKS, d), lambda i, j, *_: (i, 0, 0))
    return pl.pallas_call(
        _combine_inproj_kernel,
        grid_spec=pltpu.PrefetchScalarGridSpec(
            num_scalar_prefetch=1,
            grid=(b, s // tm),
            in_specs=[pl.BlockSpec(memory_space=pl.ANY), tok(d), modspec, modspec, full(wn), full(w),
                      full(wift), full(bcol), full(brow)],
            out_specs=[tok(d)] + out_specs,
            scratch_shapes=[pltpu.VMEM((2, tm // SUBLANES, SUBLANES, d), F32),
                            pltpu.SemaphoreType.DMA((2,))],
        ),
        out_shape=[jax.ShapeDtypeStruct((b, s, d), F32)] + out_shape,
        compiler_params=_params("arbitrary", "arbitrary"),
        name="combine_norm1_inproj",
    )(pos, ys, x1, mod_prev, mod, wn, w, wift, bcol, brow)


def _mlstm_kernel(q_ref, kt_ref, v_ref, o_ref, ifc_ref, ifr_ref, cwq_ref, cbq_ref, cwk_ref, cbk_ref, wn_ref,
                  out_ref, qbuf, kbuf, cstate, mstate):
    @pl.when(pl.program_id(1) == 0)
    def _():
        qbuf[...] = jnp.zeros(qbuf.shape, qbuf.dtype)
        kbuf[...] = jnp.zeros(kbuf.shape, kbuf.dtype)
        cstate[...] = jnp.zeros(cstate.shape, F32)
        mstate[...] = jnp.zeros(mstate.shape, F32)

    heads = range(M_HEADS)
    carry = (qbuf[...], kbuf[...], [cstate[h] for h in heads], [mstate[h][:, 0:1] for h in heads])
    for c in range(q_ref.shape[1] // MLSTM_CHUNK):
        carry = _mlstm_chunk(c, carry, q_ref, kt_ref, v_ref, o_ref, ifc_ref, ifr_ref, cwq_ref, cbq_ref,
                             cwk_ref, cbk_ref, wn_ref, out_ref)
    qbuf[...], kbuf[...] = carry[0], carry[1]
    for h in heads:
        cstate[h] = carry[2][h]
        mstate[h] = jnp.broadcast_to(carry[3][h], (1, LANES))


def _mlstm_chunk(c, carry, q_ref, kt_ref, v_ref, o_ref, ifc_ref, ifr_ref, cwq_ref, cbq_ref, cwk_ref, cbk_ref,
                 wn_ref, out_ref):
    L = MLSTM_CHUNK
    span = slice(L * c, L * (c + 1))
    q_tail, k_prev, cst, m_prev = carry
    pad = q_tail.shape[0]
    heads = range(M_HEADS)

    xq = q_ref[0, span, :]
    q_all = jnp.concatenate([q_tail, xq], axis=0)
    xk = kt_ref[0, :, span]
    k_all = jnp.concatenate([k_prev, xk], axis=1)
    r_s = lax.broadcasted_iota(jnp.int32, (L, L + pad), 0)
    c_s = lax.broadcasted_iota(jnp.int32, (L, L + pad), 1)
    r_t = lax.broadcasted_iota(jnp.int32, (2 * L, L), 0)
    c_t = lax.broadcasted_iota(jnp.int32, (2 * L, L), 1)
    acc_q = cbq_ref[...] + cwq_ref[M_CONV - 1:M_CONV, :] * xq.astype(F32)
    acc_k = cbk_ref[...] + cwk_ref[M_CONV - 1] * xk.astype(F32)
    for j in range(M_CONV - 1):
        back = M_CONV - 1 - j
        shift_q = jnp.where(c_s == r_s + (pad - back), 1.0, 0.0).astype(BF16)
        acc_q = acc_q + cwq_ref[j:j + 1, :] * _dot(shift_q, q_all)
        shift_k = jnp.where(r_t == c_t + (L - back), 1.0, 0.0).astype(BF16)
        acc_k = acc_k + cwk_ref[j] * _dot(k_all, shift_k)
    q = acc_q * _sigmoid(acc_q) * (M_DQK ** -0.5)
    kf = acc_k * _sigmoid(acc_k)
    kb = kf.astype(BF16)
    vb = v_ref[0, span, :]

    r_i = lax.broadcasted_iota(jnp.int32, (L, L), 0)
    c_i = lax.broadcasted_iota(jnp.int32, (L, L), 1)
    causal = c_i <= r_i
    tril = jnp.where(causal, 1.0, 0.0).astype(BF16)
    triu = jnp.where(r_i <= c_i, 1.0, 0.0).astype(BF16)

    icol = ifc_ref[0, span, :]
    irow = ifr_ref[0, :, span]
    ch, cm, cl = _split3(icol)
    bcum_col = _dot(tril, ch) + _dot(tril, cm) + _dot(tril, cl)
    rh, rm, rl = _split3(irow)
    bcum_row = _dot(rh, triu) + _dot(rm, triu) + _dot(rl, triu)

    lane = lax.broadcasted_iota(jnp.int32, (1, LANES), 1)
    lo_half = lane < M_DQK
    hi_half = jnp.logical_not(lo_half)
    ones = jnp.ones((L, LANES), BF16)
    pair = lambda h: slice(LANES * (h // 2), LANES * (h // 2 + 1))

    qmask =[jnp.where(lo_half if h % 2 == 0 else hi_half, q[:, pair(h)], 0.0) for h in heads]
    vaug = [jnp.concatenate([vb[:, M_DV * h:M_DV * (h + 1)], ones], axis=1) for h in heads]
    bc_col = [bcum_col[:, M_HEADS + h:M_HEADS + h + 1] for h in heads]
    bc_row = [bcum_row[M_HEADS + h:M_HEADS + h + 1, :] for h in heads]
    i_row = [irow[h:h + 1, :] for h in heads]
    b_last = [bc_row[h][:, L - 1:L] for h in heads]

    s = [_dot(qmask[h].astype(BF16), kb[pair(h), :]) for h in heads]

    dlog = [jnp.where(causal, bc_col[h] - bc_row[h] + i_row[h], NEG_INF) for h in heads]
    m_inter = [bc_col[h] + m_prev[h] for h in heads]
    m_row = [jnp.maximum(m_inter[h], jnp.max(dlog[h], axis=-1, keepdims=True)) for h in heads]
    sc = [(s[h] * jnp.exp(dlog[h] - m_row[h])).astype(BF16) for h in heads]
    qi = [(jnp.exp(m_inter[h] - m_row[h]) * qmask[h]).astype(BF16) for h in heads]

    num = [_dot(sc[h], vaug[h]) + _dot(qi[h], cst[h].astype(BF16)) for h in heads]
    for h in heads:
        den = num[h][:, M_DV:]
        hh = num[h][:, :M_DV] / jnp.maximum(jnp.abs(den), jnp.exp(-m_row[h]))
        hn = _rms(hh) * wn_ref[:, M_DV * h:M_DV * (h + 1)]
        og = _sigmoid(o_ref[0, span, M_DV * h:M_DV * (h + 1)].astype(F32))
        out_ref[0, span, M_DV * h:M_DV * (h + 1)] = (hn * og).astype(BF16)

    w_state = [b_last[h] - bc_row[h] + i_row[h] for h in heads]
    m_loc = [jnp.max(w_state[h], axis=-1, keepdims=True) for h in heads]
    ak = [(jnp.exp(w_state[h] - m_loc[h]) * kf[pair(h), :]).astype(BF16) for h in heads]
    c_loc = [_dot(ak[h], vaug[h]) for h in heads]
    m_new = [jnp.maximum(b_last[h] + m_prev[h], m_loc[h]) for h in heads]
    c_new = [jnp.exp(b_last[h] + m_prev[h] - m_new[h]) * cst[h] + jnp.exp(m_loc[h] - m_new[h]) * c_loc[h]
             for h in heads]
    return xq[L - pad:L, :], xk, c_new, m_new


def _mlstm_call(q, kt, v, o, ifc, ifr, cwq, cbq, cwk, cbk, w_mnorm):
    b, s, _ = q.shape
    L = MLSTM_CHUNK
    span = MLSTM_CHUNKS_PER_STEP * L
    tok = lambda w_: pl.BlockSpec((1, span, w_), lambda i, j: (i, j, 0))
    rows = lambda n: pl.BlockSpec((1, n, span), lambda i, j: (i, 0, j))
    full = lambda a: pl.BlockSpec(a.shape, lambda i, j: (0,) * a.ndim)
    return pl.pallas_call(
        _mlstm_kernel,
        grid=(b, s // span),
        in_specs=[tok(M_QK_W), rows(M_QK_W), tok(M_V_W), tok(M_V_W), tok(LANES), rows(2 * M_HEADS),
                  full(cwq), full(cbq), full(cwk), full(cbk), full(w_mnorm)],
        out_specs=tok(M_V_W),
        out_shape=jax.ShapeDtypeStruct((b, s, M_V_W), BF16),
        scratch_shapes=[pltpu.VMEM((2 * SUBLANES, M_QK_W), BF16),
                        pltpu.VMEM((M_QK_W, L), BF16),
                        pltpu.VMEM((M_HEADS, LANES, 2 * M_DV), F32),
                        pltpu.VMEM((M_HEADS, 1, LANES), F32)],
        compiler_params=_params("arbitrary", "arbitrary"),
        name="mlstm",
    )(q, kt, v, o, ifc, ifr, cwq, cbq, cwk, cbk, w_mnorm)


def _swa_kernel(sink_ref, q_ref, kvp_ref, kvc_ref, bias_ref, out_ref):
    W = WINDOW
    first = pl.program_id(1) == 0
    kv = [kvp_ref[0]] + [kvc_ref[0, W * u:W * (u + 1), :] for u in range(SWA_BLOCKS)]
    lane = lax.broadcasted_iota(jnp.int32, (1, LANES), 1)
    lo_half = lane < A_HEAD_DIM
    hi_half = jnp.logical_not(lo_half)
    keys, v_half = [], []
    for u in range(SWA_BLOCKS):
        keys.append(jnp.concatenate([kv[u][:, :A_KV_W], kv[u + 1][:, :A_KV_W]], axis=0))
        vals = jnp.concatenate([kv[u][:, A_KV_W:], kv[u + 1][:, A_KV_W:]], axis=0)
        zero = jnp.zeros_like(vals)
        v_half.append((jnp.where(lo_half, vals, zero), jnp.where(hi_half, vals, zero)))
    table = [jnp.where(first, 1, 0)] + [0] * (SWA_BLOCKS - 1)

    tiles = range(A_HEADS // 2)
    slots = [(u, j, p) for u in range(SWA_BLOCKS) for j in tiles for p in range(2)]
    head = {ujp: A_HEAD_ORDER[2 * ujp[1] + ujp[2]] for ujp in slots}
    scale = jnp.asarray(A_HEAD_DIM ** -0.5, BF16)
    qt = {(u, j): q_ref[0, W * u:W * (u + 1), LANES * j:LANES * (j + 1)] * scale
          for u in range(SWA_BLOCKS) for j in tiles}
    qm = {(u, j, p): jnp.where(lo_half if p == 0 else hi_half, qt[u, j], jnp.zeros_like(qt[u, j]))
          for u, j, p in slots}
    s = {ujp: _dot(qm[ujp], keys[ujp[0]], _NT) + bias_ref[table[ujp[0]], head[ujp]] for ujp in slots}
    m = {ujp: jnp.maximum(jnp.max(s[ujp], axis=-1, keepdims=True), sink_ref[head[ujp]]) for ujp in slots}
    e = {ujp: jnp.exp(s[ujp] - m[ujp]) for ujp in slots}
    denom = {ujp: jnp.sum(e[ujp], axis=-1, keepdims=True) + jnp.exp(sink_ref[head[ujp]] - m[ujp])
             for ujp in slots}
    pv = {(u, j, p): _dot(e[u, j, p].astype(BF16), v_half[u][p]) for u, j, p in slots}
    for u in range(SWA_BLOCKS):
        for j in tiles:
            out = pv[u, j, 0] / denom[u, j, 0] + pv[u, j, 1] / denom[u, j, 1]
            out_ref[0, W * u:W * (u + 1), LANES * j:LANES * (j + 1)] = out.astype(BF16)


def _swa_call(sinks, qa, kva, bias):
    b, s, _ = qa.shape
    W = WINDOW
    rows = SWA_BLOCKS * W
    return pl.pallas_call(
        _swa_kernel,
        grid=(b, s // rows),
        in_specs=[pl.BlockSpec(memory_space=pltpu.SMEM),
                  pl.BlockSpec((1, rows, A_Q_W), lambda i, j: (i, j, 0)),
                  pl.BlockSpec((1, W, 2 * A_KV_W), lambda i, j: (i, jnp.maximum(SWA_BLOCKS * j - 1, 0), 0)),
                  pl.BlockSpec((1, rows, 2 * A_KV_W), lambda i, j: (i, j, 0)),
                  pl.BlockSpec(bias.shape, lambda i, j: (0,) * bias.ndim)],
        out_specs=pl.BlockSpec((1, rows, A_Q_W), lambda i, j: (i, j, 0)),
        out_shape=jax.ShapeDtypeStruct((b, s, A_Q_W), BF16),
        compiler_params=_params("arbitrary", "arbitrary"),
        name="swa",
    )(sinks, qa, kva, kva, bias)


def _post_kernel(hm_ref, ha_ref, g_ref, x_ref, mod_ref, wbm_ref, wba_ref, wo_ref, wn2_ref,
                 wrh_ref, wrl_ref, rb_ref,
                 x1_ref, pay_ref, cls_ref, rank_ref, cnt_ref, carry):
    tm = x_ref.shape[1]

    @pl.when((pl.program_id(0) == 0) & (pl.program_id(1) == 0))
    def _():
        carry[...] = jnp.zeros(carry.shape, F32)

    g = g_ref[0]
    pm = _dot(hm_ref[0], wbm_ref[...])
    pa = _dot(ha_ref[0], wba_ref[...])
    merged = (_sigmoid(g[:, :D_MODEL]) * pm.astype(BF16)
              + _sigmoid(g[:, D_MODEL:]) * pa.astype(BF16))
    mo = _dot(merged, wo_ref[...])
    x1 = x_ref[0] + mod_ref[0, 2:3, :] * mo
    x1_ref[0] = x1
    h2 = _rms(x1) * (wn2_ref[...] * (1.0 + mod_ref[0, 4:5, :])) + mod_ref[0, 3:4, :]
    pay_ref[0, :, :D_MODEL] = h2

    hh, hl = _split2(h2)
    wrh = wrh_ref[...]
    logits = _dot(wrh, hh, _NT) + _dot(wrh, hl, _NT) + _dot(wrl_ref[...], hh, _NT)
    scores = 1.0 / (1.0 + jnp.exp(-logits))
    sel = scores + rb_ref[...]

    def row(a, r):
        return a[r:r + 1, :]

    grp = []
    for gi in range(N_GROUPS):
        v = [row(sel, gi * EXPERTS_PER_GROUP + k) for k in range(EXPERTS_PER_GROUP)]
        best = v[0] + v[1]
        for a, b in _PAIRS[1:]:
            best = jnp.maximum(best, v[a] + v[b])
        grp.append(best)
    gbest = grp[0]
    gsel = jnp.zeros_like(gbest)
    for gi in range(1, N_GROUPS):
        take = grp[gi] > gbest
        gbest = jnp.where(take, grp[gi], gbest)
        gsel = jnp.where(take, float(gi), gsel)

    sv, gv = [], []
    for k in range(EXPERTS_PER_GROUP):
        s_k = row(sel, k)
        g_k = row(scores, k)
        for gi in range(1, N_GROUPS):
            hit = gsel == float(gi)
            s_k = jnp.where(hit, row(sel, gi * EXPERTS_PER_GROUP + k), s_k)
            g_k = jnp.where(hit, row(scores, gi * EXPERTS_PER_GROUP + k), g_k)
        sv.append(s_k)
        gv.append(g_k)

    def argmax4(vals):
        bv, bi = vals[0], jnp.zeros_like(vals[0])
        for k in range(1, EXPERTS_PER_GROUP):
            take = vals[k] > bv
            bv = jnp.where(take, vals[k], bv)
            bi = jnp.where(take, float(k), bi)
        return bi

    i1 = argmax4(sv)
    i2 = argmax4([jnp.where(i1 == float(k), -jnp.inf, sv[k]) for k in range(EXPERTS_PER_GROUP)])
    w1 = jnp.zeros_like(i1)
    w2 = jnp.zeros_like(i1)
    for k in range(EXPERTS_PER_GROUP):
        w1 = jnp.where(i1 == float(k), gv[k], w1)
        w2 = jnp.where(i2 == float(k), gv[k], w2)
    wsum = w1 + w2
    w1 = w1 / wsum
    w2 = w2 / wsum
    code = i1 * float(EXPERTS_PER_GROUP) + i2
    pair_idx = jnp.zeros_like(code)
    gate_a, gate_b = w1, w2
    for k, (a, b) in enumerate(_CLASS_PAIRS):
        fwd = code == float(a * EXPERTS_PER_GROUP + b)
        rev = code == float(b * EXPERTS_PER_GROUP + a)
        pair_idx = jnp.where(fwd, float(k), jnp.where(rev, float(k), pair_idx))
        gate_a = jnp.where(rev, w2, gate_a)
        gate_b = jnp.where(rev, w1, gate_b)
    cls_f = gsel * float(PAIRS_PER_GROUP) + pair_idx
    cls_ref[0] = cls_f.astype(jnp.int32)

    grow = lax.broadcasted_iota(jnp.int32, (LANES, tm), 0)
    gmat = jnp.where(grow == 0, gate_a, jnp.where(grow == 1, gate_b, 0.0))
    pay_ref[0, :, D_MODEL:] = gmat.T

    crow = lax.broadcasted_iota(jnp.int32, (CLASS_ROWS, tm), 0).astype(F32)
    onehot = crow == cls_f
    r_i = lax.broadcasted_iota(jnp.int32, (tm, tm), 0)
    c_i = lax.broadcasted_iota(jnp.int32, (tm, tm), 1)
    upper = jnp.where(r_i <= c_i, 1.0, 0.0).astype(BF16)
    cum = _dot(jnp.where(onehot, 1.0, 0.0).astype(BF16), upper)
    before = carry[:, 0:1]
    rank = jnp.sum(jnp.where(onehot, cum - 1.0 + before, 0.0), axis=0, keepdims=True)
    rank_ref[0] = rank.astype(jnp.int32)
    total = before + cum[:, tm - 1:tm]
    carry[...] = jnp.broadcast_to(total, carry.shape)
    cnt_ref[...] = jnp.broadcast_to(total, cnt_ref.shape)


def _post_call(hm, ha, g, x, mod, wbm, wba, wo, wn2, wrh, wrl, rb):
    b, s, d = x.shape
    tm = ROUTER_TILE
    tok = lambda w_: pl.BlockSpec((1, tm, w_), lambda i, j: (i, j, 0))
    full = lambda a: pl.BlockSpec(a.shape, lambda i, j: (0,) * a.ndim)
    lanes = pl.BlockSpec((1, 1, tm), lambda i, j: (i, 0, j))
    return pl.pallas_call(
        _post_kernel,
        grid=(b, s // tm),
        in_specs=[tok(M_V_W), tok(A_Q_W), tok(2 * d), tok(d),
                  pl.BlockSpec((1, ADA_CHUNKS, d), lambda i, j: (i, 0, 0)),
                  full(wbm), full(wba), full(wo), full(wn2), full(wrh), full(wrl), full(rb)],
        out_specs=[tok(d), tok(PAYLOAD_W), lanes, lanes,
                   pl.BlockSpec((CLASS_ROWS, LANES), lambda i, j: (0, 0))],
        out_shape=[jax.ShapeDtypeStruct((b, s, d), F32),
                   jax.ShapeDtypeStruct((b, s, PAYLOAD_W), F32),
                   jax.ShapeDtypeStruct((b, 1, s), jnp.int32),
                   jax.ShapeDtypeStruct((b, 1, s), jnp.int32),
                   jax.ShapeDtypeStruct((CLASS_ROWS, LANES), F32)],
        scratch_shapes=[pltpu.VMEM((CLASS_ROWS, LANES), F32)],
        compiler_params=_params("arbitrary", "arbitrary"),
        name="merge_outproj_router",
    )(hm, ha, g, x, mod, wbm, wba, wo, wn2, wrh, wrl, rb)


def _zero_fill(off_s, cnt_s, nu_s, xs_out, zbuf, zsem, *, wait):
    chunk = zbuf.shape[0]

    def zero_rows(first, n):
        cp = pltpu.make_async_copy(zbuf.at[pl.ds(0, n), :], xs_out.at[pl.ds(first, n), :], zsem)
        cp.wait() if wait else cp.start()

    def per_class(c, carry):
        n = cnt_s[c]
        fill = (-n) & (EXPERT_TILE - 1)
        head = fill & (SUBLANES - 1)
        for k in range(SUBLANES - 1):
            pl.when(k < head)(lambda k=k: zero_rows(off_s[c] + n + k, 1))
        cur = off_s[c] + n + head
        p = chunk
        while p >= SUBLANES:
            pl.when((fill & p) != 0)(lambda cur=cur, p=p: zero_rows(pl.multiple_of(cur, SUBLANES), p))
            cur = cur + (fill & p)
            p //= 2
        return carry

    lax.fori_loop(0, N_CLASSES, per_class, 0)

    def per_chunk(r, carry):
        zero_rows(pl.multiple_of(r * chunk, SUBLANES), chunk)
        return carry

    per_tile = EXPERT_TILE // chunk
    lax.fori_loop(nu_s[0] * per_tile, (xs_out.shape[0] // EXPERT_TILE) * per_tile, per_chunk, 0)


def _dispatch_kernel(pos_s, off_s, cnt_s, nu_s, pay_ref, xs_out, zbuf, sem, zsem):
    @pl.when(pl.program_id(0) == 0)
    def _():
        zbuf[...] = jnp.zeros(zbuf.shape, F32)
        _zero_fill(off_s, cnt_s, nu_s, xs_out, zbuf, zsem, wait=False)
        _zero_fill(off_s, cnt_s, nu_s, xs_out, zbuf, zsem, wait=True)

    first_token = pl.program_id(0) * pay_ref.shape[0] * SUBLANES
    for wait in (False, True):
        _row_copies(pos_s, first_token, xs_out, pay_ref, sem, to_sorted=True, wait=wait)


def _dispatch_call(pos, off, cnt, n_used, payload, npad):
    t8, _, width = payload.shape
    tiles = ROW_TILE // SUBLANES
    return pl.pallas_call(
        _dispatch_kernel,
        grid_spec=pltpu.PrefetchScalarGridSpec(
            num_scalar_prefetch=4,
            grid=(t8 // tiles,),
            in_specs=[pl.BlockSpec((tiles, SUBLANES, width), lambda i, *_: (i, 0, 0))],
            out_specs=pl.BlockSpec(memory_space=pl.ANY),
            scratch_shapes=[pltpu.VMEM((EXPERT_TILE // 2, width), F32),
                            pltpu.SemaphoreType.DMA(()), pltpu.SemaphoreType.DMA(())],
        ),
        out_shape=jax.ShapeDtypeStruct((npad, width), F32),
        compiler_params=_params("arbitrary"),
        name="moe_dispatch",
    )(pos, off, cnt, n_used, payload)


def _moe_kernel(load_s, buf_s, ahead_s, ahead_buf_s, wait_s, nu_s, x_ref, wg_hbm, wu_hbm, wd_hbm, y_ref,
                wg_c, wu_c, wd_c, sem):
    i = pl.program_id(0)

    def copies(slot, expert, buf):
        return [pltpu.make_async_copy(hbm.at[expert], cache.at[slot, buf], sem.at[slot, buf])
                for hbm, cache in ((wg_hbm, wg_c), (wu_hbm, wu_c), (wd_hbm, wd_c))]

    for slot in range(2):
        @pl.when(i == 0)
        def _(slot=slot):
            for cp in copies(slot, load_s[slot, 0], buf_s[slot, 0]):
                cp.start()

        @pl.when(ahead_s[slot, i] >= 0)
        def _(slot=slot):
            for cp in copies(slot, ahead_s[slot, i], ahead_buf_s[slot, i]):
                cp.start()

        @pl.when(wait_s[slot, i] == 1)
        def _(slot=slot):
            for cp in copies(slot, 0, buf_s[slot, i]):
                cp.wait()

    @pl.when(i < nu_s[0])
    def _():
        x = x_ref[:, :D_MODEL].astype(BF16)
        gates = (x_ref[:, D_MODEL:D_MODEL + 1], x_ref[:, D_MODEL + 1:D_MODEL + 2])
        w = [[cache[slot, buf_s[slot, i]].astype(BF16) for cache in (wg_c, wu_c, wd_c)] for slot in range(2)]
        up = [(_dot(x, wg), _dot(x, wu)) for wg, wu, _ in w]
        act = [(hg * _sigmoid(hg) * hu).astype(BF16) for hg, hu in up]
        y_ref[...] = gates[0] * _dot(act[0], w[0][2]) + gates[1] * _dot(act[1], w[1][2])

    @pl.when(i >= nu_s[0])
    def _():
        y_ref[...] = jnp.zeros(y_ref.shape, F32)


def _moe_call(schedule, n_used, xs, wg, wu, wd):
    npad = xs.shape[0]
    te = EXPERT_TILE
    last = lambda i, *s: jnp.minimum(i, s[-1][0] - 1)
    hbm = pl.BlockSpec(memory_space=pl.ANY)
    return pl.pallas_call(
        _moe_kernel,
        grid_spec=pltpu.PrefetchScalarGridSpec(
            num_scalar_prefetch=len(schedule) + 1,
            grid=(npad // te,),
            in_specs=[pl.BlockSpec((te, PAYLOAD_W), lambda i, *s: (last(i, *s), 0)), hbm, hbm, hbm],
            out_specs=pl.BlockSpec((te, D_MODEL), lambda i, *s: (i, 0)),
            scratch_shapes=[pltpu.VMEM((2, 2, D_MODEL, D_EXPERT), F32),
                            pltpu.VMEM((2, 2, D_MODEL, D_EXPERT), F32),
                            pltpu.VMEM((2, 2, D_EXPERT, D_MODEL), F32),
                            pltpu.SemaphoreType.DMA((2, 2))],
        ),
        out_shape=jax.ShapeDtypeStruct((npad, D_MODEL), F32),
        compiler_params=_params("arbitrary"),
        name="moe_experts",
    )(*schedule, n_used, xs, wg, wu, wd)


def _combine_kernel(pos_s, ys_hbm, x1_ref, mod_ref, wf_ref, out_ref, ybuf, sem):
    x2 = _moe_residual(pos_s, ys_hbm, x1_ref, mod_ref, ybuf, sem)
    out_ref[0] = _rms(x2) * wf_ref[...]


def _combine_call(pos, ys, x1, mod, wf):
    b, s, d = x1.shape
    rows = ROW_TILE
    return pl.pallas_call(
        _combine_kernel,
        grid_spec=pltpu.PrefetchScalarGridSpec(
            num_scalar_prefetch=1,
            grid=(b, s // rows),
            in_specs=[pl.BlockSpec(memory_space=pl.ANY),
                      pl.BlockSpec((1, rows, d), lambda i, j, *_: (i, j, 0)),
                      pl.BlockSpec((1, ADA_CHUNKS, d), lambda i, j, *_: (i, 0, 0)),
                      pl.BlockSpec((1, d), lambda i, j, *_: (0, 0))],
            out_specs=pl.BlockSpec((1, rows, d), lambda i, j, *_: (i, j, 0)),
            scratch_shapes=[pltpu.VMEM((2, rows // SUBLANES, SUBLANES, d), F32),
                            pltpu.SemaphoreType.DMA((2,))],
        ),
        out_shape=jax.ShapeDtypeStruct((b, s, d), F32),
        compiler_params=_params("arbitrary", "arbitrary"),
        name="moe_combine",
    )(pos, ys, x1, mod, wf)


def _t5_bucket(n):
    max_exact = N_BUCKETS // 2
    large = max_exact + (np.log(np.maximum(n, 1) / max_exact)
                         / np.log(MAX_DISTANCE / max_exact)
                         * (N_BUCKETS - max_exact)).astype(np.int32)
    large = np.minimum(large, N_BUCKETS - 1)
    return np.where(n < max_exact, n, large).astype(np.int32)


def _bias_table(rel_bias):
    dist = np.arange(WINDOW)[:, None] + WINDOW - np.arange(2 * WINDOW)[None, :]
    bucket = _t5_bucket(np.maximum(dist, 0)).reshape(1, -1)
    onehot = (jnp.arange(N_BUCKETS, dtype=jnp.int32)[:, None] == jnp.asarray(bucket)).astype(F32)
    tab = jnp.dot(rel_bias.astype(F32).T, onehot, precision=lax.Precision.HIGHEST)
    tab = tab.reshape(A_HEADS, WINDOW, 2 * WINDOW)
    in_window = (dist >= 0) & (dist < WINDOW)
    has_key = np.arange(2 * WINDOW)[None, :] >= WINDOW
    return jnp.stack([jnp.where(jnp.asarray(in_window), tab, NEG_INF),
                      jnp.where(jnp.asarray(in_window & has_key), tab, NEG_INF)])


def _layout_w_in(w_in_l):
    pts = np.cumsum([0, M_QK_W, M_QK_W, M_V_W, M_V_W, M_HEADS, M_HEADS,
                     A_Q_W, A_KV_W, A_KV_W, D_MODEL, D_MODEL])
    wb = w_in_l.astype(BF16)
    col = lambda k: wb[:, pts[k]:pts[k + 1]]
    head = lambda h: wb[:, pts[6] + A_HEAD_DIM * h:pts[6] + A_HEAD_DIM * (h + 1)]
    w_if = jnp.concatenate([col(4), col(5)], axis=1)
    w_if_pad = jnp.pad(w_if, ((0, 0), (0, LANES - 2 * M_HEADS)))
    w = jnp.concatenate([col(0), col(2), col(3)] + [head(h) for h in A_HEAD_ORDER]
                        + [col(7), col(8), col(9), col(10), w_if_pad], axis=1)
    w_t = jnp.concatenate([col(1), w_if], axis=1).T
    return w, jnp.pad(w_t, ((0, -w_t.shape[0] % (2 * SUBLANES)), (0, 0)))


def _lookup(idx, table):
    hit = idx[:, None] == jnp.arange(table.shape[0], dtype=jnp.int32)[None, :]
    return jnp.sum(jnp.where(hit[:, :, None], table[None], 0), axis=1)


def _tile_tables(counts, n_tiles, layer):
    te = EXPERT_TILE
    tiles = (counts + te - 1) // te
    ends = jnp.cumsum(tiles)
    off = (ends - tiles) * te
    n_used = ends[-1]
    tile_idx = jnp.minimum(jnp.arange(n_tiles, dtype=jnp.int32), n_used - 1)
    tile_cls = jnp.sum((ends[None, :] <= tile_idx[:, None]).astype(jnp.int32), axis=1)
    tile_cls = jnp.minimum(tile_cls, N_CLASSES - 1)
    off_pad = jnp.pad(off, (0, CLASS_ROWS - N_CLASSES)).astype(jnp.int32)

    cls =jnp.arange(N_CLASSES, dtype=jnp.int32)
    nonempty = tiles > 0
    later = (cls[None, :] > cls[:, None]) & nonempty[None, :]
    nxt = jnp.min(jnp.where(later, cls[None, :], N_CLASSES), axis=1)
    earlier = (cls[None, :] < cls[:, None]) & nonempty[None, :]
    prv = jnp.max(jnp.where(earlier, cls[None, :], -1), axis=1)
    nxt_c = jnp.minimum(nxt, N_CLASSES - 1)
    e = jnp.asarray(np.stack([_CLASS_EA, _CLASS_EB], axis=1)) + layer * N_EXPERTS
    new = nonempty[:, None] & ((prv < 0)[:, None] | (e != _lookup(jnp.maximum(prv, 0), e)))
    new_i = new.astype(jnp.int32)
    buf = (jnp.cumsum(new_i, axis=0) + 1) % 2
    of_next = _lookup(nxt_c, jnp.concatenate([e, buf, new_i], axis=1))
    ahead = jnp.where((nxt < N_CLASSES)[:, None] & (of_next[:, 4:6] > 0), of_next[:, 0:2], -1)
    per_class = jnp.concatenate([e, buf, ahead, of_next[:, 2:4], new_i, (ends - tiles)[:, None]], axis=1)
    per_step = _lookup(tile_cls, per_class)
    steps = jnp.arange(n_tiles, dtype=jnp.int32)
    first_tile = ((steps < n_used) & (steps == per_step[:, 10]))[:, None]
    schedule = (per_step[:, 0:2], per_step[:, 2:4], jnp.where(first_tile, per_step[:, 4:6], -1),
                per_step[:, 6:8], (first_tile & (per_step[:, 8:10] > 0)).astype(jnp.int32))
    schedule = tuple(t.T.astype(jnp.int32) for t in schedule)
    return off_pad, schedule, n_used.reshape(1).astype(jnp.int32)


def kernel(x, c, w_ada, b_ada, w_norm1, w_in, conv_w, conv_b, b_igate, b_fgate, w_mnorm, sinks, rel_bias,
           w_br_m, w_br_a, w_out, w_norm2, w_router, router_bias, w_gate_e, w_up_e, w_down_e, w_final):
    b, s, d = x.shape
    depth = w_ada.shape[0]
    t = b * s
    n_tiles = t // EXPERT_TILE + N_CLASSES
    npad = n_tiles * EXPERT_TILE

    mod_all = _ada_call(jnp.pad(c, ((0, SUBLANES - b), (0, 0))), w_ada, b_ada)[:, :b]
    bias_tab = _bias_table(rel_bias)
    wrt = w_router.T
    wrh = wrt.astype(BF16)
    wrl = (wrt - wrh.astype(F32)).astype(BF16)
    rb = router_bias.reshape(N_EXPERTS, 1).astype(F32)
    row = lambda v: v.reshape(1, -1).astype(F32)

    wg_all, wu_all, wd_all = [w.reshape((-1,) + w.shape[2:]) for w in (w_gate_e, w_up_e, w_down_e)]

    moe = None
    for l in range(depth):
        mod = mod_all[l].reshape(b, ADA_CHUNKS, d)
        w_l, wift = _layout_w_in(w_in[l])
        gate_bias = jnp.concatenate([b_igate[l], b_fgate[l]]).astype(F32)
        bcol = jnp.pad(gate_bias, (0, LANES - 2 * M_HEADS)).reshape(1, LANES)
        brow = gate_bias.reshape(2 * M_HEADS, 1)

        proj_args = (mod, row(w_norm1[l]), w_l, wift, bcol, brow)
        if moe is None:
            q, kt, v, o, qa, kva, g, ifc, ifr = _inproj_call(x, *proj_args)
        else:
            x, q, kt, v, o, qa, kva, g, ifc, ifr = _combine_inproj_call(*moe, *proj_args)
        cw, cb = conv_w[l].astype(F32), conv_b[l].astype(F32)
        cwk = jnp.broadcast_to(cw[:, M_QK_W:, None], (M_CONV, M_QK_W, MLSTM_CHUNK))
        cbk = jnp.broadcast_to(cb[M_QK_W:, None], (M_QK_W, MLSTM_CHUNK))
        hm = _mlstm_call(q, kt, v, o, ifc, ifr, cw[:, :M_QK_W], row(cb[:M_QK_W]), cwk, cbk, row(w_mnorm[l]))
        ha = _swa_call(sinks[l].astype(F32), qa, kva, bias_tab)

        wba = jnp.concatenate([w_br_a[l][A_HEAD_DIM * h:A_HEAD_DIM * (h + 1)] for h in A_HEAD_ORDER])
        x1, payload, cls, rank, cnt = _post_call(
            hm, ha, g, x, mod, w_br_m[l].astype(BF16), wba.astype(BF16), w_out[l].astype(BF16),
            row(w_norm2[l]), wrh, wrl, rb)

        counts = cnt[:, 0].astype(jnp.int32)
        off, schedule, n_used = _tile_tables(counts[:N_CLASSES], n_tiles, l)
        hit = cls.reshape(t, 1) == jnp.arange(CLASS_ROWS, dtype=jnp.int32)[None, :]
        pos = jnp.sum(jnp.where(hit, off[None, :], 0), axis=1) + rank.reshape(t)
        xs = _dispatch_call(pos, off, counts, n_used,
                            payload.reshape(t // SUBLANES, SUBLANES, PAYLOAD_W), npad)
        ys = _moe_call(schedule, n_used, xs, wg_all, wu_all, wd_all)
        moe = (pos, ys, x1, mod)
    return _combine_call(*moe, row(w_final))
```
